```python
import math
import jax, jax.numpy as jnp
from jax import lax
import numpy as np

D_MODEL = 2048
BATCH = 8
SEQ = 4096
DEPTH = 4

HEAD_DIM = 128
N_QK_HEADS_A = 8
N_V_HEADS_A = 16
QK_WIDTH_A = N_QK_HEADS_A * HEAD_DIM
V_WIDTH_A = N_V_HEADS_A * HEAD_DIM
CONV_A = 4
CHUNK_A = 64
N_GROUPS_B = 8
GROUP_DIM_B = 128
WIDTH_B = N_GROUPS_B * GROUP_DIM_B
CHUNK_B = 128
N_BRANCH = 2
D_FF = 5632
CONV_FFN = 3
PLE_DIM = 256
EPS = 1e-6

SPLIT_SIZES = (QK_WIDTH_A, QK_WIDTH_A, V_WIDTH_A, V_WIDTH_A, N_V_HEADS_A, N_V_HEADS_A,
               WIDTH_B, WIDTH_B, N_BRANCH * D_MODEL)
N_IN = sum(SPLIT_SIZES)

kernel_name = "hybrid_deltanet_sgu_convglu_ple"


def rms_norm(x, gain):
    xf = x.astype(jnp.float32)
    y = xf * lax.rsqrt(jnp.mean(xf * xf, axis=-1, keepdims=True) + EPS)
    return (y * gain.astype(jnp.float32)).astype(x.dtype)


def l2_normalize(x):
    return x * lax.rsqrt(jnp.sum(x * x, axis=-1, keepdims=True) + EPS)


def causal_depthwise_conv(x, w):
    k = w.shape[0]
    return lax.conv_general_dilated(
        x, w[:, None, :].astype(x.dtype), window_strides=(1,), padding=((k - 1, 0),),
        dimension_numbers=("NWC", "WIO", "NWC"), feature_group_count=x.shape[-1])


def gated_delta_rule(q, k, v, beta, g):
    b, s, h, dk = q.shape
    dv = v.shape[-1]
    n = s // CHUNK_A

    def chunks(t):
        t = t.reshape((b, n, CHUNK_A, h) + t.shape[3:])
        return jnp.moveaxis(t, (1, 3), (0, 2))

    qc, kc, vc = chunks(q), chunks(k), chunks(v)
    bc, gc = chunks(beta), chunks(g)
    gam = jnp.cumsum(gc, axis=-1)
    causal = jnp.tril(jnp.ones((CHUNK_A, CHUNK_A), dtype=bool))
    strict = jnp.tril(jnp.ones((CHUNK_A, CHUNK_A), dtype=bool), -1)
    decay = jnp.exp(jnp.where(causal, gam[..., :, None] - gam[..., None, :], -jnp.inf))
    kb = kc * bc[..., None]
    a_mat = jnp.where(strict, jnp.einsum("nbhcd,nbhsd->nbhcs", kb, kc) * decay, 0.0)
    eye = jnp.eye(CHUNK_A, dtype=jnp.float32)
    rhs = jnp.concatenate([vc * bc[..., None], kb * jnp.exp(gam)[..., None]], axis=-1)
    sol = lax.linalg.triangular_solve(a_mat + eye, rhs, left_side=True, lower=True,
                                      unit_diagonal=True)
    u, w = sol[..., :dv], sol[..., dv:]
    attn = jnp.einsum("nbhcd,nbhsd->nbhcs", qc, kc) * decay
    q_dec = qc * jnp.exp(gam)[..., None]
    k_dec = kc * jnp.exp(gam[..., -1:] - gam)[..., None]
    chunk_decay = jnp.exp(gam[..., -1])

    def step(state, inp):
        q_i, k_i, u_i, w_i, a_i, d_i = inp
        v_new = u_i - jnp.einsum("bhcd,bhde->bhce", w_i, state)
        o = jnp.einsum("bhcd,bhde->bhce", q_i, state) + jnp.einsum("bhcs,bhse->bhce", a_i, v_new)
        state = state * d_i[..., None, None] + jnp.einsum("bhcd,bhce->bhde", k_i, v_new)
        return state, o

    s0 = jnp.zeros((b, h, dk, dv), jnp.float32)
    _, o = lax.scan(step, s0, (q_dec, k_dec, u, w, attn, chunk_decay))
    return jnp.moveaxis(o, (0, 2), (1, 3)).reshape(b, s, h, dv)


def delta_mixer(q, k, v, z, beta_logit, a_logit, conv_w, a_log, dt_bias, head_gain):
    b, s, _ = q.shape
    f32 = jnp.float32
    qkv = jax.nn.silu(causal_depthwise_conv(jnp.concatenate([q, k, v], axis=-1), conv_w))
    q, k, v = jnp.split(qkv, [QK_WIDTH_A, 2 * QK_WIDTH_A], axis=-1)
    rep = N_V_HEADS_A // N_QK_HEADS_A
    q = l2_normalize(q.reshape(b, s, N_QK_HEADS_A, HEAD_DIM).astype(f32)) * (HEAD_DIM ** -0.5)
    k = l2_normalize(k.reshape(b, s, N_QK_HEADS_A, HEAD_DIM).astype(f32))
    q = jnp.repeat(q, rep, axis=2)
    k = jnp.repeat(k, rep, axis=2)
    v = v.reshape(b, s, N_V_HEADS_A, HEAD_DIM).astype(f32)
    beta = jax.nn.sigmoid(beta_logit.astype(f32))
    g = -jnp.exp(a_log.astype(f32)) * jax.nn.softplus(a_logit.astype(f32) + dt_bias.astype(f32))
    o = gated_delta_rule(q, k, v, beta, g)
    o = rms_norm(o, head_gain) * jax.nn.silu(z.reshape(b, s, N_V_HEADS_A, HEAD_DIM).astype(f32))
    return o.reshape(b, s, V_WIDTH_A).astype(z.dtype)


def spatial_gating_mixer(u, v, norm_gain, w_s, b_s):
    b, s, _ = u.shape
    n = s // CHUNK_B
    u = jax.nn.gelu(u)
    v = rms_norm(jax.nn.gelu(v), norm_gain)
    causal = jnp.tril(jnp.ones((CHUNK_B, CHUNK_B), dtype=bool))
    w = jnp.where(causal, w_s, 0.0).astype(v.dtype)
    vc = v.reshape(b, n, CHUNK_B, N_GROUPS_B, GROUP_DIM_B)
    mixed = jnp.einsum("gts,bnsgc->bntgc", w, vc) + b_s.T[None, None, :, :, None].astype(v.dtype)
    return u * mixed.reshape(b, s, WIDTH_B)


def conv_glu_ffn(h, w_up, conv_w, conv_b, w_down):
    up = causal_depthwise_conv(h @ w_up, conv_w) + conv_b
    gate, val = jnp.split(up, 2, axis=-1)
    return (jax.nn.silu(gate) * val) @ w_down


def _fwd_setup_inputs(seed: int = 0) -> dict:
    key = jax.random.key(seed)
    ks = jax.random.split(key, 24)
    f32 = jnp.float32

    def nrm(k, shape, scale):
        return jax.random.normal(k, shape, f32) * scale

    def gain(k, shape):
        return 1.0 + 0.02 * jax.random.normal(k, shape, f32)

    dt = jnp.exp(jax.random.uniform(ks[5], (DEPTH, N_V_HEADS_A), f32, math.log(1e-3), math.log(1e-1)))
    return {
        "x": nrm(ks[0], (BATCH, SEQ, D_MODEL), 1.0),
        "p": nrm(ks[1], (DEPTH, BATCH, SEQ, PLE_DIM), 1.0),
        "norm_mix": gain(ks[2], (DEPTH, D_MODEL)),
        "w_in": nrm(ks[3], (DEPTH, D_MODEL, N_IN), D_MODEL ** -0.5),
        "conv_qkv": nrm(ks[6], (DEPTH, CONV_A, 2 * QK_WIDTH_A + V_WIDTH_A), CONV_A ** -0.5),
        "a_log": jnp.log(jax.random.uniform(ks[4], (DEPTH, N_V_HEADS_A), f32, 1.0, 16.0)),
        "dt_bias": dt + jnp.log(-jnp.expm1(-dt)),
        "head_norm": gain(ks[7], (DEPTH, HEAD_DIM)),
        "sgu_norm": gain(ks[8], (DEPTH, WIDTH_B)),
        "w_spatial": nrm(ks[9], (DEPTH, N_GROUPS_B, CHUNK_B, CHUNK_B), CHUNK_B ** -0.5),
        "b_spatial": 1.0 + nrm(ks[10], (DEPTH, N_GROUPS_B, CHUNK_B), 0.1),
        "w_branch_a": nrm(ks[11], (DEPTH, V_WIDTH_A, D_MODEL), V_WIDTH_A ** -0.5),
        "w_branch_b": nrm(ks[12], (DEPTH, WIDTH_B, D_MODEL), WIDTH_B ** -0.5),
        "w_out": nrm(ks[13], (DEPTH, D_MODEL, D_MODEL), D_MODEL ** -0.5),
        "norm_ffn": gain(ks[14], (DEPTH, D_MODEL)),
        "w_ffn_up": nrm(ks[15], (DEPTH, D_MODEL, 2 * D_FF), D_MODEL ** -0.5),
        "conv_ffn": nrm(ks[16], (DEPTH, CONV_FFN, 2 * D_FF), CONV_FFN ** -0.5),
        "b_conv_ffn": nrm(ks[17], (DEPTH, 2 * D_FF), 0.02),
        "w_ffn_down": nrm(ks[18], (DEPTH, D_FF, D_MODEL), D_FF ** -0.5),
        "norm_ple": gain(ks[19], (DEPTH, D_MODEL)),
        "w_ple_gate": nrm(ks[20], (DEPTH, D_MODEL, D_MODEL), D_MODEL ** -0.5),
        "w_ple_proj": nrm(ks[21], (DEPTH, PLE_DIM, D_MODEL), PLE_DIM ** -0.5),
        "norm_final": gain(ks[22], (D_MODEL,)),
    }


def _fwd_reference(x, p, norm_mix, w_in, conv_qkv, a_log, dt_bias, head_norm, sgu_norm, w_spatial,
              b_spatial, w_branch_a, w_branch_b, w_out, norm_ffn, w_ffn_up, conv_ffn, b_conv_ffn,
              w_ffn_down, norm_ple, w_ple_gate, w_ple_proj, norm_final):
    split_idx = np.cumsum(SPLIT_SIZES)[:-1].tolist()
    for i in range(DEPTH):
        h = rms_norm(x, norm_mix[i])
        proj = h @ w_in[i]
        q, k, v, z, b_logit, a_logit, u_b, v_b, gates = jnp.split(proj, split_idx, axis=-1)
        y_a = delta_mixer(q, k, v, z, b_logit, a_logit, conv_qkv[i], a_log[i], dt_bias[i],
                          head_norm[i]) @ w_branch_a[i]
        y_b = spatial_gating_mixer(u_b, v_b, sgu_norm[i], w_spatial[i], b_spatial[i]) @ w_branch_b[i]
        g_a, g_b = jnp.split(gates, N_BRANCH, axis=-1)
        merged = jax.nn.sigmoid(g_a) * y_a + jax.nn.sigmoid(g_b) * y_b
        x = x + merged @ w_out[i]
        h = rms_norm(x, norm_ffn[i])
        x = x + conv_glu_ffn(h, w_ffn_up[i], conv_ffn[i], b_conv_ffn[i], w_ffn_down[i])
        h = rms_norm(x, norm_ple[i])
        x = x + jax.nn.sigmoid(h @ w_ple_gate[i]) * (p[i] @ w_ple_proj[i])
    return rms_norm(x, norm_final)


import jax as _jax
import jax.numpy as _jnp

TWIN_FORMAT = 'train_step'
FWD_PARAMS = ['x', 'p', 'norm_mix', 'w_in', 'conv_qkv', 'a_log', 'dt_bias', 'head_norm', 'sgu_norm', 'w_spatial', 'b_spatial', 'w_branch_a', 'w_branch_b', 'w_out', 'norm_ffn', 'w_ffn_up', 'conv_ffn', 'b_conv_ffn', 'w_ffn_down', 'norm_ple', 'w_ple_gate', 'w_ple_proj', 'norm_final']
TWIN_WEIGHTS = ['norm_mix', 'w_in', 'conv_qkv', 'a_log', 'dt_bias', 'head_norm', 'sgu_norm', 'w_spatial', 'b_spatial', 'w_branch_a', 'w_branch_b', 'w_out', 'norm_ffn', 'w_ffn_up', 'conv_ffn', 'b_conv_ffn', 'w_ffn_down', 'norm_ple', 'w_ple_gate', 'w_ple_proj', 'norm_final']
TWIN_DIFF_INPUT = 'x'
TWIN_INPUTS = ['x', 'p', 'norm_mix', 'w_in', 'conv_qkv', 'a_log', 'dt_bias', 'head_norm', 'sgu_norm', 'w_spatial', 'b_spatial', 'w_branch_a', 'w_branch_b', 'w_out', 'norm_ffn', 'w_ffn_up', 'conv_ffn', 'b_conv_ffn', 'w_ffn_down', 'norm_ple', 'w_ple_gate', 'w_ple_proj', 'norm_final', 'loss_target', 'm_norm_mix', 'm_w_in', 'm_conv_qkv', 'm_a_log', 'm_dt_bias', 'm_head_norm', 'm_sgu_norm', 'm_w_spatial', 'm_b_spatial', 'm_w_branch_a', 'm_w_branch_b', 'm_w_out', 'm_norm_ffn', 'm_w_ffn_up', 'm_conv_ffn', 'm_b_conv_ffn', 'm_w_ffn_down', 'm_norm_ple', 'm_w_ple_gate', 'm_w_ple_proj', 'm_norm_final', 'v_norm_mix', 'v_w_in', 'v_conv_qkv', 'v_a_log', 'v_dt_bias', 'v_head_norm', 'v_sgu_norm', 'v_w_spatial', 'v_b_spatial', 'v_w_branch_a', 'v_w_branch_b', 'v_w_out', 'v_norm_ffn', 'v_w_ffn_up', 'v_conv_ffn', 'v_b_conv_ffn', 'v_w_ffn_down', 'v_norm_ple', 'v_w_ple_gate', 'v_w_ple_proj', 'v_norm_final']
TWIN_OUTPUTS = ['loss', 'grad_x', 'grad_norm_mix', 'grad_w_in', 'grad_conv_qkv', 'grad_a_log', 'grad_dt_bias', 'grad_head_norm', 'grad_sgu_norm', 'grad_w_spatial', 'grad_b_spatial', 'grad_w_branch_a', 'grad_w_branch_b', 'grad_w_out', 'grad_norm_ffn', 'grad_w_ffn_up', 'grad_conv_ffn', 'grad_b_conv_ffn', 'grad_w_ffn_down', 'grad_norm_ple', 'grad_w_ple_gate', 'grad_w_ple_proj', 'grad_norm_final', 'delta_norm_mix', 'delta_w_in', 'delta_conv_qkv', 'delta_a_log', 'delta_dt_bias', 'delta_head_norm', 'delta_sgu_norm', 'delta_w_spatial', 'delta_b_spatial', 'delta_w_branch_a', 'delta_w_branch_b', 'delta_w_out', 'delta_norm_ffn', 'delta_w_ffn_up', 'delta_conv_ffn', 'delta_b_conv_ffn', 'delta_w_ffn_down', 'delta_norm_ple', 'delta_w_ple_gate', 'delta_w_ple_proj', 'delta_norm_final', 'new_m_norm_mix', 'new_m_w_in', 'new_m_conv_qkv', 'new_m_a_log', 'new_m_dt_bias', 'new_m_head_norm', 'new_m_sgu_norm', 'new_m_w_spatial', 'new_m_b_spatial', 'new_m_w_branch_a', 'new_m_w_branch_b', 'new_m_w_out', 'new_m_norm_ffn', 'new_m_w_ffn_up', 'new_m_conv_ffn', 'new_m_b_conv_ffn', 'new_m_w_ffn_down', 'new_m_norm_ple', 'new_m_w_ple_gate', 'new_m_w_ple_proj', 'new_m_norm_final', 'new_v_norm_mix', 'new_v_w_in', 'new_v_conv_qkv', 'new_v_a_log', 'new_v_dt_bias', 'new_v_head_norm', 'new_v_sgu_norm', 'new_v_w_spatial', 'new_v_b_spatial', 'new_v_w_branch_a', 'new_v_w_branch_b', 'new_v_w_out', 'new_v_norm_ffn', 'new_v_w_ffn_up', 'new_v_conv_ffn', 'new_v_b_conv_ffn', 'new_v_w_ffn_down', 'new_v_norm_ple', 'new_v_w_ple_gate', 'new_v_w_ple_proj', 'new_v_norm_final']
TWIN_LEAF_KINDS = {'loss': 'loss', 'grad_x': 'grad_x', 'grad_norm_mix': 'grad_w', 'grad_w_in': 'grad_w', 'grad_conv_qkv': 'grad_w', 'grad_a_log': 'grad_w', 'grad_dt_bias': 'grad_w', 'grad_head_norm': 'grad_w', 'grad_sgu_norm': 'grad_w', 'grad_w_spatial': 'grad_w', 'grad_b_spatial': 'grad_w', 'grad_w_branch_a': 'grad_w', 'grad_w_branch_b': 'grad_w', 'grad_w_out': 'grad_w', 'grad_norm_ffn': 'grad_w', 'grad_w_ffn_up': 'grad_w', 'grad_conv_ffn': 'grad_w', 'grad_b_conv_ffn': 'grad_w', 'grad_w_ffn_down': 'grad_w', 'grad_norm_ple': 'grad_w', 'grad_w_ple_gate': 'grad_w', 'grad_w_ple_proj': 'grad_w', 'grad_norm_final': 'grad_w', 'delta_norm_mix': 'delta_w', 'delta_w_in': 'delta_w', 'delta_conv_qkv': 'delta_w', 'delta_a_log': 'delta_w', 'delta_dt_bias': 'delta_w', 'delta_head_norm': 'delta_w', 'delta_sgu_norm': 'delta_w', 'delta_w_spatial': 'delta_w', 'delta_b_spatial': 'delta_w', 'delta_w_branch_a': 'delta_w', 'delta_w_branch_b': 'delta_w', 'delta_w_out': 'delta_w', 'delta_norm_ffn': 'delta_w', 'delta_w_ffn_up': 'delta_w', 'delta_conv_ffn': 'delta_w', 'delta_b_conv_ffn': 'delta_w', 'delta_w_ffn_down': 'delta_w', 'delta_norm_ple': 'delta_w', 'delta_w_ple_gate': 'delta_w', 'delta_w_ple_proj': 'delta_w', 'delta_norm_final': 'delta_w', 'new_m_norm_mix': 'new_m', 'new_m_w_in': 'new_m', 'new_m_conv_qkv': 'new_m', 'new_m_a_log': 'new_m', 'new_m_dt_bias': 'new_m', 'new_m_head_norm': 'new_m', 'new_m_sgu_norm': 'new_m', 'new_m_w_spatial': 'new_m', 'new_m_b_spatial': 'new_m', 'new_m_w_branch_a': 'new_m', 'new_m_w_branch_b': 'new_m', 'new_m_w_out': 'new_m', 'new_m_norm_ffn': 'new_m', 'new_m_w_ffn_up': 'new_m', 'new_m_conv_ffn': 'new_m', 'new_m_b_conv_ffn': 'new_m', 'new_m_w_ffn_down': 'new_m', 'new_m_norm_ple': 'new_m', 'new_m_w_ple_gate': 'new_m', 'new_m_w_ple_proj': 'new_m', 'new_m_norm_final': 'new_m', 'new_v_norm_mix': 'new_v', 'new_v_w_in': 'new_v', 'new_v_conv_qkv': 'new_v', 'new_v_a_log': 'new_v', 'new_v_dt_bias': 'new_v', 'new_v_head_norm': 'new_v', 'new_v_sgu_norm': 'new_v', 'new_v_w_spatial': 'new_v', 'new_v_b_spatial': 'new_v', 'new_v_w_branch_a': 'new_v', 'new_v_w_branch_b': 'new_v', 'new_v_w_out': 'new_v', 'new_v_norm_ffn': 'new_v', 'new_v_w_ffn_up': 'new_v', 'new_v_conv_ffn': 'new_v', 'new_v_b_conv_ffn': 'new_v', 'new_v_w_ffn_down': 'new_v', 'new_v_norm_ple': 'new_v', 'new_v_w_ple_gate': 'new_v', 'new_v_w_ple_proj': 'new_v', 'new_v_norm_final': 'new_v'}


def _forward(args):
    return _fwd_reference(*[args[k] for k in FWD_PARAMS])


def _output_shape():
    def fwd():
        inp = _fwd_setup_inputs(0)
        return _fwd_reference(*[inp[k] for k in FWD_PARAMS])
    out = _jax.eval_shape(fwd)
    return out.shape, out.dtype

N_MICROBATCH = 1
ADAM_LR = 0.001
ADAM_B1 = 0.9
ADAM_B2 = 0.999
ADAM_EPS = 1e-08
ADAM_WD = 0.01
ADAM_STEP = 10
PER_EXAMPLE_BATCH_AXIS = {'x': 0, 'p': 1, 'loss_target': 0}
SHARED_INPUTS = []
_WEIGHT_DTYPES = {'norm_mix': _jnp.float32, 'w_in': _jnp.float32, 'conv_qkv': _jnp.float32, 'a_log': _jnp.float32, 'dt_bias': _jnp.float32, 'head_norm': _jnp.float32, 'sgu_norm': _jnp.float32, 'w_spatial': _jnp.float32, 'b_spatial': _jnp.float32, 'w_branch_a': _jnp.float32, 'w_branch_b': _jnp.float32, 'w_out': _jnp.float32, 'norm_ffn': _jnp.float32, 'w_ffn_up': _jnp.float32, 'conv_ffn': _jnp.float32, 'b_conv_ffn': _jnp.float32, 'w_ffn_down': _jnp.float32, 'norm_ple': _jnp.float32, 'w_ple_gate': _jnp.float32, 'w_ple_proj': _jnp.float32, 'norm_final': _jnp.float32}
MOMENT_SCALE = {'norm_mix': 5.507198e-02, 'w_in': 2.206234e-02, 'conv_qkv': 2.112173e-02, 'a_log': 9.241096e-02, 'dt_bias': 8.774065e-02, 'head_norm': 9.221455e-02, 'sgu_norm': 2.392104e-02, 'w_spatial': 2.449984e-02, 'b_spatial': 3.454066e-02, 'w_branch_a': 2.211615e-02, 'w_branch_b': 3.099517e-02, 'w_out': 3.800048e-02, 'norm_ffn': 5.247372e-02, 'w_ffn_up': 2.204326e-02, 'conv_ffn': 2.212793e-02, 'b_conv_ffn': 2.197942e-02, 'w_ffn_down': 3.599244e-02, 'norm_ple': 1.233372e-02, 'w_ple_gate': 1.230665e-02, 'w_ple_proj': 3.142556e-02, 'norm_final': 1.600932e+01}


def _to_microbatches(a, axis):
    t = _jnp.moveaxis(a, axis, 0)
    t = t.reshape((N_MICROBATCH, t.shape[0] // N_MICROBATCH) + t.shape[1:])
    return _jnp.moveaxis(t, 1, axis + 1)


def setup_inputs(seed: int = 0) -> dict:
    inp = _fwd_setup_inputs(seed)
    key = _jax.random.fold_in(_jax.random.key(seed), 7919)
    shape, _ = _output_shape()
    out = dict(inp)
    out["loss_target"] = _jax.random.normal(_jax.random.fold_in(key, 0), shape, _jnp.float32)
    for i, name in enumerate(TWIN_WEIGHTS):
        w = inp[name].astype(_jnp.float32)
        if MOMENT_SCALE is None:
            s = _jnp.sqrt(_jnp.mean(_jnp.square(w)) + 1e-30)
        else:
            s = MOMENT_SCALE[name]
        km, kv = _jax.random.split(_jax.random.fold_in(key, i + 1))
        out[name] = w
        out["m_" + name] = s * _jax.random.normal(km, w.shape, _jnp.float32)
        out["v_" + name] = (s * s) * _jax.random.uniform(kv, w.shape, _jnp.float32, 0.5, 1.5)
    if N_MICROBATCH > 1:
        for name, axis in PER_EXAMPLE_BATCH_AXIS.items():
            out[name] = _to_microbatches(out[name], axis)
    return {'x': out['x'], 'p': out['p'], 'norm_mix': out['norm_mix'], 'w_in': out['w_in'], 'conv_qkv': out['conv_qkv'], 'a_log': out['a_log'], 'dt_bias': out['dt_bias'], 'head_norm': out['head_norm'], 'sgu_norm': out['sgu_norm'], 'w_spatial': out['w_spatial'], 'b_spatial': out['b_spatial'], 'w_branch_a': out['w_branch_a'], 'w_branch_b': out['w_branch_b'], 'w_out': out['w_out'], 'norm_ffn': out['norm_ffn'], 'w_ffn_up': out['w_ffn_up'], 'conv_ffn': out['conv_ffn'], 'b_conv_ffn': out['b_conv_ffn'], 'w_ffn_down': out['w_ffn_down'], 'norm_ple': out['norm_ple'], 'w_ple_gate': out['w_ple_gate'], 'w_ple_proj': out['w_ple_proj'], 'norm_final': out['norm_final'], 'loss_target': out['loss_target'], 'm_norm_mix': out['m_norm_mix'], 'm_w_in': out['m_w_in'], 'm_conv_qkv': out['m_conv_qkv'], 'm_a_log': out['m_a_log'], 'm_dt_bias': out['m_dt_bias'], 'm_head_norm': out['m_head_norm'], 'm_sgu_norm': out['m_sgu_norm'], 'm_w_spatial': out['m_w_spatial'], 'm_b_spatial': out['m_b_spatial'], 'm_w_branch_a': out['m_w_branch_a'], 'm_w_branch_b': out['m_w_branch_b'], 'm_w_out': out['m_w_out'], 'm_norm_ffn': out['m_norm_ffn'], 'm_w_ffn_up': out['m_w_ffn_up'], 'm_conv_ffn': out['m_conv_ffn'], 'm_b_conv_ffn': out['m_b_conv_ffn'], 'm_w_ffn_down': out['m_w_ffn_down'], 'm_norm_ple': out['m_norm_ple'], 'm_w_ple_gate': out['m_w_ple_gate'], 'm_w_ple_proj': out['m_w_ple_proj'], 'm_norm_final': out['m_norm_final'], 'v_norm_mix': out['v_norm_mix'], 'v_w_in': out['v_w_in'], 'v_conv_qkv': out['v_conv_qkv'], 'v_a_log': out['v_a_log'], 'v_dt_bias': out['v_dt_bias'], 'v_head_norm': out['v_head_norm'], 'v_sgu_norm': out['v_sgu_norm'], 'v_w_spatial': out['v_w_spatial'], 'v_b_spatial': out['v_b_spatial'], 'v_w_branch_a': out['v_w_branch_a'], 'v_w_branch_b': out['v_w_branch_b'], 'v_w_out': out['v_w_out'], 'v_norm_ffn': out['v_norm_ffn'], 'v_w_ffn_up': out['v_w_ffn_up'], 'v_conv_ffn': out['v_conv_ffn'], 'v_b_conv_ffn': out['v_b_conv_ffn'], 'v_w_ffn_down': out['v_w_ffn_down'], 'v_norm_ple': out['v_norm_ple'], 'v_w_ple_gate': out['v_w_ple_gate'], 'v_w_ple_proj': out['v_w_ple_proj'], 'v_norm_final': out['v_norm_final']}


def _loss(weights, diff, rest, loss_target):
    with _jax.named_scope("forward"):
        args = {**rest, TWIN_DIFF_INPUT: diff, **{k: w.astype(_WEIGHT_DTYPES[k]) for k, w in weights.items()}}
        y = _forward(args)
    with _jax.named_scope("loss_head"):
        err = _jnp.square(y.astype(_jnp.float32) - loss_target)
        return 0.5 * _jnp.sum(_jnp.mean(err, axis=-1)) if err.ndim else 0.5 * err


def _adamw(w, g, m, v):
    m = ADAM_B1 * m + (1.0 - ADAM_B1) * g
    v = ADAM_B2 * v + (1.0 - ADAM_B2) * _jnp.square(g)
    m_hat = m / (1.0 - ADAM_B1 ** ADAM_STEP)
    v_hat = v / (1.0 - ADAM_B2 ** ADAM_STEP)
    delta = -ADAM_LR * (m_hat / (_jnp.sqrt(v_hat) + ADAM_EPS) + ADAM_WD * w)
    return delta, m, v


def reference(x, p, norm_mix, w_in, conv_qkv, a_log, dt_bias, head_norm, sgu_norm, w_spatial, b_spatial, w_branch_a, w_branch_b, w_out, norm_ffn, w_ffn_up, conv_ffn, b_conv_ffn, w_ffn_down, norm_ple, w_ple_gate, w_ple_proj, norm_final, loss_target, m_norm_mix, m_w_in, m_conv_qkv, m_a_log, m_dt_bias, m_head_norm, m_sgu_norm, m_w_spatial, m_b_spatial, m_w_branch_a, m_w_branch_b, m_w_out, m_norm_ffn, m_w_ffn_up, m_conv_ffn, m_b_conv_ffn, m_w_ffn_down, m_norm_ple, m_w_ple_gate, m_w_ple_proj, m_norm_final, v_norm_mix, v_w_in, v_conv_qkv, v_a_log, v_dt_bias, v_head_norm, v_sgu_norm, v_w_spatial, v_b_spatial, v_w_branch_a, v_w_branch_b, v_w_out, v_norm_ffn, v_w_ffn_up, v_conv_ffn, v_b_conv_ffn, v_w_ffn_down, v_norm_ple, v_w_ple_gate, v_w_ple_proj, v_norm_final):
    given = dict(x=x, p=p, norm_mix=norm_mix, w_in=w_in, conv_qkv=conv_qkv, a_log=a_log, dt_bias=dt_bias, head_norm=head_norm, sgu_norm=sgu_norm, w_spatial=w_spatial, b_spatial=b_spatial, w_branch_a=w_branch_a, w_branch_b=w_branch_b, w_out=w_out, norm_ffn=norm_ffn, w_ffn_up=w_ffn_up, conv_ffn=conv_ffn, b_conv_ffn=b_conv_ffn, w_ffn_down=w_ffn_down, norm_ple=norm_ple, w_ple_gate=w_ple_gate, w_ple_proj=w_ple_proj, norm_final=norm_final, loss_target=loss_target, m_norm_mix=m_norm_mix, m_w_in=m_w_in, m_conv_qkv=m_conv_qkv, m_a_log=m_a_log, m_dt_bias=m_dt_bias, m_head_norm=m_head_norm, m_sgu_norm=m_sgu_norm, m_w_spatial=m_w_spatial, m_b_spatial=m_b_spatial, m_w_branch_a=m_w_branch_a, m_w_branch_b=m_w_branch_b, m_w_out=m_w_out, m_norm_ffn=m_norm_ffn, m_w_ffn_up=m_w_ffn_up, m_conv_ffn=m_conv_ffn, m_b_conv_ffn=m_b_conv_ffn, m_w_ffn_down=m_w_ffn_down, m_norm_ple=m_norm_ple, m_w_ple_gate=m_w_ple_gate, m_w_ple_proj=m_w_ple_proj, m_norm_final=m_norm_final, v_norm_mix=v_norm_mix, v_w_in=v_w_in, v_conv_qkv=v_conv_qkv, v_a_log=v_a_log, v_dt_bias=v_dt_bias, v_head_norm=v_head_norm, v_sgu_norm=v_sgu_norm, v_w_spatial=v_w_spatial, v_b_spatial=v_b_spatial, v_w_branch_a=v_w_branch_a, v_w_branch_b=v_w_branch_b, v_w_out=v_w_out, v_norm_ffn=v_norm_ffn, v_w_ffn_up=v_w_ffn_up, v_conv_ffn=v_conv_ffn, v_b_conv_ffn=v_b_conv_ffn, v_w_ffn_down=v_w_ffn_down, v_norm_ple=v_norm_ple, v_w_ple_gate=v_w_ple_gate, v_w_ple_proj=v_w_ple_proj, v_norm_final=v_norm_final)
    weights = {n: given[n] for n in TWIN_WEIGHTS}
    shared = {n: given[n] for n in SHARED_INPUTS}
    per_example = {n: given[n] for n in ['x', 'p']}
    grad_fn = _jax.value_and_grad(_loss, argnums=(0, 1))

    def one_microbatch(ex, loss_target):
        ex = dict(ex)
        diff = ex.pop(TWIN_DIFF_INPUT)
        return grad_fn(weights, diff, {**shared, **ex}, loss_target)

    if N_MICROBATCH == 1:
        loss, (grad_w, grad_x) = one_microbatch(per_example, given["loss_target"])
    else:
        def body(carry, xs):
            loss_sum, grad_sum = carry
            l_k, (gw_k, gx_k) = one_microbatch(xs[0], xs[1])
            with _jax.named_scope("update"):
                return (loss_sum + l_k, _jax.tree.map(_jnp.add, grad_sum, gw_k)), gx_k

        init = (_jnp.zeros((), _jnp.float32), _jax.tree.map(_jnp.zeros_like, weights))
        (loss, grad_w), grad_x = _jax.lax.scan(body, init, (per_example, given["loss_target"]))
    with _jax.named_scope("update"):
        delta_w, new_m, new_v = {}, {}, {}
        for n in TWIN_WEIGHTS:
            delta_w[n], new_m[n], new_v[n] = _adamw(weights[n], grad_w[n], given["m_" + n], given["v_" + n])
    return (loss, grad_x, *[grad_w[n] for n in TWIN_WEIGHTS], *[delta_w[n] for n in TWIN_WEIGHTS],
            *[new_m[n] for n in TWIN_WEIGHTS], *[new_v[n] for n in TWIN_WEIGHTS])
```

```python
import functools
import math

import jax
import jax.numpy as jnp
from jax import lax
from jax.experimental import pallas as pl
from jax.experimental.pallas import tpu as pltpu

f32 = jnp.float32
bf16 = jnp.bfloat16
HI = lax.Precision.HIGHEST
SDS = jax.ShapeDtypeStruct

D_MODEL = 2048
DEPTH = 4
HEAD_DIM = 128
N_QK = 8
N_V = 16
QK_W = N_QK * HEAD_DIM
V_W = N_V * HEAD_DIM
CHUNK_A = 64
N_GROUPS = 8
GROUP_DIM = 128
WIDTH_B = N_GROUPS * GROUP_DIM
CHUNK_B = 128
D_FF = 5632
PLE_DIM = 256
EPS = 1e-6
N_IN = 12320
ADAM_LR, ADAM_B1, ADAM_B2, ADAM_EPS, ADAM_WD, ADAM_STEP = 0.001, 0.9, 0.999, 1e-08, 0.01, 10

C_Q, C_K, C_V, C_Z, C_UB, C_VB, C_GA, C_GB, PM = 0, 1024, 2048, 4096, 6144, 7168, 8192, 10240, 12288
LANES = 128
VMEM_LIMIT_BYTES = 48 * 1024 * 1024
ROW_CHUNK = 256


def _cp(*sem):
    return pltpu.CompilerParams(dimension_semantics=sem if sem else None, vmem_limit_bytes=VMEM_LIMIT_BYTES)


def _dot(a, b, prec=None):
    return jnp.dot(a, b, preferred_element_type=f32, precision=prec)


def _dot_nt(a, b, prec=None):
    return lax.dot_general(a, b, (((1,), (1,)), ((), ())), preferred_element_type=f32, precision=prec)


def _dot_tn(a, b, prec=None):
    return lax.dot_general(a, b, (((0,), (0,)), ((), ())), preferred_element_type=f32, precision=prec)


def _b(x):
    return x.astype(bf16)


def _sig(x):
    return 1.0 / (1.0 + jnp.exp(-x))


def _silu(x):
    return x * _sig(x)


def _dsilu(x):
    s = _sig(x)
    return s * (1.0 + x * (1.0 - s))


_GELU_C = 0.7978845608028654
_GELU_A = 0.044715


def _gelu(x):
    return 0.5 * x * (1.0 + jnp.tanh(_GELU_C * (x + _GELU_A * x * x * x)))


def _dgelu(x):
    t = jnp.tanh(_GELU_C * (x + _GELU_A * x * x * x))
    return 0.5 * (1.0 + t) + 0.5 * x * (1.0 - t * t) * _GELU_C * (1.0 + 3.0 * _GELU_A * x * x)


def _softplus(x):
    return jnp.maximum(x, 0.0) + jnp.log(1.0 + jnp.exp(-jnp.abs(x)))


def _divisor(n, cap, mult):
    best = None
    d = mult
    while d <= min(n, cap):
        if n % d == 0:
            best = d
        d += mult
    return best if best is not None else n


def _rows(tr, c, j=0):
    return pl.BlockSpec((tr, c), lambda i, j=j: (i, j))


def _whole(shape):
    nd = len(shape)
    return pl.BlockSpec(shape, lambda *_: (0,) * nd)


def _mm(a, b, mode, name, out_dtype=f32, c=None, tm=512, tn=512, tk=2048):
    if mode == "nn":
        (M, K), (K2, N) = a.shape, b.shape
    elif mode == "nt":
        (M, K), (N, K2) = a.shape, b.shape
    else:
        (K, M), (K2, N) = a.shape, b.shape
    assert K == K2, (a.shape, b.shape, mode)
    tm = _divisor(M, tm, LANES if mode == "tn" else 16)
    tn = _divisor(N, tn, LANES)
    tk = _divisor(K, tk, LANES if mode != "tn" else 16)
    nk = K // tk
    if mode == "nn":
        a_spec = pl.BlockSpec((tm, tk), lambda i, j, k: (i, k))
        b_spec = pl.BlockSpec((tk, tn), lambda i, j, k: (k, j))
        dn = (((1,), (0,)), ((), ()))
    elif mode == "nt":
        a_spec = pl.BlockSpec((tm, tk), lambda i, j, k: (i, k))
        b_spec = pl.BlockSpec((tn, tk), lambda i, j, k: (j, k))
        dn = (((1,), (1,)), ((), ()))
    else:
        a_spec = pl.BlockSpec((tk, tm), lambda i, j, k: (k, i))
        b_spec = pl.BlockSpec((tk, tn), lambda i, j, k: (k, j))
        dn = (((0,), (0,)), ((), ()))
    o_spec = pl.BlockSpec((tm, tn), lambda i, j, k: (i, j))
    has_c = c is not None

    def body(*refs):
        a_ref, b_ref = refs[0], refs[1]
        c_ref = refs[2] if has_c else None
        o_ref = refs[3] if has_c else refs[2]
        part = lax.dot_general(_b(a_ref[...]), _b(b_ref[...]), dn, preferred_element_type=f32)
        if nk == 1:
            if has_c:
                part = part + c_ref[...]
            o_ref[...] = part.astype(o_ref.dtype)
        else:
            acc = refs[-1]
            k = pl.program_id(2)

            @pl.when(k == 0)
            def _():
                acc[...] = part

            @pl.when(k > 0)
            def _():
                acc[...] += part

            @pl.when(k == nk - 1)
            def _():
                r = acc[...]
                if has_c:
                    r = r + c_ref[...]
                o_ref[...] = r.astype(o_ref.dtype)

    ins = [a, b] + ([c] if has_c else [])
    in_specs = [a_spec, b_spec] + ([o_spec] if has_c else [])
    return pl.pallas_call(
        body, grid=(M // tm, N // tn, nk), in_specs=in_specs, out_specs=o_spec, out_shape=SDS((M, N), out_dtype),
        scratch_shapes=[pltpu.VMEM((tm, tn), f32)] if nk > 1 else [], name=name,
        compiler_params=_cp("parallel", "parallel", "arbitrary"),
    )(*ins)


def _ew(fn, ins, out_dtypes, name, tile_bytes=2 * 1024 * 1024):
    R, C = max((x.shape for x in ins), key=lambda s: s[0])
    n_in = len(ins)
    row_bytes = 4 * C * (len(ins) + len(out_dtypes))
    tr = _divisor(R, max(16, tile_bytes // row_bytes), 16)
    in_specs = [_rows(tr, C) if x.shape[0] == R else _whole((1, C)) for x in ins]

    def body(*refs):
        res = fn(*[r[...] for r in refs[:n_in]])
        for o_ref, r in zip(refs[n_in:], res):
            o_ref[...] = r.astype(o_ref.dtype)

    outs = pl.pallas_call(
        body, grid=(R // tr,), in_specs=in_specs, out_specs=[_rows(tr, C) for _ in out_dtypes],
        out_shape=[SDS((R, C), dt) for dt in out_dtypes], name=name, compiler_params=_cp("parallel"),
    )(*ins)
    return outs


def _rms_fwd(x, gain, name):
    T, Dm = x.shape
    tr = _divisor(T, 256, 16)

    def body(x_ref, g_ref, o_ref):
        xv = x_ref[...]
        r = lax.rsqrt(jnp.mean(xv * xv, axis=-1, keepdims=True) + EPS)
        o_ref[...] = (xv * r * g_ref[...]).astype(o_ref.dtype)

    return pl.pallas_call(
        body, grid=(T // tr,), in_specs=[_rows(tr, Dm), _whole((1, Dm))], out_specs=_rows(tr, Dm),
        out_shape=SDS((T, Dm), bf16), name=name, compiler_params=_cp("parallel"),
    )(x, gain)


def _rms_bwd(x, gain, dh, dres, name):
    T, Dm = x.shape
    tr = _divisor(T, 256, 16)

    def body(x_ref, g_ref, dh_ref, dres_ref, dx_ref, dg_ref):
        xv = x_ref[...]
        dhv = dh_ref[...]
        r = lax.rsqrt(jnp.mean(xv * xv, axis=-1, keepdims=True) + EPS)
        dy = dhv * g_ref[...]
        m = jnp.mean(dy * xv, axis=-1, keepdims=True)
        dx_ref[...] = dres_ref[...] + r * dy - xv * (r * r * r * m)
        part = jnp.sum(dhv * xv * r, axis=0, keepdims=True)

        @pl.when(pl.program_id(0) == 0)
        def _():
            dg_ref[...] = part

        @pl.when(pl.program_id(0) > 0)
        def _():
            dg_ref[...] += part

    return pl.pallas_call(
        body, grid=(T // tr,), in_specs=[_rows(tr, Dm), _whole((1, Dm)), _rows(tr, Dm), _rows(tr, Dm)],
        out_specs=[_rows(tr, Dm), _whole((1, Dm))], out_shape=[SDS((T, Dm), f32), SDS((1, Dm), f32)],
        name=name, compiler_params=_cp("arbitrary"),
    )(x, gain, dh, dres)


def _shift_down(cur, prev8, s):
    rolled = pltpu.roll(cur, s, 0)
    rp = pltpu.roll(prev8, s, 0)
    row8 = lax.broadcasted_iota(jnp.int32, prev8.shape, 0)
    first = jnp.where(row8 < s, rp, rolled[:8])
    return jnp.concatenate([first, rolled[8:]], axis=0)


def _shift_up(cur, next8, s):
    R = cur.shape[0]
    rolled = pltpu.roll(cur, R - s, 0)
    rn = pltpu.roll(next8, 8 - s, 0)
    row8 = lax.broadcasted_iota(jnp.int32, next8.shape, 0)
    last = jnp.where(row8 >= 8 - s, rn, rolled[R - 8:])
    return jnp.concatenate([rolled[: R - 8], last], axis=0)


def _prev8(ref, r0, ci):
    rows = ref[pl.ds(pl.multiple_of(jnp.maximum(r0 - 8, 0), 8), 8), :]
    return jnp.where(ci > 0, rows, 0.0)


def _next8(ref, r0, R, ci, n_chunks):
    start = jnp.minimum(r0 + R, (n_chunks - 1) * R + R - 8)
    rows = ref[pl.ds(pl.multiple_of(start, 8), 8), :]
    return jnp.where(ci < n_chunks - 1, rows, 0.0)


def _w_rows(w_ref, k):
    return [w_ref[j:j + 1, :] for j in range(k)]


def _causal_conv(cur, prev8, w, k):
    acc = cur * w[k - 1]
    for s in range(1, k):
        acc = acc + _shift_down(cur, prev8, s) * w[k - 1 - s]
    return acc


def _qkvconv_fwd(proj, convw, name):
    T = proj.shape[0]
    R = min(ROW_CHUNK, T)
    n_chunks = T // R
    n_blk = (2 * QK_W + V_W) // HEAD_DIM

    def body(x_ref, w_ref, o_ref):
        p = pl.program_id(0)
        is_qk = p < 2 * N_QK
        scale = jnp.where(p < N_QK, HEAD_DIM ** -0.5, 1.0).astype(f32)
        w = _w_rows(w_ref, 4)

        def chunk(ci, carry):
            r0 = pl.multiple_of(ci * R, R)
            cur = x_ref[pl.ds(r0, R), :]
            y = _silu(_causal_conv(cur, _prev8(x_ref, r0, ci), w, 4))
            ss = jnp.sum(y * y, axis=-1, keepdims=True)
            nrm = jnp.where(is_qk, lax.rsqrt(ss + EPS) * scale, 1.0)
            o_ref[pl.ds(r0, R), :] = y * nrm
            return carry

        lax.fori_loop(0, n_chunks, chunk, 0)

    return pl.pallas_call(
        body, grid=(n_blk,),
        in_specs=[pl.BlockSpec((T, HEAD_DIM), lambda p: (0, p)), pl.BlockSpec((4, HEAD_DIM), lambda p: (0, p))],
        out_specs=pl.BlockSpec((T, HEAD_DIM), lambda p: (0, p)), out_shape=SDS((T, n_blk * HEAD_DIM), f32),
        name=name, compiler_params=_cp("parallel"),
    )(proj, convw)


def _qkvconv_bwd(proj, convw, dqkv, name):
    T = proj.shape[0]
    R = min(ROW_CHUNK, T)
    n_chunks = T // R
    n_blk = (2 * QK_W + V_W) // HEAD_DIM

    def body(x_ref, w_ref, do_ref, dx_ref, dw_ref, dc_sc):
        p = pl.program_id(0)
        is_qk = p < 2 * N_QK
        scale = jnp.where(p < N_QK, HEAD_DIM ** -0.5, 1.0).astype(f32)
        w = _w_rows(w_ref, 4)

        def phase1(ci, dw):
            r0 = pl.multiple_of(ci * R, R)
            cur = x_ref[pl.ds(r0, R), :]
            p8 = _prev8(x_ref, r0, ci)
            shifted = [cur] + [_shift_down(cur, p8, s) for s in range(1, 4)]
            c = shifted[0] * w[3]
            for s in range(1, 4):
                c = c + shifted[s] * w[3 - s]
            y = _silu(c)
            dout = do_ref[pl.ds(r0, R), :]
            n = lax.rsqrt(jnp.sum(y * y, axis=-1, keepdims=True) + EPS)
            dot_ = jnp.sum(dout * y, axis=-1, keepdims=True)
            dy = jnp.where(is_qk, scale * (n * dout - y * (n * n * n * dot_)), dout)
            dc = dy * _dsilu(c)
            dc_sc[pl.ds(r0, R), :] = dc
            return tuple(dw[j] + jnp.sum(dc * shifted[3 - j], axis=0, keepdims=True) for j in range(4))

        dw = lax.fori_loop(0, n_chunks, phase1, tuple(jnp.zeros((1, HEAD_DIM), f32) for _ in range(4)))
        for j in range(4):
            dw_ref[j:j + 1, :] = dw[j]

        def phase2(ci, carry):
            r0 = pl.multiple_of(ci * R, R)
            cur = dc_sc[pl.ds(r0, R), :]
            n8 = _next8(dc_sc, r0, R, ci, n_chunks)
            acc = cur * w[3]
            for s in range(1, 4):
                acc = acc + _shift_up(cur, n8, s) * w[3 - s]
            dx_ref[pl.ds(r0, R), :] = acc.astype(dx_ref.dtype)
            return carry

        lax.fori_loop(0, n_chunks, phase2, 0)

    col = lambda p: (0, p)
    return pl.pallas_call(
        body, grid=(n_blk,),
        in_specs=[pl.BlockSpec((T, HEAD_DIM), col), pl.BlockSpec((4, HEAD_DIM), col), pl.BlockSpec((T, HEAD_DIM), col)],
        out_specs=[pl.BlockSpec((T, HEAD_DIM), col), pl.BlockSpec((4, HEAD_DIM), col)],
        out_shape=[SDS((T, n_blk * HEAD_DIM), bf16), SDS((4, n_blk * HEAD_DIM), f32)],
        scratch_shapes=[pltpu.VMEM((T, HEAD_DIM), f32)], name=name, compiler_params=_cp("parallel"),
    )(proj, convw, dqkv)


def _tri_masks(C):
    row = lax.broadcasted_iota(jnp.int32, (C, C), 0)
    col = lax.broadcasted_iota(jnp.int32, (C, C), 1)
    return row, col


def _lane_pick(blk, lane, idx):
    return jnp.sum(jnp.where(lane == idx, blk, 0.0), axis=1, keepdims=True)


def _gate_block(ba, ea, dtb, lane):
    sig = _sig(ba)
    gblk = -ea * _softplus(ba + dtb)
    return sig, gblk


def _head_decay(gam_all, rg_all, tot, lane, h, row, col):
    C = row.shape[0]
    gam_c = _lane_pick(gam_all, lane, N_V + h)
    rg_c = _lane_pick(rg_all, lane, N_V + h)
    lane1 = lax.broadcasted_iota(jnp.int32, (1, LANES), 1)
    tot_h = jnp.sum(jnp.where(lane1 == N_V + h, tot, 0.0), axis=1, keepdims=True)
    gcb = jnp.broadcast_to(gam_c, (C, C))
    dlt = gcb - gcb.T
    dm = jnp.where(row >= col, jnp.exp(jnp.minimum(dlt, 0.0)), 0.0)
    return gam_c, rg_c, tot_h, dm


def _delta_prep(qkv, ba, ea_row, dtb_row, name):
    T = qkv.shape[0]
    C = CHUNK_A
    N = T // C

    def body(q_ref, k_ref, v_ref, ba_ref, ea_ref, dtb_ref,
             qd_ref, kd_ref, u_ref, w_ref, attn_ref, tinv_ref, cd_ref, bg_ref, at_sc, t_sc):
        row, col = _tri_masks(C)
        lane = lax.broadcasted_iota(jnp.int32, (C, LANES), 1)
        sig, gblk = _gate_block(ba_ref[...], ea_ref[...], dtb_ref[...], lane)
        bg_ref[...] = jnp.where(lane < N_V, sig, gblk)
        lower = (row >= col).astype(f32)
        upper_s = (col > row).astype(f32)
        eye = (row == col).astype(f32)
        gam_all = _dot(lower, gblk, HI)
        rg_all = _dot(upper_s, gblk, HI)
        tot = jnp.sum(gblk, axis=0, keepdims=True)
        col1 = lax.broadcasted_iota(jnp.int32, (1, C), 1)
        for h in range(N_V):
            qk = pl.ds((h // 2) * HEAD_DIM, HEAD_DIM)
            hs = pl.ds(h * HEAD_DIM, HEAD_DIM)
            qh, kh, vh = q_ref[:, qk], k_ref[:, qk], v_ref[:, hs]
            beta_c = _lane_pick(sig, lane, h)
            gam_c, rg_c, tot_h, dm = _head_decay(gam_all, rg_all, tot, lane, h, row, col)
            kk = _dot_nt(_b(kh), _b(kh))
            a = jnp.where(row > col, beta_c * kk * dm, 0.0)
            at_sc[h] = a.T
            t_sc[h] = eye
            eg = jnp.exp(gam_c)
            attn_ref[0, h] = _dot_nt(_b(qh), _b(kh)) * dm
            qd_ref[:, hs] = qh * eg
            kd_ref[:, hs] = kh * jnp.exp(rg_c)
            cd_ref[0, h] = jnp.broadcast_to(jnp.exp(tot_h), (1, LANES))
            u_ref[:, hs] = vh * beta_c
            w_ref[:, hs] = kh * (beta_c * eg)

        def fsub(i, carry):
            for h in range(N_V):
                a_col = jnp.sum(jnp.where(col == i, at_sc[h], 0.0), axis=1, keepdims=True)
                new_row = jnp.sum(a_col * t_sc[h], axis=0, keepdims=True)
                t_sc[h, pl.ds(i, 1), :] = (col1 == i).astype(f32) - new_row
            return carry

        lax.fori_loop(1, C, fsub, 0)
        for h in range(N_V):
            hs = pl.ds(h * HEAD_DIM, HEAD_DIM)
            tinv = t_sc[h]
            tinv_ref[0, h] = tinv
            u_ref[:, hs] = _dot(tinv, u_ref[:, hs], HI)
            w_ref[:, hs] = _dot(tinv, w_ref[:, hs], HI)

    big = lambda n: (n, 0)
    return pl.pallas_call(
        body, grid=(N,),
        in_specs=[pl.BlockSpec((C, QK_W), lambda n: (n, 0)), pl.BlockSpec((C, QK_W), lambda n: (n, 1)),
                  pl.BlockSpec((C, V_W), lambda n: (n, 1)), pl.BlockSpec((C, LANES), big),
                  _whole((1, LANES)), _whole((1, LANES))],
        out_specs=[pl.BlockSpec((C, V_W), big)] * 4 + [
            pl.BlockSpec((1, N_V, C, C), lambda n: (n, 0, 0, 0)), pl.BlockSpec((1, N_V, C, C), lambda n: (n, 0, 0, 0)),
            pl.BlockSpec((1, N_V, 1, LANES), lambda n: (n, 0, 0, 0)), pl.BlockSpec((C, LANES), big)],
        out_shape=[SDS((T, V_W), f32)] * 4 + [SDS((N, N_V, C, C), f32), SDS((N, N_V, C, C), f32),
                                              SDS((N, N_V, 1, LANES), f32), SDS((T, LANES), f32)],
        scratch_shapes=[pltpu.VMEM((N_V, C, C), f32), pltpu.VMEM((N_V, C, C), f32)],
        name=name, compiler_params=_cp("parallel"),
    )(qkv, qkv, qkv, ba, ea_row, dtb_row)


def _delta_scan(qd, kd, u, w, attn, cd, name):
    T = qd.shape[0]
    C = CHUNK_A
    N = T // C

    def body(qd_ref, kd_ref, u_ref, w_ref, attn_ref, cd_ref, o_ref, s_ref, vn_ref, s_sc):
        @pl.when(pl.program_id(1) == 0)
        def _():
            s_sc[...] = jnp.zeros_like(s_sc)

        s = s_sc[...]
        s_ref[0, 0] = s
        sb = _b(s)
        vn = u_ref[...] - _dot(_b(w_ref[...]), sb)
        vn_ref[...] = vn
        o_ref[...] = _dot(_b(qd_ref[...]), sb) + _dot(_b(attn_ref[0, 0]), _b(vn))
        s_sc[...] = s * cd_ref[0, 0] + _dot_tn(_b(kd_ref[...]), _b(vn))

    blk = pl.BlockSpec((C, HEAD_DIM), lambda h, n: (n, h))
    return pl.pallas_call(
        body, grid=(N_V, N),
        in_specs=[blk, blk, blk, blk, pl.BlockSpec((1, 1, C, C), lambda h, n: (n, h, 0, 0)),
                  pl.BlockSpec((1, 1, 1, LANES), lambda h, n: (n, h, 0, 0))],
        out_specs=[blk, pl.BlockSpec((1, 1, HEAD_DIM, HEAD_DIM), lambda h, n: (n, h, 0, 0)), blk],
        out_shape=[SDS((T, V_W), f32), SDS((N, N_V, HEAD_DIM, HEAD_DIM), f32), SDS((T, V_W), f32)],
        scratch_shapes=[pltpu.VMEM((HEAD_DIM, HEAD_DIM), f32)], name=name, compiler_params=_cp("parallel", "arbitrary"),
    )(qd, kd, u, w, attn, cd)


def _delta_scan_bwd(do, qd, kd, w, attn, cd, s_all, vn, name):
    T = qd.shape[0]
    C = CHUNK_A
    N = T // C

    def body(do_ref, qd_ref, kd_ref, w_ref, attn_ref, cd_ref, s_ref, vn_ref,
             dqd_ref, dkd_ref, du_ref, dw_ref, dattn_ref, dcd_ref, ds_sc):
        @pl.when(pl.program_id(1) == 0)
        def _():
            ds_sc[...] = jnp.zeros_like(ds_sc)

        row, col = _tri_masks(C)
        dsn = ds_sc[...]
        s = s_ref[0, 0]
        dob, sb, dsb = _b(do_ref[...]), _b(s), _b(dsn)
        vnb = _b(vn_ref[...])
        dqd_ref[...] = _dot_nt(dob, sb)
        dattn_ref[0, 0] = jnp.where(row >= col, _dot_nt(dob, vnb), 0.0)
        dvn = _dot_tn(_b(attn_ref[0, 0]), dob) + _dot(_b(kd_ref[...]), dsb)
        dvnb = _b(dvn)
        dkd_ref[...] = _dot_nt(vnb, dsb)
        dcd = jnp.sum(jnp.sum(s * dsn, axis=1, keepdims=True), axis=0, keepdims=True)
        dcd_ref[0, 0] = jnp.broadcast_to(dcd, (1, LANES))
        du_ref[...] = dvn
        dw_ref[...] = -_dot_nt(dvnb, sb)
        ds_sc[...] = dsn * cd_ref[0, 0] + _dot_tn(_b(qd_ref[...]), dob) - _dot_tn(_b(w_ref[...]), dvnb)

    blk = pl.BlockSpec((C, HEAD_DIM), lambda h, n: (N - 1 - n, h))
    sq = pl.BlockSpec((1, 1, C, C), lambda h, n: (N - 1 - n, h, 0, 0))
    sc = pl.BlockSpec((1, 1, 1, LANES), lambda h, n: (N - 1 - n, h, 0, 0))
    st = pl.BlockSpec((1, 1, HEAD_DIM, HEAD_DIM), lambda h, n: (N - 1 - n, h, 0, 0))
    return pl.pallas_call(
        body, grid=(N_V, N), in_specs=[blk, blk, blk, blk, sq, sc, st, blk],
        out_specs=[blk, blk, blk, blk, sq, sc],
        out_shape=[SDS((T, V_W), f32)] * 4 + [SDS((N, N_V, C, C), f32), SDS((N, N_V, 1, LANES), f32)],
        scratch_shapes=[pltpu.VMEM((HEAD_DIM, HEAD_DIM), f32)], name=name, compiler_params=_cp("parallel", "arbitrary"),
    )(do, qd, kd, w, attn, cd, s_all, vn)


def _delta_prep_bwd(qkv, ba, ea_row, dtb_row, tinv, u, w, dqd, dkd, du, dw, dattn, dcd, name):
    T = qkv.shape[0]
    C = CHUNK_A
    N = T // C

    def body(q_ref, k_ref, v_ref, ba_ref, ea_ref, dtb_ref, tinv_ref, u_ref, w_ref,
             dqd_ref, dkd_ref, du_ref, dw_ref, dattn_ref, dcd_ref,
             dqkv_ref, dba_ref, dalog_ref, ddtb_ref):
        row, col = _tri_masks(C)
        lane = lax.broadcasted_iota(jnp.int32, (C, LANES), 1)
        rowc = lax.broadcasted_iota(jnp.int32, (C, 1), 0)
        ba_v, ea, dtb = ba_ref[...], ea_ref[...], dtb_ref[...]
        sig, gblk = _gate_block(ba_v, ea, dtb, lane)
        lower = (row >= col).astype(f32)
        upper_s = (col > row).astype(f32)
        upper = (col >= row).astype(f32)
        gam_all = _dot(lower, gblk, HI)
        rg_all = _dot(upper_s, gblk, HI)
        tot = jnp.sum(gblk, axis=0, keepdims=True)
        dbeta_blk = jnp.zeros((C, LANES), f32)
        dgam_blk = jnp.zeros((C, LANES), f32)
        for j in range(N_QK):
            qk = pl.ds(j * HEAD_DIM, HEAD_DIM)
            qh, kh = q_ref[:, qk], k_ref[:, qk]
            qhb, khb = _b(qh), _b(kh)
            kk = _dot_nt(khb, khb)
            qkm = _dot_nt(qhb, khb)
            dq_j = jnp.zeros((C, HEAD_DIM), f32)
            dk_j = jnp.zeros((C, HEAD_DIM), f32)
            for h in (2 * j, 2 * j + 1):
                hs = pl.ds(h * HEAD_DIM, HEAD_DIM)
                vh = v_ref[:, hs]
                beta_c = _lane_pick(sig, lane, h)
                gam_c, rg_c, tot_h, dm = _head_decay(gam_all, rg_all, tot, lane, h, row, col)
                eg, er, cdh = jnp.exp(gam_c), jnp.exp(rg_c), jnp.exp(tot_h)
                tinv_h = tinv_ref[0, h]
                d_rv = _dot_tn(tinv_h, du_ref[:, hs], HI)
                d_rk = _dot_tn(tinv_h, dw_ref[:, hs], HI)
                da = -jnp.where(row > col, _dot_nt(_b(d_rv), _b(u_ref[:, hs])) + _dot_nt(_b(d_rk), _b(w_ref[:, hs])), 0.0)
                dqkv_ref[:, pl.ds(2 * QK_W + h * HEAD_DIM, HEAD_DIM)] = beta_c * d_rv
                dbeta = jnp.sum(d_rv * vh + d_rk * (eg * kh), axis=1, keepdims=True)
                dk_h = (beta_c * eg) * d_rk
                d_eg = jnp.sum(d_rk * kh, axis=1, keepdims=True) * beta_c
                bkd = da * dm
                dbeta = dbeta + jnp.sum(bkd * kk, axis=1, keepdims=True)
                dkk = bkd * beta_c
                ddm = da * beta_c * kk
                dattn_h = dattn_ref[0, h]
                dqk = dattn_h * dm
                ddm = ddm + dattn_h * qkm
                dqd_h, dkd_h = dqd_ref[:, hs], dkd_ref[:, hs]
                dq_j = dq_j + _dot(_b(dqk), khb) + eg * dqd_h
                dk_h = dk_h + _dot_tn(_b(dqk), qhb) + _dot(_b(dkk + dkk.T), khb) + er * dkd_h
                dk_j = dk_j + dk_h
                d_eg = d_eg + jnp.sum(dqd_h * qh, axis=1, keepdims=True)
                d_er = jnp.sum(dkd_h * kh, axis=1, keepdims=True)
                e = ddm * dm
                dgam = jnp.sum(e, axis=1, keepdims=True) - jnp.sum(e.T, axis=1, keepdims=True)
                dgam = dgam + d_eg * eg - d_er * er
                extra = jnp.sum(d_er * er, axis=0, keepdims=True) + jnp.max(dcd_ref[0, h], axis=1, keepdims=True) * cdh
                dgam = dgam + jnp.where(rowc == C - 1, extra, 0.0)
                dbeta_blk = jnp.where(lane == h, dbeta, dbeta_blk)
                dgam_blk = jnp.where(lane == N_V + h, dgam, dgam_blk)
            dqkv_ref[:, qk] = dq_j
            dqkv_ref[:, pl.ds(QK_W + j * HEAD_DIM, HEAD_DIM)] = dk_j
        dg_all = _dot(upper, dgam_blk, HI)
        dsp = dg_all * (-ea) * _sig(ba_v + dtb)
        dba_ref[...] = jnp.where(lane < N_V, dbeta_blk * sig * (1.0 - sig), dsp)
        part_alog = jnp.sum(dg_all * gblk, axis=0, keepdims=True)
        part_dtb = jnp.sum(dsp, axis=0, keepdims=True)

        @pl.when(pl.program_id(0) == 0)
        def _():
            dalog_ref[...] = part_alog
            ddtb_ref[...] = part_dtb

        @pl.when(pl.program_id(0) > 0)
        def _():
            dalog_ref[...] += part_alog
            ddtb_ref[...] += part_dtb

    big = lambda n: (n, 0)
    wide = pl.BlockSpec((C, V_W), big)
    sq = pl.BlockSpec((1, N_V, C, C), lambda n: (n, 0, 0, 0))
    return pl.pallas_call(
        body, grid=(N,),
        in_specs=[pl.BlockSpec((C, QK_W), lambda n: (n, 0)), pl.BlockSpec((C, QK_W), lambda n: (n, 1)),
                  pl.BlockSpec((C, V_W), lambda n: (n, 1)), pl.BlockSpec((C, LANES), big),
                  _whole((1, LANES)), _whole((1, LANES)), sq, wide, wide, wide, wide, wide, wide, sq,
                  pl.BlockSpec((1, N_V, 1, LANES), lambda n: (n, 0, 0, 0))],
        out_specs=[pl.BlockSpec((C, 2 * QK_W + V_W), big), pl.BlockSpec((C, LANES), big),
                   _whole((1, LANES)), _whole((1, LANES))],
        out_shape=[SDS((T, 2 * QK_W + V_W), f32), SDS((T, LANES), f32), SDS((1, LANES), f32), SDS((1, LANES), f32)],
        name=name, compiler_params=_cp("arbitrary"),
    )(qkv, qkv, qkv, ba, ea_row, dtb_row, tinv, u, w, dqd, dkd, du, dw, dattn, dcd)


def _onorm_fwd(o, proj, hg, name):
    T = o.shape[0]
    tr = _divisor(T, 256, 16)

    def body(o_ref, z_ref, g_ref, out_ref):
        g = g_ref[...]
        for h in range(N_V):
            hs = pl.ds(h * HEAD_DIM, HEAD_DIM)
            oh = o_ref[:, hs]
            r = lax.rsqrt(jnp.mean(oh * oh, axis=-1, keepdims=True) + EPS)
            out_ref[:, hs] = (oh * r * g * _silu(z_ref[:, hs])).astype(out_ref.dtype)

    return pl.pallas_call(
        body, grid=(T // tr,), in_specs=[_rows(tr, V_W), _rows(tr, V_W, C_Z // V_W), _whole((1, HEAD_DIM))],
        out_specs=_rows(tr, V_W), out_shape=SDS((T, V_W), bf16), name=name, compiler_params=_cp("parallel"),
    )(o, proj, hg)


def _onorm_bwd(don, o, proj, hg, name):
    T = o.shape[0]
    tr = _divisor(T, 256, 16)

    def body(don_ref, o_ref, z_ref, g_ref, do_ref, dz_ref, dg_ref):
        g = g_ref[...]
        dg = jnp.zeros((1, HEAD_DIM), f32)
        for h in range(N_V):
            hs = pl.ds(h * HEAD_DIM, HEAD_DIM)
            oh, zh, dh = o_ref[:, hs], z_ref[:, hs], don_ref[:, hs]
            r = lax.rsqrt(jnp.mean(oh * oh, axis=-1, keepdims=True) + EPS)
            d_n = dh * _silu(zh)
            dz_ref[:, hs] = (dh * (oh * r * g) * _dsilu(zh)).astype(dz_ref.dtype)
            dy = d_n * g
            m = jnp.mean(dy * oh, axis=-1, keepdims=True)
            do_ref[:, hs] = r * dy - oh * (r * r * r * m)
            dg = dg + jnp.sum(d_n * oh * r, axis=0, keepdims=True)

        @pl.when(pl.program_id(0) == 0)
        def _():
            dg_ref[...] = dg

        @pl.when(pl.program_id(0) > 0)
        def _():
            dg_ref[...] += dg

    return pl.pallas_call(
        body, grid=(T // tr,),
        in_specs=[_rows(tr, V_W), _rows(tr, V_W), _rows(tr, V_W, C_Z // V_W), _whole((1, HEAD_DIM))],
        out_specs=[_rows(tr, V_W), _rows(tr, V_W), _whole((1, HEAD_DIM))],
        out_shape=[SDS((T, V_W), f32), SDS((T, V_W), bf16), SDS((1, HEAD_DIM), f32)],
        name=name, compiler_params=_cp("arbitrary"),
    )(don, o, proj, hg)


def _sgu_parts(ub, vb, gain):
    gv = _gelu(vb)
    r = lax.rsqrt(jnp.mean(gv * gv, axis=-1, keepdims=True) + EPS)
    return _gelu(ub), gv, r, gv * r * gain


def _sgu_fwd(proj, gain, w_s, b_bc, name):
    T = proj.shape[0]
    C = CHUNK_B

    def body(ub_ref, vb_ref, g_ref, w_ref, b_ref, o_ref):
        row, col = _tri_masks(C)
        u, _, _, vn = _sgu_parts(ub_ref[...], vb_ref[...], g_ref[...])
        for g in range(N_GROUPS):
            gs = pl.ds(g * GROUP_DIM, GROUP_DIM)
            wg = jnp.where(row >= col, w_ref[g], 0.0)
            mixed = _dot(_b(wg), _b(vn[:, g * GROUP_DIM:(g + 1) * GROUP_DIM])) + b_ref[g]
            o_ref[:, gs] = (u[:, g * GROUP_DIM:(g + 1) * GROUP_DIM] * mixed).astype(o_ref.dtype)

    return pl.pallas_call(
        body, grid=(T // C,),
        in_specs=[_rows(C, WIDTH_B, C_UB // WIDTH_B), _rows(C, WIDTH_B, C_VB // WIDTH_B), _whole((1, WIDTH_B)),
                  _whole((N_GROUPS, C, C)), _whole((N_GROUPS, C, GROUP_DIM))],
        out_specs=_rows(C, WIDTH_B), out_shape=SDS((T, WIDTH_B), bf16), name=name, compiler_params=_cp("parallel"),
    )(proj, proj, gain, w_s, b_bc)


def _sgu_bwd(dsgu, proj, gain, w_s, b_bc, name):
    T = proj.shape[0]
    C = CHUNK_B

    def body(d_ref, ub_ref, vb_ref, g_ref, w_ref, b_ref, dub_ref, dvb_ref, dw_ref, db_ref, dg_ref):
        first = pl.program_id(0) == 0
        row, col = _tri_masks(C)
        ub, vb, gain_v = ub_ref[...], vb_ref[...], g_ref[...]
        u, gv, r, vn = _sgu_parts(ub, vb, gain_v)
        d = d_ref[...]
        dvn_parts = []
        for g in range(N_GROUPS):
            sl = slice(g * GROUP_DIM, (g + 1) * GROUP_DIM)
            wg = jnp.where(row >= col, w_ref[g], 0.0)
            vng = _b(vn[:, sl])
            mixed = _dot(_b(wg), vng) + b_ref[g]
            dub_ref[:, pl.ds(g * GROUP_DIM, GROUP_DIM)] = (d[:, sl] * mixed * _dgelu(ub[:, sl])).astype(dub_ref.dtype)
            dmix = d[:, sl] * u[:, sl]
            dmb = _b(dmix)
            dwg = jnp.where(row >= col, _dot_nt(dmb, vng), 0.0)
            dbg = jnp.sum(dmix, axis=1, keepdims=True)

            @pl.when(first)
            def _():
                dw_ref[g] = dwg
                db_ref[g] = dbg

            @pl.when(jnp.logical_not(first))
            def _():
                dw_ref[g] += dwg
                db_ref[g] += dbg

            dvn_parts.append(_dot_tn(_b(wg), dmb))
        dvn = jnp.concatenate(dvn_parts, axis=1)
        dy = dvn * gain_v
        m = jnp.mean(dy * gv, axis=-1, keepdims=True)
        dgv = r * dy - gv * (r * r * r * m)
        dvb_ref[...] = (dgv * _dgelu(vb)).astype(dvb_ref.dtype)
        dgain = jnp.sum(dvn * gv * r, axis=0, keepdims=True)

        @pl.when(first)
        def _():
            dg_ref[...] = dgain

        @pl.when(jnp.logical_not(first))
        def _():
            dg_ref[...] += dgain

    return pl.pallas_call(
        body, grid=(T // C,),
        in_specs=[_rows(C, WIDTH_B), _rows(C, WIDTH_B, C_UB // WIDTH_B), _rows(C, WIDTH_B, C_VB // WIDTH_B),
                  _whole((1, WIDTH_B)), _whole((N_GROUPS, C, C)), _whole((N_GROUPS, C, GROUP_DIM))],
        out_specs=[_rows(C, WIDTH_B), _rows(C, WIDTH_B), _whole((N_GROUPS, C, C)), _whole((N_GROUPS, C, 1)),
                   _whole((1, WIDTH_B))],
        out_shape=[SDS((T, WIDTH_B), bf16), SDS((T, WIDTH_B), bf16), SDS((N_GROUPS, C, C), f32),
                   SDS((N_GROUPS, C, 1), f32), SDS((1, WIDTH_B), f32)],
        name=name, compiler_params=_cp("arbitrary"),
    )(dsgu, proj, proj, gain, w_s, b_bc)


def _merge_fwd(proj, ya, yb, name):
    T = proj.shape[0]
    tr = _divisor(T, 256, 16)

    def body(ga_ref, gb_ref, ya_ref, yb_ref, o_ref):
        o_ref[...] = (_sig(ga_ref[...]) * ya_ref[...] + _sig(gb_ref[...]) * yb_ref[...]).astype(o_ref.dtype)

    return pl.pallas_call(
        body, grid=(T // tr,),
        in_specs=[_rows(tr, D_MODEL, C_GA // D_MODEL), _rows(tr, D_MODEL, C_GB // D_MODEL), _rows(tr, D_MODEL), _rows(tr, D_MODEL)],
        out_specs=_rows(tr, D_MODEL), out_shape=SDS((T, D_MODEL), bf16), name=name, compiler_params=_cp("parallel"),
    )(proj, proj, ya, yb)


def _merge_bwd(dm, proj, ya, yb, name):
    T = proj.shape[0]
    tr = _divisor(T, 256, 16)

    def body(dm_ref, ga_ref, gb_ref, ya_ref, yb_ref, dya_ref, dyb_ref, dga_ref, dgb_ref):
        d = dm_ref[...]
        sa, sb = _sig(ga_ref[...]), _sig(gb_ref[...])
        dya_ref[...] = (d * sa).astype(bf16)
        dyb_ref[...] = (d * sb).astype(bf16)
        dga_ref[...] = (d * ya_ref[...] * sa * (1.0 - sa)).astype(bf16)
        dgb_ref[...] = (d * yb_ref[...] * sb * (1.0 - sb)).astype(bf16)

    return pl.pallas_call(
        body, grid=(T // tr,),
        in_specs=[_rows(tr, D_MODEL), _rows(tr, D_MODEL, C_GA // D_MODEL), _rows(tr, D_MODEL, C_GB // D_MODEL),
                  _rows(tr, D_MODEL), _rows(tr, D_MODEL)],
        out_specs=[_rows(tr, D_MODEL)] * 4, out_shape=[SDS((T, D_MODEL), bf16)] * 4, name=name,
        compiler_params=_cp("parallel"),
    )(dm, proj, proj, ya, yb)


def _ffnconv_fwd(upg, upv, wg, wv, bg, bv, name):
    T, F = upg.shape
    R = min(ROW_CHUNK, T)
    n_chunks = T // R

    def body(g_ref, v_ref, wg_ref, wv_ref, bg_ref, bv_ref, o_ref):
        wgv, wvv = _w_rows(wg_ref, 3), _w_rows(wv_ref, 3)

        def chunk(ci, carry):
            r0 = pl.multiple_of(ci * R, R)
            cg = _causal_conv(g_ref[pl.ds(r0, R), :], _prev8(g_ref, r0, ci), wgv, 3) + bg_ref[...]
            cv = _causal_conv(v_ref[pl.ds(r0, R), :], _prev8(v_ref, r0, ci), wvv, 3) + bv_ref[...]
            o_ref[pl.ds(r0, R), :] = (_silu(cg) * cv).astype(o_ref.dtype)
            return carry

        lax.fori_loop(0, n_chunks, chunk, 0)

    col = lambda p: (0, p)
    return pl.pallas_call(
        body, grid=(F // LANES,),
        in_specs=[pl.BlockSpec((T, LANES), col)] * 2 + [pl.BlockSpec((3, LANES), col)] * 2 + [pl.BlockSpec((1, LANES), col)] * 2,
        out_specs=pl.BlockSpec((T, LANES), col), out_shape=SDS((T, F), bf16), name=name, compiler_params=_cp("parallel"),
    )(upg, upv, wg, wv, bg, bv)


def _ffnconv_bwd(dact, upg, upv, wg, wv, bg, bv, name):
    T, F = upg.shape
    R = min(ROW_CHUNK, T)
    n_chunks = T // R

    def body(d_ref, g_ref, v_ref, wg_ref, wv_ref, bg_ref, bv_ref,
             dg_ref, dv_ref, dwg_ref, dwv_ref, dbg_ref, dbv_ref, sg, sv):
        wgv, wvv = _w_rows(wg_ref, 3), _w_rows(wv_ref, 3)

        def phase1(ci, carry):
            dwg, dwv = carry
            r0 = pl.multiple_of(ci * R, R)
            gcur, vcur = g_ref[pl.ds(r0, R), :], v_ref[pl.ds(r0, R), :]
            gp, vp = _prev8(g_ref, r0, ci), _prev8(v_ref, r0, ci)
            gsh = [gcur] + [_shift_down(gcur, gp, s) for s in (1, 2)]
            vsh = [vcur] + [_shift_down(vcur, vp, s) for s in (1, 2)]
            cg = gsh[0] * wgv[2] + gsh[1] * wgv[1] + gsh[2] * wgv[0] + bg_ref[...]
            cv = vsh[0] * wvv[2] + vsh[1] * wvv[1] + vsh[2] * wvv[0] + bv_ref[...]
            d = d_ref[pl.ds(r0, R), :]
            dcv = d * _silu(cg)
            dcg = d * cv * _dsilu(cg)
            sg[pl.ds(r0, R), :] = dcg
            sv[pl.ds(r0, R), :] = dcv
            rg = [jnp.sum(dcg * gsh[2 - j], axis=0, keepdims=True) for j in range(3)] + [jnp.sum(dcg, axis=0, keepdims=True)]
            rv = [jnp.sum(dcv * vsh[2 - j], axis=0, keepdims=True) for j in range(3)] + [jnp.sum(dcv, axis=0, keepdims=True)]
            return tuple(a + b for a, b in zip(dwg, rg)), tuple(a + b for a, b in zip(dwv, rv))

        z4 = tuple(jnp.zeros((1, LANES), f32) for _ in range(4))
        dwg, dwv = lax.fori_loop(0, n_chunks, phase1, (z4, z4))
        for j in range(3):
            dwg_ref[j:j + 1, :] = dwg[j]
            dwv_ref[j:j + 1, :] = dwv[j]
        dbg_ref[...] = dwg[3]
        dbv_ref[...] = dwv[3]

        def phase2(ci, carry):
            r0 = pl.multiple_of(ci * R, R)
            for sc, wv_, out in ((sg, wgv, dg_ref), (sv, wvv, dv_ref)):
                cur = sc[pl.ds(r0, R), :]
                n8 = _next8(sc, r0, R, ci, n_chunks)
                acc = cur * wv_[2] + _shift_up(cur, n8, 1) * wv_[1] + _shift_up(cur, n8, 2) * wv_[0]
                out[pl.ds(r0, R), :] = acc.astype(out.dtype)
            return carry

        lax.fori_loop(0, n_chunks, phase2, 0)

    col = lambda p: (0, p)
    big, w3, b1 = pl.BlockSpec((T, LANES), col), pl.BlockSpec((3, LANES), col), pl.BlockSpec((1, LANES), col)
    return pl.pallas_call(
        body, grid=(F // LANES,), in_specs=[big, big, big, w3, w3, b1, b1], out_specs=[big, big, w3, w3, b1, b1],
        out_shape=[SDS((T, F), bf16), SDS((T, F), bf16), SDS((3, F), f32), SDS((3, F), f32), SDS((1, F), f32), SDS((1, F), f32)],
        scratch_shapes=[pltpu.VMEM((T, LANES), f32), pltpu.VMEM((T, LANES), f32)], name=name, compiler_params=_cp("parallel"),
    )(dact, upg, upv, wg, wv, bg, bv)


def _loss_head(x, gain, target, name):
    T, Dm = x.shape
    tr = _divisor(T, 256, 16)

    def body(x_ref, g_ref, t_ref, l_ref, dx_ref, dg_ref):
        xv, g = x_ref[...], g_ref[...]
        r = lax.rsqrt(jnp.mean(xv * xv, axis=-1, keepdims=True) + EPS)
        err = xv * r * g - t_ref[...]
        part_l = 0.5 * jnp.sum(jnp.mean(err * err, axis=-1, keepdims=True), axis=0, keepdims=True)
        dy = err * (1.0 / Dm)
        dyg = dy * g
        m = jnp.mean(dyg * xv, axis=-1, keepdims=True)
        dx_ref[...] = r * dyg - xv * (r * r * r * m)
        part_g = jnp.sum(dy * xv * r, axis=0, keepdims=True)
        part_l = jnp.broadcast_to(part_l, (1, LANES))

        @pl.when(pl.program_id(0) == 0)
        def _():
            l_ref[...] = part_l
            dg_ref[...] = part_g

        @pl.when(pl.program_id(0) > 0)
        def _():
            l_ref[...] += part_l
            dg_ref[...] += part_g

    return pl.pallas_call(
        body, grid=(T // tr,), in_specs=[_rows(tr, Dm), _whole((1, Dm)), _rows(tr, Dm)],
        out_specs=[_whole((1, LANES)), _rows(tr, Dm), _whole((1, Dm))],
        out_shape=[SDS((1, LANES), f32), SDS((T, Dm), f32), SDS((1, Dm), f32)], name=name, compiler_params=_cp("arbitrary"),
    )(x, gain, target)


def _lane_row(vec, offset):
    return jnp.pad(vec.astype(f32), (offset, LANES - offset - vec.shape[0]))[None]


def _layer_fwd(x, p_i, W, S, li):
    nm = lambda s: f"{s}_l{li}"
    sv = {"x0": x}
    h1 = _rms_fwd(x, S["norm_mix"], nm("rms_mix"))
    proj = _mm(h1, W["in_main"], "nn", nm("proj_main"))
    ba = _mm(h1, W["in_ba"], "nn", nm("proj_ba"))
    qkv = _qkvconv_fwd(proj, S["conv_qkv"], nm("qkvconv"))
    qd, kd, u, w, attn, tinv, cd, _ = _delta_prep(qkv, ba, S["ea_row"], S["dtb_row"], nm("delta_prep"))
    o, s_all, vn = _delta_scan(qd, kd, u, w, attn, cd, nm("delta_scan"))
    on = _onorm_fwd(o, proj, S["head_norm"], nm("onorm"))
    ya = _mm(on, W["branch_a"], "nn", nm("branch_a"))
    sgu = _sgu_fwd(proj, S["sgu_norm"], S["w_spatial"], S["b_bc"], nm("sgu"))
    yb = _mm(sgu, W["branch_b"], "nn", nm("branch_b"))
    merged = _merge_fwd(proj, ya, yb, nm("merge"))
    x1 = _mm(merged, W["out"], "nn", nm("out_proj"), c=x)
    h2 = _rms_fwd(x1, S["norm_ffn"], nm("rms_ffn"))
    upg = _mm(h2, W["up_g"], "nn", nm("ffn_up_g"))
    upv = _mm(h2, W["up_v"], "nn", nm("ffn_up_v"))
    act = _ffnconv_fwd(upg, upv, S["conv_g"], S["conv_v"], S["bias_g"], S["bias_v"], nm("ffnconv"))
    x2 = _mm(act, W["down"], "nn", nm("ffn_down"), c=x1, tk=2816)
    h3 = _rms_fwd(x2, S["norm_ple"], nm("rms_ple"))
    gl = _mm(h3, W["ple_gate"], "nn", nm("ple_gate"))
    pp = _mm(p_i, W["ple_proj"], "nn", nm("ple_proj"))
    (x3,) = _ew(lambda a, g, q: (a + _sig(g) * q,), [x2, gl, pp], [f32], nm("ple_mix"))
    sv.update(h1=h1, proj=proj, ba=ba, qkv=qkv, qd=qd, kd=kd, u=u, w=w, attn=attn, tinv=tinv, cd=cd, o=o, s_all=s_all,
              vn=vn, on=on, ya=ya, sgu=sgu, yb=yb, merged=merged, x1=x1, h2=h2, upg=upg, upv=upv, act=act, x2=x2,
              h3=h3, gl=gl, pp=pp, p=p_i)
    return x3, sv


def _layer_bwd(dx3, sv, W, S, li):
    nm = lambda s: f"{s}_l{li}"
    G = {}
    dgl, dpp = _ew(lambda d, g, q: ((lambda s: (d * q * s * (1.0 - s), d * s))(_sig(g))),
                   [dx3, sv["gl"], sv["pp"]], [bf16, bf16], nm("ple_mix_bwd"))
    G["w_ple_gate"] = _mm(sv["h3"], dgl, "tn", nm("d_ple_gate"))
    G["w_ple_proj"] = _mm(sv["p"], dpp, "tn", nm("d_ple_proj"))
    dh3 = _mm(dgl, W["ple_gate"], "nt", nm("dh_ple"))
    dx2, G["norm_ple"] = _rms_bwd(sv["x2"], S["norm_ple"], dh3, dx3, nm("rms_ple_bwd"))
    dact = _mm(dx2, W["down"], "nt", nm("d_act"))
    G["w_ffn_down"] = _mm(sv["act"], dx2, "tn", nm("d_ffn_down"))
    dupg, dupv, dcg, dcv, dbg, dbv = _ffnconv_bwd(dact, sv["upg"], sv["upv"], S["conv_g"], S["conv_v"], S["bias_g"],
                                                  S["bias_v"], nm("ffnconv_bwd"))
    G["conv_ffn"] = jnp.concatenate([dcg, dcv], axis=1)
    G["b_conv_ffn"] = jnp.concatenate([dbg, dbv], axis=1)
    G["w_ffn_up"] = jnp.concatenate([_mm(sv["h2"], dupg, "tn", nm("d_ffn_up_g")), _mm(sv["h2"], dupv, "tn", nm("d_ffn_up_v"))], axis=1)
    dh2 = _mm(dupg, W["up_g"], "nt", nm("dh_ffn_g"), tk=2816)
    dh2 = _mm(dupv, W["up_v"], "nt", nm("dh_ffn_v"), c=dh2, tk=2816)
    dx1, G["norm_ffn"] = _rms_bwd(sv["x1"], S["norm_ffn"], dh2, dx2, nm("rms_ffn_bwd"))
    dmerged = _mm(dx1, W["out"], "nt", nm("d_merged"))
    G["w_out"] = _mm(sv["merged"], dx1, "tn", nm("d_w_out"))
    dya, dyb, dga, dgb = _merge_bwd(dmerged, sv["proj"], sv["ya"], sv["yb"], nm("merge_bwd"))
    G["w_branch_a"] = _mm(sv["on"], dya, "tn", nm("d_branch_a"))
    G["w_branch_b"] = _mm(sv["sgu"], dyb, "tn", nm("d_branch_b"))
    don = _mm(dya, W["branch_a"], "nt", nm("d_on"))
    dsgu = _mm(dyb, W["branch_b"], "nt", nm("d_sgu"))
    dub, dvb, G["w_spatial"], db_s, G["sgu_norm"] = _sgu_bwd(dsgu, sv["proj"], S["sgu_norm"], S["w_spatial"], S["b_bc"], nm("sgu_bwd"))
    G["b_spatial"] = db_s.reshape(N_GROUPS, CHUNK_B)
    do, dz, G["head_norm"] = _onorm_bwd(don, sv["o"], sv["proj"], S["head_norm"], nm("onorm_bwd"))
    dqd, dkd, du, dw, dattn, dcd = _delta_scan_bwd(do, sv["qd"], sv["kd"], sv["w"], sv["attn"], sv["cd"], sv["s_all"], sv["vn"],
                                                   nm("delta_scan_bwd"))
    dqkv_n, dba, dalog_row, ddtb_row = _delta_prep_bwd(sv["qkv"], sv["ba"], S["ea_row"], S["dtb_row"], sv["tinv"], sv["u"], sv["w"],
                                                       dqd, dkd, du, dw, dattn, dcd, nm("delta_prep_bwd"))
    G["a_log"] = dalog_row[0, N_V:2 * N_V]
    G["dt_bias"] = ddtb_row[0, N_V:2 * N_V]
    dqkv, G["conv_qkv"] = _qkvconv_bwd(sv["proj"], S["conv_qkv"], dqkv_n, nm("qkvconv_bwd"))
    dproj = jnp.concatenate([dqkv, dz, dub, dvb, dga, dgb], axis=1)
    G["in_main"] = _mm(sv["h1"], dproj, "tn", nm("d_in_main"))
    G["in_ba"] = _mm(sv["h1"], dba, "tn", nm("d_in_ba"))
    dh1 = _mm(dba, W["in_ba"], "nt", nm("dh_mix_ba"))
    dh1 = _mm(dproj, W["in_main"], "nt", nm("dh_mix"), c=dh1)
    dx0, G["norm_mix"] = _rms_bwd(sv["x0"], S["norm_mix"], dh1, dx1, nm("rms_mix_bwd"))
    return dx0, G


_BA0 = C_UB
_BA1 = C_UB + 2 * N_V


def _big_weights(full):
    w_in = full["w_in"]
    ba = jnp.pad(w_in[:, _BA0:_BA1], ((0, 0), (0, LANES - 2 * N_V)))
    return dict(
        in_main=jnp.concatenate([w_in[:, :_BA0], w_in[:, _BA1:]], axis=1), in_ba=ba,
        branch_a=full["w_branch_a"], branch_b=full["w_branch_b"], out=full["w_out"],
        up_g=full["w_ffn_up"][:, :D_FF], up_v=full["w_ffn_up"][:, D_FF:], down=full["w_ffn_down"],
        ple_gate=full["w_ple_gate"], ple_proj=full["w_ple_proj"])


def _small_params(sm, i):
    return dict(
        norm_mix=sm["norm_mix"][i][None], conv_qkv=sm["conv_qkv"][i], ea_row=_lane_row(jnp.exp(sm["a_log"][i]), N_V),
        dtb_row=_lane_row(sm["dt_bias"][i], N_V), head_norm=sm["head_norm"][i][None], sgu_norm=sm["sgu_norm"][i][None],
        w_spatial=sm["w_spatial"][i],
        b_bc=jnp.broadcast_to(sm["b_spatial"][i][:, :, None], (N_GROUPS, CHUNK_B, GROUP_DIM)),
        norm_ffn=sm["norm_ffn"][i][None], conv_g=sm["conv_ffn"][i][:, :D_FF], conv_v=sm["conv_ffn"][i][:, D_FF:],
        bias_g=sm["b_conv_ffn"][i][None, :D_FF], bias_v=sm["b_conv_ffn"][i][None, D_FF:], norm_ple=sm["norm_ple"][i][None])


def _w_in_grad(G):
    return jnp.concatenate([G["in_main"][:, :_BA0], G["in_ba"][:, :2 * N_V], G["in_main"][:, _BA0:]], axis=1)


MESH = pl.DeviceIdType.MESH
N_CHIPS = 4
ANY = pl.BlockSpec(memory_space=pl.ANY)
ROW_ALIGN = 32


def _place():
    x, y, c = lax.axis_index("x"), lax.axis_index("y"), lax.axis_index("c")
    chips = [(1 - x, y), (x, 1 - y), (1 - x, 1 - y)]
    return x, y, c, 2 * x + y, chips, (x, y, 1 - c)


def _halves(rows, c):
    h = rows // 2
    return pl.ds(pl.multiple_of(c * h, 16), h), pl.ds(pl.multiple_of((1 - c) * h, 16), h)


def _remote(src, dst, send_sems, recv_sems, k, dev):
    return pltpu.make_async_remote_copy(src_ref=src, dst_ref=dst, send_sem=send_sems.at[k], recv_sem=recv_sems.at[k],
                                        device_id=dev, device_id_type=MESH)


def _gather_shards(bufs, name):
    n = len(bufs)

    def body(*refs):
        ins, outs = refs[:n], refs[n:2 * n]
        send_sems, recv_sems, loc_sems = refs[2 * n:]
        x, y, c, me, chips, sib = _place()
        for i in range(n):
            mine, _ = _halves(ins[i].shape[0], c)
            pltpu.make_async_copy(ins[i], outs[i].at[me], loc_sems.at[i]).start()
            for k, (px, py) in enumerate(chips):
                _remote(ins[i].at[mine], outs[i].at[me, mine], send_sems, recv_sems, 6 * i + k, (px, py, c)).start()
        for i in range(n):
            mine, _ = _halves(ins[i].shape[0], c)
            for k, (px, py) in enumerate(chips):
                pc = 2 * px + py
                _remote(ins[i].at[mine], outs[i].at[pc, mine], send_sems, recv_sems, 6 * i + k, (px, py, c)).wait_recv()
                _remote(outs[i].at[pc, mine], outs[i].at[pc, mine], send_sems, recv_sems, 6 * i + 3 + k, sib).start()
        for i in range(n):
            mine, other = _halves(ins[i].shape[0], c)
            for k, (px, py) in enumerate(chips):
                pc = 2 * px + py
                _remote(outs[i].at[pc, mine], outs[i].at[pc, other], send_sems, recv_sems, 6 * i + 3 + k, sib).wait_recv()
        for i in range(n):
            mine, _ = _halves(ins[i].shape[0], c)
            for k, (px, py) in enumerate(chips):
                pc = 2 * px + py
                _remote(ins[i].at[mine], outs[i].at[me, mine], send_sems, recv_sems, 6 * i + k, (px, py, c)).wait_send()
                _remote(outs[i].at[pc, mine], outs[i].at[pc, mine], send_sems, recv_sems, 6 * i + 3 + k, sib).wait_send()
            pltpu.make_async_copy(ins[i], outs[i].at[me], loc_sems.at[i]).wait()

    return pl.pallas_call(
        body, in_specs=[ANY] * n, out_specs=[ANY] * n,
        out_shape=[SDS((N_CHIPS,) + b.shape, b.dtype) for b in bufs],
        scratch_shapes=[pltpu.SemaphoreType.DMA((6 * n,)), pltpu.SemaphoreType.DMA((6 * n,)), pltpu.SemaphoreType.DMA((n,))],
        name=name,
    )(*bufs)


def _rs_pair(g, name):
    _, R, L = g.shape
    h = R // 2

    def body(g_ref, mine_ref, got_ref, send_sems, recv_sems, loc_sem):
        x, y, c, me, chips, sib = _place()
        mine, other = _halves(R, c)
        lc = pltpu.make_async_copy(g_ref.at[:, mine], mine_ref, loc_sem)
        lc.start()
        cp = _remote(g_ref.at[:, other], got_ref, send_sems, recv_sems, 0, sib)
        cp.start()
        cp.wait()
        lc.wait()

    return pl.pallas_call(
        body, in_specs=[ANY], out_specs=[ANY, ANY], out_shape=[SDS((N_CHIPS, h, L), g.dtype)] * 2,
        scratch_shapes=[pltpu.SemaphoreType.DMA((1,)), pltpu.SemaphoreType.DMA((1,)), pltpu.SemaphoreType.DMA],
        name=name,
    )(g)


def _rs_chips(p, name):
    _, h, L = p.shape

    def body(p_ref, own_ref, got_ref, send_sems, recv_sems, loc_sem):
        x, y, c, me, chips, sib = _place()
        lc = pltpu.make_async_copy(p_ref.at[me], own_ref, loc_sem)
        lc.start()
        cps = []
        for k, (px, py) in enumerate(chips):
            cp = _remote(p_ref.at[2 * px + py], got_ref.at[k], send_sems, recv_sems, k, (px, py, c))
            cp.start()
            cps.append(cp)
        for cp in cps:
            cp.wait()
        lc.wait()

    return pl.pallas_call(
        body, in_specs=[ANY], out_specs=[ANY, ANY], out_shape=[SDS((h, L), p.dtype), SDS((3, h, L), p.dtype)],
        scratch_shapes=[pltpu.SemaphoreType.DMA((3,)), pltpu.SemaphoreType.DMA((3,)), pltpu.SemaphoreType.DMA],
        name=name,
    )(p)


def _rs_join(q, name):
    h, L = q.shape

    def body(q_ref, f_ref, send_sems, recv_sems, loc_sem):
        x, y, c, me, chips, sib = _place()
        mine, _ = _halves(2 * h, c)
        lc = pltpu.make_async_copy(q_ref, f_ref.at[mine], loc_sem)
        lc.start()
        cp = _remote(q_ref, f_ref.at[mine], send_sems, recv_sems, 0, sib)
        cp.start()
        cp.wait()
        lc.wait()

    return pl.pallas_call(
        body, in_specs=[ANY], out_specs=ANY, out_shape=SDS((2 * h, L), q.dtype),
        scratch_shapes=[pltpu.SemaphoreType.DMA((1,)), pltpu.SemaphoreType.DMA((1,)), pltpu.SemaphoreType.DMA],
        name=name,
    )(q)


def _reduce_scatter(g, tag):
    mine, got = _rs_pair(g, f"rs_pair_{tag}")
    _, h, L = mine.shape
    (p,) = _ew(lambda a, b: (a + b,), [mine.reshape(N_CHIPS * h, L), got.reshape(N_CHIPS * h, L)], [f32], f"rs_add2_{tag}")
    own, parts = _rs_chips(p.reshape(N_CHIPS, h, L), f"rs_chips_{tag}")
    (q,) = _ew(lambda a, b, c, d: ((a + c) + (b + d),), [own, parts[0], parts[1], parts[2]], [f32], f"rs_add4_{tag}")
    return _rs_join(q, f"rs_join_{tag}")


def _allreduce_small(buf, name):
    R, L = buf.shape

    def body(x_ref, o_ref, r0, s1, r1, send_sems, recv_sems):
        x, y, c, me, chips, sib = _place()
        cp = _remote(x_ref, r0, send_sems, recv_sems, 0, sib)
        cp.start()
        cp.wait()
        s1[...] = x_ref[...] + r0[...]
        cps = []
        for k, (px, py) in enumerate(chips):
            cp = _remote(s1, r1.at[k], send_sems, recv_sems, 1 + k, (px, py, c))
            cp.start()
            cps.append(cp)
        for cp in cps:
            cp.wait()
        o_ref[...] = (s1[...] + r1[1]) + (r1[0] + r1[2])

    vm = pl.BlockSpec(memory_space=pltpu.VMEM)
    return pl.pallas_call(
        body, in_specs=[vm], out_specs=vm, out_shape=SDS((R, L), f32),
        scratch_shapes=[pltpu.VMEM((R, L), f32), pltpu.VMEM((R, L), f32), pltpu.VMEM((3, R, L), f32),
                        pltpu.SemaphoreType.DMA((4,)), pltpu.SemaphoreType.DMA((4,))],
        name=name, compiler_params=pltpu.CompilerParams(vmem_limit_bytes=VMEM_LIMIT_BYTES),
    )(buf)


BIG = ("w_in", "w_branch_a", "w_branch_b", "w_out", "w_ffn_up", "w_ffn_down", "w_ple_gate", "w_ple_proj")
COL_SHARDED = ("w_in", "w_branch_b", "w_ffn_up", "w_ple_proj")
SMALL_REPL = ("norm_mix", "a_log", "dt_bias", "head_norm", "sgu_norm", "w_spatial", "b_spatial", "norm_ffn", "b_conv_ffn",
              "norm_ple", "norm_final")
SMALL_COLS = ("conv_qkv", "conv_ffn")
WEIGHTS = ("norm_mix", "w_in", "conv_qkv", "a_log", "dt_bias", "head_norm", "sgu_norm", "w_spatial", "b_spatial", "w_branch_a",
           "w_branch_b", "w_out", "norm_ffn", "w_ffn_up", "conv_ffn", "b_conv_ffn", "w_ffn_down", "norm_ple", "w_ple_gate",
           "w_ple_proj", "norm_final")


def _flat(arrs, dtype, lead=()):
    parts = [a.astype(dtype).reshape(lead + (-1,)) for a in arrs]
    cat = jnp.concatenate(parts, axis=-1) if len(parts) > 1 else parts[0]
    n = cat.shape[-1]
    unit = LANES * ROW_ALIGN
    padded = -(-n // unit) * unit
    if padded != n:
        cat = jnp.pad(cat, [(0, 0)] * len(lead) + [(0, padded - n)])
    return cat.reshape(lead + (padded // LANES, LANES))


def _unflat(buf, shapes, lead=()):
    flat = buf.reshape(lead + (-1,))
    out, off = [], 0
    for s in shapes:
        n = math.prod(s)
        out.append(flat[..., off:off + n].reshape(lead + tuple(s)))
        off += n
    return out


def _adamw(w, g, m, v):
    m2 = ADAM_B1 * m + (1.0 - ADAM_B1) * g
    v2 = ADAM_B2 * v + (1.0 - ADAM_B2) * (g * g)
    m_hat = m2 * (1.0 / (1.0 - ADAM_B1 ** ADAM_STEP))
    v_hat = v2 * (1.0 / (1.0 - ADAM_B2 ** ADAM_STEP))
    delta = -ADAM_LR * (m_hat / (jnp.sqrt(v_hat) + ADAM_EPS) + ADAM_WD * w)
    return delta, m2, v2


def kernel(x, p, norm_mix, w_in, conv_qkv, a_log, dt_bias, head_norm, sgu_norm, w_spatial, b_spatial, w_branch_a, w_branch_b, w_out, norm_ffn, w_ffn_up, conv_ffn, b_conv_ffn, w_ffn_down, norm_ple, w_ple_gate, w_ple_proj, norm_final, loss_target, m_norm_mix, m_w_in, m_conv_qkv, m_a_log, m_dt_bias, m_head_norm, m_sgu_norm, m_w_spatial, m_b_spatial, m_w_branch_a, m_w_branch_b, m_w_out, m_norm_ffn, m_w_ffn_up, m_conv_ffn, m_b_conv_ffn, m_w_ffn_down, m_norm_ple, m_w_ple_gate, m_w_ple_proj, m_norm_final, v_norm_mix, v_w_in, v_conv_qkv, v_a_log, v_dt_bias, v_head_norm, v_sgu_norm, v_w_spatial, v_b_spatial, v_w_branch_a, v_w_branch_b, v_w_out, v_norm_ffn, v_w_ffn_up, v_conv_ffn, v_b_conv_ffn, v_w_ffn_down, v_norm_ple, v_w_ple_gate, v_w_ple_proj, v_norm_final):
    w = dict(norm_mix=norm_mix, w_in=w_in, conv_qkv=conv_qkv, a_log=a_log, dt_bias=dt_bias, head_norm=head_norm, sgu_norm=sgu_norm,
             w_spatial=w_spatial, b_spatial=b_spatial, w_branch_a=w_branch_a, w_branch_b=w_branch_b, w_out=w_out, norm_ffn=norm_ffn,
             w_ffn_up=w_ffn_up, conv_ffn=conv_ffn, b_conv_ffn=b_conv_ffn, w_ffn_down=w_ffn_down, norm_ple=norm_ple,
             w_ple_gate=w_ple_gate, w_ple_proj=w_ple_proj, norm_final=norm_final)
    m = dict(norm_mix=m_norm_mix, w_in=m_w_in, conv_qkv=m_conv_qkv, a_log=m_a_log, dt_bias=m_dt_bias, head_norm=m_head_norm,
             sgu_norm=m_sgu_norm, w_spatial=m_w_spatial, b_spatial=m_b_spatial, w_branch_a=m_w_branch_a, w_branch_b=m_w_branch_b,
             w_out=m_w_out, norm_ffn=m_norm_ffn, w_ffn_up=m_w_ffn_up, conv_ffn=m_conv_ffn, b_conv_ffn=m_b_conv_ffn,
             w_ffn_down=m_w_ffn_down, norm_ple=m_norm_ple, w_ple_gate=m_w_ple_gate, w_ple_proj=m_w_ple_proj, norm_final=m_norm_final)
    v = dict(norm_mix=v_norm_mix, w_in=v_w_in, conv_qkv=v_conv_qkv, a_log=v_a_log, dt_bias=v_dt_bias, head_norm=v_head_norm,
             sgu_norm=v_sgu_norm, w_spatial=v_w_spatial, b_spatial=v_b_spatial, w_branch_a=v_w_branch_a, w_branch_b=v_w_branch_b,
             w_out=v_w_out, norm_ffn=v_norm_ffn, w_ffn_up=v_w_ffn_up, conv_ffn=v_conv_ffn, b_conv_ffn=v_b_conv_ffn,
             w_ffn_down=v_w_ffn_down, norm_ple=v_norm_ple, w_ple_gate=v_w_ple_gate, w_ple_proj=v_w_ple_proj, norm_final=v_norm_final)
    chip = 2 * lax.axis_index("x") + lax.axis_index("y")
    shard_shapes = {k: w[k].shape[1:] for k in BIG}

    layer_bufs = [_flat([w[k][i] for k in BIG], bf16) for i in range(DEPTH)]
    conv_buf = _flat([w[k] for k in SMALL_COLS], f32)
    gathered = _gather_shards(layer_bufs + [conv_buf], "gather_weights")

    def whole(parts, col_sharded):
        if not col_sharded:
            return parts.reshape((-1,) + parts.shape[2:])
        return jnp.moveaxis(parts, 0, -2).reshape(parts.shape[1:-1] + (N_CHIPS * parts.shape[-1],))

    sm = {k: w[k] for k in SMALL_REPL}
    for k, parts in zip(SMALL_COLS, _unflat(gathered[DEPTH], [w[k].shape for k in SMALL_COLS], (N_CHIPS,))):
        sm[k] = whole(parts, True)
    Ws = []
    for i in range(DEPTH):
        parts = _unflat(gathered[i], [shard_shapes[k] for k in BIG], (N_CHIPS,))
        Ws.append(_big_weights({k: whole(pt, k in COL_SHARDED) for k, pt in zip(BIG, parts)}))
    Ss = [_small_params(sm, i) for i in range(DEPTH)]

    h = x[0]
    saved = []
    for i in range(DEPTH):
        h, sv = _layer_fwd(h, p[i, 0], Ws[i], Ss[i], i)
        saved.append(sv)
    loss_row, dh, g_final = _loss_head(h, sm["norm_final"][None], loss_target[0], "loss_head")

    gsm = {k: [None] * DEPTH for k in SMALL_REPL + SMALL_COLS if k != "norm_final"}
    gbig = {k: [None] * DEPTH for k in BIG}
    for i in reversed(range(DEPTH)):
        dh, G = _layer_bwd(dh, saved[i], Ws[i], Ss[i], i)
        G["w_in"] = _w_in_grad(G)
        by_chip = []
        for k in BIG:
            g = G[k]
            if k in COL_SHARDED:
                g = jnp.moveaxis(g.reshape(g.shape[0], N_CHIPS, -1), 1, 0)
            else:
                g = g.reshape(N_CHIPS, -1, g.shape[1])
            by_chip.append(g)
        summed = _reduce_scatter(_flat(by_chip, f32, (N_CHIPS,)), f"l{i}")
        for k, g in zip(BIG, _unflat(summed, [shard_shapes[k] for k in BIG])):
            gbig[k][i] = g
        for k in gsm:
            gsm[k][i] = G[k].reshape(w[k].shape[1:-1] + (-1,)) if k in SMALL_COLS else G[k].reshape(w[k].shape[1:])

    small_names = [k for k in SMALL_REPL + SMALL_COLS if k != "norm_final"]
    small_local = [jnp.stack(gsm[k]) for k in small_names] + [g_final.reshape(-1), loss_row[0, :1]]
    small_shapes = [a.shape for a in small_local]
    small_sum = _unflat(_allreduce_small(_flat(small_local, f32), "allreduce_small"), small_shapes)
    gs = dict(zip(small_names + ["norm_final"], small_sum[:-1]))
    loss = small_sum[-1][0]
    for k in SMALL_COLS:
        n = w[k].shape[-1]
        gs[k] = sum(jnp.where(chip == j, gs[k][..., j * n:(j + 1) * n], 0.0) for j in range(N_CHIPS))

    grads, deltas, new_m, new_v = {}, {}, {}, {}
    for k in BIG:
        g = jnp.stack(gbig[k])
        two = lambda a: a.reshape(-1, a.shape[-1])
        d, m2, v2 = _ew(_adamw, [two(w[k]), two(g), two(m[k]), two(v[k])], [f32, f32, f32], f"adamw_{k}")
        grads[k], deltas[k], new_m[k], new_v[k] = g, d.reshape(g.shape), m2.reshape(g.shape), v2.reshape(g.shape)
    small_all = [k for k in WEIGHTS if k not in BIG]
    shapes = [w[k].shape for k in small_all]
    d, m2, v2 = _ew(_adamw, [_flat([w[k] for k in small_all], f32), _flat([gs[k] for k in small_all], f32),
                             _flat([m[k] for k in small_all], f32), _flat([v[k] for k in small_all], f32)],
                    [f32, f32, f32], "adamw_small")
    for k, a, b, c_ in zip(small_all, _unflat(d, shapes), _unflat(m2, shapes), _unflat(v2, shapes)):
        grads[k], deltas[k], new_m[k], new_v[k] = gs[k], a, b, c_
    return (loss, dh[None], *[grads[k] for k in WEIGHTS], *[deltas[k] for k in WEIGHTS],
            *[new_m[k] for k in WEIGHTS], *[new_v[k] for k in WEIGHTS])
```

```python
import functools
import math

import jax
import jax.numpy as jnp
from jax import lax
from jax.experimental import pallas as pl
from jax.experimental.pallas import tpu as pltpu

f32 = jnp.float32
bf16 = jnp.bfloat16
HI = lax.Precision.HIGHEST
SDS = jax.ShapeDtypeStruct

D_MODEL = 2048
DEPTH = 4
HEAD_DIM = 128
N_QK = 8
N_V = 16
QK_W = N_QK * HEAD_DIM
V_W = N_V * HEAD_DIM
CHUNK_A = 64
N_GROUPS = 8
GROUP_DIM = 128
WIDTH_B = N_GROUPS * GROUP_DIM
CHUNK_B = 128
D_FF = 5632
PLE_DIM = 256
EPS = 1e-6
N_IN = 12320
ADAM_LR, ADAM_B1, ADAM_B2, ADAM_EPS, ADAM_WD, ADAM_STEP = 0.001, 0.9, 0.999, 1e-08, 0.01, 10

C_Q, C_K, C_V, C_Z, C_UB, C_VB, C_GA, C_GB, PM = 0, 1024, 2048, 4096, 6144, 7168, 8192, 10240, 12288
LANES = 128
VMEM_LIMIT_BYTES = 48 * 1024 * 1024
ROW_CHUNK = 256


def _cp(*sem):
    return pltpu.CompilerParams(dimension_semantics=sem if sem else None, vmem_limit_bytes=VMEM_LIMIT_BYTES)


def _dot(a, b, prec=None):
    return jnp.dot(a, b, preferred_element_type=f32, precision=prec)


def _dot_nt(a, b, prec=None):
    return lax.dot_general(a, b, (((1,), (1,)), ((), ())), preferred_element_type=f32, precision=prec)


def _dot_tn(a, b, prec=None):
    return lax.dot_general(a, b, (((0,), (0,)), ((), ())), preferred_element_type=f32, precision=prec)


def _b(x):
    return x.astype(bf16)


def _sig(x):
    return 1.0 / (1.0 + jnp.exp(-x))


def _silu(x):
    return x * _sig(x)


def _dsilu(x):
    s = _sig(x)
    return s * (1.0 + x * (1.0 - s))


_GELU_C = 0.7978845608028654
_GELU_A = 0.044715


def _gelu(x):
    return 0.5 * x * (1.0 + jnp.tanh(_GELU_C * (x + _GELU_A * x * x * x)))


def _dgelu(x):
    t = jnp.tanh(_GELU_C * (x + _GELU_A * x * x * x))
    return 0.5 * (1.0 + t) + 0.5 * x * (1.0 - t * t) * _GELU_C * (1.0 + 3.0 * _GELU_A * x * x)


def _softplus(x):
    return jnp.maximum(x, 0.0) + jnp.log(1.0 + jnp.exp(-jnp.abs(x)))


def _divisor(n, cap, mult):
    best = None
    d = mult
    while d <= min(n, cap):
        if n % d == 0:
            best = d
        d += mult
    return best if best is not None else n


def _rows(tr, c, j=0):
    return pl.BlockSpec((tr, c), lambda i, j=j: (i, j))


def _whole(shape):
    nd = len(shape)
    return pl.BlockSpec(shape, lambda *_: (0,) * nd)


def _mm(a, b, mode, name, out_dtype=f32, c=None, tm=512, tn=512, tk=2048):
    if mode == "nn":
        (M, K), (K2, N) = a.shape, b.shape
    elif mode == "nt":
        (M, K), (N, K2) = a.shape, b.shape
    else:
        (K, M), (K2, N) = a.shape, b.shape
    assert K == K2, (a.shape, b.shape, mode)
    tm = _divisor(M, tm, LANES if mode == "tn" else 16)
    tn = _divisor(N, tn, LANES)
    tk = _divisor(K, tk, LANES if mode != "tn" else 16)
    nk = K // tk
    if mode == "nn":
        a_spec = pl.BlockSpec((tm, tk), lambda i, j, k: (i, k))
        b_spec = pl.BlockSpec((tk, tn), lambda i, j, k: (k, j))
        dn = (((1,), (0,)), ((), ()))
    elif mode == "nt":
        a_spec = pl.BlockSpec((tm, tk), lambda i, j, k: (i, k))
        b_spec = pl.BlockSpec((tn, tk), lambda i, j, k: (j, k))
        dn = (((1,), (1,)), ((), ()))
    else:
        a_spec = pl.BlockSpec((tk, tm), lambda i, j, k: (k, i))
        b_spec = pl.BlockSpec((tk, tn), lambda i, j, k: (k, j))
        dn = (((0,), (0,)), ((), ()))
    o_spec = pl.BlockSpec((tm, tn), lambda i, j, k: (i, j))
    has_c = c is not None

    def body(*refs):
        a_ref, b_ref = refs[0], refs[1]
        c_ref = refs[2] if has_c else None
        o_ref = refs[3] if has_c else refs[2]
        part = lax.dot_general(_b(a_ref[...]), _b(b_ref[...]), dn, preferred_element_type=f32)
        if nk == 1:
            if has_c:
                part = part + c_ref[...]
            o_ref[...] = part.astype(o_ref.dtype)
        else:
            acc = refs[-1]
            k = pl.program_id(2)

            @pl.when(k == 0)
            def _():
                acc[...] = part

            @pl.when(k > 0)
            def _():
                acc[...] += part

            @pl.when(k == nk - 1)
            def _():
                r = acc[...]
                if has_c:
                    r = r + c_ref[...]
                o_ref[...] = r.astype(o_ref.dtype)

    ins = [a, b] + ([c] if has_c else [])
    in_specs = [a_spec, b_spec] + ([o_spec] if has_c else [])
    return pl.pallas_call(
        body, grid=(M // tm, N // tn, nk), in_specs=in_specs, out_specs=o_spec, out_shape=SDS((M, N), out_dtype),
        scratch_shapes=[pltpu.VMEM((tm, tn), f32)] if nk > 1 else [], name=name,
        compiler_params=_cp("parallel", "parallel", "arbitrary"),
    )(*ins)


def _ew(fn, ins, out_dtypes, name, tile_bytes=2 * 1024 * 1024):
    R, C = max((x.shape for x in ins), key=lambda s: s[0])
    n_in = len(ins)
    row_bytes = 4 * C * (len(ins) + len(out_dtypes))
    tr = _divisor(R, max(16, tile_bytes // row_bytes), 16)
    in_specs = [_rows(tr, C) if x.shape[0] == R else _whole((1, C)) for x in ins]

    def body(*refs):
        res = fn(*[r[...] for r in refs[:n_in]])
        for o_ref, r in zip(refs[n_in:], res):
            o_ref[...] = r.astype(o_ref.dtype)

    outs = pl.pallas_call(
        body, grid=(R // tr,), in_specs=in_specs, out_specs=[_rows(tr, C) for _ in out_dtypes],
        out_shape=[SDS((R, C), dt) for dt in out_dtypes], name=name, compiler_params=_cp("parallel"),
    )(*ins)
    return outs


def _rms_fwd(x, gain, name):
    T, Dm = x.shape
    tr = _divisor(T, 256, 16)

    def body(x_ref, g_ref, o_ref):
        xv = x_ref[...]
        r = lax.rsqrt(jnp.mean(xv * xv, axis=-1, keepdims=True) + EPS)
        o_ref[...] = (xv * r * g_ref[...]).astype(o_ref.dtype)

    return pl.pallas_call(
        body, grid=(T // tr,), in_specs=[_rows(tr, Dm), _whole((1, Dm))], out_specs=_rows(tr, Dm),
        out_shape=SDS((T, Dm), bf16), name=name, compiler_params=_cp("parallel"),
    )(x, gain)


def _rms_bwd(x, gain, dh, dres, name):
    T, Dm = x.shape
    tr = _divisor(T, 256, 16)

    def body(x_ref, g_ref, dh_ref, dres_ref, dx_ref, dg_ref):
        xv = x_ref[...]
        dhv = dh_ref[...]
        r = lax.rsqrt(jnp.mean(xv * xv, axis=-1, keepdims=True) + EPS)
        dy = dhv * g_ref[...]
        m = jnp.mean(dy * xv, axis=-1, keepdims=True)
        dx_ref[...] = dres_ref[...] + r * dy - xv * (r * r * r * m)
        part = jnp.sum(dhv * xv * r, axis=0, keepdims=True)

        @pl.when(pl.program_id(0) == 0)
        def _():
            dg_ref[...] = part

        @pl.when(pl.program_id(0) > 0)
        def _():
            dg_ref[...] += part

    return pl.pallas_call(
        body, grid=(T // tr,), in_specs=[_rows(tr, Dm), _whole((1, Dm)), _rows(tr, Dm), _rows(tr, Dm)],
        out_specs=[_rows(tr, Dm), _whole((1, Dm))], out_shape=[SDS((T, Dm), f32), SDS((1, Dm), f32)],
        name=name, compiler_params=_cp("arbitrary"),
    )(x, gain, dh, dres)


def _shift_down(cur, prev8, s):
    rolled = pltpu.roll(cur, s, 0)
    rp = pltpu.roll(prev8, s, 0)
    row8 = lax.broadcasted_iota(jnp.int32, prev8.shape, 0)
    first = jnp.where(row8 < s, rp, rolled[:8])
    return jnp.concatenate([first, rolled[8:]], axis=0)


def _shift_up(cur, next8, s):
    R = cur.shape[0]
    rolled = pltpu.roll(cur, R - s, 0)
    rn = pltpu.roll(next8, 8 - s, 0)
    row8 = lax.broadcasted_iota(jnp.int32, next8.shape, 0)
    last = jnp.where(row8 >= 8 - s, rn, rolled[R - 8:])
    return jnp.concatenate([rolled[: R - 8], last], axis=0)


def _prev8(ref, r0, ci):
    rows = ref[pl.ds(pl.multiple_of(jnp.maximum(r0 - 8, 0), 8), 8), :]
    return jnp.where(ci > 0, rows, 0.0)


def _next8(ref, r0, R, ci, n_chunks):
    start = jnp.minimum(r0 + R, (n_chunks - 1) * R + R - 8)
    rows = ref[pl.ds(pl.multiple_of(start, 8), 8), :]
    return jnp.where(ci < n_chunks - 1, rows, 0.0)


def _w_rows(w_ref, k):
    return [w_ref[j:j + 1, :] for j in range(k)]


def _causal_conv(cur, prev8, w, k):
    acc = cur * w[k - 1]
    for s in range(1, k):
        acc = acc + _shift_down(cur, prev8, s) * w[k - 1 - s]
    return acc


def _qkvconv_fwd(proj, convw, name):
    T = proj.shape[0]
    R = min(ROW_CHUNK, T)
    n_chunks = T // R
    n_blk = (2 * QK_W + V_W) // HEAD_DIM

    def body(x_ref, w_ref, o_ref):
        p = pl.program_id(0)
        is_qk = p < 2 * N_QK
        scale = jnp.where(p < N_QK, HEAD_DIM ** -0.5, 1.0).astype(f32)
        w = _w_rows(w_ref, 4)

        def chunk(ci, carry):
            r0 = pl.multiple_of(ci * R, R)
            cur = x_ref[pl.ds(r0, R), :]
            y = _silu(_causal_conv(cur, _prev8(x_ref, r0, ci), w, 4))
            ss = jnp.sum(y * y, axis=-1, keepdims=True)
            nrm = jnp.where(is_qk, lax.rsqrt(ss + EPS) * scale, 1.0)
            o_ref[pl.ds(r0, R), :] = y * nrm
            return carry

        lax.fori_loop(0, n_chunks, chunk, 0)

    return pl.pallas_call(
        body, grid=(n_blk,),
        in_specs=[pl.BlockSpec((T, HEAD_DIM), lambda p: (0, p)), pl.BlockSpec((4, HEAD_DIM), lambda p: (0, p))],
        out_specs=pl.BlockSpec((T, HEAD_DIM), lambda p: (0, p)), out_shape=SDS((T, n_blk * HEAD_DIM), f32),
        name=name, compiler_params=_cp("parallel"),
    )(proj, convw)


def _qkvconv_bwd(proj, convw, dqkv, name):
    T = proj.shape[0]
    R = min(ROW_CHUNK, T)
    n_chunks = T // R
    n_blk = (2 * QK_W + V_W) // HEAD_DIM

    def body(x_ref, w_ref, do_ref, dx_ref, dw_ref, dc_sc):
        p = pl.program_id(0)
        is_qk = p < 2 * N_QK
        scale = jnp.where(p < N_QK, HEAD_DIM ** -0.5, 1.0).astype(f32)
        w = _w_rows(w_ref, 4)

        def phase1(ci, dw):
            r0 = pl.multiple_of(ci * R, R)
            cur = x_ref[pl.ds(r0, R), :]
            p8 = _prev8(x_ref, r0, ci)
            shifted = [cur] + [_shift_down(cur, p8, s) for s in range(1, 4)]
            c = shifted[0] * w[3]
            for s in range(1, 4):
                c = c + shifted[s] * w[3 - s]
            y = _silu(c)
            dout = do_ref[pl.ds(r0, R), :]
            n = lax.rsqrt(jnp.sum(y * y, axis=-1, keepdims=True) + EPS)
            dot_ = jnp.sum(dout * y, axis=-1, keepdims=True)
            dy = jnp.where(is_qk, scale * (n * dout - y * (n * n * n * dot_)), dout)
            dc = dy * _dsilu(c)
            dc_sc[pl.ds(r0, R), :] = dc
            return tuple(dw[j] + jnp.sum(dc * shifted[3 - j], axis=0, keepdims=True) for j in range(4))

        dw = lax.fori_loop(0, n_chunks, phase1, tuple(jnp.zeros((1, HEAD_DIM), f32) for _ in range(4)))
        for j in range(4):
            dw_ref[j:j + 1, :] = dw[j]

        def phase2(ci, carry):
            r0 = pl.multiple_of(ci * R, R)
            cur = dc_sc[pl.ds(r0, R), :]
            n8 = _next8(dc_sc, r0, R, ci, n_chunks)
            acc = cur * w[3]
            for s in range(1, 4):
                acc = acc + _shift_up(cur, n8, s) * w[3 - s]
            dx_ref[pl.ds(r0, R), :] = acc.astype(dx_ref.dtype)
            return carry

        lax.fori_loop(0, n_chunks, phase2, 0)

    col = lambda p: (0, p)
    return pl.pallas_call(
        body, grid=(n_blk,),
        in_specs=[pl.BlockSpec((T, HEAD_DIM), col), pl.BlockSpec((4, HEAD_DIM), col), pl.BlockSpec((T, HEAD_DIM), col)],
        out_specs=[pl.BlockSpec((T, HEAD_DIM), col), pl.BlockSpec((4, HEAD_DIM), col)],
        out_shape=[SDS((T, n_blk * HEAD_DIM), bf16), SDS((4, n_blk * HEAD_DIM), f32)],
        scratch_shapes=[pltpu.VMEM((T, HEAD_DIM), f32)], name=name, compiler_params=_cp("parallel"),
    )(proj, convw, dqkv)


def _tri_masks(C):
    row = lax.broadcasted_iota(jnp.int32, (C, C), 0)
    col = lax.broadcasted_iota(jnp.int32, (C, C), 1)
    return row, col


def _lane_pick(blk, lane, idx):
    return jnp.sum(jnp.where(lane == idx, blk, 0.0), axis=1, keepdims=True)


def _gate_block(ba, ea, dtb, lane):
    sig = _sig(ba)
    gblk = -ea * _softplus(ba + dtb)
    return sig, gblk


def _head_decay(gam_all, rg_all, tot, lane, h, row, col):
    C = row.shape[0]
    gam_c = _lane_pick(gam_all, lane, N_V + h)
    rg_c = _lane_pick(rg_all, lane, N_V + h)
    lane1 = lax.broadcasted_iota(jnp.int32, (1, LANES), 1)
    tot_h = jnp.sum(jnp.where(lane1 == N_V + h, tot, 0.0), axis=1, keepdims=True)
    gcb = jnp.broadcast_to(gam_c, (C, C))
    dlt = gcb - gcb.T
    dm = jnp.where(row >= col, jnp.exp(jnp.minimum(dlt, 0.0)), 0.0)
    return gam_c, rg_c, tot_h, dm


def _delta_prep(qkv, ba, ea_row, dtb_row, name):
    T = qkv.shape[0]
    C = CHUNK_A
    N = T // C

    def body(q_ref, k_ref, v_ref, ba_ref, ea_ref, dtb_ref,
             qd_ref, kd_ref, u_ref, w_ref, attn_ref, tinv_ref, cd_ref, bg_ref, at_sc, t_sc):
        row, col = _tri_masks(C)
        lane = lax.broadcasted_iota(jnp.int32, (C, LANES), 1)
        sig, gblk = _gate_block(ba_ref[...], ea_ref[...], dtb_ref[...], lane)
        bg_ref[...] = jnp.where(lane < N_V, sig, gblk)
        lower = (row >= col).astype(f32)
        upper_s = (col > row).astype(f32)
        eye = (row == col).astype(f32)
        gam_all = _dot(lower, gblk, HI)
        rg_all = _dot(upper_s, gblk, HI)
        tot = jnp.sum(gblk, axis=0, keepdims=True)
        col1 = lax.broadcasted_iota(jnp.int32, (1, C), 1)
        for h in range(N_V):
            qk = pl.ds((h // 2) * HEAD_DIM, HEAD_DIM)
            hs = pl.ds(h * HEAD_DIM, HEAD_DIM)
            qh, kh, vh = q_ref[:, qk], k_ref[:, qk], v_ref[:, hs]
            beta_c = _lane_pick(sig, lane, h)
            gam_c, rg_c, tot_h, dm = _head_decay(gam_all, rg_all, tot, lane, h, row, col)
            kk = _dot_nt(_b(kh), _b(kh))
            a = jnp.where(row > col, beta_c * kk * dm, 0.0)
            at_sc[h] = a.T
            t_sc[h] = eye
            eg = jnp.exp(gam_c)
            attn_ref[0, h] = _dot_nt(_b(qh), _b(kh)) * dm
            qd_ref[:, hs] = qh * eg
            kd_ref[:, hs] = kh * jnp.exp(rg_c)
            cd_ref[0, h] = jnp.broadcast_to(jnp.exp(tot_h), (1, LANES))
            u_ref[:, hs] = vh * beta_c
            w_ref[:, hs] = kh * (beta_c * eg)

        def fsub(i, carry):
            for h in range(N_V):
                a_col = jnp.sum(jnp.where(col == i, at_sc[h], 0.0), axis=1, keepdims=True)
                new_row = jnp.sum(a_col * t_sc[h], axis=0, keepdims=True)
                t_sc[h, pl.ds(i, 1), :] = (col1 == i).astype(f32) - new_row
            return carry

        lax.fori_loop(1, C, fsub, 0)
        for h in range(N_V):
            hs = pl.ds(h * HEAD_DIM, HEAD_DIM)
            tinv = t_sc[h]
            tinv_ref[0, h] = tinv
            u_ref[:, hs] = _dot(tinv, u_ref[:, hs], HI)
            w_ref[:, hs] = _dot(tinv, w_ref[:, hs], HI)

    big = lambda n: (n, 0)
    return pl.pallas_call(
        body, grid=(N,),
        in_specs=[pl.BlockSpec((C, QK_W), lambda n: (n, 0)), pl.BlockSpec((C, QK_W), lambda n: (n, 1)),
                  pl.BlockSpec((C, V_W), lambda n: (n, 1)), pl.BlockSpec((C, LANES), big),
                  _whole((1, LANES)), _whole((1, LANES))],
        out_specs=[pl.BlockSpec((C, V_W), big)] * 4 + [
            pl.BlockSpec((1, N_V, C, C), lambda n: (n, 0, 0, 0)), pl.BlockSpec((1, N_V, C, C), lambda n: (n, 0, 0, 0)),
            pl.BlockSpec((1, N_V, 1, LANES), lambda n: (n, 0, 0, 0)), pl.BlockSpec((C, LANES), big)],
        out_shape=[SDS((T, V_W), f32)] * 4 + [SDS((N, N_V, C, C), f32), SDS((N, N_V, C, C), f32),
                                              SDS((N, N_V, 1, LANES), f32), SDS((T, LANES), f32)],
        scratch_shapes=[pltpu.VMEM((N_V, C, C), f32), pltpu.VMEM((N_V, C, C), f32)],
        name=name, compiler_params=_cp("parallel"),
    )(qkv, qkv, qkv, ba, ea_row, dtb_row)


def _delta_scan(qd, kd, u, w, attn, cd, name):
    T = qd.shape[0]
    C = CHUNK_A
    N = T // C

    def body(qd_ref, kd_ref, u_ref, w_ref, attn_ref, cd_ref, o_ref, s_ref, vn_ref, s_sc):
        @pl.when(pl.program_id(1) == 0)
        def _():
            s_sc[...] = jnp.zeros_like(s_sc)

        s = s_sc[...]
        s_ref[0, 0] = s
        sb = _b(s)
        vn = u_ref[...] - _dot(_b(w_ref[...]), sb)
        vn_ref[...] = vn
        o_ref[...] = _dot(_b(qd_ref[...]), sb) + _dot(_b(attn_ref[0, 0]), _b(vn))
        s_sc[...] = s * cd_ref[0, 0] + _dot_tn(_b(kd_ref[...]), _b(vn))

    blk = pl.BlockSpec((C, HEAD_DIM), lambda h, n: (n, h))
    return pl.pallas_call(
        body, grid=(N_V, N),
        in_specs=[blk, blk, blk, blk, pl.BlockSpec((1, 1, C, C), lambda h, n: (n, h, 0, 0)),
                  pl.BlockSpec((1, 1, 1, LANES), lambda h, n: (n, h, 0, 0))],
        out_specs=[blk, pl.BlockSpec((1, 1, HEAD_DIM, HEAD_DIM), lambda h, n: (n, h, 0, 0)), blk],
        out_shape=[SDS((T, V_W), f32), SDS((N, N_V, HEAD_DIM, HEAD_DIM), f32), SDS((T, V_W), f32)],
        scratch_shapes=[pltpu.VMEM((HEAD_DIM, HEAD_DIM), f32)], name=name, compiler_params=_cp("parallel", "arbitrary"),
    )(qd, kd, u, w, attn, cd)


def _delta_scan_bwd(do, qd, kd, w, attn, cd, s_all, vn, name):
    T = qd.shape[0]
    C = CHUNK_A
    N = T // C

    def body(do_ref, qd_ref, kd_ref, w_ref, attn_ref, cd_ref, s_ref, vn_ref,
             dqd_ref, dkd_ref, du_ref, dw_ref, dattn_ref, dcd_ref, ds_sc):
        @pl.when(pl.program_id(1) == 0)
        def _():
            ds_sc[...] = jnp.zeros_like(ds_sc)

        row, col = _tri_masks(C)
        dsn = ds_sc[...]
        s = s_ref[0, 0]
        dob, sb, dsb = _b(do_ref[...]), _b(s), _b(dsn)
        vnb = _b(vn_ref[...])
        dqd_ref[...] = _dot_nt(dob, sb)
        dattn_ref[0, 0] = jnp.where(row >= col, _dot_nt(dob, vnb), 0.0)
        dvn = _dot_tn(_b(attn_ref[0, 0]), dob) + _dot(_b(kd_ref[...]), dsb)
        dvnb = _b(dvn)
        dkd_ref[...] = _dot_nt(vnb, dsb)
        dcd = jnp.sum(jnp.sum(s * dsn, axis=1, keepdims=True), axis=0, keepdims=True)
        dcd_ref[0, 0] = jnp.broadcast_to(dcd, (1, LANES))
        du_ref[...] = dvn
        dw_ref[...] = -_dot_nt(dvnb, sb)
        ds_sc[...] = dsn * cd_ref[0, 0] + _dot_tn(_b(qd_ref[...]), dob) - _dot_tn(_b(w_ref[...]), dvnb)

    blk = pl.BlockSpec((C, HEAD_DIM), lambda h, n: (N - 1 - n, h))
    sq = pl.BlockSpec((1, 1, C, C), lambda h, n: (N - 1 - n, h, 0, 0))
    sc = pl.BlockSpec((1, 1, 1, LANES), lambda h, n: (N - 1 - n, h, 0, 0))
    st = pl.BlockSpec((1, 1, HEAD_DIM, HEAD_DIM), lambda h, n: (N - 1 - n, h, 0, 0))
    return pl.pallas_call(
        body, grid=(N_V, N), in_specs=[blk, blk, blk, blk, sq, sc, st, blk],
        out_specs=[blk, blk, blk, blk, sq, sc],
        out_shape=[SDS((T, V_W), f32)] * 4 + [SDS((N, N_V, C, C), f32), SDS((N, N_V, 1, LANES), f32)],
        scratch_shapes=[pltpu.VMEM((HEAD_DIM, HEAD_DIM), f32)], name=name, compiler_params=_cp("parallel", "arbitrary"),
    )(do, qd, kd, w, attn, cd, s_all, vn)


def _delta_prep_bwd(qkv, ba, ea_row, dtb_row, tinv, u, w, dqd, dkd, du, dw, dattn, dcd, name):
    T = qkv.shape[0]
    C = CHUNK_A
    N = T // C

    def body(q_ref, k_ref, v_ref, ba_ref, ea_ref, dtb_ref, tinv_ref, u_ref, w_ref,
             dqd_ref, dkd_ref, du_ref, dw_ref, dattn_ref, dcd_ref,
             dqkv_ref, dba_ref, dalog_ref, ddtb_ref):
        row, col = _tri_masks(C)
        lane = lax.broadcasted_iota(jnp.int32, (C, LANES), 1)
        rowc = lax.broadcasted_iota(jnp.int32, (C, 1), 0)
        ba_v, ea, dtb = ba_ref[...], ea_ref[...], dtb_ref[...]
        sig, gblk = _gate_block(ba_v, ea, dtb, lane)
        lower = (row >= col).astype(f32)
        upper_s = (col > row).astype(f32)
        upper = (col >= row).astype(f32)
        gam_all = _dot(lower, gblk, HI)
        rg_all = _dot(upper_s, gblk, HI)
        tot = jnp.sum(gblk, axis=0, keepdims=True)
        dbeta_blk = jnp.zeros((C, LANES), f32)
        dgam_blk = jnp.zeros((C, LANES), f32)
        for j in range(N_QK):
            qk = pl.ds(j * HEAD_DIM, HEAD_DIM)
            qh, kh = q_ref[:, qk], k_ref[:, qk]
            qhb, khb = _b(qh), _b(kh)
            kk = _dot_nt(khb, khb)
            qkm = _dot_nt(qhb, khb)
            dq_j = jnp.zeros((C, HEAD_DIM), f32)
            dk_j = jnp.zeros((C, HEAD_DIM), f32)
            for h in (2 * j, 2 * j + 1):
                hs = pl.ds(h * HEAD_DIM, HEAD_DIM)
                vh = v_ref[:, hs]
                beta_c = _lane_pick(sig, lane, h)
                gam_c, rg_c, tot_h, dm = _head_decay(gam_all, rg_all, tot, lane, h, row, col)
                eg, er, cdh = jnp.exp(gam_c), jnp.exp(rg_c), jnp.exp(tot_h)
                tinv_h = tinv_ref[0, h]
                d_rv = _dot_tn(tinv_h, du_ref[:, hs], HI)
                d_rk = _dot_tn(tinv_h, dw_ref[:, hs], HI)
                da = -jnp.where(row > col, _dot_nt(_b(d_rv), _b(u_ref[:, hs])) + _dot_nt(_b(d_rk), _b(w_ref[:, hs])), 0.0)
                dqkv_ref[:, pl.ds(2 * QK_W + h * HEAD_DIM, HEAD_DIM)] = beta_c * d_rv
                dbeta = jnp.sum(d_rv * vh + d_rk * (eg * kh), axis=1, keepdims=True)
                dk_h = (beta_c * eg) * d_rk
                d_eg = jnp.sum(d_rk * kh, axis=1, keepdims=True) * beta_c
                bkd = da * dm
                dbeta = dbeta + jnp.sum(bkd * kk, axis=1, keepdims=True)
                dkk = bkd * beta_c
                ddm = da * beta_c * kk
                dattn_h = dattn_ref[0, h]
                dqk = dattn_h * dm
                ddm = ddm + dattn_h * qkm
                dqd_h, dkd_h = dqd_ref[:, hs], dkd_ref[:, hs]
                dq_j = dq_j + _dot(_b(dqk), khb) + eg * dqd_h
                dk_h = dk_h + _dot_tn(_b(dqk), qhb) + _dot(_b(dkk + dkk.T), khb) + er * dkd_h
                dk_j = dk_j + dk_h
                d_eg = d_eg + jnp.sum(dqd_h * qh, axis=1, keepdims=True)
                d_er = jnp.sum(dkd_h * kh, axis=1, keepdims=True)
                e = ddm * dm
                dgam = jnp.sum(e, axis=1, keepdims=True) - jnp.sum(e.T, axis=1, keepdims=True)
                dgam = dgam + d_eg * eg - d_er * er
                extra = jnp.sum(d_er * er, axis=0, keepdims=True) + jnp.max(dcd_ref[0, h], axis=1, keepdims=True) * cdh
                dgam = dgam + jnp.where(rowc == C - 1, extra, 0.0)
                dbeta_blk = jnp.where(lane == h, dbeta, dbeta_blk)
                dgam_blk = jnp.where(lane == N_V + h, dgam, dgam_blk)
            dqkv_ref[:, qk] = dq_j
            dqkv_ref[:, pl.ds(QK_W + j * HEAD_DIM, HEAD_DIM)] = dk_j
        dg_all = _dot(upper, dgam_blk, HI)
        dsp = dg_all * (-ea) * _sig(ba_v + dtb)
        dba_ref[...] = jnp.where(lane < N_V, dbeta_blk * sig * (1.0 - sig), dsp)
        part_alog = jnp.sum(dg_all * gblk, axis=0, keepdims=True)
        part_dtb = jnp.sum(dsp, axis=0, keepdims=True)

        @pl.when(pl.program_id(0) == 0)
        def _():
            dalog_ref[...] = part_alog
            ddtb_ref[...] = part_dtb

        @pl.when(pl.program_id(0) > 0)
        def _():
            dalog_ref[...] += part_alog
            ddtb_ref[...] += part_dtb

    big = lambda n: (n, 0)
    wide = pl.BlockSpec((C, V_W), big)
    sq = pl.BlockSpec((1, N_V, C, C), lambda n: (n, 0, 0, 0))
    return pl.pallas_call(
        body, grid=(N,),
        in_specs=[pl.BlockSpec((C, QK_W), lambda n: (n, 0)), pl.BlockSpec((C, QK_W), lambda n: (n, 1)),
                  pl.BlockSpec((C, V_W), lambda n: (n, 1)), pl.BlockSpec((C, LANES), big),
                  _whole((1, LANES)), _whole((1, LANES)), sq, wide, wide, wide, wide, wide, wide, sq,
                  pl.BlockSpec((1, N_V, 1, LANES), lambda n: (n, 0, 0, 0))],
        out_specs=[pl.BlockSpec((C, 2 * QK_W + V_W), big), pl.BlockSpec((C, LANES), big),
                   _whole((1, LANES)), _whole((1, LANES))],
        out_shape=[SDS((T, 2 * QK_W + V_W), f32), SDS((T, LANES), f32), SDS((1, LANES), f32), SDS((1, LANES), f32)],
        name=name, compiler_params=_cp("arbitrary"),
    )(qkv, qkv, qkv, ba, ea_row, dtb_row, tinv, u, w, dqd, dkd, du, dw, dattn, dcd)


def _onorm_fwd(o, proj, hg, name):
    T = o.shape[0]
    tr = _divisor(T, 256, 16)

    def body(o_ref, z_ref, g_ref, out_ref):
        g = g_ref[...]
        for h in range(N_V):
            hs = pl.ds(h * HEAD_DIM, HEAD_DIM)
            oh = o_ref[:, hs]
            r = lax.rsqrt(jnp.mean(oh * oh, axis=-1, keepdims=True) + EPS)
            out_ref[:, hs] = (oh * r * g * _silu(z_ref[:, hs])).astype(out_ref.dtype)

    return pl.pallas_call(
        body, grid=(T // tr,), in_specs=[_rows(tr, V_W), _rows(tr, V_W, C_Z // V_W), _whole((1, HEAD_DIM))],
        out_specs=_rows(tr, V_W), out_shape=SDS((T, V_W), bf16), name=name, compiler_params=_cp("parallel"),
    )(o, proj, hg)


def _onorm_bwd(don, o, proj, hg, name):
    T = o.shape[0]
    tr = _divisor(T, 256, 16)

    def body(don_ref, o_ref, z_ref, g_ref, do_ref, dz_ref, dg_ref):
        g = g_ref[...]
        dg = jnp.zeros((1, HEAD_DIM), f32)
        for h in range(N_V):
            hs = pl.ds(h * HEAD_DIM, HEAD_DIM)
            oh, zh, dh = o_ref[:, hs], z_ref[:, hs], don_ref[:, hs]
            r = lax.rsqrt(jnp.mean(oh * oh, axis=-1, keepdims=True) + EPS)
            d_n = dh * _silu(zh)
            dz_ref[:, hs] = (dh * (oh * r * g) * _dsilu(zh)).astype(dz_ref.dtype)
            dy = d_n * g
            m = jnp.mean(dy * oh, axis=-1, keepdims=True)
            do_ref[:, hs] = r * dy - oh * (r * r * r * m)
            dg = dg + jnp.sum(d_n * oh * r, axis=0, keepdims=True)

        @pl.when(pl.program_id(0) == 0)
        def _():
            dg_ref[...] = dg

        @pl.when(pl.program_id(0) > 0)
        def _():
            dg_ref[...] += dg

    return pl.pallas_call(
        body, grid=(T // tr,),
        in_specs=[_rows(tr, V_W), _rows(tr, V_W), _rows(tr, V_W, C_Z // V_W), _whole((1, HEAD_DIM))],
        out_specs=[_rows(tr, V_W), _rows(tr, V_W), _whole((1, HEAD_DIM))],
        out_shape=[SDS((T, V_W), f32), SDS((T, V_W), bf16), SDS((1, HEAD_DIM), f32)],
        name=name, compiler_params=_cp("arbitrary"),
    )(don, o, proj, hg)


def _sgu_parts(ub, vb, gain):
    gv = _gelu(vb)
    r = lax.rsqrt(jnp.mean(gv * gv, axis=-1, keepdims=True) + EPS)
    return _gelu(ub), gv, r, gv * r * gain


def _sgu_fwd(proj, gain, w_s, b_bc, name):
    T = proj.shape[0]
    C = CHUNK_B

    def body(ub_ref, vb_ref, g_ref, w_ref, b_ref, o_ref):
        row, col = _tri_masks(C)
        u, _, _, vn = _sgu_parts(ub_ref[...], vb_ref[...], g_ref[...])
        for g in range(N_GROUPS):
            gs = pl.ds(g * GROUP_DIM, GROUP_DIM)
            wg = jnp.where(row >= col, w_ref[g], 0.0)
            mixed = _dot(_b(wg), _b(vn[:, g * GROUP_DIM:(g + 1) * GROUP_DIM])) + b_ref[g]
            o_ref[:, gs] = (u[:, g * GROUP_DIM:(g + 1) * GROUP_DIM] * mixed).astype(o_ref.dtype)

    return pl.pallas_call(
        body, grid=(T // C,),
        in_specs=[_rows(C, WIDTH_B, C_UB // WIDTH_B), _rows(C, WIDTH_B, C_VB // WIDTH_B), _whole((1, WIDTH_B)),
                  _whole((N_GROUPS, C, C)), _whole((N_GROUPS, C, GROUP_DIM))],
        out_specs=_rows(C, WIDTH_B), out_shape=SDS((T, WIDTH_B), bf16), name=name, compiler_params=_cp("parallel"),
    )(proj, proj, gain, w_s, b_bc)


def _sgu_bwd(dsgu, proj, gain, w_s, b_bc, name):
    T = proj.shape[0]
    C = CHUNK_B

    def body(d_ref, ub_ref, vb_ref, g_ref, w_ref, b_ref, dub_ref, dvb_ref, dw_ref, db_ref, dg_ref):
        first = pl.program_id(0) == 0
        row, col = _tri_masks(C)
        ub, vb, gain_v = ub_ref[...], vb_ref[...], g_ref[...]
        u, gv, r, vn = _sgu_parts(ub, vb, gain_v)
        d = d_ref[...]
        dvn_parts = []
        for g in range(N_GROUPS):
            sl = slice(g * GROUP_DIM, (g + 1) * GROUP_DIM)
            wg = jnp.where(row >= col, w_ref[g], 0.0)
            vng = _b(vn[:, sl])
            mixed = _dot(_b(wg), vng) + b_ref[g]
            dub_ref[:, pl.ds(g * GROUP_DIM, GROUP_DIM)] = (d[:, sl] * mixed * _dgelu(ub[:, sl])).astype(dub_ref.dtype)
            dmix = d[:, sl] * u[:, sl]
            dmb = _b(dmix)
            dwg = jnp.where(row >= col, _dot_nt(dmb, vng), 0.0)
            dbg = jnp.sum(dmix, axis=1, keepdims=True)

            @pl.when(first)
            def _():
                dw_ref[g] = dwg
                db_ref[g] = dbg

            @pl.when(jnp.logical_not(first))
            def _():
                dw_ref[g] += dwg
                db_ref[g] += dbg

            dvn_parts.append(_dot_tn(_b(wg), dmb))
        dvn = jnp.concatenate(dvn_parts, axis=1)
        dy = dvn * gain_v
        m = jnp.mean(dy * gv, axis=-1, keepdims=True)
        dgv = r * dy - gv * (r * r * r * m)
        dvb_ref[...] = (dgv * _dgelu(vb)).astype(dvb_ref.dtype)
        dgain = jnp.sum(dvn * gv * r, axis=0, keepdims=True)

        @pl.when(first)
        def _():
            dg_ref[...] = dgain

        @pl.when(jnp.logical_not(first))
        def _():
            dg_ref[...] += dgain

    return pl.pallas_call(
        body, grid=(T // C,),
        in_specs=[_rows(C, WIDTH_B), _rows(C, WIDTH_B, C_UB // WIDTH_B), _rows(C, WIDTH_B, C_VB // WIDTH_B),
                  _whole((1, WIDTH_B)), _whole((N_GROUPS, C, C)), _whole((N_GROUPS, C, GROUP_DIM))],
        out_specs=[_rows(C, WIDTH_B), _rows(C, WIDTH_B), _whole((N_GROUPS, C, C)), _whole((N_GROUPS, C, 1)),
                   _whole((1, WIDTH_B))],
        out_shape=[SDS((T, WIDTH_B), bf16), SDS((T, WIDTH_B), bf16), SDS((N_GROUPS, C, C), f32),
                   SDS((N_GROUPS, C, 1), f32), SDS((1, WIDTH_B), f32)],
        name=name, compiler_params=_cp("arbitrary"),
    )(dsgu, proj, proj, gain, w_s, b_bc)


def _merge_fwd(proj, ya, yb, name):
    T = proj.shape[0]
    tr = _divisor(T, 256, 16)

    def body(ga_ref, gb_ref, ya_ref, yb_ref, o_ref):
        o_ref[...] = (_sig(ga_ref[...]) * ya_ref[...] + _sig(gb_ref[...]) * yb_ref[...]).astype(o_ref.dtype)

    return pl.pallas_call(
        body, grid=(T // tr,),
        in_specs=[_rows(tr, D_MODEL, C_GA // D_MODEL), _rows(tr, D_MODEL, C_GB // D_MODEL), _rows(tr, D_MODEL), _rows(tr, D_MODEL)],
        out_specs=_rows(tr, D_MODEL), out_shape=SDS((T, D_MODEL), bf16), name=name, compiler_params=_cp("parallel"),
    )(proj, proj, ya, yb)


def _merge_bwd(dm, proj, ya, yb, name):
    T = proj.shape[0]
    tr = _divisor(T, 256, 16)

    def body(dm_ref, ga_ref, gb_ref, ya_ref, yb_ref, dya_ref, dyb_ref, dga_ref, dgb_ref):
        d = dm_ref[...]
        sa, sb = _sig(ga_ref[...]), _sig(gb_ref[...])
        dya_ref[...] = (d * sa).astype(bf16)
        dyb_ref[...] = (d * sb).astype(bf16)
        dga_ref[...] = (d * ya_ref[...] * sa * (1.0 - sa)).astype(bf16)
        dgb_ref[...] = (d * yb_ref[...] * sb * (1.0 - sb)).astype(bf16)

    return pl.pallas_call(
        body, grid=(T // tr,),
        in_specs=[_rows(tr, D_MODEL), _rows(tr, D_MODEL, C_GA // D_MODEL), _rows(tr, D_MODEL, C_GB // D_MODEL),
                  _rows(tr, D_MODEL), _rows(tr, D_MODEL)],
        out_specs=[_rows(tr, D_MODEL)] * 4, out_shape=[SDS((T, D_MODEL), bf16)] * 4, name=name,
        compiler_params=_cp("parallel"),
    )(dm, proj, proj, ya, yb)


def _ffnconv_fwd(upg, upv, wg, wv, bg, bv, name):
    T, F = upg.shape
    R = min(ROW_CHUNK, T)
    n_chunks = T // R

    def body(g_ref, v_ref, wg_ref, wv_ref, bg_ref, bv_ref, o_ref):
        wgv, wvv = _w_rows(wg_ref, 3), _w_rows(wv_ref, 3)

        def chunk(ci, carry):
            r0 = pl.multiple_of(ci * R, R)
            cg = _causal_conv(g_ref[pl.ds(r0, R), :], _prev8(g_ref, r0, ci), wgv, 3) + bg_ref[...]
            cv = _causal_conv(v_ref[pl.ds(r0, R), :], _prev8(v_ref, r0, ci), wvv, 3) + bv_ref[...]
            o_ref[pl.ds(r0, R), :] = (_silu(cg) * cv).astype(o_ref.dtype)
            return carry

        lax.fori_loop(0, n_chunks, chunk, 0)

    col = lambda p: (0, p)
    return pl.pallas_call(
        body, grid=(F // LANES,),
        in_specs=[pl.BlockSpec((T, LANES), col)] * 2 + [pl.BlockSpec((3, LANES), col)] * 2 + [pl.BlockSpec((1, LANES), col)] * 2,
        out_specs=pl.BlockSpec((T, LANES), col), out_shape=SDS((T, F), bf16), name=name, compiler_params=_cp("parallel"),
    )(upg, upv, wg, wv, bg, bv)


def _ffnconv_bwd(dact, upg, upv, wg, wv, bg, bv, name):
    T, F = upg.shape
    R = min(ROW_CHUNK, T)
    n_chunks = T // R

    def body(d_ref, g_ref, v_ref, wg_ref, wv_ref, bg_ref, bv_ref,
             dg_ref, dv_ref, dwg_ref, dwv_ref, dbg_ref, dbv_ref, sg, sv):
        wgv, wvv = _w_rows(wg_ref, 3), _w_rows(wv_ref, 3)

        def phase1(ci, carry):
            dwg, dwv = carry
            r0 = pl.multiple_of(ci * R, R)
            gcur, vcur = g_ref[pl.ds(r0, R), :], v_ref[pl.ds(r0, R), :]
            gp, vp = _prev8(g_ref, r0, ci), _prev8(v_ref, r0, ci)
            gsh = [gcur] + [_shift_down(gcur, gp, s) for s in (1, 2)]
            vsh = [vcur] + [_shift_down(vcur, vp, s) for s in (1, 2)]
            cg = gsh[0] * wgv[2] + gsh[1] * wgv[1] + gsh[2] * wgv[0] + bg_ref[...]
            cv = vsh[0] * wvv[2] + vsh[1] * wvv[1] + vsh[2] * wvv[0] + bv_ref[...]
            d = d_ref[pl.ds(r0, R), :]
            dcv = d * _silu(cg)
            dcg = d * cv * _dsilu(cg)
            sg[pl.ds(r0, R), :] = dcg
            sv[pl.ds(r0, R), :] = dcv
            rg = [jnp.sum(dcg * gsh[2 - j], axis=0, keepdims=True) for j in range(3)] + [jnp.sum(dcg, axis=0, keepdims=True)]
            rv = [jnp.sum(dcv * vsh[2 - j], axis=0, keepdims=True) for j in range(3)] + [jnp.sum(dcv, axis=0, keepdims=True)]
            return tuple(a + b for a, b in zip(dwg, rg)), tuple(a + b for a, b in zip(dwv, rv))

        z4 = tuple(jnp.zeros((1, LANES), f32) for _ in range(4))
        dwg, dwv = lax.fori_loop(0, n_chunks, phase1, (z4, z4))
        for j in range(3):
            dwg_ref[j:j + 1, :] = dwg[j]
            dwv_ref[j:j + 1, :] = dwv[j]
        dbg_ref[...] = dwg[3]
        dbv_ref[...] = dwv[3]

        def phase2(ci, carry):
            r0 = pl.multiple_of(ci * R, R)
            for sc, wv_, out in ((sg, wgv, dg_ref), (sv, wvv, dv_ref)):
                cur = sc[pl.ds(r0, R), :]
                n8 = _next8(sc, r0, R, ci, n_chunks)
                acc = cur * wv_[2] + _shift_up(cur, n8, 1) * wv_[1] + _shift_up(cur, n8, 2) * wv_[0]
                out[pl.ds(r0, R), :] = acc.astype(out.dtype)
            return carry

        lax.fori_loop(0, n_chunks, phase2, 0)

    col = lambda p: (0, p)
    big, w3, b1 = pl.BlockSpec((T, LANES), col), pl.BlockSpec((3, LANES), col), pl.BlockSpec((1, LANES), col)
    return pl.pallas_call(
        body, grid=(F // LANES,), in_specs=[big, big, big, w3, w3, b1, b1], out_specs=[big, big, w3, w3, b1, b1],
        out_shape=[SDS((T, F), bf16), SDS((T, F), bf16), SDS((3, F), f32), SDS((3, F), f32), SDS((1, F), f32), SDS((1, F), f32)],
        scratch_shapes=[pltpu.VMEM((T, LANES), f32), pltpu.VMEM((T, LANES), f32)], name=name, compiler_params=_cp("parallel"),
    )(dact, upg, upv, wg, wv, bg, bv)


def _loss_head(x, gain, target, name):
    T, Dm = x.shape
    tr = _divisor(T, 256, 16)

    def body(x_ref, g_ref, t_ref, l_ref, dx_ref, dg_ref):
        xv, g = x_ref[...], g_ref[...]
        r = lax.rsqrt(jnp.mean(xv * xv, axis=-1, keepdims=True) + EPS)
        err = xv * r * g - t_ref[...]
        part_l = 0.5 * jnp.sum(jnp.mean(err * err, axis=-1, keepdims=True), axis=0, keepdims=True)
        dy = err * (1.0 / Dm)
        dyg = dy * g
        m = jnp.mean(dyg * xv, axis=-1, keepdims=True)
        dx_ref[...] = r * dyg - xv * (r * r * r * m)
        part_g = jnp.sum(dy * xv * r, axis=0, keepdims=True)
        part_l = jnp.broadcast_to(part_l, (1, LANES))

        @pl.when(pl.program_id(0) == 0)
        def _():
            l_ref[...] = part_l
            dg_ref[...] = part_g

        @pl.when(pl.program_id(0) > 0)
        def _():
            l_ref[...] += part_l
            dg_ref[...] += part_g

    return pl.pallas_call(
        body, grid=(T // tr,), in_specs=[_rows(tr, Dm), _whole((1, Dm)), _rows(tr, Dm)],
        out_specs=[_whole((1, LANES)), _rows(tr, Dm), _whole((1, Dm))],
        out_shape=[SDS((1, LANES), f32), SDS((T, Dm), f32), SDS((1, Dm), f32)], name=name, compiler_params=_cp("arbitrary"),
    )(x, gain, target)


def _lane_row(vec, offset):
    return jnp.pad(vec.astype(f32), (offset, LANES - offset - vec.shape[0]))[None]


def _layer_fwd(x, p_i, W, S, li):
    nm = lambda s: f"{s}_l{li}"
    sv = {"x0": x}
    h1 = _rms_fwd(x, S["norm_mix"], nm("rms_mix"))
    proj = _mm(h1, W["in_main"], "nn", nm("proj_main"))
    ba = _mm(h1, W["in_ba"], "nn", nm("proj_ba"))
    qkv = _qkvconv_fwd(proj, S["conv_qkv"], nm("qkvconv"))
    qd, kd, u, w, attn, tinv, cd, _ = _delta_prep(qkv, ba, S["ea_row"], S["dtb_row"], nm("delta_prep"))
    o, s_all, vn = _delta_scan(qd, kd, u, w, attn, cd, nm("delta_scan"))
    on = _onorm_fwd(o, proj, S["head_norm"], nm("onorm"))
    ya = _mm(on, W["branch_a"], "nn", nm("branch_a"))
    sgu = _sgu_fwd(proj, S["sgu_norm"], S["w_spatial"], S["b_bc"], nm("sgu"))
    yb = _mm(sgu, W["branch_b"], "nn", nm("branch_b"))
    merged = _merge_fwd(proj, ya, yb, nm("merge"))
    x1 = _mm(merged, W["out"], "nn", nm("out_proj"), c=x)
    h2 = _rms_fwd(x1, S["norm_ffn"], nm("rms_ffn"))
    upg = _mm(h2, W["up_g"], "nn", nm("ffn_up_g"))
    upv = _mm(h2, W["up_v"], "nn", nm("ffn_up_v"))
    act = _ffnconv_fwd(upg, upv, S["conv_g"], S["conv_v"], S["bias_g"], S["bias_v"], nm("ffnconv"))
    x2 = _mm(act, W["down"], "nn", nm("ffn_down"), c=x1, tk=2816)
    h3 = _rms_fwd(x2, S["norm_ple"], nm("rms_ple"))
    gl = _mm(h3, W["ple_gate"], "nn", nm("ple_gate"))
    pp = _mm(p_i, W["ple_proj"], "nn", nm("ple_proj"))
    (x3,) = _ew(lambda a, g, q: (a + _sig(g) * q,), [x2, gl, pp], [f32], nm("ple_mix"))
    sv.update(h1=h1, proj=proj, ba=ba, qkv=qkv, qd=qd, kd=kd, u=u, w=w, attn=attn, tinv=tinv, cd=cd, o=o, s_all=s_all,
              vn=vn, on=on, ya=ya, sgu=sgu, yb=yb, merged=merged, x1=x1, h2=h2, upg=upg, upv=upv, act=act, x2=x2,
              h3=h3, gl=gl, pp=pp, p=p_i)
    return x3, sv


def _layer_bwd(dx3, sv, W, S, li):
    nm = lambda s: f"{s}_l{li}"
    G = {}
    dgl, dpp = _ew(lambda d, g, q: ((lambda s: (d * q * s * (1.0 - s), d * s))(_sig(g))),
                   [dx3, sv["gl"], sv["pp"]], [bf16, bf16], nm("ple_mix_bwd"))
    G["w_ple_gate"] = _mm(sv["h3"], dgl, "tn", nm("d_ple_gate"))
    G["w_ple_proj"] = _mm(sv["p"], dpp, "tn", nm("d_ple_proj"))
    dh3 = _mm(dgl, W["ple_gate"], "nt", nm("dh_ple"))
    dx2, G["norm_ple"] = _rms_bwd(sv["x2"], S["norm_ple"], dh3, dx3, nm("rms_ple_bwd"))
    dact = _mm(dx2, W["down"], "nt", nm("d_act"))
    G["w_ffn_down"] = _mm(sv["act"], dx2, "tn", nm("d_ffn_down"))
    dupg, dupv, dcg, dcv, dbg, dbv = _ffnconv_bwd(dact, sv["upg"], sv["upv"], S["conv_g"], S["conv_v"], S["bias_g"],
                                                  S["bias_v"], nm("ffnconv_bwd"))
    G["conv_ffn"] = jnp.concatenate([dcg, dcv], axis=1)
    G["b_conv_ffn"] = jnp.concatenate([dbg, dbv], axis=1)
    G["up_g"] = _mm(sv["h2"], dupg, "tn", nm("d_ffn_up_g"))
    G["up_v"] = _mm(sv["h2"], dupv, "tn", nm("d_ffn_up_v"))
    dh2 = _mm(dupg, W["up_g"], "nt", nm("dh_ffn_g"), tk=2816)
    dh2 = _mm(dupv, W["up_v"], "nt", nm("dh_ffn_v"), c=dh2, tk=2816)
    dx1, G["norm_ffn"] = _rms_bwd(sv["x1"], S["norm_ffn"], dh2, dx2, nm("rms_ffn_bwd"))
    dmerged = _mm(dx1, W["out"], "nt", nm("d_merged"))
    G["w_out"] = _mm(sv["merged"], dx1, "tn", nm("d_w_out"))
    dya, dyb, dga, dgb = _merge_bwd(dmerged, sv["proj"], sv["ya"], sv["yb"], nm("merge_bwd"))
    G["w_branch_a"] = _mm(sv["on"], dya, "tn", nm("d_branch_a"))
    G["w_branch_b"] = _mm(sv["sgu"], dyb, "tn", nm("d_branch_b"))
    don = _mm(dya, W["branch_a"], "nt", nm("d_on"))
    dsgu = _mm(dyb, W["branch_b"], "nt", nm("d_sgu"))
    dub, dvb, G["w_spatial"], db_s, G["sgu_norm"] = _sgu_bwd(dsgu, sv["proj"], S["sgu_norm"], S["w_spatial"], S["b_bc"], nm("sgu_bwd"))
    G["b_spatial"] = db_s.reshape(N_GROUPS, CHUNK_B)
    do, dz, G["head_norm"] = _onorm_bwd(don, sv["o"], sv["proj"], S["head_norm"], nm("onorm_bwd"))
    dqd, dkd, du, dw, dattn, dcd = _delta_scan_bwd(do, sv["qd"], sv["kd"], sv["w"], sv["attn"], sv["cd"], sv["s_all"], sv["vn"],
                                                   nm("delta_scan_bwd"))
    dqkv_n, dba, dalog_row, ddtb_row = _delta_prep_bwd(sv["qkv"], sv["ba"], S["ea_row"], S["dtb_row"], sv["tinv"], sv["u"], sv["w"],
                                                       dqd, dkd, du, dw, dattn, dcd, nm("delta_prep_bwd"))
    G["a_log"] = dalog_row[0, N_V:2 * N_V]
    G["dt_bias"] = ddtb_row[0, N_V:2 * N_V]
    dqkv, G["conv_qkv"] = _qkvconv_bwd(sv["proj"], S["conv_qkv"], dqkv_n, nm("qkvconv_bwd"))
    dproj = jnp.concatenate([dqkv, dz, dub, dvb, dga, dgb], axis=1)
    G["in_main"] = _mm(sv["h1"], dproj, "tn", nm("d_in_main"))
    G["in_ba"] = _mm(sv["h1"], dba, "tn", nm("d_in_ba"))
    dh1 = _mm(dba, W["in_ba"], "nt", nm("dh_mix_ba"))
    dh1 = _mm(dproj, W["in_main"], "nt", nm("dh_mix"), c=dh1)
    dx0, G["norm_mix"] = _rms_bwd(sv["x0"], S["norm_mix"], dh1, dx1, nm("rms_mix_bwd"))
    return dx0, G


_BA0 = C_UB
_BA1 = C_UB + 2 * N_V


def _cols(segments, lo, hi):
    out = []
    for start, a in segments:
        s_lo, s_hi = max(lo, start), min(hi, start + a.shape[1])
        if s_lo < s_hi:
            out.append(a[:, s_lo - start:s_hi - start])
    return out


def _big_weights(sh):
    n_in = N_IN // N_CHIPS
    w_in = [(j * n_in, a) for j, a in enumerate(sh["w_in"])]
    rows = lambda k: jnp.concatenate(sh[k], axis=0)
    cols = lambda parts: jnp.concatenate(parts, axis=1)
    ba = jnp.pad(cols(_cols(w_in, _BA0, _BA1)), ((0, 0), (0, LANES - 2 * N_V)))
    return dict(
        in_main=cols(_cols(w_in, 0, _BA0) + _cols(w_in, _BA1, N_IN)), in_ba=ba,
        branch_a=rows("w_branch_a"), branch_b=cols(sh["w_branch_b"]), out=rows("w_out"),
        up_g=cols(sh["w_ffn_up"][:2]), up_v=cols(sh["w_ffn_up"][2:]), down=rows("w_ffn_down"),
        ple_gate=rows("w_ple_gate"), ple_proj=cols(sh["w_ple_proj"]))


def _grads_by_chip(G):
    n_in = N_IN // N_CHIPS
    w_in = [(0, G["in_main"][:, :_BA0]), (_BA0, G["in_ba"][:, :2 * N_V]), (_BA1, G["in_main"][:, _BA0:])]
    split = lambda g: jnp.stack(jnp.split(g, N_CHIPS, axis=1))
    by_rows = lambda g: g.reshape(N_CHIPS, -1, g.shape[1])
    return dict(
        w_in=jnp.stack([jnp.concatenate(_cols(w_in, j * n_in, (j + 1) * n_in), axis=1) for j in range(N_CHIPS)]),
        w_branch_a=by_rows(G["w_branch_a"]), w_branch_b=split(G["w_branch_b"]), w_out=by_rows(G["w_out"]),
        w_ffn_up=jnp.stack(jnp.split(G["up_g"], 2, axis=1) + jnp.split(G["up_v"], 2, axis=1)),
        w_ffn_down=by_rows(G["w_ffn_down"]), w_ple_gate=by_rows(G["w_ple_gate"]), w_ple_proj=split(G["w_ple_proj"]))


def _small_params(sm, i):
    return dict(
        norm_mix=sm["norm_mix"][i][None], conv_qkv=sm["conv_qkv"][i], ea_row=_lane_row(jnp.exp(sm["a_log"][i]), N_V),
        dtb_row=_lane_row(sm["dt_bias"][i], N_V), head_norm=sm["head_norm"][i][None], sgu_norm=sm["sgu_norm"][i][None],
        w_spatial=sm["w_spatial"][i],
        b_bc=jnp.broadcast_to(sm["b_spatial"][i][:, :, None], (N_GROUPS, CHUNK_B, GROUP_DIM)),
        norm_ffn=sm["norm_ffn"][i][None], conv_g=sm["conv_ffn"][i][:, :D_FF], conv_v=sm["conv_ffn"][i][:, D_FF:],
        bias_g=sm["b_conv_ffn"][i][None, :D_FF], bias_v=sm["b_conv_ffn"][i][None, D_FF:], norm_ple=sm["norm_ple"][i][None])


MESH = pl.DeviceIdType.MESH
N_CHIPS = 4
ANY = pl.BlockSpec(memory_space=pl.ANY)
ROW_ALIGN = 32


def _place():
    x, y, c = lax.axis_index("x"), lax.axis_index("y"), lax.axis_index("c")
    chips = [(1 - x, y), (x, 1 - y), (1 - x, 1 - y)]
    return x, y, c, 2 * x + y, chips, (x, y, 1 - c)


def _halves(rows, c):
    h = rows // 2
    return pl.ds(pl.multiple_of(c * h, 16), h), pl.ds(pl.multiple_of((1 - c) * h, 16), h)


def _remote(src, dst, send_sems, recv_sems, k, dev):
    return pltpu.make_async_remote_copy(src_ref=src, dst_ref=dst, send_sem=send_sems.at[k], recv_sem=recv_sems.at[k],
                                        device_id=dev, device_id_type=MESH)


def _gather_shards(bufs, name):
    n = len(bufs)

    def body(*refs):
        ins, outs = refs[:n], refs[n:2 * n]
        send_sems, recv_sems = refs[2 * n:]
        x, y, c, me, chips, sib = _place()
        for i in range(n):
            mine, _ = _halves(ins[i].shape[0], c)
            for k, (px, py) in enumerate(chips):
                _remote(ins[i].at[mine], outs[i].at[me, mine], send_sems, recv_sems, 6 * i + k, (px, py, c)).start()
        for i in range(n):
            mine, _ = _halves(ins[i].shape[0], c)
            for k, (px, py) in enumerate(chips):
                pc = 2 * px + py
                _remote(ins[i].at[mine], outs[i].at[pc, mine], send_sems, recv_sems, 6 * i + k, (px, py, c)).wait_recv()
                _remote(outs[i].at[pc, mine], outs[i].at[pc, mine], send_sems, recv_sems, 6 * i + 3 + k, sib).start()
        for i in range(n):
            mine, other = _halves(ins[i].shape[0], c)
            for k, (px, py) in enumerate(chips):
                pc = 2 * px + py
                _remote(outs[i].at[pc, mine], outs[i].at[pc, other], send_sems, recv_sems, 6 * i + 3 + k, sib).wait_recv()
        for i in range(n):
            mine, _ = _halves(ins[i].shape[0], c)
            for k, (px, py) in enumerate(chips):
                pc = 2 * px + py
                _remote(ins[i].at[mine], outs[i].at[me, mine], send_sems, recv_sems, 6 * i + k, (px, py, c)).wait_send()
                _remote(outs[i].at[pc, mine], outs[i].at[pc, mine], send_sems, recv_sems, 6 * i + 3 + k, sib).wait_send()

    return pl.pallas_call(
        body, in_specs=[ANY] * n, out_specs=[ANY] * n,
        out_shape=[SDS((N_CHIPS,) + b.shape, b.dtype) for b in bufs],
        scratch_shapes=[pltpu.SemaphoreType.DMA((6 * n,)), pltpu.SemaphoreType.DMA((6 * n,))],
        name=name,
    )(*bufs)


def _exchange(srcs_of, out_shapes, n_sems, name):
    n = len(out_shapes)

    def body(*refs):
        n_in = len(refs) - n - 2
        ins, outs = refs[:n_in], refs[n_in:n_in + n]
        send_sems, recv_sems = refs[-2:]
        copies = [_remote(s, d, send_sems, recv_sems, k, dev) for s, d, k, dev in srcs_of(ins, outs, _place())]
        for cp in copies:
            cp.start()
        for cp in copies:
            cp.wait()

    def call(*arrs):
        return pl.pallas_call(
            body, in_specs=[ANY] * len(arrs), out_specs=[ANY] * n, out_shape=out_shapes,
            scratch_shapes=[pltpu.SemaphoreType.DMA((n_sems,)), pltpu.SemaphoreType.DMA((n_sems,))], name=name,
        )(*arrs)

    return call


def _pick(idx, parts):
    out = parts[-1]
    for j in reversed(range(len(parts) - 1)):
        out = jnp.where(idx == j, parts[j], out)
    return out


def _rs_pair(gs, name):
    def copies(ins, outs, place):
        x, y, c, me, chips, sib = place
        return [(g.at[:, _halves(g.shape[1], c)[1]], o, i, sib) for i, (g, o) in enumerate(zip(ins, outs))]

    shapes = [SDS((N_CHIPS, g.shape[1] // 2, g.shape[2]), g.dtype) for g in gs]
    return _exchange(copies, shapes, len(gs), name)(*gs)


def _rs_chips(ps, name):
    def copies(ins, outs, place):
        x, y, c, me, chips, sib = place
        return [(p.at[2 * px + py], o.at[k], 3 * i + k, (px, py, c))
                for i, (p, o) in enumerate(zip(ins, outs)) for k, (px, py) in enumerate(chips)]

    shapes = [SDS((3,) + p.shape[1:], p.dtype) for p in ps]
    return _exchange(copies, shapes, 3 * len(ps), name)(*ps)


def _rs_join(qs, name):
    def copies(ins, outs, place):
        return [(q, o, i, place[5]) for i, (q, o) in enumerate(zip(ins, outs))]

    return _exchange(copies, [SDS(q.shape, q.dtype) for q in qs], len(qs), name)(*qs)


def _reduce_scatter(gs, names, chip, core, tag):
    gots = _rs_pair(gs, f"rs_pair_{tag}")
    ps = []
    for g, got, k in zip(gs, gots, names):
        h = g.shape[1] // 2
        mine = jnp.where(core == 0, g[:, :h], g[:, h:])
        (p,) = _ew(lambda a, b: (a + b,), [mine.reshape(-1, g.shape[2]), got.reshape(-1, g.shape[2])], [f32], f"rs_add2_{k}_{tag}")
        ps.append(p.reshape(got.shape))
    parts = _rs_chips(ps, f"rs_chips_{tag}")
    qs = []
    for p, part, k in zip(ps, parts, names):
        own = _pick(chip, [p[j] for j in range(N_CHIPS)])
        (q,) = _ew(lambda a, b, c, d: ((a + c) + (b + d),), [own, part[0], part[1], part[2]], [f32], f"rs_add4_{k}_{tag}")
        qs.append(q)
    others = _rs_join(qs, f"rs_join_{tag}")
    return [jnp.where(core == 0, jnp.concatenate([q, o], axis=0), jnp.concatenate([o, q], axis=0)) for q, o in zip(qs, others)]


def _allreduce_small(buf, name):
    R, L = buf.shape

    def body(x_ref, o_ref, r0, s1, r1, send_sems, recv_sems):
        x, y, c, me, chips, sib = _place()
        cp = _remote(x_ref, r0, send_sems, recv_sems, 0, sib)
        cp.start()
        cp.wait()
        s1[...] = x_ref[...] + r0[...]
        cps = []
        for k, (px, py) in enumerate(chips):
            cp = _remote(s1, r1.at[k], send_sems, recv_sems, 1 + k, (px, py, c))
            cp.start()
            cps.append(cp)
        for cp in cps:
            cp.wait()
        o_ref[...] = (s1[...] + r1[1]) + (r1[0] + r1[2])

    vm = pl.BlockSpec(memory_space=pltpu.VMEM)
    return pl.pallas_call(
        body, in_specs=[vm], out_specs=vm, out_shape=SDS((R, L), f32),
        scratch_shapes=[pltpu.VMEM((R, L), f32), pltpu.VMEM((R, L), f32), pltpu.VMEM((3, R, L), f32),
                        pltpu.SemaphoreType.DMA((4,)), pltpu.SemaphoreType.DMA((4,))],
        name=name, compiler_params=pltpu.CompilerParams(vmem_limit_bytes=VMEM_LIMIT_BYTES),
    )(buf)


BIG = ("w_in", "w_branch_a", "w_branch_b", "w_out", "w_ffn_up", "w_ffn_down", "w_ple_gate", "w_ple_proj")
SMALL_REPL = ("norm_mix", "a_log", "dt_bias", "head_norm", "sgu_norm", "w_spatial", "b_spatial", "norm_ffn", "b_conv_ffn",
              "norm_ple", "norm_final")
SMALL_COLS = ("conv_qkv", "conv_ffn")
WEIGHTS = ("norm_mix", "w_in", "conv_qkv", "a_log", "dt_bias", "head_norm", "sgu_norm", "w_spatial", "b_spatial", "w_branch_a",
           "w_branch_b", "w_out", "norm_ffn", "w_ffn_up", "conv_ffn", "b_conv_ffn", "w_ffn_down", "norm_ple", "w_ple_gate",
           "w_ple_proj", "norm_final")


def _flat(arrs, dtype):
    cat = jnp.concatenate([a.astype(dtype).reshape(-1) for a in arrs])
    unit = LANES * ROW_ALIGN
    cat = jnp.pad(cat, (0, -cat.shape[0] % unit))
    return cat.reshape(-1, LANES)


def _unflat(buf, shapes):
    flat = buf.reshape(-1)
    out, off = [], 0
    for s in shapes:
        n = math.prod(s)
        out.append(flat[off:off + n].reshape(s))
        off += n
    return out


def _adamw(w, g, m, v):
    m2 = ADAM_B1 * m + (1.0 - ADAM_B1) * g
    v2 = ADAM_B2 * v + (1.0 - ADAM_B2) * (g * g)
    m_hat = m2 * (1.0 / (1.0 - ADAM_B1 ** ADAM_STEP))
    v_hat = v2 * (1.0 / (1.0 - ADAM_B2 ** ADAM_STEP))
    delta = -ADAM_LR * (m_hat / (jnp.sqrt(v_hat) + ADAM_EPS) + ADAM_WD * w)
    return delta, m2, v2


def kernel(x, p, norm_mix, w_in, conv_qkv, a_log, dt_bias, head_norm, sgu_norm, w_spatial, b_spatial, w_branch_a, w_branch_b, w_out, norm_ffn, w_ffn_up, conv_ffn, b_conv_ffn, w_ffn_down, norm_ple, w_ple_gate, w_ple_proj, norm_final, loss_target, m_norm_mix, m_w_in, m_conv_qkv, m_a_log, m_dt_bias, m_head_norm, m_sgu_norm, m_w_spatial, m_b_spatial, m_w_branch_a, m_w_branch_b, m_w_out, m_norm_ffn, m_w_ffn_up, m_conv_ffn, m_b_conv_ffn, m_w_ffn_down, m_norm_ple, m_w_ple_gate, m_w_ple_proj, m_norm_final, v_norm_mix, v_w_in, v_conv_qkv, v_a_log, v_dt_bias, v_head_norm, v_sgu_norm, v_w_spatial, v_b_spatial, v_w_branch_a, v_w_branch_b, v_w_out, v_norm_ffn, v_w_ffn_up, v_conv_ffn, v_b_conv_ffn, v_w_ffn_down, v_norm_ple, v_w_ple_gate, v_w_ple_proj, v_norm_final):
    w = dict(norm_mix=norm_mix, w_in=w_in, conv_qkv=conv_qkv, a_log=a_log, dt_bias=dt_bias, head_norm=head_norm, sgu_norm=sgu_norm,
             w_spatial=w_spatial, b_spatial=b_spatial, w_branch_a=w_branch_a, w_branch_b=w_branch_b, w_out=w_out, norm_ffn=norm_ffn,
             w_ffn_up=w_ffn_up, conv_ffn=conv_ffn, b_conv_ffn=b_conv_ffn, w_ffn_down=w_ffn_down, norm_ple=norm_ple,
             w_ple_gate=w_ple_gate, w_ple_proj=w_ple_proj, norm_final=norm_final)
    m = dict(norm_mix=m_norm_mix, w_in=m_w_in, conv_qkv=m_conv_qkv, a_log=m_a_log, dt_bias=m_dt_bias, head_norm=m_head_norm,
             sgu_norm=m_sgu_norm, w_spatial=m_w_spatial, b_spatial=m_b_spatial, w_branch_a=m_w_branch_a, w_branch_b=m_w_branch_b,
             w_out=m_w_out, norm_ffn=m_norm_ffn, w_ffn_up=m_w_ffn_up, conv_ffn=m_conv_ffn, b_conv_ffn=m_b_conv_ffn,
             w_ffn_down=m_w_ffn_down, norm_ple=m_norm_ple, w_ple_gate=m_w_ple_gate, w_ple_proj=m_w_ple_proj, norm_final=m_norm_final)
    v = dict(norm_mix=v_norm_mix, w_in=v_w_in, conv_qkv=v_conv_qkv, a_log=v_a_log, dt_bias=v_dt_bias, head_norm=v_head_norm,
             sgu_norm=v_sgu_norm, w_spatial=v_w_spatial, b_spatial=v_b_spatial, w_branch_a=v_w_branch_a, w_branch_b=v_w_branch_b,
             w_out=v_w_out, norm_ffn=v_norm_ffn, w_ffn_up=v_w_ffn_up, conv_ffn=v_conv_ffn, b_conv_ffn=v_b_conv_ffn,
             w_ffn_down=v_w_ffn_down, norm_ple=v_norm_ple, w_ple_gate=v_w_ple_gate, w_ple_proj=v_w_ple_proj, norm_final=v_norm_final)
    chip = 2 * lax.axis_index("x") + lax.axis_index("y")
    core = lax.axis_index("c")

    def by_chip(own, gathered):
        return [jnp.where(chip == j, own, gathered[j]) for j in range(N_CHIPS)]

    conv_buf = _flat([w[k] for k in SMALL_COLS], f32)
    (conv_all,) = _gather_shards([conv_buf], "gather_conv")
    sm = {k: w[k] for k in SMALL_REPL}
    conv_parts = [_unflat(b, [w[k].shape for k in SMALL_COLS]) for b in by_chip(conv_buf, conv_all)]
    for n, k in enumerate(SMALL_COLS):
        sm[k] = jnp.concatenate([conv_parts[j][n] for j in range(N_CHIPS)], axis=-1)
    Ws = []
    for i in range(DEPTH):
        own = [w[k][i].astype(bf16) for k in BIG]
        gathered = _gather_shards(own, f"gather_l{i}")
        Ws.append(_big_weights({k: by_chip(o, g) for k, o, g in zip(BIG, own, gathered)}))
    Ss = [_small_params(sm, i) for i in range(DEPTH)]

    h = x[0]
    saved = []
    for i in range(DEPTH):
        h, sv = _layer_fwd(h, p[i, 0], Ws[i], Ss[i], i)
        saved.append(sv)
    loss_row, dh, g_final = _loss_head(h, sm["norm_final"][None], loss_target[0], "loss_head")

    gsm = {k: [None] * DEPTH for k in SMALL_REPL + SMALL_COLS if k != "norm_final"}
    gbig = {k: [None] * DEPTH for k in BIG}
    for i in reversed(range(DEPTH)):
        dh, G = _layer_bwd(dh, saved[i], Ws[i], Ss[i], i)
        parts = _grads_by_chip(G)
        for k, g in zip(BIG, _reduce_scatter([parts[k] for k in BIG], BIG, chip, core, f"l{i}")):
            gbig[k][i] = g
        for k in gsm:
            gsm[k][i] = G[k].reshape(w[k].shape[1:-1] + (-1,)) if k in SMALL_COLS else G[k].reshape(w[k].shape[1:])

    small_names = [k for k in SMALL_REPL + SMALL_COLS if k != "norm_final"]
    small_local = [jnp.stack(gsm[k]) for k in small_names] + [g_final.reshape(-1), loss_row[0, :1]]
    small_shapes = [a.shape for a in small_local]
    small_sum = _unflat(_allreduce_small(_flat(small_local, f32), "allreduce_small"), small_shapes)
    gs = dict(zip(small_names + ["norm_final"], small_sum[:-1]))
    loss = small_sum[-1][0]
    for k in SMALL_COLS:
        n = w[k].shape[-1]
        gs[k] = _pick(chip, [gs[k][..., j * n:(j + 1) * n] for j in range(N_CHIPS)])

    grads, deltas, new_m, new_v = {}, {}, {}, {}
    for k in BIG:
        g = jnp.stack(gbig[k])
        two = lambda a: a.reshape(-1, a.shape[-1])
        d, m2, v2 = _ew(_adamw, [two(w[k]), two(g), two(m[k]), two(v[k])], [f32, f32, f32], f"adamw_{k}")
        grads[k], deltas[k], new_m[k], new_v[k] = g, d.reshape(g.shape), m2.reshape(g.shape), v2.reshape(g.shape)
    small_all = [k for k in WEIGHTS if k not in BIG]
    shapes = [w[k].shape for k in small_all]
    d, m2, v2 = _ew(_adamw, [_flat([w[k] for k in small_all], f32), _flat([gs[k] for k in small_all], f32),
                             _flat([m[k] for k in small_all], f32), _flat([v[k] for k in small_all], f32)],
                    [f32, f32, f32], "adamw_small")
    for k, a, b, c_ in zip(small_all, _unflat(d, shapes), _unflat(m2, shapes), _unflat(v2, shapes)):
        grads[k], deltas[k], new_m[k], new_v[k] = gs[k], a, b, c_
    return (loss, dh[None], *[grads[k] for k in WEIGHTS], *[deltas[k] for k in WEIGHTS],
            *[new_m[k] for k in WEIGHTS], *[new_v[k] for k in WEIGHTS])
```

```python
import functools
import math

import jax
import jax.numpy as jnp
from jax import lax
from jax.experimental import pallas as pl
from jax.experimental.pallas import tpu as pltpu

f32 = jnp.float32
bf16 = jnp.bfloat16
HI = lax.Precision.HIGHEST
SDS = jax.ShapeDtypeStruct

D_MODEL = 2048
DEPTH = 4
HEAD_DIM = 128
N_QK = 8
N_V = 16
QK_W = N_QK * HEAD_DIM
V_W = N_V * HEAD_DIM
CHUNK_A = 64
N_GROUPS = 8
GROUP_DIM = 128
WIDTH_B = N_GROUPS * GROUP_DIM
CHUNK_B = 128
D_FF = 5632
PLE_DIM = 256
EPS = 1e-6
N_IN = 12320
ADAM_LR, ADAM_B1, ADAM_B2, ADAM_EPS, ADAM_WD, ADAM_STEP = 0.001, 0.9, 0.999, 1e-08, 0.01, 10

C_Q, C_K, C_V, C_Z, C_UB, C_VB, C_GA, C_GB, PM = 0, 1024, 2048, 4096, 6144, 7168, 8192, 10240, 12288
LANES = 128
VMEM_LIMIT_BYTES = 48 * 1024 * 1024
ROW_CHUNK = 256


def _cp(*sem):
    return pltpu.CompilerParams(dimension_semantics=sem if sem else None, vmem_limit_bytes=VMEM_LIMIT_BYTES)


def _dot(a, b, prec=None):
    return jnp.dot(a, b, preferred_element_type=f32, precision=prec)


def _dot_nt(a, b, prec=None):
    return lax.dot_general(a, b, (((1,), (1,)), ((), ())), preferred_element_type=f32, precision=prec)


def _dot_tn(a, b, prec=None):
    return lax.dot_general(a, b, (((0,), (0,)), ((), ())), preferred_element_type=f32, precision=prec)


def _b(x):
    return x.astype(bf16)


def _sig(x):
    return 1.0 / (1.0 + jnp.exp(-x))


def _silu(x):
    return x * _sig(x)


def _dsilu(x):
    s = _sig(x)
    return s * (1.0 + x * (1.0 - s))


_GELU_C = 0.7978845608028654
_GELU_A = 0.044715


def _gelu(x):
    return 0.5 * x * (1.0 + jnp.tanh(_GELU_C * (x + _GELU_A * x * x * x)))


def _dgelu(x):
    t = jnp.tanh(_GELU_C * (x + _GELU_A * x * x * x))
    return 0.5 * (1.0 + t) + 0.5 * x * (1.0 - t * t) * _GELU_C * (1.0 + 3.0 * _GELU_A * x * x)


def _softplus(x):
    return jnp.maximum(x, 0.0) + jnp.log(1.0 + jnp.exp(-jnp.abs(x)))


def _divisor(n, cap, mult):
    best = None
    d = mult
    while d <= min(n, cap):
        if n % d == 0:
            best = d
        d += mult
    return best if best is not None else n


def _rows(tr, c, j=0):
    return pl.BlockSpec((tr, c), lambda i, j=j: (i, j))


def _whole(shape):
    nd = len(shape)
    return pl.BlockSpec(shape, lambda *_: (0,) * nd)


def _mm(a, b, mode, name, out_dtype=f32, c=None, tm=512, tn=512, tk=2048):
    if mode == "nn":
        (M, K), (K2, N) = a.shape, b.shape
    elif mode == "nt":
        (M, K), (N, K2) = a.shape, b.shape
    else:
        (K, M), (K2, N) = a.shape, b.shape
    assert K == K2, (a.shape, b.shape, mode)
    tm = _divisor(M, tm, LANES if mode == "tn" else 16)
    tn = _divisor(N, tn, LANES)
    tk = _divisor(K, tk, LANES if mode != "tn" else 16)
    nk = K // tk
    if mode == "nn":
        a_spec = pl.BlockSpec((tm, tk), lambda i, j, k: (i, k))
        b_spec = pl.BlockSpec((tk, tn), lambda i, j, k: (k, j))
        dn = (((1,), (0,)), ((), ()))
    elif mode == "nt":
        a_spec = pl.BlockSpec((tm, tk), lambda i, j, k: (i, k))
        b_spec = pl.BlockSpec((tn, tk), lambda i, j, k: (j, k))
        dn = (((1,), (1,)), ((), ()))
    else:
        a_spec = pl.BlockSpec((tk, tm), lambda i, j, k: (k, i))
        b_spec = pl.BlockSpec((tk, tn), lambda i, j, k: (k, j))
        dn = (((0,), (0,)), ((), ()))
    o_spec = pl.BlockSpec((tm, tn), lambda i, j, k: (i, j))
    has_c = c is not None

    def body(*refs):
        a_ref, b_ref = refs[0], refs[1]
        c_ref = refs[2] if has_c else None
        o_ref = refs[3] if has_c else refs[2]
        part = lax.dot_general(_b(a_ref[...]), _b(b_ref[...]), dn, preferred_element_type=f32)
        if nk == 1:
            if has_c:
                part = part + c_ref[...]
            o_ref[...] = part.astype(o_ref.dtype)
        else:
            acc = refs[-1]
            k = pl.program_id(2)

            @pl.when(k == 0)
            def _():
                acc[...] = part

            @pl.when(k > 0)
            def _():
                acc[...] += part

            @pl.when(k == nk - 1)
            def _():
                r = acc[...]
                if has_c:
                    r = r + c_ref[...]
                o_ref[...] = r.astype(o_ref.dtype)

    ins = [a, b] + ([c] if has_c else [])
    in_specs = [a_spec, b_spec] + ([o_spec] if has_c else [])
    return pl.pallas_call(
        body, grid=(M // tm, N // tn, nk), in_specs=in_specs, out_specs=o_spec, out_shape=SDS((M, N), out_dtype),
        scratch_shapes=[pltpu.VMEM((tm, tn), f32)] if nk > 1 else [], name=name,
        compiler_params=_cp("parallel", "parallel", "arbitrary"),
    )(*ins)


def _ew(fn, ins, out_dtypes, name, tile_bytes=2 * 1024 * 1024):
    R, C = max((x.shape for x in ins), key=lambda s: s[0])
    n_in = len(ins)
    row_bytes = 4 * C * (len(ins) + len(out_dtypes))
    tr = _divisor(R, max(16, tile_bytes // row_bytes), 16)
    in_specs = [_rows(tr, C) if x.shape[0] == R else _whole((1, C)) for x in ins]

    def body(*refs):
        res = fn(*[r[...] for r in refs[:n_in]])
        for o_ref, r in zip(refs[n_in:], res):
            o_ref[...] = r.astype(o_ref.dtype)

    outs = pl.pallas_call(
        body, grid=(R // tr,), in_specs=in_specs, out_specs=[_rows(tr, C) for _ in out_dtypes],
        out_shape=[SDS((R, C), dt) for dt in out_dtypes], name=name, compiler_params=_cp("parallel"),
    )(*ins)
    return outs


def _rms_fwd(x, gain, name):
    T, Dm = x.shape
    tr = _divisor(T, 256, 16)

    def body(x_ref, g_ref, o_ref):
        xv = x_ref[...]
        r = lax.rsqrt(jnp.mean(xv * xv, axis=-1, keepdims=True) + EPS)
        o_ref[...] = (xv * r * g_ref[...]).astype(o_ref.dtype)

    return pl.pallas_call(
        body, grid=(T // tr,), in_specs=[_rows(tr, Dm), _whole((1, Dm))], out_specs=_rows(tr, Dm),
        out_shape=SDS((T, Dm), bf16), name=name, compiler_params=_cp("parallel"),
    )(x, gain)


def _rms_bwd(x, gain, dh, dres, name):
    T, Dm = x.shape
    tr = _divisor(T, 256, 16)

    def body(x_ref, g_ref, dh_ref, dres_ref, dx_ref, dg_ref):
        xv = x_ref[...]
        dhv = dh_ref[...]
        r = lax.rsqrt(jnp.mean(xv * xv, axis=-1, keepdims=True) + EPS)
        dy = dhv * g_ref[...]
        m = jnp.mean(dy * xv, axis=-1, keepdims=True)
        dx_ref[...] = dres_ref[...] + r * dy - xv * (r * r * r * m)
        part = jnp.sum(dhv * xv * r, axis=0, keepdims=True)

        @pl.when(pl.program_id(0) == 0)
        def _():
            dg_ref[...] = part

        @pl.when(pl.program_id(0) > 0)
        def _():
            dg_ref[...] += part

    return pl.pallas_call(
        body, grid=(T // tr,), in_specs=[_rows(tr, Dm), _whole((1, Dm)), _rows(tr, Dm), _rows(tr, Dm)],
        out_specs=[_rows(tr, Dm), _whole((1, Dm))], out_shape=[SDS((T, Dm), f32), SDS((1, Dm), f32)],
        name=name, compiler_params=_cp("arbitrary"),
    )(x, gain, dh, dres)


def _shift_down(cur, prev8, s):
    rolled = pltpu.roll(cur, s, 0)
    rp = pltpu.roll(prev8, s, 0)
    row8 = lax.broadcasted_iota(jnp.int32, prev8.shape, 0)
    first = jnp.where(row8 < s, rp, rolled[:8])
    return jnp.concatenate([first, rolled[8:]], axis=0)


def _shift_up(cur, next8, s):
    R = cur.shape[0]
    rolled = pltpu.roll(cur, R - s, 0)
    rn = pltpu.roll(next8, 8 - s, 0)
    row8 = lax.broadcasted_iota(jnp.int32, next8.shape, 0)
    last = jnp.where(row8 >= 8 - s, rn, rolled[R - 8:])
    return jnp.concatenate([rolled[: R - 8], last], axis=0)


def _prev8(ref, r0, ci):
    rows = ref[pl.ds(pl.multiple_of(jnp.maximum(r0 - 8, 0), 8), 8), :]
    return jnp.where(ci > 0, rows, 0.0)


def _next8(ref, r0, R, ci, n_chunks):
    start = jnp.minimum(r0 + R, (n_chunks - 1) * R + R - 8)
    rows = ref[pl.ds(pl.multiple_of(start, 8), 8), :]
    return jnp.where(ci < n_chunks - 1, rows, 0.0)


def _w_rows(w_ref, k):
    return [w_ref[j:j + 1, :] for j in range(k)]


def _causal_conv(cur, prev8, w, k):
    acc = cur * w[k - 1]
    for s in range(1, k):
        acc = acc + _shift_down(cur, prev8, s) * w[k - 1 - s]
    return acc


def _qkvconv_fwd(proj, convw, name):
    T = proj.shape[0]
    R = min(ROW_CHUNK, T)
    n_chunks = T // R
    n_blk = (2 * QK_W + V_W) // HEAD_DIM

    def body(x_ref, w_ref, o_ref):
        p = pl.program_id(0)
        is_qk = p < 2 * N_QK
        scale = jnp.where(p < N_QK, HEAD_DIM ** -0.5, 1.0).astype(f32)
        w = _w_rows(w_ref, 4)

        def chunk(ci, carry):
            r0 = pl.multiple_of(ci * R, R)
            cur = x_ref[pl.ds(r0, R), :]
            y = _silu(_causal_conv(cur, _prev8(x_ref, r0, ci), w, 4))
            ss = jnp.sum(y * y, axis=-1, keepdims=True)
            nrm = jnp.where(is_qk, lax.rsqrt(ss + EPS) * scale, 1.0)
            o_ref[pl.ds(r0, R), :] = y * nrm
            return carry

        lax.fori_loop(0, n_chunks, chunk, 0)

    return pl.pallas_call(
        body, grid=(n_blk,),
        in_specs=[pl.BlockSpec((T, HEAD_DIM), lambda p: (0, p)), pl.BlockSpec((4, HEAD_DIM), lambda p: (0, p))],
        out_specs=pl.BlockSpec((T, HEAD_DIM), lambda p: (0, p)), out_shape=SDS((T, n_blk * HEAD_DIM), f32),
        name=name, compiler_params=_cp("parallel"),
    )(proj, convw)


def _qkvconv_bwd(proj, convw, dqkv, name):
    T = proj.shape[0]
    R = min(ROW_CHUNK, T)
    n_chunks = T // R
    n_blk = (2 * QK_W + V_W) // HEAD_DIM

    def body(x_ref, w_ref, do_ref, dx_ref, dw_ref, dc_sc):
        p = pl.program_id(0)
        is_qk = p < 2 * N_QK
        scale = jnp.where(p < N_QK, HEAD_DIM ** -0.5, 1.0).astype(f32)
        w = _w_rows(w_ref, 4)

        def phase1(ci, dw):
            r0 = pl.multiple_of(ci * R, R)
            cur = x_ref[pl.ds(r0, R), :]
            p8 = _prev8(x_ref, r0, ci)
            shifted = [cur] + [_shift_down(cur, p8, s) for s in range(1, 4)]
            c = shifted[0] * w[3]
            for s in range(1, 4):
                c = c + shifted[s] * w[3 - s]
            y = _silu(c)
            dout = do_ref[pl.ds(r0, R), :]
            n = lax.rsqrt(jnp.sum(y * y, axis=-1, keepdims=True) + EPS)
            dot_ = jnp.sum(dout * y, axis=-1, keepdims=True)
            dy = jnp.where(is_qk, scale * (n * dout - y * (n * n * n * dot_)), dout)
            dc = dy * _dsilu(c)
            dc_sc[pl.ds(r0, R), :] = dc
            return tuple(dw[j] + jnp.sum(dc * shifted[3 - j], axis=0, keepdims=True) for j in range(4))

        dw = lax.fori_loop(0, n_chunks, phase1, tuple(jnp.zeros((1, HEAD_DIM), f32) for _ in range(4)))
        for j in range(4):
            dw_ref[j:j + 1, :] = dw[j]

        def phase2(ci, carry):
            r0 = pl.multiple_of(ci * R, R)
            cur = dc_sc[pl.ds(r0, R), :]
            n8 = _next8(dc_sc, r0, R, ci, n_chunks)
            acc = cur * w[3]
            for s in range(1, 4):
                acc = acc + _shift_up(cur, n8, s) * w[3 - s]
            dx_ref[pl.ds(r0, R), :] = acc.astype(dx_ref.dtype)
            return carry

        lax.fori_loop(0, n_chunks, phase2, 0)

    col = lambda p: (0, p)
    return pl.pallas_call(
        body, grid=(n_blk,),
        in_specs=[pl.BlockSpec((T, HEAD_DIM), col), pl.BlockSpec((4, HEAD_DIM), col), pl.BlockSpec((T, HEAD_DIM), col)],
        out_specs=[pl.BlockSpec((T, HEAD_DIM), col), pl.BlockSpec((4, HEAD_DIM), col)],
        out_shape=[SDS((T, n_blk * HEAD_DIM), bf16), SDS((4, n_blk * HEAD_DIM), f32)],
        scratch_shapes=[pltpu.VMEM((T, HEAD_DIM), f32)], name=name, compiler_params=_cp("parallel"),
    )(proj, convw, dqkv)


def _tri_masks(C):
    row = lax.broadcasted_iota(jnp.int32, (C, C), 0)
    col = lax.broadcasted_iota(jnp.int32, (C, C), 1)
    return row, col


def _lane_pick(blk, lane, idx):
    return jnp.sum(jnp.where(lane == idx, blk, 0.0), axis=1, keepdims=True)


def _gate_block(ba, ea, dtb, lane):
    sig = _sig(ba)
    gblk = -ea * _softplus(ba + dtb)
    return sig, gblk


def _head_decay(gam_all, rg_all, tot, lane, h, row, col):
    C = row.shape[0]
    gam_c = _lane_pick(gam_all, lane, N_V + h)
    rg_c = _lane_pick(rg_all, lane, N_V + h)
    lane1 = lax.broadcasted_iota(jnp.int32, (1, LANES), 1)
    tot_h = jnp.sum(jnp.where(lane1 == N_V + h, tot, 0.0), axis=1, keepdims=True)
    gcb = jnp.broadcast_to(gam_c, (C, C))
    dlt = gcb - gcb.T
    dm = jnp.where(row >= col, jnp.exp(jnp.minimum(dlt, 0.0)), 0.0)
    return gam_c, rg_c, tot_h, dm


def _delta_prep(qkv, ba, ea_row, dtb_row, name):
    T = qkv.shape[0]
    C = CHUNK_A
    N = T // C

    SUB = 16

    def body(q_ref, k_ref, v_ref, ba_ref, ea_ref, dtb_ref,
             qd_ref, kd_ref, u_ref, w_ref, attn_ref, tinv_ref, cd_ref, bg_ref, at_sc, t_sc, a_sc, rk_sc):
        row, col = _tri_masks(C)
        lane = lax.broadcasted_iota(jnp.int32, (C, LANES), 1)
        sig, gblk = _gate_block(ba_ref[...], ea_ref[...], dtb_ref[...], lane)
        bg_ref[...] = jnp.where(lane < N_V, sig, gblk)
        lower = (row >= col).astype(f32)
        upper_s = (col > row).astype(f32)
        eye = (row == col).astype(f32)
        same16 = (row >> 4) == (col >> 4)
        same32 = (row >> 5) == (col >> 5)
        gam_all = _dot(lower, gblk, HI)
        rg_all = _dot(upper_s, gblk, HI)
        tot = jnp.sum(gblk, axis=0, keepdims=True)
        for h in range(N_V):
            qk = pl.ds((h // 2) * HEAD_DIM, HEAD_DIM)
            hs = pl.ds(h * HEAD_DIM, HEAD_DIM)
            qh, kh, vh = q_ref[:, qk], k_ref[:, qk], v_ref[:, hs]
            beta_c = _lane_pick(sig, lane, h)
            gam_c, rg_c, tot_h, dm = _head_decay(gam_all, rg_all, tot, lane, h, row, col)
            kk = _dot_nt(_b(kh), _b(kh))
            a = jnp.where(row > col, beta_c * kk * dm, 0.0)
            a_sc[h] = a
            at_sc[h] = jnp.where(same16, a, 0.0).T
            t_sc[h] = eye
            eg = jnp.exp(gam_c)
            attn_ref[0, h] = (_dot_nt(_b(qh), _b(kh)) * dm).astype(attn_ref.dtype)
            qd_ref[:, hs] = (qh * eg).astype(qd_ref.dtype)
            kd_ref[:, hs] = (kh * jnp.exp(rg_c)).astype(kd_ref.dtype)
            cd_ref[0, h] = jnp.broadcast_to(jnp.exp(tot_h), (1, LANES))
            u_ref[:, hs] = vh * beta_c
            rk_sc[:, hs] = kh * (beta_c * eg)

        first_col = (row >> 4) << 4

        def fsub(i, carry):
            for h in range(N_V):
                t = t_sc[h]
                a_col = jnp.sum(jnp.where(col == first_col + i, at_sc[h], 0.0), axis=1, keepdims=True)
                prod = a_col * t
                sums = [jnp.sum(prod[b * SUB:(b + 1) * SUB], axis=0, keepdims=True) for b in range(C // SUB)]
                new = eye - jnp.concatenate([jnp.broadcast_to(s, (SUB, C)) for s in sums], axis=0)
                t_sc[h] = jnp.where(row - first_col == i, new, t)
            return carry

        lax.fori_loop(1, SUB, fsub, 0)
        for h in range(N_V):
            hs = pl.ds(h * HEAD_DIM, HEAD_DIM)
            a = a_sc[h]
            p16 = t_sc[h]
            low16 = jnp.where(jnp.logical_and(same32, jnp.logical_not(same16)), a, 0.0)
            p32 = p16 - _dot(_dot(p16, low16, HI), p16, HI)
            tinv = p32 - _dot(_dot(p32, jnp.where(same32, 0.0, a), HI), p32, HI)
            tinv_ref[0, h] = tinv
            u_ref[:, hs] = _dot(tinv, u_ref[:, hs], HI)
            w_ref[:, hs] = _dot(tinv, rk_sc[:, hs], HI).astype(w_ref.dtype)

    big = lambda n: (n, 0)
    return pl.pallas_call(
        body, grid=(N,),
        in_specs=[pl.BlockSpec((C, QK_W), lambda n: (n, 0)), pl.BlockSpec((C, QK_W), lambda n: (n, 1)),
                  pl.BlockSpec((C, V_W), lambda n: (n, 1)), pl.BlockSpec((C, LANES), big),
                  _whole((1, LANES)), _whole((1, LANES))],
        out_specs=[pl.BlockSpec((C, V_W), big)] * 4 + [
            pl.BlockSpec((1, N_V, C, C), lambda n: (n, 0, 0, 0)), pl.BlockSpec((1, N_V, C, C), lambda n: (n, 0, 0, 0)),
            pl.BlockSpec((1, N_V, 1, LANES), lambda n: (n, 0, 0, 0)), pl.BlockSpec((C, LANES), big)],
        out_shape=[SDS((T, V_W), bf16), SDS((T, V_W), bf16), SDS((T, V_W), f32), SDS((T, V_W), bf16),
                   SDS((N, N_V, C, C), bf16), SDS((N, N_V, C, C), f32), SDS((N, N_V, 1, LANES), f32), SDS((T, LANES), f32)],
        scratch_shapes=[pltpu.VMEM((N_V, C, C), f32)] * 3 + [pltpu.VMEM((C, V_W), f32)],
        name=name, compiler_params=_cp("parallel"),
    )(qkv, qkv, qkv, ba, ea_row, dtb_row)


def _delta_scan(qd, kd, u, w, attn, cd, name):
    T = qd.shape[0]
    C = CHUNK_A
    N = T // C

    def body(qd_ref, kd_ref, u_ref, w_ref, attn_ref, cd_ref, o_ref, s_ref, vn_ref, s_sc):
        @pl.when(pl.program_id(0) == 0)
        def _():
            s_sc[...] = jnp.zeros_like(s_sc)

        for h in range(N_V):
            hs = pl.ds(h * HEAD_DIM, HEAD_DIM)
            s = s_sc[h]
            s_ref[0, h] = s
            sb = _b(s)
            vn = u_ref[:, hs] - _dot(w_ref[:, hs], sb)
            vnb = _b(vn)
            vn_ref[:, hs] = vnb
            o_ref[:, hs] = _dot(qd_ref[:, hs], sb) + _dot(attn_ref[0, h], vnb)
            s_sc[h] = s * cd_ref[0, h] + _dot_tn(kd_ref[:, hs], vnb)

    blk = pl.BlockSpec((C, V_W), lambda n: (n, 0))
    per_head = lambda a, b: pl.BlockSpec((1, N_V, a, b), lambda n: (n, 0, 0, 0))
    return pl.pallas_call(
        body, grid=(N,), in_specs=[blk, blk, blk, blk, per_head(C, C), per_head(1, LANES)],
        out_specs=[blk, per_head(HEAD_DIM, HEAD_DIM), blk],
        out_shape=[SDS((T, V_W), f32), SDS((N, N_V, HEAD_DIM, HEAD_DIM), f32), SDS((T, V_W), bf16)],
        scratch_shapes=[pltpu.VMEM((N_V, HEAD_DIM, HEAD_DIM), f32)], name=name, compiler_params=_cp("arbitrary"),
    )(qd, kd, u, w, attn, cd)


def _delta_scan_bwd(do, qd, kd, w, attn, cd, s_all, vn, name):
    T = qd.shape[0]
    C = CHUNK_A
    N = T // C

    def body(do_ref, qd_ref, kd_ref, w_ref, attn_ref, cd_ref, s_ref, vn_ref,
             dqd_ref, dkd_ref, du_ref, dw_ref, dattn_ref, dcd_ref, ds_sc):
        @pl.when(pl.program_id(0) == 0)
        def _():
            ds_sc[...] = jnp.zeros_like(ds_sc)

        row, col = _tri_masks(C)
        for h in range(N_V):
            hs = pl.ds(h * HEAD_DIM, HEAD_DIM)
            dsn = ds_sc[h]
            s = s_ref[0, h]
            dob, sb, dsb = _b(do_ref[:, hs]), _b(s), _b(dsn)
            vnb = vn_ref[:, hs]
            dqd_ref[:, hs] = _dot_nt(dob, sb)
            dattn_ref[0, h] = jnp.where(row >= col, _dot_nt(dob, vnb), 0.0)
            dvn = _dot_tn(attn_ref[0, h], dob) + _dot(kd_ref[:, hs], dsb)
            dvnb = _b(dvn)
            dkd_ref[:, hs] = _dot_nt(vnb, dsb)
            dcd = jnp.sum(jnp.sum(s * dsn, axis=1, keepdims=True), axis=0, keepdims=True)
            dcd_ref[0, h] = jnp.broadcast_to(dcd, (1, LANES))
            du_ref[:, hs] = dvn
            dw_ref[:, hs] = -_dot_nt(dvnb, sb)
            ds_sc[h] = dsn * cd_ref[0, h] + _dot_tn(qd_ref[:, hs], dob) - _dot_tn(w_ref[:, hs], dvnb)

    blk = pl.BlockSpec((C, V_W), lambda n: (N - 1 - n, 0))
    per_head = lambda a, b: pl.BlockSpec((1, N_V, a, b), lambda n: (N - 1 - n, 0, 0, 0))
    return pl.pallas_call(
        body, grid=(N,),
        in_specs=[blk, blk, blk, blk, per_head(C, C), per_head(1, LANES), per_head(HEAD_DIM, HEAD_DIM), blk],
        out_specs=[blk, blk, blk, blk, per_head(C, C), per_head(1, LANES)],
        out_shape=[SDS((T, V_W), f32)] * 4 + [SDS((N, N_V, C, C), f32), SDS((N, N_V, 1, LANES), f32)],
        scratch_shapes=[pltpu.VMEM((N_V, HEAD_DIM, HEAD_DIM), f32)], name=name, compiler_params=_cp("arbitrary"),
    )(do, qd, kd, w, attn, cd, s_all, vn)


def _delta_prep_bwd(qkv, ba, ea_row, dtb_row, tinv, u, w, dqd, dkd, du, dw, dattn, dcd, name):
    T = qkv.shape[0]
    C = CHUNK_A
    N = T // C

    def body(q_ref, k_ref, v_ref, ba_ref, ea_ref, dtb_ref, tinv_ref, u_ref, w_ref,
             dqd_ref, dkd_ref, du_ref, dw_ref, dattn_ref, dcd_ref,
             dqkv_ref, dba_ref, dalog_ref, ddtb_ref):
        row, col = _tri_masks(C)
        lane = lax.broadcasted_iota(jnp.int32, (C, LANES), 1)
        rowc = lax.broadcasted_iota(jnp.int32, (C, 1), 0)
        ba_v, ea, dtb = ba_ref[...], ea_ref[...], dtb_ref[...]
        sig, gblk = _gate_block(ba_v, ea, dtb, lane)
        lower = (row >= col).astype(f32)
        upper_s = (col > row).astype(f32)
        upper = (col >= row).astype(f32)
        gam_all = _dot(lower, gblk, HI)
        rg_all = _dot(upper_s, gblk, HI)
        tot = jnp.sum(gblk, axis=0, keepdims=True)
        dbeta_blk = jnp.zeros((C, LANES), f32)
        dgam_blk = jnp.zeros((C, LANES), f32)
        for j in range(N_QK):
            qk = pl.ds(j * HEAD_DIM, HEAD_DIM)
            qh, kh = q_ref[:, qk], k_ref[:, qk]
            qhb, khb = _b(qh), _b(kh)
            kk = _dot_nt(khb, khb)
            qkm = _dot_nt(qhb, khb)
            dq_j = jnp.zeros((C, HEAD_DIM), f32)
            dk_j = jnp.zeros((C, HEAD_DIM), f32)
            for h in (2 * j, 2 * j + 1):
                hs = pl.ds(h * HEAD_DIM, HEAD_DIM)
                vh = v_ref[:, hs]
                beta_c = _lane_pick(sig, lane, h)
                gam_c, rg_c, tot_h, dm = _head_decay(gam_all, rg_all, tot, lane, h, row, col)
                eg, er, cdh = jnp.exp(gam_c), jnp.exp(rg_c), jnp.exp(tot_h)
                tinv_h = tinv_ref[0, h]
                d_rv = _dot_tn(tinv_h, du_ref[:, hs], HI)
                d_rk = _dot_tn(tinv_h, dw_ref[:, hs], HI)
                da = -jnp.where(row > col, _dot_nt(_b(d_rv), _b(u_ref[:, hs])) + _dot_nt(_b(d_rk), _b(w_ref[:, hs])), 0.0)
                dqkv_ref[:, pl.ds(2 * QK_W + h * HEAD_DIM, HEAD_DIM)] = beta_c * d_rv
                dbeta = jnp.sum(d_rv * vh + d_rk * (eg * kh), axis=1, keepdims=True)
                dk_h = (beta_c * eg) * d_rk
                d_eg = jnp.sum(d_rk * kh, axis=1, keepdims=True) * beta_c
                bkd = da * dm
                dbeta = dbeta + jnp.sum(bkd * kk, axis=1, keepdims=True)
                dkk = bkd * beta_c
                ddm = da * beta_c * kk
                dattn_h = dattn_ref[0, h]
                dqk = dattn_h * dm
                ddm = ddm + dattn_h * qkm
                dqd_h, dkd_h = dqd_ref[:, hs], dkd_ref[:, hs]
                dq_j = dq_j + _dot(_b(dqk), khb) + eg * dqd_h
                dk_h = dk_h + _dot_tn(_b(dqk), qhb) + _dot(_b(dkk + dkk.T), khb) + er * dkd_h
                dk_j = dk_j + dk_h
                d_eg = d_eg + jnp.sum(dqd_h * qh, axis=1, keepdims=True)
                d_er = jnp.sum(dkd_h * kh, axis=1, keepdims=True)
                e = ddm * dm
                dgam = jnp.sum(e, axis=1, keepdims=True) - jnp.sum(e.T, axis=1, keepdims=True)
                dgam = dgam + d_eg * eg - d_er * er
                extra = jnp.sum(d_er * er, axis=0, keepdims=True) + jnp.max(dcd_ref[0, h], axis=1, keepdims=True) * cdh
                dgam = dgam + jnp.where(rowc == C - 1, extra, 0.0)
                dbeta_blk = jnp.where(lane == h, dbeta, dbeta_blk)
                dgam_blk = jnp.where(lane == N_V + h, dgam, dgam_blk)
            dqkv_ref[:, qk] = dq_j
            dqkv_ref[:, pl.ds(QK_W + j * HEAD_DIM, HEAD_DIM)] = dk_j
        dg_all = _dot(upper, dgam_blk, HI)
        dsp = dg_all * (-ea) * _sig(ba_v + dtb)
        dba_ref[...] = jnp.where(lane < N_V, dbeta_blk * sig * (1.0 - sig), dsp)
        part_alog = jnp.sum(dg_all * gblk, axis=0, keepdims=True)
        part_dtb = jnp.sum(dsp, axis=0, keepdims=True)

        @pl.when(pl.program_id(0) == 0)
        def _():
            dalog_ref[...] = part_alog
            ddtb_ref[...] = part_dtb

        @pl.when(pl.program_id(0) > 0)
        def _():
            dalog_ref[...] += part_alog
            ddtb_ref[...] += part_dtb

    big = lambda n: (n, 0)
    wide = pl.BlockSpec((C, V_W), big)
    sq = pl.BlockSpec((1, N_V, C, C), lambda n: (n, 0, 0, 0))
    return pl.pallas_call(
        body, grid=(N,),
        in_specs=[pl.BlockSpec((C, QK_W), lambda n: (n, 0)), pl.BlockSpec((C, QK_W), lambda n: (n, 1)),
                  pl.BlockSpec((C, V_W), lambda n: (n, 1)), pl.BlockSpec((C, LANES), big),
                  _whole((1, LANES)), _whole((1, LANES)), sq, wide, wide, wide, wide, wide, wide, sq,
                  pl.BlockSpec((1, N_V, 1, LANES), lambda n: (n, 0, 0, 0))],
        out_specs=[pl.BlockSpec((C, 2 * QK_W + V_W), big), pl.BlockSpec((C, LANES), big),
                   _whole((1, LANES)), _whole((1, LANES))],
        out_shape=[SDS((T, 2 * QK_W + V_W), f32), SDS((T, LANES), f32), SDS((1, LANES), f32), SDS((1, LANES), f32)],
        name=name, compiler_params=_cp("arbitrary"),
    )(qkv, qkv, qkv, ba, ea_row, dtb_row, tinv, u, w, dqd, dkd, du, dw, dattn, dcd)


def _onorm_fwd(o, proj, hg, name):
    T = o.shape[0]
    tr = _divisor(T, 256, 16)

    def body(o_ref, z_ref, g_ref, out_ref):
        g = g_ref[...]
        for h in range(N_V):
            hs = pl.ds(h * HEAD_DIM, HEAD_DIM)
            oh = o_ref[:, hs]
            r = lax.rsqrt(jnp.mean(oh * oh, axis=-1, keepdims=True) + EPS)
            out_ref[:, hs] = (oh * r * g * _silu(z_ref[:, hs])).astype(out_ref.dtype)

    return pl.pallas_call(
        body, grid=(T // tr,), in_specs=[_rows(tr, V_W), _rows(tr, V_W, C_Z // V_W), _whole((1, HEAD_DIM))],
        out_specs=_rows(tr, V_W), out_shape=SDS((T, V_W), bf16), name=name, compiler_params=_cp("parallel"),
    )(o, proj, hg)


def _onorm_bwd(don, o, proj, hg, name):
    T = o.shape[0]
    tr = _divisor(T, 256, 16)

    def body(don_ref, o_ref, z_ref, g_ref, do_ref, dz_ref, dg_ref):
        g = g_ref[...]
        dg = jnp.zeros((1, HEAD_DIM), f32)
        for h in range(N_V):
            hs = pl.ds(h * HEAD_DIM, HEAD_DIM)
            oh, zh, dh = o_ref[:, hs], z_ref[:, hs], don_ref[:, hs]
            r = lax.rsqrt(jnp.mean(oh * oh, axis=-1, keepdims=True) + EPS)
            d_n = dh * _silu(zh)
            dz_ref[:, hs] = (dh * (oh * r * g) * _dsilu(zh)).astype(dz_ref.dtype)
            dy = d_n * g
            m = jnp.mean(dy * oh, axis=-1, keepdims=True)
            do_ref[:, hs] = r * dy - oh * (r * r * r * m)
            dg = dg + jnp.sum(d_n * oh * r, axis=0, keepdims=True)

        @pl.when(pl.program_id(0) == 0)
        def _():
            dg_ref[...] = dg

        @pl.when(pl.program_id(0) > 0)
        def _():
            dg_ref[...] += dg

    return pl.pallas_call(
        body, grid=(T // tr,),
        in_specs=[_rows(tr, V_W), _rows(tr, V_W), _rows(tr, V_W, C_Z // V_W), _whole((1, HEAD_DIM))],
        out_specs=[_rows(tr, V_W), _rows(tr, V_W), _whole((1, HEAD_DIM))],
        out_shape=[SDS((T, V_W), f32), SDS((T, V_W), bf16), SDS((1, HEAD_DIM), f32)],
        name=name, compiler_params=_cp("arbitrary"),
    )(don, o, proj, hg)


def _sgu_parts(ub, vb, gain):
    gv = _gelu(vb)
    r = lax.rsqrt(jnp.mean(gv * gv, axis=-1, keepdims=True) + EPS)
    return _gelu(ub), gv, r, gv * r * gain


def _sgu_fwd(proj, gain, w_s, b_bc, name):
    T = proj.shape[0]
    C = CHUNK_B

    def body(ub_ref, vb_ref, g_ref, w_ref, b_ref, o_ref):
        row, col = _tri_masks(C)
        u, _, _, vn = _sgu_parts(ub_ref[...], vb_ref[...], g_ref[...])
        for g in range(N_GROUPS):
            gs = pl.ds(g * GROUP_DIM, GROUP_DIM)
            wg = jnp.where(row >= col, w_ref[g], 0.0)
            mixed = _dot(_b(wg), _b(vn[:, g * GROUP_DIM:(g + 1) * GROUP_DIM])) + b_ref[g]
            o_ref[:, gs] = (u[:, g * GROUP_DIM:(g + 1) * GROUP_DIM] * mixed).astype(o_ref.dtype)

    return pl.pallas_call(
        body, grid=(T // C,),
        in_specs=[_rows(C, WIDTH_B, C_UB // WIDTH_B), _rows(C, WIDTH_B, C_VB // WIDTH_B), _whole((1, WIDTH_B)),
                  _whole((N_GROUPS, C, C)), _whole((N_GROUPS, C, GROUP_DIM))],
        out_specs=_rows(C, WIDTH_B), out_shape=SDS((T, WIDTH_B), bf16), name=name, compiler_params=_cp("parallel"),
    )(proj, proj, gain, w_s, b_bc)


def _sgu_bwd(dsgu, proj, gain, w_s, b_bc, name):
    T = proj.shape[0]
    C = CHUNK_B

    def body(d_ref, ub_ref, vb_ref, g_ref, w_ref, b_ref, dub_ref, dvb_ref, dw_ref, db_ref, dg_ref):
        first = pl.program_id(0) == 0
        row, col = _tri_masks(C)
        ub, vb, gain_v = ub_ref[...], vb_ref[...], g_ref[...]
        u, gv, r, vn = _sgu_parts(ub, vb, gain_v)
        d = d_ref[...]
        dvn_parts = []
        for g in range(N_GROUPS):
            sl = slice(g * GROUP_DIM, (g + 1) * GROUP_DIM)
            wg = jnp.where(row >= col, w_ref[g], 0.0)
            vng = _b(vn[:, sl])
            mixed = _dot(_b(wg), vng) + b_ref[g]
            dub_ref[:, pl.ds(g * GROUP_DIM, GROUP_DIM)] = (d[:, sl] * mixed * _dgelu(ub[:, sl])).astype(dub_ref.dtype)
            dmix = d[:, sl] * u[:, sl]
            dmb = _b(dmix)
            dwg = jnp.where(row >= col, _dot_nt(dmb, vng), 0.0)
            dbg = jnp.sum(dmix, axis=1, keepdims=True)

            @pl.when(first)
            def _():
                dw_ref[g] = dwg
                db_ref[g] = dbg

            @pl.when(jnp.logical_not(first))
            def _():
                dw_ref[g] += dwg
                db_ref[g] += dbg

            dvn_parts.append(_dot_tn(_b(wg), dmb))
        dvn = jnp.concatenate(dvn_parts, axis=1)
        dy = dvn * gain_v
        m = jnp.mean(dy * gv, axis=-1, keepdims=True)
        dgv = r * dy - gv * (r * r * r * m)
        dvb_ref[...] = (dgv * _dgelu(vb)).astype(dvb_ref.dtype)
        dgain = jnp.sum(dvn * gv * r, axis=0, keepdims=True)

        @pl.when(first)
        def _():
            dg_ref[...] = dgain

        @pl.when(jnp.logical_not(first))
        def _():
            dg_ref[...] += dgain

    return pl.pallas_call(
        body, grid=(T // C,),
        in_specs=[_rows(C, WIDTH_B), _rows(C, WIDTH_B, C_UB // WIDTH_B), _rows(C, WIDTH_B, C_VB // WIDTH_B),
                  _whole((1, WIDTH_B)), _whole((N_GROUPS, C, C)), _whole((N_GROUPS, C, GROUP_DIM))],
        out_specs=[_rows(C, WIDTH_B), _rows(C, WIDTH_B), _whole((N_GROUPS, C, C)), _whole((N_GROUPS, C, 1)),
                   _whole((1, WIDTH_B))],
        out_shape=[SDS((T, WIDTH_B), bf16), SDS((T, WIDTH_B), bf16), SDS((N_GROUPS, C, C), f32),
                   SDS((N_GROUPS, C, 1), f32), SDS((1, WIDTH_B), f32)],
        name=name, compiler_params=_cp("arbitrary"),
    )(dsgu, proj, proj, gain, w_s, b_bc)


def _merge_fwd(proj, ya, yb, name):
    T = proj.shape[0]
    tr = _divisor(T, 256, 16)

    def body(ga_ref, gb_ref, ya_ref, yb_ref, o_ref):
        o_ref[...] = (_sig(ga_ref[...]) * ya_ref[...] + _sig(gb_ref[...]) * yb_ref[...]).astype(o_ref.dtype)

    return pl.pallas_call(
        body, grid=(T // tr,),
        in_specs=[_rows(tr, D_MODEL, C_GA // D_MODEL), _rows(tr, D_MODEL, C_GB // D_MODEL), _rows(tr, D_MODEL), _rows(tr, D_MODEL)],
        out_specs=_rows(tr, D_MODEL), out_shape=SDS((T, D_MODEL), bf16), name=name, compiler_params=_cp("parallel"),
    )(proj, proj, ya, yb)


def _merge_bwd(dm, proj, ya, yb, name):
    T = proj.shape[0]
    tr = _divisor(T, 256, 16)

    def body(dm_ref, ga_ref, gb_ref, ya_ref, yb_ref, dya_ref, dyb_ref, dga_ref, dgb_ref):
        d = dm_ref[...]
        sa, sb = _sig(ga_ref[...]), _sig(gb_ref[...])
        dya_ref[...] = (d * sa).astype(bf16)
        dyb_ref[...] = (d * sb).astype(bf16)
        dga_ref[...] = (d * ya_ref[...] * sa * (1.0 - sa)).astype(bf16)
        dgb_ref[...] = (d * yb_ref[...] * sb * (1.0 - sb)).astype(bf16)

    return pl.pallas_call(
        body, grid=(T // tr,),
        in_specs=[_rows(tr, D_MODEL), _rows(tr, D_MODEL, C_GA // D_MODEL), _rows(tr, D_MODEL, C_GB // D_MODEL),
                  _rows(tr, D_MODEL), _rows(tr, D_MODEL)],
        out_specs=[_rows(tr, D_MODEL)] * 4, out_shape=[SDS((T, D_MODEL), bf16)] * 4, name=name,
        compiler_params=_cp("parallel"),
    )(dm, proj, proj, ya, yb)


def _ffnconv_fwd(upg, upv, wg, wv, bg, bv, name):
    T, F = upg.shape
    R = min(ROW_CHUNK, T)
    n_chunks = T // R

    def body(g_ref, v_ref, wg_ref, wv_ref, bg_ref, bv_ref, o_ref):
        wgv, wvv = _w_rows(wg_ref, 3), _w_rows(wv_ref, 3)

        def chunk(ci, carry):
            r0 = pl.multiple_of(ci * R, R)
            cg = _causal_conv(g_ref[pl.ds(r0, R), :], _prev8(g_ref, r0, ci), wgv, 3) + bg_ref[...]
            cv = _causal_conv(v_ref[pl.ds(r0, R), :], _prev8(v_ref, r0, ci), wvv, 3) + bv_ref[...]
            o_ref[pl.ds(r0, R), :] = (_silu(cg) * cv).astype(o_ref.dtype)
            return carry

        lax.fori_loop(0, n_chunks, chunk, 0)

    col = lambda p: (0, p)
    return pl.pallas_call(
        body, grid=(F // LANES,),
        in_specs=[pl.BlockSpec((T, LANES), col)] * 2 + [pl.BlockSpec((3, LANES), col)] * 2 + [pl.BlockSpec((1, LANES), col)] * 2,
        out_specs=pl.BlockSpec((T, LANES), col), out_shape=SDS((T, F), bf16), name=name, compiler_params=_cp("parallel"),
    )(upg, upv, wg, wv, bg, bv)


def _ffnconv_bwd(dact, upg, upv, wg, wv, bg, bv, name):
    T, F = upg.shape
    R = min(ROW_CHUNK, T)
    n_chunks = T // R

    def body(d_ref, g_ref, v_ref, wg_ref, wv_ref, bg_ref, bv_ref,
             dg_ref, dv_ref, dwg_ref, dwv_ref, dbg_ref, dbv_ref, sg, sv):
        wgv, wvv = _w_rows(wg_ref, 3), _w_rows(wv_ref, 3)

        def phase1(ci, carry):
            dwg, dwv = carry
            r0 = pl.multiple_of(ci * R, R)
            gcur, vcur = g_ref[pl.ds(r0, R), :], v_ref[pl.ds(r0, R), :]
            gp, vp = _prev8(g_ref, r0, ci), _prev8(v_ref, r0, ci)
            gsh = [gcur] + [_shift_down(gcur, gp, s) for s in (1, 2)]
            vsh = [vcur] + [_shift_down(vcur, vp, s) for s in (1, 2)]
            cg = gsh[0] * wgv[2] + gsh[1] * wgv[1] + gsh[2] * wgv[0] + bg_ref[...]
            cv = vsh[0] * wvv[2] + vsh[1] * wvv[1] + vsh[2] * wvv[0] + bv_ref[...]
            d = d_ref[pl.ds(r0, R), :]
            dcv = d * _silu(cg)
            dcg = d * cv * _dsilu(cg)
            sg[pl.ds(r0, R), :] = dcg
            sv[pl.ds(r0, R), :] = dcv
            rg = [jnp.sum(dcg * gsh[2 - j], axis=0, keepdims=True) for j in range(3)] + [jnp.sum(dcg, axis=0, keepdims=True)]
            rv = [jnp.sum(dcv * vsh[2 - j], axis=0, keepdims=True) for j in range(3)] + [jnp.sum(dcv, axis=0, keepdims=True)]
            return tuple(a + b for a, b in zip(dwg, rg)), tuple(a + b for a, b in zip(dwv, rv))

        z4 = tuple(jnp.zeros((1, LANES), f32) for _ in range(4))
        dwg, dwv = lax.fori_loop(0, n_chunks, phase1, (z4, z4))
        for j in range(3):
            dwg_ref[j:j + 1, :] = dwg[j]
            dwv_ref[j:j + 1, :] = dwv[j]
        dbg_ref[...] = dwg[3]
        dbv_ref[...] = dwv[3]

        def phase2(ci, carry):
            r0 = pl.multiple_of(ci * R, R)
            for sc, wv_, out in ((sg, wgv, dg_ref), (sv, wvv, dv_ref)):
                cur = sc[pl.ds(r0, R), :]
                n8 = _next8(sc, r0, R, ci, n_chunks)
                acc = cur * wv_[2] + _shift_up(cur, n8, 1) * wv_[1] + _shift_up(cur, n8, 2) * wv_[0]
                out[pl.ds(r0, R), :] = acc.astype(out.dtype)
            return carry

        lax.fori_loop(0, n_chunks, phase2, 0)

    col = lambda p: (0, p)
    big, w3, b1 = pl.BlockSpec((T, LANES), col), pl.BlockSpec((3, LANES), col), pl.BlockSpec((1, LANES), col)
    return pl.pallas_call(
        body, grid=(F // LANES,), in_specs=[big, big, big, w3, w3, b1, b1], out_specs=[big, big, w3, w3, b1, b1],
        out_shape=[SDS((T, F), bf16), SDS((T, F), bf16), SDS((3, F), f32), SDS((3, F), f32), SDS((1, F), f32), SDS((1, F), f32)],
        scratch_shapes=[pltpu.VMEM((T, LANES), f32), pltpu.VMEM((T, LANES), f32)], name=name, compiler_params=_cp("parallel"),
    )(dact, upg, upv, wg, wv, bg, bv)


def _loss_head(x, gain, target, name):
    T, Dm = x.shape
    tr = _divisor(T, 256, 16)

    def body(x_ref, g_ref, t_ref, l_ref, dx_ref, dg_ref):
        xv, g = x_ref[...], g_ref[...]
        r = lax.rsqrt(jnp.mean(xv * xv, axis=-1, keepdims=True) + EPS)
        err = xv * r * g - t_ref[...]
        part_l = 0.5 * jnp.sum(jnp.mean(err * err, axis=-1, keepdims=True), axis=0, keepdims=True)
        dy = err * (1.0 / Dm)
        dyg = dy * g
        m = jnp.mean(dyg * xv, axis=-1, keepdims=True)
        dx_ref[...] = r * dyg - xv * (r * r * r * m)
        part_g = jnp.sum(dy * xv * r, axis=0, keepdims=True)
        part_l = jnp.broadcast_to(part_l, (1, LANES))

        @pl.when(pl.program_id(0) == 0)
        def _():
            l_ref[...] = part_l
            dg_ref[...] = part_g

        @pl.when(pl.program_id(0) > 0)
        def _():
            l_ref[...] += part_l
            dg_ref[...] += part_g

    return pl.pallas_call(
        body, grid=(T // tr,), in_specs=[_rows(tr, Dm), _whole((1, Dm)), _rows(tr, Dm)],
        out_specs=[_whole((1, LANES)), _rows(tr, Dm), _whole((1, Dm))],
        out_shape=[SDS((1, LANES), f32), SDS((T, Dm), f32), SDS((1, Dm), f32)], name=name, compiler_params=_cp("arbitrary"),
    )(x, gain, target)


def _lane_row(vec, offset):
    return jnp.pad(vec.astype(f32), (offset, LANES - offset - vec.shape[0]))[None]


def _layer_fwd(x, p_i, W, S, li):
    nm = lambda s: f"{s}_l{li}"
    sv = {"x0": x}
    h1 = _rms_fwd(x, S["norm_mix"], nm("rms_mix"))
    proj = _mm(h1, W["in_main"], "nn", nm("proj_main"))
    ba = _mm(h1, W["in_ba"], "nn", nm("proj_ba"))
    qkv = _qkvconv_fwd(proj, S["conv_qkv"], nm("qkvconv"))
    qd, kd, u, w, attn, tinv, cd, _ = _delta_prep(qkv, ba, S["ea_row"], S["dtb_row"], nm("delta_prep"))
    o, s_all, vn = _delta_scan(qd, kd, u, w, attn, cd, nm("delta_scan"))
    on = _onorm_fwd(o, proj, S["head_norm"], nm("onorm"))
    ya = _mm(on, W["branch_a"], "nn", nm("branch_a"))
    sgu = _sgu_fwd(proj, S["sgu_norm"], S["w_spatial"], S["b_bc"], nm("sgu"))
    yb = _mm(sgu, W["branch_b"], "nn", nm("branch_b"))
    merged = _merge_fwd(proj, ya, yb, nm("merge"))
    x1 = _mm(merged, W["out"], "nn", nm("out_proj"), c=x)
    h2 = _rms_fwd(x1, S["norm_ffn"], nm("rms_ffn"))
    upg = _mm(h2, W["up_g"], "nn", nm("ffn_up_g"))
    upv = _mm(h2, W["up_v"], "nn", nm("ffn_up_v"))
    act = _ffnconv_fwd(upg, upv, S["conv_g"], S["conv_v"], S["bias_g"], S["bias_v"], nm("ffnconv"))
    x2 = _mm(act, W["down"], "nn", nm("ffn_down"), c=x1, tk=2816)
    h3 = _rms_fwd(x2, S["norm_ple"], nm("rms_ple"))
    gl = _mm(h3, W["ple_gate"], "nn", nm("ple_gate"))
    pp = _mm(p_i, W["ple_proj"], "nn", nm("ple_proj"))
    (x3,) = _ew(lambda a, g, q: (a + _sig(g) * q,), [x2, gl, pp], [f32], nm("ple_mix"))
    sv.update(h1=h1, proj=proj, ba=ba, qkv=qkv, qd=qd, kd=kd, u=u, w=w, attn=attn, tinv=tinv, cd=cd, o=o, s_all=s_all,
              vn=vn, on=on, ya=ya, sgu=sgu, yb=yb, merged=merged, x1=x1, h2=h2, upg=upg, upv=upv, act=act, x2=x2,
              h3=h3, gl=gl, pp=pp, p=p_i)
    return x3, sv


def _layer_bwd(dx3, sv, W, S, li):
    nm = lambda s: f"{s}_l{li}"
    G = {}
    dgl, dpp = _ew(lambda d, g, q: ((lambda s: (d * q * s * (1.0 - s), d * s))(_sig(g))),
                   [dx3, sv["gl"], sv["pp"]], [bf16, bf16], nm("ple_mix_bwd"))
    G["w_ple_gate"] = _mm(sv["h3"], dgl, "tn", nm("d_ple_gate"))
    G["w_ple_proj"] = _mm(sv["p"], dpp, "tn", nm("d_ple_proj"))
    dh3 = _mm(dgl, W["ple_gate"], "nt", nm("dh_ple"))
    dx2, G["norm_ple"] = _rms_bwd(sv["x2"], S["norm_ple"], dh3, dx3, nm("rms_ple_bwd"))
    dact = _mm(dx2, W["down"], "nt", nm("d_act"))
    G["w_ffn_down"] = _mm(sv["act"], dx2, "tn", nm("d_ffn_down"))
    dupg, dupv, dcg, dcv, dbg, dbv = _ffnconv_bwd(dact, sv["upg"], sv["upv"], S["conv_g"], S["conv_v"], S["bias_g"],
                                                  S["bias_v"], nm("ffnconv_bwd"))
    G["conv_ffn"] = jnp.concatenate([dcg, dcv], axis=1)
    G["b_conv_ffn"] = jnp.concatenate([dbg, dbv], axis=1)
    G["up_g"] = _mm(sv["h2"], dupg, "tn", nm("d_ffn_up_g"))
    G["up_v"] = _mm(sv["h2"], dupv, "tn", nm("d_ffn_up_v"))
    dh2 = _mm(dupg, W["up_g"], "nt", nm("dh_ffn_g"), tk=2816)
    dh2 = _mm(dupv, W["up_v"], "nt", nm("dh_ffn_v"), c=dh2, tk=2816)
    dx1, G["norm_ffn"] = _rms_bwd(sv["x1"], S["norm_ffn"], dh2, dx2, nm("rms_ffn_bwd"))
    dmerged = _mm(dx1, W["out"], "nt", nm("d_merged"))
    G["w_out"] = _mm(sv["merged"], dx1, "tn", nm("d_w_out"))
    dya, dyb, dga, dgb = _merge_bwd(dmerged, sv["proj"], sv["ya"], sv["yb"], nm("merge_bwd"))
    G["w_branch_a"] = _mm(sv["on"], dya, "tn", nm("d_branch_a"))
    G["w_branch_b"] = _mm(sv["sgu"], dyb, "tn", nm("d_branch_b"))
    don = _mm(dya, W["branch_a"], "nt", nm("d_on"))
    dsgu = _mm(dyb, W["branch_b"], "nt", nm("d_sgu"))
    dub, dvb, G["w_spatial"], db_s, G["sgu_norm"] = _sgu_bwd(dsgu, sv["proj"], S["sgu_norm"], S["w_spatial"], S["b_bc"], nm("sgu_bwd"))
    G["b_spatial"] = db_s.reshape(N_GROUPS, CHUNK_B)
    do, dz, G["head_norm"] = _onorm_bwd(don, sv["o"], sv["proj"], S["head_norm"], nm("onorm_bwd"))
    dqd, dkd, du, dw, dattn, dcd = _delta_scan_bwd(do, sv["qd"], sv["kd"], sv["w"], sv["attn"], sv["cd"], sv["s_all"], sv["vn"],
                                                   nm("delta_scan_bwd"))
    dqkv_n, dba, dalog_row, ddtb_row = _delta_prep_bwd(sv["qkv"], sv["ba"], S["ea_row"], S["dtb_row"], sv["tinv"], sv["u"], sv["w"],
                                                       dqd, dkd, du, dw, dattn, dcd, nm("delta_prep_bwd"))
    G["a_log"] = dalog_row[0, N_V:2 * N_V]
    G["dt_bias"] = ddtb_row[0, N_V:2 * N_V]
    dqkv, G["conv_qkv"] = _qkvconv_bwd(sv["proj"], S["conv_qkv"], dqkv_n, nm("qkvconv_bwd"))
    dproj = jnp.concatenate([dqkv, dz, dub, dvb, dga, dgb], axis=1)
    G["in_main"] = _mm(sv["h1"], dproj, "tn", nm("d_in_main"))
    G["in_ba"] = _mm(sv["h1"], dba, "tn", nm("d_in_ba"))
    dh1 = _mm(dba, W["in_ba"], "nt", nm("dh_mix_ba"))
    dh1 = _mm(dproj, W["in_main"], "nt", nm("dh_mix"), c=dh1)
    dx0, G["norm_mix"] = _rms_bwd(sv["x0"], S["norm_mix"], dh1, dx1, nm("rms_mix_bwd"))
    return dx0, G


_BA0 = C_UB
_BA1 = C_UB + 2 * N_V


def _cols(segments, lo, hi):
    out = []
    for start, a in segments:
        s_lo, s_hi = max(lo, start), min(hi, start + a.shape[1])
        if s_lo < s_hi:
            out.append(a[:, s_lo - start:s_hi - start])
    return out


def _big_weights(sh):
    n_in = N_IN // N_CHIPS
    w_in = [(j * n_in, a) for j, a in enumerate(sh["w_in"])]
    rows = lambda k: jnp.concatenate(sh[k], axis=0)
    cols = lambda parts: jnp.concatenate(parts, axis=1)
    ba = jnp.pad(cols(_cols(w_in, _BA0, _BA1)), ((0, 0), (0, LANES - 2 * N_V)))
    return dict(
        in_main=cols(_cols(w_in, 0, _BA0) + _cols(w_in, _BA1, N_IN)), in_ba=ba,
        branch_a=rows("w_branch_a"), branch_b=cols(sh["w_branch_b"]), out=rows("w_out"),
        up_g=cols(sh["w_ffn_up"][:2]), up_v=cols(sh["w_ffn_up"][2:]), down=rows("w_ffn_down"),
        ple_gate=rows("w_ple_gate"), ple_proj=cols(sh["w_ple_proj"]))


def _grads_by_chip(G):
    n_in = N_IN // N_CHIPS
    w_in = [(0, G["in_main"][:, :_BA0]), (_BA0, G["in_ba"][:, :2 * N_V]), (_BA1, G["in_main"][:, _BA0:])]
    split = lambda g: jnp.stack(jnp.split(g, N_CHIPS, axis=1))
    by_rows = lambda g: g.reshape(N_CHIPS, -1, g.shape[1])
    return dict(
        w_in=jnp.stack([jnp.concatenate(_cols(w_in, j * n_in, (j + 1) * n_in), axis=1) for j in range(N_CHIPS)]),
        w_branch_a=by_rows(G["w_branch_a"]), w_branch_b=split(G["w_branch_b"]), w_out=by_rows(G["w_out"]),
        w_ffn_up=jnp.stack(jnp.split(G["up_g"], 2, axis=1) + jnp.split(G["up_v"], 2, axis=1)),
        w_ffn_down=by_rows(G["w_ffn_down"]), w_ple_gate=by_rows(G["w_ple_gate"]), w_ple_proj=split(G["w_ple_proj"]))


def _small_params(sm, i):
    return dict(
        norm_mix=sm["norm_mix"][i][None], conv_qkv=sm["conv_qkv"][i], ea_row=_lane_row(jnp.exp(sm["a_log"][i]), N_V),
        dtb_row=_lane_row(sm["dt_bias"][i], N_V), head_norm=sm["head_norm"][i][None], sgu_norm=sm["sgu_norm"][i][None],
        w_spatial=sm["w_spatial"][i],
        b_bc=jnp.broadcast_to(sm["b_spatial"][i][:, :, None], (N_GROUPS, CHUNK_B, GROUP_DIM)),
        norm_ffn=sm["norm_ffn"][i][None], conv_g=sm["conv_ffn"][i][:, :D_FF], conv_v=sm["conv_ffn"][i][:, D_FF:],
        bias_g=sm["b_conv_ffn"][i][None, :D_FF], bias_v=sm["b_conv_ffn"][i][None, D_FF:], norm_ple=sm["norm_ple"][i][None])


MESH = pl.DeviceIdType.MESH
N_CHIPS = 4
ANY = pl.BlockSpec(memory_space=pl.ANY)
ROW_ALIGN = 32


def _place():
    x, y, c = lax.axis_index("x"), lax.axis_index("y"), lax.axis_index("c")
    chips = [(1 - x, y), (x, 1 - y), (1 - x, 1 - y)]
    return x, y, c, 2 * x + y, chips, (x, y, 1 - c)


def _halves(rows, c):
    h = rows // 2
    return pl.ds(pl.multiple_of(c * h, 16), h), pl.ds(pl.multiple_of((1 - c) * h, 16), h)


def _remote(src, dst, send_sems, recv_sems, k, dev):
    return pltpu.make_async_remote_copy(src_ref=src, dst_ref=dst, send_sem=send_sems.at[k], recv_sem=recv_sems.at[k],
                                        device_id=dev, device_id_type=MESH)


def _gather_shards(bufs, name):
    n = len(bufs)

    def body(*refs):
        ins, outs = refs[:n], refs[n:2 * n]
        send_sems, recv_sems = refs[2 * n:]
        x, y, c, me, chips, sib = _place()
        for i in range(n):
            mine, _ = _halves(ins[i].shape[0], c)
            for k, (px, py) in enumerate(chips):
                _remote(ins[i].at[mine], outs[i].at[me, mine], send_sems, recv_sems, 6 * i + k, (px, py, c)).start()
        for i in range(n):
            mine, _ = _halves(ins[i].shape[0], c)
            for k, (px, py) in enumerate(chips):
                pc = 2 * px + py
                _remote(ins[i].at[mine], outs[i].at[pc, mine], send_sems, recv_sems, 6 * i + k, (px, py, c)).wait_recv()
                _remote(outs[i].at[pc, mine], outs[i].at[pc, mine], send_sems, recv_sems, 6 * i + 3 + k, sib).start()
        for i in range(n):
            mine, other = _halves(ins[i].shape[0], c)
            for k, (px, py) in enumerate(chips):
                pc = 2 * px + py
                _remote(outs[i].at[pc, mine], outs[i].at[pc, other], send_sems, recv_sems, 6 * i + 3 + k, sib).wait_recv()
        for i in range(n):
            mine, _ = _halves(ins[i].shape[0], c)
            for k, (px, py) in enumerate(chips):
                pc = 2 * px + py
                _remote(ins[i].at[mine], outs[i].at[me, mine], send_sems, recv_sems, 6 * i + k, (px, py, c)).wait_send()
                _remote(outs[i].at[pc, mine], outs[i].at[pc, mine], send_sems, recv_sems, 6 * i + 3 + k, sib).wait_send()

    return pl.pallas_call(
        body, in_specs=[ANY] * n, out_specs=[ANY] * n,
        out_shape=[SDS((N_CHIPS,) + b.shape, b.dtype) for b in bufs],
        scratch_shapes=[pltpu.SemaphoreType.DMA((6 * n,)), pltpu.SemaphoreType.DMA((6 * n,))],
        name=name,
    )(*bufs)


def _exchange(srcs_of, out_shapes, n_sems, name):
    n = len(out_shapes)

    def body(*refs):
        n_in = len(refs) - n - 2
        ins, outs = refs[:n_in], refs[n_in:n_in + n]
        send_sems, recv_sems = refs[-2:]
        copies = [_remote(s, d, send_sems, recv_sems, k, dev) for s, d, k, dev in srcs_of(ins, outs, _place())]
        for cp in copies:
            cp.start()
        for cp in copies:
            cp.wait()

    def call(*arrs):
        return pl.pallas_call(
            body, in_specs=[ANY] * len(arrs), out_specs=[ANY] * n, out_shape=out_shapes,
            scratch_shapes=[pltpu.SemaphoreType.DMA((n_sems,)), pltpu.SemaphoreType.DMA((n_sems,))], name=name,
        )(*arrs)

    return call


def _pick(idx, parts):
    out = parts[-1]
    for j in reversed(range(len(parts) - 1)):
        out = jnp.where(idx == j, parts[j], out)
    return out


def _rs_pair(gs, name):
    def copies(ins, outs, place):
        x, y, c, me, chips, sib = place
        return [(g.at[:, _halves(g.shape[1], c)[1]], o, i, sib) for i, (g, o) in enumerate(zip(ins, outs))]

    shapes = [SDS((N_CHIPS, g.shape[1] // 2, g.shape[2]), g.dtype) for g in gs]
    return _exchange(copies, shapes, len(gs), name)(*gs)


def _rs_chips(ps, name):
    def copies(ins, outs, place):
        x, y, c, me, chips, sib = place
        return [(p.at[2 * px + py], o.at[k], 3 * i + k, (px, py, c))
                for i, (p, o) in enumerate(zip(ins, outs)) for k, (px, py) in enumerate(chips)]

    shapes = [SDS((3,) + p.shape[1:], p.dtype) for p in ps]
    return _exchange(copies, shapes, 3 * len(ps), name)(*ps)


def _rs_join(qs, name):
    def copies(ins, outs, place):
        return [(q, o, i, place[5]) for i, (q, o) in enumerate(zip(ins, outs))]

    return _exchange(copies, [SDS(q.shape, q.dtype) for q in qs], len(qs), name)(*qs)


def _reduce_scatter(gs, names, chip, core, tag):
    gots = _rs_pair(gs, f"rs_pair_{tag}")
    ps = []
    for g, got, k in zip(gs, gots, names):
        h = g.shape[1] // 2
        mine = jnp.where(core == 0, g[:, :h], g[:, h:])
        (p,) = _ew(lambda a, b: (a + b,), [mine.reshape(-1, g.shape[2]), got.reshape(-1, g.shape[2])], [bf16], f"rs_add2_{k}_{tag}")
        ps.append(p.reshape(got.shape))
    parts = _rs_chips(ps, f"rs_chips_{tag}")
    qs = []
    up = lambda a: a.astype(f32)
    for p, part, k in zip(ps, parts, names):
        own = _pick(chip, [p[j] for j in range(N_CHIPS)])
        (q,) = _ew(lambda a, b, c, d: ((up(a) + up(c)) + (up(b) + up(d)),), [own, part[0], part[1], part[2]], [f32], f"rs_add4_{k}_{tag}")
        qs.append(q)
    others = _rs_join(qs, f"rs_join_{tag}")
    return [jnp.where(core == 0, jnp.concatenate([q, o], axis=0), jnp.concatenate([o, q], axis=0)) for q, o in zip(qs, others)]


def _allreduce_small(buf, name):
    R, L = buf.shape

    def body(x_ref, o_ref, r0, s1, r1, send_sems, recv_sems):
        x, y, c, me, chips, sib = _place()
        cp = _remote(x_ref, r0, send_sems, recv_sems, 0, sib)
        cp.start()
        cp.wait()
        s1[...] = x_ref[...] + r0[...]
        cps = []
        for k, (px, py) in enumerate(chips):
            cp = _remote(s1, r1.at[k], send_sems, recv_sems, 1 + k, (px, py, c))
            cp.start()
            cps.append(cp)
        for cp in cps:
            cp.wait()
        o_ref[...] = (s1[...] + r1[1]) + (r1[0] + r1[2])

    vm = pl.BlockSpec(memory_space=pltpu.VMEM)
    return pl.pallas_call(
        body, in_specs=[vm], out_specs=vm, out_shape=SDS((R, L), f32),
        scratch_shapes=[pltpu.VMEM((R, L), f32), pltpu.VMEM((R, L), f32), pltpu.VMEM((3, R, L), f32),
                        pltpu.SemaphoreType.DMA((4,)), pltpu.SemaphoreType.DMA((4,))],
        name=name, compiler_params=pltpu.CompilerParams(vmem_limit_bytes=VMEM_LIMIT_BYTES),
    )(buf)


BIG = ("w_in", "w_branch_a", "w_branch_b", "w_out", "w_ffn_up", "w_ffn_down", "w_ple_gate", "w_ple_proj")
SMALL_REPL = ("norm_mix", "a_log", "dt_bias", "head_norm", "sgu_norm", "w_spatial", "b_spatial", "norm_ffn", "b_conv_ffn",
              "norm_ple", "norm_final")
SMALL_COLS = ("conv_qkv", "conv_ffn")
WEIGHTS = ("norm_mix", "w_in", "conv_qkv", "a_log", "dt_bias", "head_norm", "sgu_norm", "w_spatial", "b_spatial", "w_branch_a",
           "w_branch_b", "w_out", "norm_ffn", "w_ffn_up", "conv_ffn", "b_conv_ffn", "w_ffn_down", "norm_ple", "w_ple_gate",
           "w_ple_proj", "norm_final")


def _flat(arrs, dtype):
    cat = jnp.concatenate([a.astype(dtype).reshape(-1) for a in arrs])
    unit = LANES * ROW_ALIGN
    cat = jnp.pad(cat, (0, -cat.shape[0] % unit))
    return cat.reshape(-1, LANES)


def _unflat(buf, shapes):
    flat = buf.reshape(-1)
    out, off = [], 0
    for s in shapes:
        n = math.prod(s)
        out.append(flat[off:off + n].reshape(s))
        off += n
    return out


def _adamw(w, g, m, v):
    m2 = ADAM_B1 * m + (1.0 - ADAM_B1) * g
    v2 = ADAM_B2 * v + (1.0 - ADAM_B2) * (g * g)
    m_hat = m2 * (1.0 / (1.0 - ADAM_B1 ** ADAM_STEP))
    v_hat = v2 * (1.0 / (1.0 - ADAM_B2 ** ADAM_STEP))
    delta = -ADAM_LR * (m_hat / (jnp.sqrt(v_hat) + ADAM_EPS) + ADAM_WD * w)
    return delta, m2, v2


def kernel(x, p, norm_mix, w_in, conv_qkv, a_log, dt_bias, head_norm, sgu_norm, w_spatial, b_spatial, w_branch_a, w_branch_b, w_out, norm_ffn, w_ffn_up, conv_ffn, b_conv_ffn, w_ffn_down, norm_ple, w_ple_gate, w_ple_proj, norm_final, loss_target, m_norm_mix, m_w_in, m_conv_qkv, m_a_log, m_dt_bias, m_head_norm, m_sgu_norm, m_w_spatial, m_b_spatial, m_w_branch_a, m_w_branch_b, m_w_out, m_norm_ffn, m_w_ffn_up, m_conv_ffn, m_b_conv_ffn, m_w_ffn_down, m_norm_ple, m_w_ple_gate, m_w_ple_proj, m_norm_final, v_norm_mix, v_w_in, v_conv_qkv, v_a_log, v_dt_bias, v_head_norm, v_sgu_norm, v_w_spatial, v_b_spatial, v_w_branch_a, v_w_branch_b, v_w_out, v_norm_ffn, v_w_ffn_up, v_conv_ffn, v_b_conv_ffn, v_w_ffn_down, v_norm_ple, v_w_ple_gate, v_w_ple_proj, v_norm_final):
    w = dict(norm_mix=norm_mix, w_in=w_in, conv_qkv=conv_qkv, a_log=a_log, dt_bias=dt_bias, head_norm=head_norm, sgu_norm=sgu_norm,
             w_spatial=w_spatial, b_spatial=b_spatial, w_branch_a=w_branch_a, w_branch_b=w_branch_b, w_out=w_out, norm_ffn=norm_ffn,
             w_ffn_up=w_ffn_up, conv_ffn=conv_ffn, b_conv_ffn=b_conv_ffn, w_ffn_down=w_ffn_down, norm_ple=norm_ple,
             w_ple_gate=w_ple_gate, w_ple_proj=w_ple_proj, norm_final=norm_final)
    m = dict(norm_mix=m_norm_mix, w_in=m_w_in, conv_qkv=m_conv_qkv, a_log=m_a_log, dt_bias=m_dt_bias, head_norm=m_head_norm,
             sgu_norm=m_sgu_norm, w_spatial=m_w_spatial, b_spatial=m_b_spatial, w_branch_a=m_w_branch_a, w_branch_b=m_w_branch_b,
             w_out=m_w_out, norm_ffn=m_norm_ffn, w_ffn_up=m_w_ffn_up, conv_ffn=m_conv_ffn, b_conv_ffn=m_b_conv_ffn,
             w_ffn_down=m_w_ffn_down, norm_ple=m_norm_ple, w_ple_gate=m_w_ple_gate, w_ple_proj=m_w_ple_proj, norm_final=m_norm_final)
    v = dict(norm_mix=v_norm_mix, w_in=v_w_in, conv_qkv=v_conv_qkv, a_log=v_a_log, dt_bias=v_dt_bias, head_norm=v_head_norm,
             sgu_norm=v_sgu_norm, w_spatial=v_w_spatial, b_spatial=v_b_spatial, w_branch_a=v_w_branch_a, w_branch_b=v_w_branch_b,
             w_out=v_w_out, norm_ffn=v_norm_ffn, w_ffn_up=v_w_ffn_up, conv_ffn=v_conv_ffn, b_conv_ffn=v_b_conv_ffn,
             w_ffn_down=v_w_ffn_down, norm_ple=v_norm_ple, w_ple_gate=v_w_ple_gate, w_ple_proj=v_w_ple_proj, norm_final=v_norm_final)
    chip = 2 * lax.axis_index("x") + lax.axis_index("y")
    core = lax.axis_index("c")

    def by_chip(own, gathered):
        return [jnp.where(chip == j, own, gathered[j]) for j in range(N_CHIPS)]

    conv_buf = _flat([w[k] for k in SMALL_COLS], f32)
    (conv_all,) = _gather_shards([conv_buf], "gather_conv")
    sm = {k: w[k] for k in SMALL_REPL}
    conv_parts = [_unflat(b, [w[k].shape for k in SMALL_COLS]) for b in by_chip(conv_buf, conv_all)]
    for n, k in enumerate(SMALL_COLS):
        sm[k] = jnp.concatenate([conv_parts[j][n] for j in range(N_CHIPS)], axis=-1)
    Ws = []
    for i in range(DEPTH):
        own = [w[k][i].astype(bf16) for k in BIG]
        gathered = _gather_shards(own, f"gather_l{i}")
        Ws.append(_big_weights({k: by_chip(o, g) for k, o, g in zip(BIG, own, gathered)}))
    Ss = [_small_params(sm, i) for i in range(DEPTH)]

    h = x[0]
    saved = []
    for i in range(DEPTH):
        h, sv = _layer_fwd(h, p[i, 0], Ws[i], Ss[i], i)
        saved.append(sv)
    loss_row, dh, g_final = _loss_head(h, sm["norm_final"][None], loss_target[0], "loss_head")

    gsm = {k: [None] * DEPTH for k in SMALL_REPL + SMALL_COLS if k != "norm_final"}
    gbig = {k: [None] * DEPTH for k in BIG}
    for i in reversed(range(DEPTH)):
        dh, G = _layer_bwd(dh, saved[i], Ws[i], Ss[i], i)
        parts = _grads_by_chip(G)
        for k, g in zip(BIG, _reduce_scatter([parts[k] for k in BIG], BIG, chip, core, f"l{i}")):
            gbig[k][i] = g
        for k in gsm:
            gsm[k][i] = G[k].reshape(w[k].shape[1:-1] + (-1,)) if k in SMALL_COLS else G[k].reshape(w[k].shape[1:])

    small_names = [k for k in SMALL_REPL + SMALL_COLS if k != "norm_final"]
    small_local = [jnp.stack(gsm[k]) for k in small_names] + [g_final.reshape(-1), loss_row[0, :1]]
    small_shapes = [a.shape for a in small_local]
    small_sum = _unflat(_allreduce_small(_flat(small_local, f32), "allreduce_small"), small_shapes)
    gs = dict(zip(small_names + ["norm_final"], small_sum[:-1]))
    loss = small_sum[-1][0]
    for k in SMALL_COLS:
        n = w[k].shape[-1]
        gs[k] = _pick(chip, [gs[k][..., j * n:(j + 1) * n] for j in range(N_CHIPS)])

    grads, deltas, new_m, new_v = {}, {}, {}, {}
    for k in BIG:
        g = jnp.stack(gbig[k])
        two = lambda a: a.reshape(-1, a.shape[-1])
        d, m2, v2 = _ew(_adamw, [two(w[k]), two(g), two(m[k]), two(v[k])], [f32, f32, f32], f"adamw_{k}")
        grads[k], deltas[k], new_m[k], new_v[k] = g, d.reshape(g.shape), m2.reshape(g.shape), v2.reshape(g.shape)
    small_all = [k for k in WEIGHTS if k not in BIG]
    shapes = [w[k].shape for k in small_all]
    d, m2, v2 = _ew(_adamw, [_flat([w[k] for k in small_all], f32), _flat([gs[k] for k in small_all], f32),
                             _flat([m[k] for k in small_all], f32), _flat([v[k] for k in small_all], f32)],
                    [f32, f32, f32], "adamw_small")
    for k, a, b, c_ in zip(small_all, _unflat(d, shapes), _unflat(m2, shapes), _unflat(v2, shapes)):
        grads[k], deltas[k], new_m[k], new_v[k] = gs[k], a, b, c_
    return (loss, dh[None], *[grads[k] for k in WEIGHTS], *[deltas[k] for k in WEIGHTS],
            *[new_m[k] for k in WEIGHTS], *[new_v[k] for k in WEIGHTS])
```

```python
import functools
import math

import jax
import jax.numpy as jnp
from jax import lax
from jax.experimental import pallas as pl
from jax.experimental.pallas import tpu as pltpu

f32 = jnp.float32
bf16 = jnp.bfloat16
HI = lax.Precision.HIGHEST
SDS = jax.ShapeDtypeStruct

D_MODEL = 2048
DEPTH = 4
HEAD_DIM = 128
N_QK = 8
N_V = 16
QK_W = N_QK * HEAD_DIM
V_W = N_V * HEAD_DIM
CHUNK_A = 64
N_GROUPS = 8
GROUP_DIM = 128
WIDTH_B = N_GROUPS * GROUP_DIM
CHUNK_B = 128
D_FF = 5632
PLE_DIM = 256
EPS = 1e-6
N_IN = 12320
ADAM_LR, ADAM_B1, ADAM_B2, ADAM_EPS, ADAM_WD, ADAM_STEP = 0.001, 0.9, 0.999, 1e-08, 0.01, 10

C_Q, C_K, C_V, C_Z, C_UB, C_VB, C_GA, C_GB, PM = 0, 1024, 2048, 4096, 6144, 7168, 8192, 10240, 12288
LANES = 128
VMEM_LIMIT_BYTES = 48 * 1024 * 1024
ROW_CHUNK = 256


def _cp(*sem):
    return pltpu.CompilerParams(dimension_semantics=sem if sem else None, vmem_limit_bytes=VMEM_LIMIT_BYTES)


def _dot(a, b, prec=None):
    return jnp.dot(a, b, preferred_element_type=f32, precision=prec)


def _dot_nt(a, b, prec=None):
    return lax.dot_general(a, b, (((1,), (1,)), ((), ())), preferred_element_type=f32, precision=prec)


def _dot_tn(a, b, prec=None):
    return lax.dot_general(a, b, (((0,), (0,)), ((), ())), preferred_element_type=f32, precision=prec)


def _b(x):
    return x.astype(bf16)


def _sig(x):
    return 1.0 / (1.0 + jnp.exp(-x))


def _silu(x):
    return x * _sig(x)


def _dsilu(x):
    s = _sig(x)
    return s * (1.0 + x * (1.0 - s))


_GELU_C = 0.7978845608028654
_GELU_A = 0.044715


def _gelu(x):
    return 0.5 * x * (1.0 + jnp.tanh(_GELU_C * (x + _GELU_A * x * x * x)))


def _dgelu(x):
    t = jnp.tanh(_GELU_C * (x + _GELU_A * x * x * x))
    return 0.5 * (1.0 + t) + 0.5 * x * (1.0 - t * t) * _GELU_C * (1.0 + 3.0 * _GELU_A * x * x)


def _softplus(x):
    return jnp.maximum(x, 0.0) + jnp.log(1.0 + jnp.exp(-jnp.abs(x)))


def _divisor(n, cap, mult):
    best = None
    d = mult
    while d <= min(n, cap):
        if n % d == 0:
            best = d
        d += mult
    return best if best is not None else n


def _rows(tr, c, j=0):
    return pl.BlockSpec((tr, c), lambda i, j=j: (i, j))


def _whole(shape):
    nd = len(shape)
    return pl.BlockSpec(shape, lambda *_: (0,) * nd)


def _mm(a, b, mode, name, out_dtype=f32, c=None, tm=1024, tn=1024, tk=2048):
    if mode == "nn":
        (M, K), (K2, N) = a.shape, b.shape
    elif mode == "nt":
        (M, K), (N, K2) = a.shape, b.shape
    else:
        (K, M), (K2, N) = a.shape, b.shape
    assert K == K2, (a.shape, b.shape, mode)
    tm = _divisor(M, tm, LANES if mode == "tn" else 16)
    tn = _divisor(N, tn, LANES)
    tk = _divisor(K, tk, LANES if mode != "tn" else 16)
    nk = K // tk
    if mode == "nn":
        a_spec = pl.BlockSpec((tm, tk), lambda i, j, k: (i, k))
        b_spec = pl.BlockSpec((tk, tn), lambda i, j, k: (k, j))
        dn = (((1,), (0,)), ((), ()))
    elif mode == "nt":
        a_spec = pl.BlockSpec((tm, tk), lambda i, j, k: (i, k))
        b_spec = pl.BlockSpec((tn, tk), lambda i, j, k: (j, k))
        dn = (((1,), (1,)), ((), ()))
    else:
        a_spec = pl.BlockSpec((tk, tm), lambda i, j, k: (k, i))
        b_spec = pl.BlockSpec((tk, tn), lambda i, j, k: (k, j))
        dn = (((0,), (0,)), ((), ()))
    o_spec = pl.BlockSpec((tm, tn), lambda i, j, k: (i, j))
    has_c = c is not None

    def body(*refs):
        a_ref, b_ref = refs[0], refs[1]
        c_ref = refs[2] if has_c else None
        o_ref = refs[3] if has_c else refs[2]
        part = lax.dot_general(_b(a_ref[...]), _b(b_ref[...]), dn, preferred_element_type=f32)
        if nk == 1:
            if has_c:
                part = part + c_ref[...]
            o_ref[...] = part.astype(o_ref.dtype)
        else:
            acc = refs[-1]
            k = pl.program_id(2)

            @pl.when(k == 0)
            def _():
                acc[...] = part

            @pl.when(k > 0)
            def _():
                acc[...] += part

            @pl.when(k == nk - 1)
            def _():
                r = acc[...]
                if has_c:
                    r = r + c_ref[...]
                o_ref[...] = r.astype(o_ref.dtype)

    ins = [a, b] + ([c] if has_c else [])
    in_specs = [a_spec, b_spec] + ([o_spec] if has_c else [])
    return pl.pallas_call(
        body, grid=(M // tm, N // tn, nk), in_specs=in_specs, out_specs=o_spec, out_shape=SDS((M, N), out_dtype),
        scratch_shapes=[pltpu.VMEM((tm, tn), f32)] if nk > 1 else [], name=name,
        compiler_params=_cp("parallel", "parallel", "arbitrary"),
    )(*ins)


def _ew(fn, ins, out_dtypes, name, tile_bytes=2 * 1024 * 1024):
    R, C = max((x.shape for x in ins), key=lambda s: s[0])
    n_in = len(ins)
    row_bytes = 4 * C * (len(ins) + len(out_dtypes))
    tr = _divisor(R, max(16, tile_bytes // row_bytes), 16)
    in_specs = [_rows(tr, C) if x.shape[0] == R else _whole((1, C)) for x in ins]

    def body(*refs):
        res = fn(*[r[...] for r in refs[:n_in]])
        for o_ref, r in zip(refs[n_in:], res):
            o_ref[...] = r.astype(o_ref.dtype)

    outs = pl.pallas_call(
        body, grid=(R // tr,), in_specs=in_specs, out_specs=[_rows(tr, C) for _ in out_dtypes],
        out_shape=[SDS((R, C), dt) for dt in out_dtypes], name=name, compiler_params=_cp("parallel"),
    )(*ins)
    return outs


def _rms_fwd(x, gain, name):
    T, Dm = x.shape
    tr = _divisor(T, 256, 16)

    def body(x_ref, g_ref, o_ref):
        xv = x_ref[...]
        r = lax.rsqrt(jnp.mean(xv * xv, axis=-1, keepdims=True) + EPS)
        o_ref[...] = (xv * r * g_ref[...]).astype(o_ref.dtype)

    return pl.pallas_call(
        body, grid=(T // tr,), in_specs=[_rows(tr, Dm), _whole((1, Dm))], out_specs=_rows(tr, Dm),
        out_shape=SDS((T, Dm), bf16), name=name, compiler_params=_cp("parallel"),
    )(x, gain)


def _rms_bwd(x, gain, dh, dres, name):
    T, Dm = x.shape
    tr = _divisor(T, 256, 16)

    def body(x_ref, g_ref, dh_ref, dres_ref, dx_ref, dg_ref):
        xv = x_ref[...]
        dhv = dh_ref[...]
        r = lax.rsqrt(jnp.mean(xv * xv, axis=-1, keepdims=True) + EPS)
        dy = dhv * g_ref[...]
        m = jnp.mean(dy * xv, axis=-1, keepdims=True)
        dx_ref[...] = dres_ref[...] + r * dy - xv * (r * r * r * m)
        part = jnp.sum(dhv * xv * r, axis=0, keepdims=True)

        @pl.when(pl.program_id(0) == 0)
        def _():
            dg_ref[...] = part

        @pl.when(pl.program_id(0) > 0)
        def _():
            dg_ref[...] += part

    return pl.pallas_call(
        body, grid=(T // tr,), in_specs=[_rows(tr, Dm), _whole((1, Dm)), _rows(tr, Dm), _rows(tr, Dm)],
        out_specs=[_rows(tr, Dm), _whole((1, Dm))], out_shape=[SDS((T, Dm), f32), SDS((1, Dm), f32)],
        name=name, compiler_params=_cp("arbitrary"),
    )(x, gain, dh, dres)


def _shift_down(cur, prev8, s):
    rolled = pltpu.roll(cur, s, 0)
    rp = pltpu.roll(prev8, s, 0)
    row8 = lax.broadcasted_iota(jnp.int32, prev8.shape, 0)
    first = jnp.where(row8 < s, rp, rolled[:8])
    return jnp.concatenate([first, rolled[8:]], axis=0)


def _shift_up(cur, next8, s):
    R = cur.shape[0]
    rolled = pltpu.roll(cur, R - s, 0)
    rn = pltpu.roll(next8, 8 - s, 0)
    row8 = lax.broadcasted_iota(jnp.int32, next8.shape, 0)
    last = jnp.where(row8 >= 8 - s, rn, rolled[R - 8:])
    return jnp.concatenate([rolled[: R - 8], last], axis=0)


def _prev8(ref, r0, ci):
    rows = ref[pl.ds(pl.multiple_of(jnp.maximum(r0 - 8, 0), 8), 8), :]
    return jnp.where(ci > 0, rows, 0.0)


def _next8(ref, r0, R, ci, n_chunks):
    start = jnp.minimum(r0 + R, (n_chunks - 1) * R + R - 8)
    rows = ref[pl.ds(pl.multiple_of(start, 8), 8), :]
    return jnp.where(ci < n_chunks - 1, rows, 0.0)


def _w_rows(w_ref, k):
    return [w_ref[j:j + 1, :] for j in range(k)]


def _causal_conv(cur, prev8, w, k):
    acc = cur * w[k - 1]
    for s in range(1, k):
        acc = acc + _shift_down(cur, prev8, s) * w[k - 1 - s]
    return acc


def _qkvconv_fwd(proj, convw, name):
    T = proj.shape[0]
    R = min(ROW_CHUNK, T)
    n_chunks = T // R
    n_blk = (2 * QK_W + V_W) // HEAD_DIM

    def body(x_ref, w_ref, o_ref):
        p = pl.program_id(0)
        is_qk = p < 2 * N_QK
        scale = jnp.where(p < N_QK, HEAD_DIM ** -0.5, 1.0).astype(f32)
        w = _w_rows(w_ref, 4)

        def chunk(ci, carry):
            r0 = pl.multiple_of(ci * R, R)
            cur = x_ref[pl.ds(r0, R), :]
            y = _silu(_causal_conv(cur, _prev8(x_ref, r0, ci), w, 4))
            ss = jnp.sum(y * y, axis=-1, keepdims=True)
            nrm = jnp.where(is_qk, lax.rsqrt(ss + EPS) * scale, 1.0)
            o_ref[pl.ds(r0, R), :] = y * nrm
            return carry

        lax.fori_loop(0, n_chunks, chunk, 0)

    return pl.pallas_call(
        body, grid=(n_blk,),
        in_specs=[pl.BlockSpec((T, HEAD_DIM), lambda p: (0, p)), pl.BlockSpec((4, HEAD_DIM), lambda p: (0, p))],
        out_specs=pl.BlockSpec((T, HEAD_DIM), lambda p: (0, p)), out_shape=SDS((T, n_blk * HEAD_DIM), f32),
        name=name, compiler_params=_cp("parallel"),
    )(proj, convw)


def _qkvconv_bwd(proj, convw, dqkv, name):
    T = proj.shape[0]
    R = min(ROW_CHUNK, T)
    n_chunks = T // R
    n_blk = (2 * QK_W + V_W) // HEAD_DIM

    def body(x_ref, w_ref, do_ref, dx_ref, dw_ref, dc_sc):
        p = pl.program_id(0)
        is_qk = p < 2 * N_QK
        scale = jnp.where(p < N_QK, HEAD_DIM ** -0.5, 1.0).astype(f32)
        w = _w_rows(w_ref, 4)

        def phase1(ci, dw):
            r0 = pl.multiple_of(ci * R, R)
            cur = x_ref[pl.ds(r0, R), :]
            p8 = _prev8(x_ref, r0, ci)
            shifted = [cur] + [_shift_down(cur, p8, s) for s in range(1, 4)]
            c = shifted[0] * w[3]
            for s in range(1, 4):
                c = c + shifted[s] * w[3 - s]
            y = _silu(c)
            dout = do_ref[pl.ds(r0, R), :]
            n = lax.rsqrt(jnp.sum(y * y, axis=-1, keepdims=True) + EPS)
            dot_ = jnp.sum(dout * y, axis=-1, keepdims=True)
            dy = jnp.where(is_qk, scale * (n * dout - y * (n * n * n * dot_)), dout)
            dc = dy * _dsilu(c)
            dc_sc[pl.ds(r0, R), :] = dc
            return tuple(dw[j] + jnp.sum(dc * shifted[3 - j], axis=0, keepdims=True) for j in range(4))

        dw = lax.fori_loop(0, n_chunks, phase1, tuple(jnp.zeros((1, HEAD_DIM), f32) for _ in range(4)))
        for j in range(4):
            dw_ref[j:j + 1, :] = dw[j]

        def phase2(ci, carry):
            r0 = pl.multiple_of(ci * R, R)
            cur = dc_sc[pl.ds(r0, R), :]
            n8 = _next8(dc_sc, r0, R, ci, n_chunks)
            acc = cur * w[3]
            for s in range(1, 4):
                acc = acc + _shift_up(cur, n8, s) * w[3 - s]
            dx_ref[pl.ds(r0, R), :] = acc.astype(dx_ref.dtype)
            return carry

        lax.fori_loop(0, n_chunks, phase2, 0)

    col = lambda p: (0, p)
    return pl.pallas_call(
        body, grid=(n_blk,),
        in_specs=[pl.BlockSpec((T, HEAD_DIM), col), pl.BlockSpec((4, HEAD_DIM), col), pl.BlockSpec((T, HEAD_DIM), col)],
        out_specs=[pl.BlockSpec((T, HEAD_DIM), col), pl.BlockSpec((4, HEAD_DIM), col)],
        out_shape=[SDS((T, n_blk * HEAD_DIM), bf16), SDS((4, n_blk * HEAD_DIM), f32)],
        scratch_shapes=[pltpu.VMEM((T, HEAD_DIM), f32)], name=name, compiler_params=_cp("parallel"),
    )(proj, convw, dqkv)


def _tri_masks(C):
    row = lax.broadcasted_iota(jnp.int32, (C, C), 0)
    col = lax.broadcasted_iota(jnp.int32, (C, C), 1)
    return row, col


def _lane_pick(blk, lane, idx):
    return jnp.sum(jnp.where(lane == idx, blk, 0.0), axis=1, keepdims=True)


def _gate_block(ba, ea, dtb, lane):
    sig = _sig(ba)
    gblk = -ea * _softplus(ba + dtb)
    return sig, gblk


def _head_decay(gam_all, rg_all, tot, lane, h, row, col):
    C = row.shape[0]
    gam_c = _lane_pick(gam_all, lane, N_V + h)
    rg_c = _lane_pick(rg_all, lane, N_V + h)
    lane1 = lax.broadcasted_iota(jnp.int32, (1, LANES), 1)
    tot_h = jnp.sum(jnp.where(lane1 == N_V + h, tot, 0.0), axis=1, keepdims=True)
    gcb = jnp.broadcast_to(gam_c, (C, C))
    dlt = gcb - gcb.T
    dm = jnp.where(row >= col, jnp.exp(jnp.minimum(dlt, 0.0)), 0.0)
    return gam_c, rg_c, tot_h, dm


def _delta_prep(qkv, ba, ea_row, dtb_row, name):
    T = qkv.shape[0]
    C = CHUNK_A
    N = T // C

    SUB = 16

    def body(q_ref, k_ref, v_ref, ba_ref, ea_ref, dtb_ref,
             qd_ref, kd_ref, u_ref, w_ref, attn_ref, tinv_ref, cd_ref, bg_ref, at_sc, t_sc, a_sc, rk_sc):
        row, col = _tri_masks(C)
        lane = lax.broadcasted_iota(jnp.int32, (C, LANES), 1)
        sig, gblk = _gate_block(ba_ref[...], ea_ref[...], dtb_ref[...], lane)
        bg_ref[...] = jnp.where(lane < N_V, sig, gblk)
        lower = (row >= col).astype(f32)
        upper_s = (col > row).astype(f32)
        eye = (row == col).astype(f32)
        same16 = (row >> 4) == (col >> 4)
        same32 = (row >> 5) == (col >> 5)
        gam_all = _dot(lower, gblk, HI)
        rg_all = _dot(upper_s, gblk, HI)
        tot = jnp.sum(gblk, axis=0, keepdims=True)
        for h in range(N_V):
            qk = pl.ds((h // 2) * HEAD_DIM, HEAD_DIM)
            hs = pl.ds(h * HEAD_DIM, HEAD_DIM)
            qh, kh, vh = q_ref[:, qk], k_ref[:, qk], v_ref[:, hs]
            beta_c = _lane_pick(sig, lane, h)
            gam_c, rg_c, tot_h, dm = _head_decay(gam_all, rg_all, tot, lane, h, row, col)
            kk = _dot_nt(_b(kh), _b(kh))
            a = jnp.where(row > col, beta_c * kk * dm, 0.0)
            a_sc[h] = a
            at_sc[h] = jnp.where(same16, a, 0.0).T
            t_sc[h] = eye
            eg = jnp.exp(gam_c)
            attn_ref[0, h] = (_dot_nt(_b(qh), _b(kh)) * dm).astype(attn_ref.dtype)
            qd_ref[:, hs] = (qh * eg).astype(qd_ref.dtype)
            kd_ref[:, hs] = (kh * jnp.exp(rg_c)).astype(kd_ref.dtype)
            cd_ref[0, h] = jnp.broadcast_to(jnp.exp(tot_h), (1, LANES))
            u_ref[:, hs] = vh * beta_c
            rk_sc[:, hs] = kh * (beta_c * eg)

        first_col = (row >> 4) << 4

        def fsub(i, carry):
            for h in range(N_V):
                t = t_sc[h]
                a_col = jnp.sum(jnp.where(col == first_col + i, at_sc[h], 0.0), axis=1, keepdims=True)
                prod = a_col * t
                sums = [jnp.sum(prod[b * SUB:(b + 1) * SUB], axis=0, keepdims=True) for b in range(C // SUB)]
                new = eye - jnp.concatenate([jnp.broadcast_to(s, (SUB, C)) for s in sums], axis=0)
                t_sc[h] = jnp.where(row - first_col == i, new, t)
            return carry

        lax.fori_loop(1, SUB, fsub, 0)
        for h in range(N_V):
            hs = pl.ds(h * HEAD_DIM, HEAD_DIM)
            a = a_sc[h]
            p16 = t_sc[h]
            low16 = jnp.where(jnp.logical_and(same32, jnp.logical_not(same16)), a, 0.0)
            p32 = p16 - _dot(_dot(p16, low16, HI), p16, HI)
            tinv = p32 - _dot(_dot(p32, jnp.where(same32, 0.0, a), HI), p32, HI)
            tinv_ref[0, h] = tinv
            u_ref[:, hs] = _dot(tinv, u_ref[:, hs], HI)
            w_ref[:, hs] = _dot(tinv, rk_sc[:, hs], HI).astype(w_ref.dtype)

    big = lambda n: (n, 0)
    return pl.pallas_call(
        body, grid=(N,),
        in_specs=[pl.BlockSpec((C, QK_W), lambda n: (n, 0)), pl.BlockSpec((C, QK_W), lambda n: (n, 1)),
                  pl.BlockSpec((C, V_W), lambda n: (n, 1)), pl.BlockSpec((C, LANES), big),
                  _whole((1, LANES)), _whole((1, LANES))],
        out_specs=[pl.BlockSpec((C, V_W), big)] * 4 + [
            pl.BlockSpec((1, N_V, C, C), lambda n: (n, 0, 0, 0)), pl.BlockSpec((1, N_V, C, C), lambda n: (n, 0, 0, 0)),
            pl.BlockSpec((1, N_V, 1, LANES), lambda n: (n, 0, 0, 0)), pl.BlockSpec((C, LANES), big)],
        out_shape=[SDS((T, V_W), bf16), SDS((T, V_W), bf16), SDS((T, V_W), f32), SDS((T, V_W), bf16),
                   SDS((N, N_V, C, C), bf16), SDS((N, N_V, C, C), f32), SDS((N, N_V, 1, LANES), f32), SDS((T, LANES), f32)],
        scratch_shapes=[pltpu.VMEM((N_V, C, C), f32)] * 3 + [pltpu.VMEM((C, V_W), f32)],
        name=name, compiler_params=_cp("parallel"),
    )(qkv, qkv, qkv, ba, ea_row, dtb_row)


def _delta_scan(qd, kd, u, w, attn, cd, name):
    T = qd.shape[0]
    C = CHUNK_A
    N = T // C

    def body(qd_ref, kd_ref, u_ref, w_ref, attn_ref, cd_ref, o_ref, s_ref, vn_ref, s_sc):
        @pl.when(pl.program_id(0) == 0)
        def _():
            s_sc[...] = jnp.zeros_like(s_sc)

        for h in range(N_V):
            hs = pl.ds(h * HEAD_DIM, HEAD_DIM)
            s = s_sc[h]
            s_ref[0, h] = s
            sb = _b(s)
            vn = u_ref[:, hs] - _dot(w_ref[:, hs], sb)
            vnb = _b(vn)
            vn_ref[:, hs] = vnb
            o_ref[:, hs] = _dot(qd_ref[:, hs], sb) + _dot(attn_ref[0, h], vnb)
            s_sc[h] = s * cd_ref[0, h] + _dot_tn(kd_ref[:, hs], vnb)

    blk = pl.BlockSpec((C, V_W), lambda n: (n, 0))
    per_head = lambda a, b: pl.BlockSpec((1, N_V, a, b), lambda n: (n, 0, 0, 0))
    return pl.pallas_call(
        body, grid=(N,), in_specs=[blk, blk, blk, blk, per_head(C, C), per_head(1, LANES)],
        out_specs=[blk, per_head(HEAD_DIM, HEAD_DIM), blk],
        out_shape=[SDS((T, V_W), f32), SDS((N, N_V, HEAD_DIM, HEAD_DIM), f32), SDS((T, V_W), bf16)],
        scratch_shapes=[pltpu.VMEM((N_V, HEAD_DIM, HEAD_DIM), f32)], name=name, compiler_params=_cp("arbitrary"),
    )(qd, kd, u, w, attn, cd)


def _delta_scan_bwd(do, qd, kd, w, attn, cd, s_all, vn, name):
    T = qd.shape[0]
    C = CHUNK_A
    N = T // C

    def body(do_ref, qd_ref, kd_ref, w_ref, attn_ref, cd_ref, s_ref, vn_ref,
             dqd_ref, dkd_ref, du_ref, dw_ref, dattn_ref, dcd_ref, ds_sc):
        @pl.when(pl.program_id(0) == 0)
        def _():
            ds_sc[...] = jnp.zeros_like(ds_sc)

        row, col = _tri_masks(C)
        for h in range(N_V):
            hs = pl.ds(h * HEAD_DIM, HEAD_DIM)
            dsn = ds_sc[h]
            s = s_ref[0, h]
            dob, sb, dsb = _b(do_ref[:, hs]), _b(s), _b(dsn)
            vnb = vn_ref[:, hs]
            dqd_ref[:, hs] = _dot_nt(dob, sb)
            dattn_ref[0, h] = jnp.where(row >= col, _dot_nt(dob, vnb), 0.0)
            dvn = _dot_tn(attn_ref[0, h], dob) + _dot(kd_ref[:, hs], dsb)
            dvnb = _b(dvn)
            dkd_ref[:, hs] = _dot_nt(vnb, dsb)
            dcd = jnp.sum(jnp.sum(s * dsn, axis=1, keepdims=True), axis=0, keepdims=True)
            dcd_ref[0, h] = jnp.broadcast_to(dcd, (1, LANES))
            du_ref[:, hs] = dvn
            dw_ref[:, hs] = -_dot_nt(dvnb, sb)
            ds_sc[h] = dsn * cd_ref[0, h] + _dot_tn(qd_ref[:, hs], dob) - _dot_tn(w_ref[:, hs], dvnb)

    blk = pl.BlockSpec((C, V_W), lambda n: (N - 1 - n, 0))
    per_head = lambda a, b: pl.BlockSpec((1, N_V, a, b), lambda n: (N - 1 - n, 0, 0, 0))
    return pl.pallas_call(
        body, grid=(N,),
        in_specs=[blk, blk, blk, blk, per_head(C, C), per_head(1, LANES), per_head(HEAD_DIM, HEAD_DIM), blk],
        out_specs=[blk, blk, blk, blk, per_head(C, C), per_head(1, LANES)],
        out_shape=[SDS((T, V_W), f32)] * 4 + [SDS((N, N_V, C, C), f32), SDS((N, N_V, 1, LANES), f32)],
        scratch_shapes=[pltpu.VMEM((N_V, HEAD_DIM, HEAD_DIM), f32)], name=name, compiler_params=_cp("arbitrary"),
    )(do, qd, kd, w, attn, cd, s_all, vn)


def _delta_prep_bwd(qkv, ba, ea_row, dtb_row, tinv, u, w, dqd, dkd, du, dw, dattn, dcd, name):
    T = qkv.shape[0]
    C = CHUNK_A
    N = T // C

    def body(q_ref, k_ref, v_ref, ba_ref, ea_ref, dtb_ref, tinv_ref, u_ref, w_ref,
             dqd_ref, dkd_ref, du_ref, dw_ref, dattn_ref, dcd_ref,
             dqkv_ref, dba_ref, dalog_ref, ddtb_ref):
        row, col = _tri_masks(C)
        lane = lax.broadcasted_iota(jnp.int32, (C, LANES), 1)
        rowc = lax.broadcasted_iota(jnp.int32, (C, 1), 0)
        ba_v, ea, dtb = ba_ref[...], ea_ref[...], dtb_ref[...]
        sig, gblk = _gate_block(ba_v, ea, dtb, lane)
        lower = (row >= col).astype(f32)
        upper_s = (col > row).astype(f32)
        upper = (col >= row).astype(f32)
        gam_all = _dot(lower, gblk, HI)
        rg_all = _dot(upper_s, gblk, HI)
        tot = jnp.sum(gblk, axis=0, keepdims=True)
        dbeta_blk = jnp.zeros((C, LANES), f32)
        dgam_blk = jnp.zeros((C, LANES), f32)
        for j in range(N_QK):
            qk = pl.ds(j * HEAD_DIM, HEAD_DIM)
            qh, kh = q_ref[:, qk], k_ref[:, qk]
            qhb, khb = _b(qh), _b(kh)
            kk = _dot_nt(khb, khb)
            qkm = _dot_nt(qhb, khb)
            dq_j = jnp.zeros((C, HEAD_DIM), f32)
            dk_j = jnp.zeros((C, HEAD_DIM), f32)
            for h in (2 * j, 2 * j + 1):
                hs = pl.ds(h * HEAD_DIM, HEAD_DIM)
                vh = v_ref[:, hs]
                beta_c = _lane_pick(sig, lane, h)
                gam_c, rg_c, tot_h, dm = _head_decay(gam_all, rg_all, tot, lane, h, row, col)
                eg, er, cdh = jnp.exp(gam_c), jnp.exp(rg_c), jnp.exp(tot_h)
                tinv_h = tinv_ref[0, h]
                d_rv = _dot_tn(tinv_h, du_ref[:, hs], HI)
                d_rk = _dot_tn(tinv_h, dw_ref[:, hs], HI)
                da = -jnp.where(row > col, _dot_nt(_b(d_rv), _b(u_ref[:, hs])) + _dot_nt(_b(d_rk), _b(w_ref[:, hs])), 0.0)
                dqkv_ref[:, pl.ds(2 * QK_W + h * HEAD_DIM, HEAD_DIM)] = beta_c * d_rv
                dbeta = jnp.sum(d_rv * vh + d_rk * (eg * kh), axis=1, keepdims=True)
                dk_h = (beta_c * eg) * d_rk
                d_eg = jnp.sum(d_rk * kh, axis=1, keepdims=True) * beta_c
                bkd = da * dm
                dbeta = dbeta + jnp.sum(bkd * kk, axis=1, keepdims=True)
                dkk = bkd * beta_c
                ddm = da * beta_c * kk
                dattn_h = dattn_ref[0, h]
                dqk = dattn_h * dm
                ddm = ddm + dattn_h * qkm
                dqd_h, dkd_h = dqd_ref[:, hs], dkd_ref[:, hs]
                dq_j = dq_j + _dot(_b(dqk), khb) + eg * dqd_h
                dk_h = dk_h + _dot_tn(_b(dqk), qhb) + _dot(_b(dkk + dkk.T), khb) + er * dkd_h
                dk_j = dk_j + dk_h
                d_eg = d_eg + jnp.sum(dqd_h * qh, axis=1, keepdims=True)
                d_er = jnp.sum(dkd_h * kh, axis=1, keepdims=True)
                e = ddm * dm
                dgam = jnp.sum(e, axis=1, keepdims=True) - jnp.sum(e.T, axis=1, keepdims=True)
                dgam = dgam + d_eg * eg - d_er * er
                extra = jnp.sum(d_er * er, axis=0, keepdims=True) + jnp.max(dcd_ref[0, h], axis=1, keepdims=True) * cdh
                dgam = dgam + jnp.where(rowc == C - 1, extra, 0.0)
                dbeta_blk = jnp.where(lane == h, dbeta, dbeta_blk)
                dgam_blk = jnp.where(lane == N_V + h, dgam, dgam_blk)
            dqkv_ref[:, qk] = dq_j
            dqkv_ref[:, pl.ds(QK_W + j * HEAD_DIM, HEAD_DIM)] = dk_j
        dg_all = _dot(upper, dgam_blk, HI)
        dsp = dg_all * (-ea) * _sig(ba_v + dtb)
        dba_ref[...] = jnp.where(lane < N_V, dbeta_blk * sig * (1.0 - sig), dsp)
        part_alog = jnp.sum(dg_all * gblk, axis=0, keepdims=True)
        part_dtb = jnp.sum(dsp, axis=0, keepdims=True)

        @pl.when(pl.program_id(0) == 0)
        def _():
            dalog_ref[...] = part_alog
            ddtb_ref[...] = part_dtb

        @pl.when(pl.program_id(0) > 0)
        def _():
            dalog_ref[...] += part_alog
            ddtb_ref[...] += part_dtb

    big = lambda n: (n, 0)
    wide = pl.BlockSpec((C, V_W), big)
    sq = pl.BlockSpec((1, N_V, C, C), lambda n: (n, 0, 0, 0))
    return pl.pallas_call(
        body, grid=(N,),
        in_specs=[pl.BlockSpec((C, QK_W), lambda n: (n, 0)), pl.BlockSpec((C, QK_W), lambda n: (n, 1)),
                  pl.BlockSpec((C, V_W), lambda n: (n, 1)), pl.BlockSpec((C, LANES), big),
                  _whole((1, LANES)), _whole((1, LANES)), sq, wide, wide, wide, wide, wide, wide, sq,
                  pl.BlockSpec((1, N_V, 1, LANES), lambda n: (n, 0, 0, 0))],
        out_specs=[pl.BlockSpec((C, 2 * QK_W + V_W), big), pl.BlockSpec((C, LANES), big),
                   _whole((1, LANES)), _whole((1, LANES))],
        out_shape=[SDS((T, 2 * QK_W + V_W), f32), SDS((T, LANES), f32), SDS((1, LANES), f32), SDS((1, LANES), f32)],
        name=name, compiler_params=_cp("arbitrary"),
    )(qkv, qkv, qkv, ba, ea_row, dtb_row, tinv, u, w, dqd, dkd, du, dw, dattn, dcd)


def _onorm_fwd(o, proj, hg, name):
    T = o.shape[0]
    tr = _divisor(T, 256, 16)

    def body(o_ref, z_ref, g_ref, out_ref):
        g = g_ref[...]
        for h in range(N_V):
            hs = pl.ds(h * HEAD_DIM, HEAD_DIM)
            oh = o_ref[:, hs]
            r = lax.rsqrt(jnp.mean(oh * oh, axis=-1, keepdims=True) + EPS)
            out_ref[:, hs] = (oh * r * g * _silu(z_ref[:, hs])).astype(out_ref.dtype)

    return pl.pallas_call(
        body, grid=(T // tr,), in_specs=[_rows(tr, V_W), _rows(tr, V_W, C_Z // V_W), _whole((1, HEAD_DIM))],
        out_specs=_rows(tr, V_W), out_shape=SDS((T, V_W), bf16), name=name, compiler_params=_cp("parallel"),
    )(o, proj, hg)


def _onorm_bwd(don, o, proj, hg, name):
    T = o.shape[0]
    tr = _divisor(T, 256, 16)

    def body(don_ref, o_ref, z_ref, g_ref, do_ref, dz_ref, dg_ref):
        g = g_ref[...]
        dg = jnp.zeros((1, HEAD_DIM), f32)
        for h in range(N_V):
            hs = pl.ds(h * HEAD_DIM, HEAD_DIM)
            oh, zh, dh = o_ref[:, hs], z_ref[:, hs], don_ref[:, hs]
            r = lax.rsqrt(jnp.mean(oh * oh, axis=-1, keepdims=True) + EPS)
            d_n = dh * _silu(zh)
            dz_ref[:, hs] = (dh * (oh * r * g) * _dsilu(zh)).astype(dz_ref.dtype)
            dy = d_n * g
            m = jnp.mean(dy * oh, axis=-1, keepdims=True)
            do_ref[:, hs] = r * dy - oh * (r * r * r * m)
            dg = dg + jnp.sum(d_n * oh * r, axis=0, keepdims=True)

        @pl.when(pl.program_id(0) == 0)
        def _():
            dg_ref[...] = dg

        @pl.when(pl.program_id(0) > 0)
        def _():
            dg_ref[...] += dg

    return pl.pallas_call(
        body, grid=(T // tr,),
        in_specs=[_rows(tr, V_W), _rows(tr, V_W), _rows(tr, V_W, C_Z // V_W), _whole((1, HEAD_DIM))],
        out_specs=[_rows(tr, V_W), _rows(tr, V_W), _whole((1, HEAD_DIM))],
        out_shape=[SDS((T, V_W), f32), SDS((T, V_W), bf16), SDS((1, HEAD_DIM), f32)],
        name=name, compiler_params=_cp("arbitrary"),
    )(don, o, proj, hg)


def _sgu_parts(ub, vb, gain):
    gv = _gelu(vb)
    r = lax.rsqrt(jnp.mean(gv * gv, axis=-1, keepdims=True) + EPS)
    return _gelu(ub), gv, r, gv * r * gain


def _sgu_fwd(proj, gain, w_s, b_bc, name):
    T = proj.shape[0]
    C = CHUNK_B

    def body(ub_ref, vb_ref, g_ref, w_ref, b_ref, o_ref):
        row, col = _tri_masks(C)
        u, _, _, vn = _sgu_parts(ub_ref[...], vb_ref[...], g_ref[...])
        for g in range(N_GROUPS):
            gs = pl.ds(g * GROUP_DIM, GROUP_DIM)
            wg = jnp.where(row >= col, w_ref[g], 0.0)
            mixed = _dot(_b(wg), _b(vn[:, g * GROUP_DIM:(g + 1) * GROUP_DIM])) + b_ref[g]
            o_ref[:, gs] = (u[:, g * GROUP_DIM:(g + 1) * GROUP_DIM] * mixed).astype(o_ref.dtype)

    return pl.pallas_call(
        body, grid=(T // C,),
        in_specs=[_rows(C, WIDTH_B, C_UB // WIDTH_B), _rows(C, WIDTH_B, C_VB // WIDTH_B), _whole((1, WIDTH_B)),
                  _whole((N_GROUPS, C, C)), _whole((N_GROUPS, C, GROUP_DIM))],
        out_specs=_rows(C, WIDTH_B), out_shape=SDS((T, WIDTH_B), bf16), name=name, compiler_params=_cp("parallel"),
    )(proj, proj, gain, w_s, b_bc)


def _sgu_bwd(dsgu, proj, gain, w_s, b_bc, name):
    T = proj.shape[0]
    C = CHUNK_B

    def body(d_ref, ub_ref, vb_ref, g_ref, w_ref, b_ref, dub_ref, dvb_ref, dw_ref, db_ref, dg_ref):
        first = pl.program_id(0) == 0
        row, col = _tri_masks(C)
        ub, vb, gain_v = ub_ref[...], vb_ref[...], g_ref[...]
        u, gv, r, vn = _sgu_parts(ub, vb, gain_v)
        d = d_ref[...]
        dvn_parts = []
        for g in range(N_GROUPS):
            sl = slice(g * GROUP_DIM, (g + 1) * GROUP_DIM)
            wg = jnp.where(row >= col, w_ref[g], 0.0)
            vng = _b(vn[:, sl])
            mixed = _dot(_b(wg), vng) + b_ref[g]
            dub_ref[:, pl.ds(g * GROUP_DIM, GROUP_DIM)] = (d[:, sl] * mixed * _dgelu(ub[:, sl])).astype(dub_ref.dtype)
            dmix = d[:, sl] * u[:, sl]
            dmb = _b(dmix)
            dwg = jnp.where(row >= col, _dot_nt(dmb, vng), 0.0)
            dbg = jnp.sum(dmix, axis=1, keepdims=True)

            @pl.when(first)
            def _():
                dw_ref[g] = dwg
                db_ref[g] = dbg

            @pl.when(jnp.logical_not(first))
            def _():
                dw_ref[g] += dwg
                db_ref[g] += dbg

            dvn_parts.append(_dot_tn(_b(wg), dmb))
        dvn = jnp.concatenate(dvn_parts, axis=1)
        dy = dvn * gain_v
        m = jnp.mean(dy * gv, axis=-1, keepdims=True)
        dgv = r * dy - gv * (r * r * r * m)
        dvb_ref[...] = (dgv * _dgelu(vb)).astype(dvb_ref.dtype)
        dgain = jnp.sum(dvn * gv * r, axis=0, keepdims=True)

        @pl.when(first)
        def _():
            dg_ref[...] = dgain

        @pl.when(jnp.logical_not(first))
        def _():
            dg_ref[...] += dgain

    return pl.pallas_call(
        body, grid=(T // C,),
        in_specs=[_rows(C, WIDTH_B), _rows(C, WIDTH_B, C_UB // WIDTH_B), _rows(C, WIDTH_B, C_VB // WIDTH_B),
                  _whole((1, WIDTH_B)), _whole((N_GROUPS, C, C)), _whole((N_GROUPS, C, GROUP_DIM))],
        out_specs=[_rows(C, WIDTH_B), _rows(C, WIDTH_B), _whole((N_GROUPS, C, C)), _whole((N_GROUPS, C, 1)),
                   _whole((1, WIDTH_B))],
        out_shape=[SDS((T, WIDTH_B), bf16), SDS((T, WIDTH_B), bf16), SDS((N_GROUPS, C, C), f32),
                   SDS((N_GROUPS, C, 1), f32), SDS((1, WIDTH_B), f32)],
        name=name, compiler_params=_cp("arbitrary"),
    )(dsgu, proj, proj, gain, w_s, b_bc)


def _merge_fwd(proj, ya, yb, name):
    T = proj.shape[0]
    tr = _divisor(T, 256, 16)

    def body(ga_ref, gb_ref, ya_ref, yb_ref, o_ref):
        o_ref[...] = (_sig(ga_ref[...]) * ya_ref[...] + _sig(gb_ref[...]) * yb_ref[...]).astype(o_ref.dtype)

    return pl.pallas_call(
        body, grid=(T // tr,),
        in_specs=[_rows(tr, D_MODEL, C_GA // D_MODEL), _rows(tr, D_MODEL, C_GB // D_MODEL), _rows(tr, D_MODEL), _rows(tr, D_MODEL)],
        out_specs=_rows(tr, D_MODEL), out_shape=SDS((T, D_MODEL), bf16), name=name, compiler_params=_cp("parallel"),
    )(proj, proj, ya, yb)


def _merge_bwd(dm, proj, ya, yb, name):
    T = proj.shape[0]
    tr = _divisor(T, 256, 16)

    def body(dm_ref, ga_ref, gb_ref, ya_ref, yb_ref, dya_ref, dyb_ref, dga_ref, dgb_ref):
        d = dm_ref[...]
        sa, sb = _sig(ga_ref[...]), _sig(gb_ref[...])
        dya_ref[...] = (d * sa).astype(bf16)
        dyb_ref[...] = (d * sb).astype(bf16)
        dga_ref[...] = (d * ya_ref[...] * sa * (1.0 - sa)).astype(bf16)
        dgb_ref[...] = (d * yb_ref[...] * sb * (1.0 - sb)).astype(bf16)

    return pl.pallas_call(
        body, grid=(T // tr,),
        in_specs=[_rows(tr, D_MODEL), _rows(tr, D_MODEL, C_GA // D_MODEL), _rows(tr, D_MODEL, C_GB // D_MODEL),
                  _rows(tr, D_MODEL), _rows(tr, D_MODEL)],
        out_specs=[_rows(tr, D_MODEL)] * 4, out_shape=[SDS((T, D_MODEL), bf16)] * 4, name=name,
        compiler_params=_cp("parallel"),
    )(dm, proj, proj, ya, yb)


def _ffnconv_fwd(upg, upv, wg, wv, bg, bv, name):
    T, F = upg.shape
    R = min(ROW_CHUNK, T)
    n_chunks = T // R

    def body(g_ref, v_ref, wg_ref, wv_ref, bg_ref, bv_ref, o_ref):
        wgv, wvv = _w_rows(wg_ref, 3), _w_rows(wv_ref, 3)

        def chunk(ci, carry):
            r0 = pl.multiple_of(ci * R, R)
            cg = _causal_conv(g_ref[pl.ds(r0, R), :], _prev8(g_ref, r0, ci), wgv, 3) + bg_ref[...]
            cv = _causal_conv(v_ref[pl.ds(r0, R), :], _prev8(v_ref, r0, ci), wvv, 3) + bv_ref[...]
            o_ref[pl.ds(r0, R), :] = (_silu(cg) * cv).astype(o_ref.dtype)
            return carry

        lax.fori_loop(0, n_chunks, chunk, 0)

    col = lambda p: (0, p)
    return pl.pallas_call(
        body, grid=(F // LANES,),
        in_specs=[pl.BlockSpec((T, LANES), col)] * 2 + [pl.BlockSpec((3, LANES), col)] * 2 + [pl.BlockSpec((1, LANES), col)] * 2,
        out_specs=pl.BlockSpec((T, LANES), col), out_shape=SDS((T, F), bf16), name=name, compiler_params=_cp("parallel"),
    )(upg, upv, wg, wv, bg, bv)


def _ffnconv_bwd(dact, upg, upv, wg, wv, bg, bv, name):
    T, F = upg.shape
    R = min(ROW_CHUNK, T)
    n_chunks = T // R

    def body(d_ref, g_ref, v_ref, wg_ref, wv_ref, bg_ref, bv_ref,
             dg_ref, dv_ref, dwg_ref, dwv_ref, dbg_ref, dbv_ref, sg, sv):
        wgv, wvv = _w_rows(wg_ref, 3), _w_rows(wv_ref, 3)

        def phase1(ci, carry):
            dwg, dwv = carry
            r0 = pl.multiple_of(ci * R, R)
            gcur, vcur = g_ref[pl.ds(r0, R), :], v_ref[pl.ds(r0, R), :]
            gp, vp = _prev8(g_ref, r0, ci), _prev8(v_ref, r0, ci)
            gsh = [gcur] + [_shift_down(gcur, gp, s) for s in (1, 2)]
            vsh = [vcur] + [_shift_down(vcur, vp, s) for s in (1, 2)]
            cg = gsh[0] * wgv[2] + gsh[1] * wgv[1] + gsh[2] * wgv[0] + bg_ref[...]
            cv = vsh[0] * wvv[2] + vsh[1] * wvv[1] + vsh[2] * wvv[0] + bv_ref[...]
            d = d_ref[pl.ds(r0, R), :]
            dcv = d * _silu(cg)
            dcg = d * cv * _dsilu(cg)
            sg[pl.ds(r0, R), :] = dcg
            sv[pl.ds(r0, R), :] = dcv
            rg = [jnp.sum(dcg * gsh[2 - j], axis=0, keepdims=True) for j in range(3)] + [jnp.sum(dcg, axis=0, keepdims=True)]
            rv = [jnp.sum(dcv * vsh[2 - j], axis=0, keepdims=True) for j in range(3)] + [jnp.sum(dcv, axis=0, keepdims=True)]
            return tuple(a + b for a, b in zip(dwg, rg)), tuple(a + b for a, b in zip(dwv, rv))

        z4 = tuple(jnp.zeros((1, LANES), f32) for _ in range(4))
        dwg, dwv = lax.fori_loop(0, n_chunks, phase1, (z4, z4))
        for j in range(3):
            dwg_ref[j:j + 1, :] = dwg[j]
            dwv_ref[j:j + 1, :] = dwv[j]
        dbg_ref[...] = dwg[3]
        dbv_ref[...] = dwv[3]

        def phase2(ci, carry):
            r0 = pl.multiple_of(ci * R, R)
            for sc, wv_, out in ((sg, wgv, dg_ref), (sv, wvv, dv_ref)):
                cur = sc[pl.ds(r0, R), :]
                n8 = _next8(sc, r0, R, ci, n_chunks)
                acc = cur * wv_[2] + _shift_up(cur, n8, 1) * wv_[1] + _shift_up(cur, n8, 2) * wv_[0]
                out[pl.ds(r0, R), :] = acc.astype(out.dtype)
            return carry

        lax.fori_loop(0, n_chunks, phase2, 0)

    col = lambda p: (0, p)
    big, w3, b1 = pl.BlockSpec((T, LANES), col), pl.BlockSpec((3, LANES), col), pl.BlockSpec((1, LANES), col)
    return pl.pallas_call(
        body, grid=(F // LANES,), in_specs=[big, big, big, w3, w3, b1, b1], out_specs=[big, big, w3, w3, b1, b1],
        out_shape=[SDS((T, F), bf16), SDS((T, F), bf16), SDS((3, F), f32), SDS((3, F), f32), SDS((1, F), f32), SDS((1, F), f32)],
        scratch_shapes=[pltpu.VMEM((T, LANES), f32), pltpu.VMEM((T, LANES), f32)], name=name, compiler_params=_cp("parallel"),
    )(dact, upg, upv, wg, wv, bg, bv)


def _loss_head(x, gain, target, name):
    T, Dm = x.shape
    tr = _divisor(T, 256, 16)

    def body(x_ref, g_ref, t_ref, l_ref, dx_ref, dg_ref):
        xv, g = x_ref[...], g_ref[...]
        r = lax.rsqrt(jnp.mean(xv * xv, axis=-1, keepdims=True) + EPS)
        err = xv * r * g - t_ref[...]
        part_l = 0.5 * jnp.sum(jnp.mean(err * err, axis=-1, keepdims=True), axis=0, keepdims=True)
        dy = err * (1.0 / Dm)
        dyg = dy * g
        m = jnp.mean(dyg * xv, axis=-1, keepdims=True)
        dx_ref[...] = r * dyg - xv * (r * r * r * m)
        part_g = jnp.sum(dy * xv * r, axis=0, keepdims=True)
        part_l = jnp.broadcast_to(part_l, (1, LANES))

        @pl.when(pl.program_id(0) == 0)
        def _():
            l_ref[...] = part_l
            dg_ref[...] = part_g

        @pl.when(pl.program_id(0) > 0)
        def _():
            l_ref[...] += part_l
            dg_ref[...] += part_g

    return pl.pallas_call(
        body, grid=(T // tr,), in_specs=[_rows(tr, Dm), _whole((1, Dm)), _rows(tr, Dm)],
        out_specs=[_whole((1, LANES)), _rows(tr, Dm), _whole((1, Dm))],
        out_shape=[SDS((1, LANES), f32), SDS((T, Dm), f32), SDS((1, Dm), f32)], name=name, compiler_params=_cp("arbitrary"),
    )(x, gain, target)


def _lane_row(vec, offset):
    return jnp.pad(vec.astype(f32), (offset, LANES - offset - vec.shape[0]))[None]


def _layer_fwd(x, p_i, W, S, li):
    nm = lambda s: f"{s}_l{li}"
    sv = {"x0": x}
    h1 = _rms_fwd(x, S["norm_mix"], nm("rms_mix"))
    proj = _mm(h1, W["in_main"], "nn", nm("proj_main"))
    ba = _mm(h1, W["in_ba"], "nn", nm("proj_ba"))
    qkv = _qkvconv_fwd(proj, S["conv_qkv"], nm("qkvconv"))
    qd, kd, u, w, attn, tinv, cd, _ = _delta_prep(qkv, ba, S["ea_row"], S["dtb_row"], nm("delta_prep"))
    o, s_all, vn = _delta_scan(qd, kd, u, w, attn, cd, nm("delta_scan"))
    on = _onorm_fwd(o, proj, S["head_norm"], nm("onorm"))
    ya = _mm(on, W["branch_a"], "nn", nm("branch_a"))
    sgu = _sgu_fwd(proj, S["sgu_norm"], S["w_spatial"], S["b_bc"], nm("sgu"))
    yb = _mm(sgu, W["branch_b"], "nn", nm("branch_b"))
    merged = _merge_fwd(proj, ya, yb, nm("merge"))
    x1 = _mm(merged, W["out"], "nn", nm("out_proj"), c=x)
    h2 = _rms_fwd(x1, S["norm_ffn"], nm("rms_ffn"))
    upg = _mm(h2, W["up_g"], "nn", nm("ffn_up_g"))
    upv = _mm(h2, W["up_v"], "nn", nm("ffn_up_v"))
    act = _ffnconv_fwd(upg, upv, S["conv_g"], S["conv_v"], S["bias_g"], S["bias_v"], nm("ffnconv"))
    x2 = _mm(act, W["down"], "nn", nm("ffn_down"), c=x1, tk=2816)
    h3 = _rms_fwd(x2, S["norm_ple"], nm("rms_ple"))
    gl = _mm(h3, W["ple_gate"], "nn", nm("ple_gate"))
    pp = _mm(p_i, W["ple_proj"], "nn", nm("ple_proj"))
    (x3,) = _ew(lambda a, g, q: (a + _sig(g) * q,), [x2, gl, pp], [f32], nm("ple_mix"))
    sv.update(h1=h1, proj=proj, ba=ba, qkv=qkv, qd=qd, kd=kd, u=u, w=w, attn=attn, tinv=tinv, cd=cd, o=o, s_all=s_all,
              vn=vn, on=on, ya=ya, sgu=sgu, yb=yb, merged=merged, x1=x1, h2=h2, upg=upg, upv=upv, act=act, x2=x2,
              h3=h3, gl=gl, pp=pp, p=p_i)
    return x3, sv


def _layer_bwd(dx3, sv, W, S, li):
    nm = lambda s: f"{s}_l{li}"
    G = {}
    dgl, dpp = _ew(lambda d, g, q: ((lambda s: (d * q * s * (1.0 - s), d * s))(_sig(g))),
                   [dx3, sv["gl"], sv["pp"]], [bf16, bf16], nm("ple_mix_bwd"))
    G["w_ple_gate"] = _mm(sv["h3"], dgl, "tn", nm("d_ple_gate"))
    G["w_ple_proj"] = _mm(sv["p"], dpp, "tn", nm("d_ple_proj"))
    dh3 = _mm(dgl, W["ple_gate"], "nt", nm("dh_ple"))
    dx2, G["norm_ple"] = _rms_bwd(sv["x2"], S["norm_ple"], dh3, dx3, nm("rms_ple_bwd"))
    dact = _mm(dx2, W["down"], "nt", nm("d_act"))
    G["w_ffn_down"] = _mm(sv["act"], dx2, "tn", nm("d_ffn_down"))
    dupg, dupv, dcg, dcv, dbg, dbv = _ffnconv_bwd(dact, sv["upg"], sv["upv"], S["conv_g"], S["conv_v"], S["bias_g"],
                                                  S["bias_v"], nm("ffnconv_bwd"))
    G["conv_ffn"] = jnp.concatenate([dcg, dcv], axis=1)
    G["b_conv_ffn"] = jnp.concatenate([dbg, dbv], axis=1)
    G["up_g"] = _mm(sv["h2"], dupg, "tn", nm("d_ffn_up_g"))
    G["up_v"] = _mm(sv["h2"], dupv, "tn", nm("d_ffn_up_v"))
    dh2 = _mm(dupg, W["up_g"], "nt", nm("dh_ffn_g"), tk=2816)
    dh2 = _mm(dupv, W["up_v"], "nt", nm("dh_ffn_v"), c=dh2, tk=2816)
    dx1, G["norm_ffn"] = _rms_bwd(sv["x1"], S["norm_ffn"], dh2, dx2, nm("rms_ffn_bwd"))
    dmerged = _mm(dx1, W["out"], "nt", nm("d_merged"))
    G["w_out"] = _mm(sv["merged"], dx1, "tn", nm("d_w_out"))
    dya, dyb, dga, dgb = _merge_bwd(dmerged, sv["proj"], sv["ya"], sv["yb"], nm("merge_bwd"))
    G["w_branch_a"] = _mm(sv["on"], dya, "tn", nm("d_branch_a"))
    G["w_branch_b"] = _mm(sv["sgu"], dyb, "tn", nm("d_branch_b"))
    don = _mm(dya, W["branch_a"], "nt", nm("d_on"))
    dsgu = _mm(dyb, W["branch_b"], "nt", nm("d_sgu"))
    dub, dvb, G["w_spatial"], db_s, G["sgu_norm"] = _sgu_bwd(dsgu, sv["proj"], S["sgu_norm"], S["w_spatial"], S["b_bc"], nm("sgu_bwd"))
    G["b_spatial"] = db_s.reshape(N_GROUPS, CHUNK_B)
    do, dz, G["head_norm"] = _onorm_bwd(don, sv["o"], sv["proj"], S["head_norm"], nm("onorm_bwd"))
    dqd, dkd, du, dw, dattn, dcd = _delta_scan_bwd(do, sv["qd"], sv["kd"], sv["w"], sv["attn"], sv["cd"], sv["s_all"], sv["vn"],
                                                   nm("delta_scan_bwd"))
    dqkv_n, dba, dalog_row, ddtb_row = _delta_prep_bwd(sv["qkv"], sv["ba"], S["ea_row"], S["dtb_row"], sv["tinv"], sv["u"], sv["w"],
                                                       dqd, dkd, du, dw, dattn, dcd, nm("delta_prep_bwd"))
    G["a_log"] = dalog_row[0, N_V:2 * N_V]
    G["dt_bias"] = ddtb_row[0, N_V:2 * N_V]
    dqkv, G["conv_qkv"] = _qkvconv_bwd(sv["proj"], S["conv_qkv"], dqkv_n, nm("qkvconv_bwd"))
    dproj = jnp.concatenate([dqkv, dz, dub, dvb, dga, dgb], axis=1)
    G["in_main"] = _mm(sv["h1"], dproj, "tn", nm("d_in_main"))
    G["in_ba"] = _mm(sv["h1"], dba, "tn", nm("d_in_ba"))
    dh1 = _mm(dba, W["in_ba"], "nt", nm("dh_mix_ba"))
    dh1 = _mm(dproj, W["in_main"], "nt", nm("dh_mix"), c=dh1)
    dx0, G["norm_mix"] = _rms_bwd(sv["x0"], S["norm_mix"], dh1, dx1, nm("rms_mix_bwd"))
    return dx0, G


_BA0 = C_UB
_BA1 = C_UB + 2 * N_V


def _cols(segments, lo, hi):
    out = []
    for start, a in segments:
        s_lo, s_hi = max(lo, start), min(hi, start + a.shape[1])
        if s_lo < s_hi:
            out.append(a[:, s_lo - start:s_hi - start])
    return out


def _big_weights(sh):
    n_in = N_IN // N_CHIPS
    w_in = [(j * n_in, a) for j, a in enumerate(sh["w_in"])]
    rows = lambda k: jnp.concatenate(sh[k], axis=0)
    cols = lambda parts: jnp.concatenate(parts, axis=1)
    ba = jnp.pad(cols(_cols(w_in, _BA0, _BA1)), ((0, 0), (0, LANES - 2 * N_V)))
    return dict(
        in_main=cols(_cols(w_in, 0, _BA0) + _cols(w_in, _BA1, N_IN)), in_ba=ba,
        branch_a=rows("w_branch_a"), branch_b=cols(sh["w_branch_b"]), out=rows("w_out"),
        up_g=cols(sh["w_ffn_up"][:2]), up_v=cols(sh["w_ffn_up"][2:]), down=rows("w_ffn_down"),
        ple_gate=rows("w_ple_gate"), ple_proj=cols(sh["w_ple_proj"]))


def _grads_by_chip(G):
    n_in = N_IN // N_CHIPS
    w_in = [(0, G["in_main"][:, :_BA0]), (_BA0, G["in_ba"][:, :2 * N_V]), (_BA1, G["in_main"][:, _BA0:])]
    split = lambda g: jnp.stack(jnp.split(g, N_CHIPS, axis=1))
    by_rows = lambda g: g.reshape(N_CHIPS, -1, g.shape[1])
    return dict(
        w_in=jnp.stack([jnp.concatenate(_cols(w_in, j * n_in, (j + 1) * n_in), axis=1) for j in range(N_CHIPS)]),
        w_branch_a=by_rows(G["w_branch_a"]), w_branch_b=split(G["w_branch_b"]), w_out=by_rows(G["w_out"]),
        w_ffn_up=jnp.stack(jnp.split(G["up_g"], 2, axis=1) + jnp.split(G["up_v"], 2, axis=1)),
        w_ffn_down=by_rows(G["w_ffn_down"]), w_ple_gate=by_rows(G["w_ple_gate"]), w_ple_proj=split(G["w_ple_proj"]))


def _small_params(sm, i):
    return dict(
        norm_mix=sm["norm_mix"][i][None], conv_qkv=sm["conv_qkv"][i], ea_row=_lane_row(jnp.exp(sm["a_log"][i]), N_V),
        dtb_row=_lane_row(sm["dt_bias"][i], N_V), head_norm=sm["head_norm"][i][None], sgu_norm=sm["sgu_norm"][i][None],
        w_spatial=sm["w_spatial"][i],
        b_bc=jnp.broadcast_to(sm["b_spatial"][i][:, :, None], (N_GROUPS, CHUNK_B, GROUP_DIM)),
        norm_ffn=sm["norm_ffn"][i][None], conv_g=sm["conv_ffn"][i][:, :D_FF], conv_v=sm["conv_ffn"][i][:, D_FF:],
        bias_g=sm["b_conv_ffn"][i][None, :D_FF], bias_v=sm["b_conv_ffn"][i][None, D_FF:], norm_ple=sm["norm_ple"][i][None])


MESH = pl.DeviceIdType.MESH
N_CHIPS = 4
ANY = pl.BlockSpec(memory_space=pl.ANY)
ROW_ALIGN = 32


def _place():
    x, y, c = lax.axis_index("x"), lax.axis_index("y"), lax.axis_index("c")
    chips = [(1 - x, y), (x, 1 - y), (1 - x, 1 - y)]
    return x, y, c, 2 * x + y, chips, (x, y, 1 - c)


def _halves(rows, c):
    h = rows // 2
    return pl.ds(pl.multiple_of(c * h, 16), h), pl.ds(pl.multiple_of((1 - c) * h, 16), h)


def _remote(src, dst, send_sems, recv_sems, k, dev):
    return pltpu.make_async_remote_copy(src_ref=src, dst_ref=dst, send_sem=send_sems.at[k], recv_sem=recv_sems.at[k],
                                        device_id=dev, device_id_type=MESH)


def _gather_shards(bufs, name):
    n = len(bufs)

    def body(*refs):
        ins, outs = refs[:n], refs[n:2 * n]
        send_sems, recv_sems = refs[2 * n:]
        x, y, c, me, chips, sib = _place()
        for i in range(n):
            mine, _ = _halves(ins[i].shape[0], c)
            for k, (px, py) in enumerate(chips):
                _remote(ins[i].at[mine], outs[i].at[me, mine], send_sems, recv_sems, 6 * i + k, (px, py, c)).start()
        for i in range(n):
            mine, _ = _halves(ins[i].shape[0], c)
            for k, (px, py) in enumerate(chips):
                pc = 2 * px + py
                _remote(ins[i].at[mine], outs[i].at[pc, mine], send_sems, recv_sems, 6 * i + k, (px, py, c)).wait_recv()
                _remote(outs[i].at[pc, mine], outs[i].at[pc, mine], send_sems, recv_sems, 6 * i + 3 + k, sib).start()
        for i in range(n):
            mine, other = _halves(ins[i].shape[0], c)
            for k, (px, py) in enumerate(chips):
                pc = 2 * px + py
                _remote(outs[i].at[pc, mine], outs[i].at[pc, other], send_sems, recv_sems, 6 * i + 3 + k, sib).wait_recv()
        for i in range(n):
            mine, _ = _halves(ins[i].shape[0], c)
            for k, (px, py) in enumerate(chips):
                pc = 2 * px + py
                _remote(ins[i].at[mine], outs[i].at[me, mine], send_sems, recv_sems, 6 * i + k, (px, py, c)).wait_send()
                _remote(outs[i].at[pc, mine], outs[i].at[pc, mine], send_sems, recv_sems, 6 * i + 3 + k, sib).wait_send()

    return pl.pallas_call(
        body, in_specs=[ANY] * n, out_specs=[ANY] * n,
        out_shape=[SDS((N_CHIPS,) + b.shape, b.dtype) for b in bufs],
        scratch_shapes=[pltpu.SemaphoreType.DMA((6 * n,)), pltpu.SemaphoreType.DMA((6 * n,))],
        name=name,
    )(*bufs)


def _exchange(srcs_of, out_shapes, n_sems, name):
    n = len(out_shapes)

    def body(*refs):
        n_in = len(refs) - n - 2
        ins, outs = refs[:n_in], refs[n_in:n_in + n]
        send_sems, recv_sems = refs[-2:]
        copies = [_remote(s, d, send_sems, recv_sems, k, dev) for s, d, k, dev in srcs_of(ins, outs, _place())]
        for cp in copies:
            cp.start()
        for cp in copies:
            cp.wait()

    def call(*arrs):
        return pl.pallas_call(
            body, in_specs=[ANY] * len(arrs), out_specs=[ANY] * n, out_shape=out_shapes,
            scratch_shapes=[pltpu.SemaphoreType.DMA((n_sems,)), pltpu.SemaphoreType.DMA((n_sems,))], name=name,
        )(*arrs)

    return call


HBM = pl.BlockSpec(memory_space=pltpu.HBM)
SEM = pl.BlockSpec(memory_space=pltpu.SEMAPHORE)
_SIDE_EFFECT = pltpu.SideEffectType.DATAFLOW_SIDE_EFFECTING


def _exchange_start(copies_of, srcs, land_shapes, n_sems, name):
    ns, nl = len(srcs), len(land_shapes)

    def body(*refs):
        ins = refs[:ns + nl]
        send_sems, recv_sems = refs[ns + nl], refs[ns + nl + 1]
        token = refs[-1]
        for s, d, k, dev in copies_of(ins[:ns], ins[ns:], _place()):
            _remote(s, d, send_sems, recv_sems, k, dev).start()
        token[...] = jnp.zeros_like(token)

    lands = [lax.empty(s.shape, s.dtype) for s in land_shapes]
    operands = [pltpu.with_memory_space_constraint(a, pltpu.HBM) for a in list(srcs) + lands]
    outs = pl.pallas_call(
        body, name=name,
        out_shape=(pltpu.SemaphoreType.DMA((n_sems,)), pltpu.SemaphoreType.DMA((n_sems,)),
                   *[pltpu.HBM(a.shape, a.dtype) for a in operands], SDS((8, LANES), f32)),
        in_specs=[HBM] * (ns + nl), out_specs=(SEM, SEM, *[HBM] * (ns + nl), pl.BlockSpec(memory_space=pltpu.VMEM)),
        input_output_aliases={i: 2 + i for i in range(ns + nl)},
        compiler_params=pltpu.CompilerParams(has_side_effects=_SIDE_EFFECT),
    )(*operands)
    return dict(send=outs[0], recv=outs[1], bufs=list(outs[2:2 + ns + nl]), token=outs[-1], ns=ns, copies_of=copies_of)


def _exchange_wait(handle, after, name):
    ns, bufs, copies_of = handle["ns"], handle["bufs"], handle["copies_of"]
    nb = len(bufs)

    def body(*refs):
        ins = refs[:nb]
        send_sems, recv_sems = refs[nb], refs[nb + 1]
        for s, d, k, dev in copies_of(ins[:ns], ins[ns:], _place()):
            cp = _remote(s, d, send_sems, recv_sems, k, dev)
            cp.wait_send()
            cp.wait_recv()

    outs = pl.pallas_call(
        body, name=name, out_shape=tuple(pltpu.HBM(a.shape, a.dtype) for a in bufs),
        in_specs=[HBM] * nb + [SEM, SEM, ANY], out_specs=tuple([HBM] * nb),
        input_output_aliases={i: i for i in range(nb)},
        compiler_params=pltpu.CompilerParams(has_side_effects=_SIDE_EFFECT),
    )(*bufs, handle["send"], handle["recv"], after)
    return list(outs[:ns]), list(outs[ns:])


def _pick(idx, parts):
    out = parts[-1]
    for j in reversed(range(len(parts) - 1)):
        out = jnp.where(idx == j, parts[j], out)
    return out


def _rs_pair(gs, name):
    def copies(ins, outs, place):
        x, y, c, me, chips, sib = place
        return [(g.at[:, _halves(g.shape[1], c)[1]], o, i, sib) for i, (g, o) in enumerate(zip(ins, outs))]

    shapes = [SDS((N_CHIPS, g.shape[1] // 2, g.shape[2]), g.dtype) for g in gs]
    return _exchange(copies, shapes, len(gs), name)(*gs)


def _rs_join(qs, name):
    def copies(ins, outs, place):
        return [(q, o, i, place[5]) for i, (q, o) in enumerate(zip(ins, outs))]

    return _exchange(copies, [SDS(q.shape, q.dtype) for q in qs], len(qs), name)(*qs)


def _gather_copies(ins, outs, place):
    x, y, c, me, chips, sib = place
    out = []
    for i, (s, o) in enumerate(zip(ins, outs)):
        mine, _ = _halves(s.shape[0], c)
        out += [(s.at[mine], o.at[me, mine], 3 * i + k, (px, py, c)) for k, (px, py) in enumerate(chips)]
    return out


def _gather_start(shards, name):
    return _exchange_start(_gather_copies, shards, [SDS((N_CHIPS,) + s.shape, s.dtype) for s in shards], 3 * len(shards), name)


def _gather_forward(lands, name):
    n = len(lands)

    def body(*refs):
        ins, outs = refs[:n], refs[n:2 * n]
        send_sems, recv_sems = refs[2 * n:]
        x, y, c, me, chips, sib = _place()
        copies = []
        for i in range(n):
            mine, _ = _halves(ins[i].shape[1], c)
            for k, (px, py) in enumerate(chips):
                pc = 2 * px + py
                copies.append(_remote(ins[i].at[pc, mine], outs[i].at[pc, mine], send_sems, recv_sems, 3 * i + k, sib))
        for cp in copies:
            cp.start()
        for cp in copies:
            cp.wait()

    return pl.pallas_call(
        body, in_specs=[ANY] * n, out_specs=[ANY] * n, out_shape=[SDS(a.shape, a.dtype) for a in lands],
        input_output_aliases={i: i for i in range(n)},
        scratch_shapes=[pltpu.SemaphoreType.DMA((3 * n,)), pltpu.SemaphoreType.DMA((3 * n,))], name=name,
    )(*lands)


def _rs_chips_copies(ins, outs, place):
    x, y, c, me, chips, sib = place
    return [(p.at[2 * px + py], o.at[k], 3 * i + k, (px, py, c))
            for i, (p, o) in enumerate(zip(ins, outs)) for k, (px, py) in enumerate(chips)]


def _rs_begin(gs, names, core, tag):
    gots = _rs_pair(gs, f"rs_pair_{tag}")
    ps = []
    for g, got, k in zip(gs, gots, names):
        h = g.shape[1] // 2
        mine = jnp.where(core == 0, g[:, :h], g[:, h:])
        (p,) = _ew(lambda a, b: (a + b,), [mine.reshape(-1, g.shape[2]), got.reshape(-1, g.shape[2])], [bf16], f"rs_add2_{k}_{tag}")
        ps.append(p.reshape(got.shape))
    lands = [SDS((3,) + p.shape[1:], p.dtype) for p in ps]
    return _exchange_start(_rs_chips_copies, ps, lands, 3 * len(ps), f"rs_chips_start_{tag}")


def _rs_end(handle, after, names, chip, core, tag):
    ps, parts = _exchange_wait(handle, after, f"rs_chips_wait_{tag}")
    qs = []
    up = lambda a: a.astype(f32)
    for p, part, k in zip(ps, parts, names):
        own = _pick(chip, [p[j] for j in range(N_CHIPS)])
        (q,) = _ew(lambda a, b, c, d: ((up(a) + up(c)) + (up(b) + up(d)),), [own, part[0], part[1], part[2]], [f32], f"rs_add4_{k}_{tag}")
        qs.append(q)
    others = _rs_join(qs, f"rs_join_{tag}")
    return [jnp.where(core == 0, jnp.concatenate([q, o], axis=0), jnp.concatenate([o, q], axis=0)) for q, o in zip(qs, others)]


def _allreduce_small(buf, name):
    R, L = buf.shape

    def body(x_ref, o_ref, r0, s1, r1, send_sems, recv_sems):
        x, y, c, me, chips, sib = _place()
        cp = _remote(x_ref, r0, send_sems, recv_sems, 0, sib)
        cp.start()
        cp.wait()
        s1[...] = x_ref[...] + r0[...]
        cps = []
        for k, (px, py) in enumerate(chips):
            cp = _remote(s1, r1.at[k], send_sems, recv_sems, 1 + k, (px, py, c))
            cp.start()
            cps.append(cp)
        for cp in cps:
            cp.wait()
        o_ref[...] = (s1[...] + r1[1]) + (r1[0] + r1[2])

    vm = pl.BlockSpec(memory_space=pltpu.VMEM)
    return pl.pallas_call(
        body, in_specs=[vm], out_specs=vm, out_shape=SDS((R, L), f32),
        scratch_shapes=[pltpu.VMEM((R, L), f32), pltpu.VMEM((R, L), f32), pltpu.VMEM((3, R, L), f32),
                        pltpu.SemaphoreType.DMA((4,)), pltpu.SemaphoreType.DMA((4,))],
        name=name, compiler_params=pltpu.CompilerParams(vmem_limit_bytes=VMEM_LIMIT_BYTES),
    )(buf)


BIG = ("w_in", "w_branch_a", "w_branch_b", "w_out", "w_ffn_up", "w_ffn_down", "w_ple_gate", "w_ple_proj")
SMALL_REPL = ("norm_mix", "a_log", "dt_bias", "head_norm", "sgu_norm", "w_spatial", "b_spatial", "norm_ffn", "b_conv_ffn",
              "norm_ple", "norm_final")
SMALL_COLS = ("conv_qkv", "conv_ffn")
WEIGHTS = ("norm_mix", "w_in", "conv_qkv", "a_log", "dt_bias", "head_norm", "sgu_norm", "w_spatial", "b_spatial", "w_branch_a",
           "w_branch_b", "w_out", "norm_ffn", "w_ffn_up", "conv_ffn", "b_conv_ffn", "w_ffn_down", "norm_ple", "w_ple_gate",
           "w_ple_proj", "norm_final")


def _flat(arrs, dtype):
    cat = jnp.concatenate([a.astype(dtype).reshape(-1) for a in arrs])
    unit = LANES * ROW_ALIGN
    cat = jnp.pad(cat, (0, -cat.shape[0] % unit))
    return cat.reshape(-1, LANES)


def _unflat(buf, shapes):
    flat = buf.reshape(-1)
    out, off = [], 0
    for s in shapes:
        n = math.prod(s)
        out.append(flat[off:off + n].reshape(s))
        off += n
    return out


def _adamw(w, g, m, v):
    m2 = ADAM_B1 * m + (1.0 - ADAM_B1) * g
    v2 = ADAM_B2 * v + (1.0 - ADAM_B2) * (g * g)
    m_hat = m2 * (1.0 / (1.0 - ADAM_B1 ** ADAM_STEP))
    v_hat = v2 * (1.0 / (1.0 - ADAM_B2 ** ADAM_STEP))
    delta = -ADAM_LR * (m_hat / (jnp.sqrt(v_hat) + ADAM_EPS) + ADAM_WD * w)
    return delta, m2, v2


def kernel(x, p, norm_mix, w_in, conv_qkv, a_log, dt_bias, head_norm, sgu_norm, w_spatial, b_spatial, w_branch_a, w_branch_b, w_out, norm_ffn, w_ffn_up, conv_ffn, b_conv_ffn, w_ffn_down, norm_ple, w_ple_gate, w_ple_proj, norm_final, loss_target, m_norm_mix, m_w_in, m_conv_qkv, m_a_log, m_dt_bias, m_head_norm, m_sgu_norm, m_w_spatial, m_b_spatial, m_w_branch_a, m_w_branch_b, m_w_out, m_norm_ffn, m_w_ffn_up, m_conv_ffn, m_b_conv_ffn, m_w_ffn_down, m_norm_ple, m_w_ple_gate, m_w_ple_proj, m_norm_final, v_norm_mix, v_w_in, v_conv_qkv, v_a_log, v_dt_bias, v_head_norm, v_sgu_norm, v_w_spatial, v_b_spatial, v_w_branch_a, v_w_branch_b, v_w_out, v_norm_ffn, v_w_ffn_up, v_conv_ffn, v_b_conv_ffn, v_w_ffn_down, v_norm_ple, v_w_ple_gate, v_w_ple_proj, v_norm_final):
    w = dict(norm_mix=norm_mix, w_in=w_in, conv_qkv=conv_qkv, a_log=a_log, dt_bias=dt_bias, head_norm=head_norm, sgu_norm=sgu_norm,
             w_spatial=w_spatial, b_spatial=b_spatial, w_branch_a=w_branch_a, w_branch_b=w_branch_b, w_out=w_out, norm_ffn=norm_ffn,
             w_ffn_up=w_ffn_up, conv_ffn=conv_ffn, b_conv_ffn=b_conv_ffn, w_ffn_down=w_ffn_down, norm_ple=norm_ple,
             w_ple_gate=w_ple_gate, w_ple_proj=w_ple_proj, norm_final=norm_final)
    m = dict(norm_mix=m_norm_mix, w_in=m_w_in, conv_qkv=m_conv_qkv, a_log=m_a_log, dt_bias=m_dt_bias, head_norm=m_head_norm,
             sgu_norm=m_sgu_norm, w_spatial=m_w_spatial, b_spatial=m_b_spatial, w_branch_a=m_w_branch_a, w_branch_b=m_w_branch_b,
             w_out=m_w_out, norm_ffn=m_norm_ffn, w_ffn_up=m_w_ffn_up, conv_ffn=m_conv_ffn, b_conv_ffn=m_b_conv_ffn,
             w_ffn_down=m_w_ffn_down, norm_ple=m_norm_ple, w_ple_gate=m_w_ple_gate, w_ple_proj=m_w_ple_proj, norm_final=m_norm_final)
    v = dict(norm_mix=v_norm_mix, w_in=v_w_in, conv_qkv=v_conv_qkv, a_log=v_a_log, dt_bias=v_dt_bias, head_norm=v_head_norm,
             sgu_norm=v_sgu_norm, w_spatial=v_w_spatial, b_spatial=v_b_spatial, w_branch_a=v_w_branch_a, w_branch_b=v_w_branch_b,
             w_out=v_w_out, norm_ffn=v_norm_ffn, w_ffn_up=v_w_ffn_up, conv_ffn=v_conv_ffn, b_conv_ffn=v_b_conv_ffn,
             w_ffn_down=v_w_ffn_down, norm_ple=v_norm_ple, w_ple_gate=v_w_ple_gate, w_ple_proj=v_w_ple_proj, norm_final=v_norm_final)
    chip = 2 * lax.axis_index("x") + lax.axis_index("y")
    core = lax.axis_index("c")

    def by_chip(own, gathered):
        return [jnp.where(chip == j, own, gathered[j]) for j in range(N_CHIPS)]

    conv_buf = _flat([w[k] for k in SMALL_COLS], f32)
    (conv_all,) = _gather_shards([conv_buf], "gather_conv")
    sm = {k: w[k] for k in SMALL_REPL}
    conv_parts = [_unflat(b, [w[k].shape for k in SMALL_COLS]) for b in by_chip(conv_buf, conv_all)]
    for n, k in enumerate(SMALL_COLS):
        sm[k] = jnp.concatenate([conv_parts[j][n] for j in range(N_CHIPS)], axis=-1)
    Ss = [_small_params(sm, i) for i in range(DEPTH)]
    owns = [[w[k][i].astype(bf16) for k in BIG] for i in range(DEPTH)]
    gathers = [_gather_start(owns[i], f"gather_start_l{i}") for i in range(DEPTH)]
    started = sum(g["token"][0, 0] for g in gathers)

    h = x[0]
    saved, Ws = [], []
    for i in range(DEPTH):
        own, lands = _exchange_wait(gathers[i], gathers[-1]["token"] if i == 0 else h, f"gather_wait_l{i}")
        lands = _gather_forward(lands, f"gather_forward_l{i}")
        Ws.append(_big_weights({k: by_chip(o, g) for k, o, g in zip(BIG, own, lands)}))
        S_i = dict(Ss[i], norm_mix=Ss[i]["norm_mix"] + started) if i == 0 else Ss[i]
        h, sv = _layer_fwd(h, p[i, 0], Ws[i], S_i, i)
        saved.append(sv)
    loss_row, dh, g_final = _loss_head(h, sm["norm_final"][None], loss_target[0], "loss_head")

    gsm = {k: [None] * DEPTH for k in SMALL_REPL + SMALL_COLS if k != "norm_final"}
    gbig = {k: [None] * DEPTH for k in BIG}
    pending = None

    def finish(after):
        j, handle = pending
        for k, g in zip(BIG, _rs_end(handle, after, BIG, chip, core, f"l{j}")):
            gbig[k][j] = g

    for i in reversed(range(DEPTH)):
        S_i = Ss[i] if pending is None else dict(Ss[i], norm_ple=Ss[i]["norm_ple"] + pending[1]["token"][0, 0])
        dh, G = _layer_bwd(dh, saved[i], Ws[i], S_i, i)
        if pending is not None:
            finish(dh)
        parts = _grads_by_chip(G)
        pending = (i, _rs_begin([parts[k] for k in BIG], BIG, core, f"l{i}"))
        for k in gsm:
            gsm[k][i] = G[k].reshape(w[k].shape[1:-1] + (-1,)) if k in SMALL_COLS else G[k].reshape(w[k].shape[1:])

    small_names = [k for k in SMALL_REPL + SMALL_COLS if k != "norm_final"]
    small_local = [jnp.stack(gsm[k]) for k in small_names] + [g_final.reshape(-1), loss_row[0, :1]]
    small_shapes = [a.shape for a in small_local]
    small_buf = _allreduce_small(_flat(small_local, f32), "allreduce_small")
    finish(small_buf)
    small_sum = _unflat(small_buf, small_shapes)
    gs = dict(zip(small_names + ["norm_final"], small_sum[:-1]))
    loss = small_sum[-1][0]
    for k in SMALL_COLS:
        n = w[k].shape[-1]
        gs[k] = _pick(chip, [gs[k][..., j * n:(j + 1) * n] for j in range(N_CHIPS)])

    grads, deltas, new_m, new_v = {}, {}, {}, {}
    for k in BIG:
        g = jnp.stack(gbig[k])
        two = lambda a: a.reshape(-1, a.shape[-1])
        d, m2, v2 = _ew(_adamw, [two(w[k]), two(g), two(m[k]), two(v[k])], [f32, f32, f32], f"adamw_{k}")
        grads[k], deltas[k], new_m[k], new_v[k] = g, d.reshape(g.shape), m2.reshape(g.shape), v2.reshape(g.shape)
    small_all = [k for k in WEIGHTS if k not in BIG]
    shapes = [w[k].shape for k in small_all]
    d, m2, v2 = _ew(_adamw, [_flat([w[k] for k in small_all], f32), _flat([gs[k] for k in small_all], f32),
                             _flat([m[k] for k in small_all], f32), _flat([v[k] for k in small_all], f32)],
                    [f32, f32, f32], "adamw_small")
    for k, a, b, c_ in zip(small_all, _unflat(d, shapes), _unflat(m2, shapes), _unflat(v2, shapes)):
        grads[k], deltas[k], new_m[k], new_v[k] = gs[k], a, b, c_
    return (loss, dh[None], *[grads[k] for k in WEIGHTS], *[deltas[k] for k in WEIGHTS],
            *[new_m[k] for k in WEIGHTS], *[new_v[k] for k in WEIGHTS])
```

```python
import functools
import math

import jax
import jax.numpy as jnp
from jax import lax
from jax.experimental import pallas as pl
from jax.experimental.pallas import tpu as pltpu

f32 = jnp.float32
bf16 = jnp.bfloat16
HI = lax.Precision.HIGHEST
SDS = jax.ShapeDtypeStruct

D_MODEL = 2048
DEPTH = 4
HEAD_DIM = 128
N_QK = 8
N_V = 16
QK_W = N_QK * HEAD_DIM
V_W = N_V * HEAD_DIM
CHUNK_A = 64
N_GROUPS = 8
GROUP_DIM = 128
WIDTH_B = N_GROUPS * GROUP_DIM
CHUNK_B = 128
D_FF = 5632
PLE_DIM = 256
EPS = 1e-6
N_IN = 12320
ADAM_LR, ADAM_B1, ADAM_B2, ADAM_EPS, ADAM_WD, ADAM_STEP = 0.001, 0.9, 0.999, 1e-08, 0.01, 10

C_Q, C_K, C_V, C_Z, C_UB, C_VB, C_GA, C_GB, PM = 0, 1024, 2048, 4096, 6144, 7168, 8192, 10240, 12288
LANES = 128
VMEM_LIMIT_BYTES = 48 * 1024 * 1024
ROW_CHUNK = 256


def _cp(*sem):
    return pltpu.CompilerParams(dimension_semantics=sem if sem else None, vmem_limit_bytes=VMEM_LIMIT_BYTES)


def _dot(a, b, prec=None):
    return jnp.dot(a, b, preferred_element_type=f32, precision=prec)


def _dot_nt(a, b, prec=None):
    return lax.dot_general(a, b, (((1,), (1,)), ((), ())), preferred_element_type=f32, precision=prec)


def _dot_tn(a, b, prec=None):
    return lax.dot_general(a, b, (((0,), (0,)), ((), ())), preferred_element_type=f32, precision=prec)


def _b(x):
    return x.astype(bf16)


def _dot3(a, b, dims=(((1,), (0,)), ((), ()))):
    ah, bh = _b(a), _b(b)
    al, bl = _b(a - ah.astype(f32)), _b(b - bh.astype(f32))
    dg = lambda p, q: lax.dot_general(p, q, dims, preferred_element_type=f32)
    return dg(ah, bh) + (dg(ah, bl) + dg(al, bh))


_TN = (((0,), (0,)), ((), ()))


def _sig(x):
    return 1.0 / (1.0 + jnp.exp(-x))


def _silu(x):
    return x * _sig(x)


def _dsilu(x):
    s = _sig(x)
    return s * (1.0 + x * (1.0 - s))


_GELU_C = 0.7978845608028654
_GELU_A = 0.044715


def _gelu(x):
    return 0.5 * x * (1.0 + jnp.tanh(_GELU_C * (x + _GELU_A * x * x * x)))


def _dgelu(x):
    t = jnp.tanh(_GELU_C * (x + _GELU_A * x * x * x))
    return 0.5 * (1.0 + t) + 0.5 * x * (1.0 - t * t) * _GELU_C * (1.0 + 3.0 * _GELU_A * x * x)


def _softplus(x):
    return jnp.maximum(x, 0.0) + jnp.log(1.0 + jnp.exp(-jnp.abs(x)))


def _divisor(n, cap, mult):
    best = None
    d = mult
    while d <= min(n, cap):
        if n % d == 0:
            best = d
        d += mult
    return best if best is not None else n


def _rows(tr, c, j=0):
    return pl.BlockSpec((tr, c), lambda i, j=j: (i, j))


def _whole(shape):
    nd = len(shape)
    return pl.BlockSpec(shape, lambda *_: (0,) * nd)


def _mm(a, b, mode, name, out_dtype=f32, c=None, tm=1024, tn=1024, tk=2048):
    if mode == "nn":
        (M, K), (K2, N) = a.shape, b.shape
    elif mode == "nt":
        (M, K), (N, K2) = a.shape, b.shape
    else:
        (K, M), (K2, N) = a.shape, b.shape
    assert K == K2, (a.shape, b.shape, mode)
    tm = _divisor(M, tm, LANES if mode == "tn" else 16)
    tn = _divisor(N, tn, LANES)
    tk = _divisor(K, tk, LANES if mode != "tn" else 16)
    nk = K // tk
    if mode == "nn":
        a_spec = pl.BlockSpec((tm, tk), lambda i, j, k: (i, k))
        b_spec = pl.BlockSpec((tk, tn), lambda i, j, k: (k, j))
        dn = (((1,), (0,)), ((), ()))
    elif mode == "nt":
        a_spec = pl.BlockSpec((tm, tk), lambda i, j, k: (i, k))
        b_spec = pl.BlockSpec((tn, tk), lambda i, j, k: (j, k))
        dn = (((1,), (1,)), ((), ()))
    else:
        a_spec = pl.BlockSpec((tk, tm), lambda i, j, k: (k, i))
        b_spec = pl.BlockSpec((tk, tn), lambda i, j, k: (k, j))
        dn = (((0,), (0,)), ((), ()))
    o_spec = pl.BlockSpec((tm, tn), lambda i, j, k: (i, j))
    has_c = c is not None

    def body(*refs):
        a_ref, b_ref = refs[0], refs[1]
        c_ref = refs[2] if has_c else None
        o_ref = refs[3] if has_c else refs[2]
        part = lax.dot_general(_b(a_ref[...]), _b(b_ref[...]), dn, preferred_element_type=f32)
        if nk == 1:
            if has_c:
                part = part + c_ref[...]
            o_ref[...] = part.astype(o_ref.dtype)
        else:
            acc = refs[-1]
            k = pl.program_id(2)

            @pl.when(k == 0)
            def _():
                acc[...] = part

            @pl.when(k > 0)
            def _():
                acc[...] += part

            @pl.when(k == nk - 1)
            def _():
                r = acc[...]
                if has_c:
                    r = r + c_ref[...]
                o_ref[...] = r.astype(o_ref.dtype)

    ins = [a, b] + ([c] if has_c else [])
    in_specs = [a_spec, b_spec] + ([o_spec] if has_c else [])
    return pl.pallas_call(
        body, grid=(M // tm, N // tn, nk), in_specs=in_specs, out_specs=o_spec, out_shape=SDS((M, N), out_dtype),
        scratch_shapes=[pltpu.VMEM((tm, tn), f32)] if nk > 1 else [], name=name,
        compiler_params=_cp("parallel", "parallel", "arbitrary"),
    )(*ins)


def _ew(fn, ins, out_dtypes, name, tile_bytes=2 * 1024 * 1024):
    R, C = max((x.shape for x in ins), key=lambda s: s[0])
    n_in = len(ins)
    row_bytes = 4 * C * (len(ins) + len(out_dtypes))
    tr = _divisor(R, max(16, tile_bytes // row_bytes), 16)
    in_specs = [_rows(tr, C) if x.shape[0] == R else _whole((1, C)) for x in ins]

    def body(*refs):
        res = fn(*[r[...] for r in refs[:n_in]])
        for o_ref, r in zip(refs[n_in:], res):
            o_ref[...] = r.astype(o_ref.dtype)

    outs = pl.pallas_call(
        body, grid=(R // tr,), in_specs=in_specs, out_specs=[_rows(tr, C) for _ in out_dtypes],
        out_shape=[SDS((R, C), dt) for dt in out_dtypes], name=name, compiler_params=_cp("parallel"),
    )(*ins)
    return outs


def _rms_fwd(x, gain, name):
    T, Dm = x.shape
    tr = _divisor(T, 256, 16)

    def body(x_ref, g_ref, o_ref):
        xv = x_ref[...]
        r = lax.rsqrt(jnp.mean(xv * xv, axis=-1, keepdims=True) + EPS)
        o_ref[...] = (xv * r * g_ref[...]).astype(o_ref.dtype)

    return pl.pallas_call(
        body, grid=(T // tr,), in_specs=[_rows(tr, Dm), _whole((1, Dm))], out_specs=_rows(tr, Dm),
        out_shape=SDS((T, Dm), bf16), name=name, compiler_params=_cp("parallel"),
    )(x, gain)


def _rms_bwd(x, gain, dh, dres, name):
    T, Dm = x.shape
    tr = _divisor(T, 256, 16)

    def body(x_ref, g_ref, dh_ref, dres_ref, dx_ref, dg_ref):
        xv = x_ref[...]
        dhv = dh_ref[...]
        r = lax.rsqrt(jnp.mean(xv * xv, axis=-1, keepdims=True) + EPS)
        dy = dhv * g_ref[...]
        m = jnp.mean(dy * xv, axis=-1, keepdims=True)
        dx_ref[...] = dres_ref[...] + r * dy - xv * (r * r * r * m)
        part = jnp.sum(dhv * xv * r, axis=0, keepdims=True)

        @pl.when(pl.program_id(0) == 0)
        def _():
            dg_ref[...] = part

        @pl.when(pl.program_id(0) > 0)
        def _():
            dg_ref[...] += part

    return pl.pallas_call(
        body, grid=(T // tr,), in_specs=[_rows(tr, Dm), _whole((1, Dm)), _rows(tr, Dm), _rows(tr, Dm)],
        out_specs=[_rows(tr, Dm), _whole((1, Dm))], out_shape=[SDS((T, Dm), f32), SDS((1, Dm), f32)],
        name=name, compiler_params=_cp("arbitrary"),
    )(x, gain, dh, dres)


def _shift_down(cur, prev8, s):
    rolled = pltpu.roll(cur, s, 0)
    rp = pltpu.roll(prev8, s, 0)
    row8 = lax.broadcasted_iota(jnp.int32, prev8.shape, 0)
    first = jnp.where(row8 < s, rp, rolled[:8])
    return jnp.concatenate([first, rolled[8:]], axis=0)


def _shift_up(cur, next8, s):
    R = cur.shape[0]
    rolled = pltpu.roll(cur, R - s, 0)
    rn = pltpu.roll(next8, 8 - s, 0)
    row8 = lax.broadcasted_iota(jnp.int32, next8.shape, 0)
    last = jnp.where(row8 >= 8 - s, rn, rolled[R - 8:])
    return jnp.concatenate([rolled[: R - 8], last], axis=0)


def _prev8(ref, r0, ci):
    rows = ref[pl.ds(pl.multiple_of(jnp.maximum(r0 - 8, 0), 8), 8), :]
    return jnp.where(ci > 0, rows, 0.0)


def _next8(ref, r0, R, ci, n_chunks):
    start = jnp.minimum(r0 + R, (n_chunks - 1) * R + R - 8)
    rows = ref[pl.ds(pl.multiple_of(start, 8), 8), :]
    return jnp.where(ci < n_chunks - 1, rows, 0.0)


def _w_rows(w_ref, k):
    return [w_ref[j:j + 1, :] for j in range(k)]


def _causal_conv(cur, prev8, w, k):
    acc = cur * w[k - 1]
    for s in range(1, k):
        acc = acc + _shift_down(cur, prev8, s) * w[k - 1 - s]
    return acc


def _qkvconv_fwd(proj, convw, name):
    T = proj.shape[0]
    R = min(ROW_CHUNK, T)
    n_chunks = T // R
    n_blk = (2 * QK_W + V_W) // HEAD_DIM

    def body(x_ref, w_ref, o_ref):
        p = pl.program_id(0)
        is_qk = p < 2 * N_QK
        scale = jnp.where(p < N_QK, HEAD_DIM ** -0.5, 1.0).astype(f32)
        w = _w_rows(w_ref, 4)

        def chunk(ci, carry):
            r0 = pl.multiple_of(ci * R, R)
            cur = x_ref[pl.ds(r0, R), :]
            y = _silu(_causal_conv(cur, _prev8(x_ref, r0, ci), w, 4))
            ss = jnp.sum(y * y, axis=-1, keepdims=True)
            nrm = jnp.where(is_qk, lax.rsqrt(ss + EPS) * scale, 1.0)
            o_ref[pl.ds(r0, R), :] = y * nrm
            return carry

        lax.fori_loop(0, n_chunks, chunk, 0)

    return pl.pallas_call(
        body, grid=(n_blk,),
        in_specs=[pl.BlockSpec((T, HEAD_DIM), lambda p: (0, p)), pl.BlockSpec((4, HEAD_DIM), lambda p: (0, p))],
        out_specs=pl.BlockSpec((T, HEAD_DIM), lambda p: (0, p)), out_shape=SDS((T, n_blk * HEAD_DIM), f32),
        name=name, compiler_params=_cp("parallel"),
    )(proj, convw)


def _qkvconv_bwd(proj, convw, dqkv, name):
    T = proj.shape[0]
    R = min(ROW_CHUNK, T)
    n_chunks = T // R
    n_blk = (2 * QK_W + V_W) // HEAD_DIM

    def body(x_ref, w_ref, do_ref, dx_ref, dw_ref, dc_sc):
        p = pl.program_id(0)
        is_qk = p < 2 * N_QK
        scale = jnp.where(p < N_QK, HEAD_DIM ** -0.5, 1.0).astype(f32)
        w = _w_rows(w_ref, 4)

        def phase1(ci, dw):
            r0 = pl.multiple_of(ci * R, R)
            cur = x_ref[pl.ds(r0, R), :]
            p8 = _prev8(x_ref, r0, ci)
            shifted = [cur] + [_shift_down(cur, p8, s) for s in range(1, 4)]
            c = shifted[0] * w[3]
            for s in range(1, 4):
                c = c + shifted[s] * w[3 - s]
            y = _silu(c)
            dout = do_ref[pl.ds(r0, R), :]
            n = lax.rsqrt(jnp.sum(y * y, axis=-1, keepdims=True) + EPS)
            dot_ = jnp.sum(dout * y, axis=-1, keepdims=True)
            dy = jnp.where(is_qk, scale * (n * dout - y * (n * n * n * dot_)), dout)
            dc = dy * _dsilu(c)
            dc_sc[pl.ds(r0, R), :] = dc
            return tuple(dw[j] + jnp.sum(dc * shifted[3 - j], axis=0, keepdims=True) for j in range(4))

        dw = lax.fori_loop(0, n_chunks, phase1, tuple(jnp.zeros((1, HEAD_DIM), f32) for _ in range(4)))
        for j in range(4):
            dw_ref[j:j + 1, :] = dw[j]

        def phase2(ci, carry):
            r0 = pl.multiple_of(ci * R, R)
            cur = dc_sc[pl.ds(r0, R), :]
            n8 = _next8(dc_sc, r0, R, ci, n_chunks)
            acc = cur * w[3]
            for s in range(1, 4):
                acc = acc + _shift_up(cur, n8, s) * w[3 - s]
            dx_ref[pl.ds(r0, R), :] = acc.astype(dx_ref.dtype)
            return carry

        lax.fori_loop(0, n_chunks, phase2, 0)

    col = lambda p: (0, p)
    return pl.pallas_call(
        body, grid=(n_blk,),
        in_specs=[pl.BlockSpec((T, HEAD_DIM), col), pl.BlockSpec((4, HEAD_DIM), col), pl.BlockSpec((T, HEAD_DIM), col)],
        out_specs=[pl.BlockSpec((T, HEAD_DIM), col), pl.BlockSpec((4, HEAD_DIM), col)],
        out_shape=[SDS((T, n_blk * HEAD_DIM), bf16), SDS((4, n_blk * HEAD_DIM), f32)],
        scratch_shapes=[pltpu.VMEM((T, HEAD_DIM), f32)], name=name, compiler_params=_cp("parallel"),
    )(proj, convw, dqkv)


def _tri_masks(C):
    row = lax.broadcasted_iota(jnp.int32, (C, C), 0)
    col = lax.broadcasted_iota(jnp.int32, (C, C), 1)
    return row, col


def _lane_pick(blk, lane, idx):
    return jnp.sum(jnp.where(lane == idx, blk, 0.0), axis=1, keepdims=True)


def _gate_block(ba, ea, dtb, lane):
    sig = _sig(ba)
    gblk = -ea * _softplus(ba + dtb)
    return sig, gblk


def _head_decay(gam_all, rg_all, tot, lane, h, row, col):
    C = row.shape[0]
    gam_c = _lane_pick(gam_all, lane, N_V + h)
    rg_c = _lane_pick(rg_all, lane, N_V + h)
    lane1 = lax.broadcasted_iota(jnp.int32, (1, LANES), 1)
    tot_h = jnp.sum(jnp.where(lane1 == N_V + h, tot, 0.0), axis=1, keepdims=True)
    gcb = jnp.broadcast_to(gam_c, (C, C))
    dlt = gcb - gcb.T
    dm = jnp.where(row >= col, jnp.exp(jnp.minimum(dlt, 0.0)), 0.0)
    return gam_c, rg_c, tot_h, dm


def _delta_prep(qkv, ba, ea_row, dtb_row, name):
    T = qkv.shape[0]
    C = CHUNK_A
    N = T // C

    SUB = 16

    def body(q_ref, k_ref, v_ref, ba_ref, ea_ref, dtb_ref,
             qd_ref, kd_ref, u_ref, w_ref, attn_ref, tinv_ref, cd_ref, bg_ref, at_sc, t_sc, a_sc, rk_sc):
        row, col = _tri_masks(C)
        lane = lax.broadcasted_iota(jnp.int32, (C, LANES), 1)
        sig, gblk = _gate_block(ba_ref[...], ea_ref[...], dtb_ref[...], lane)
        bg_ref[...] = jnp.where(lane < N_V, sig, gblk)
        lower = (row >= col).astype(f32)
        upper_s = (col > row).astype(f32)
        eye = (row == col).astype(f32)
        same16 = (row >> 4) == (col >> 4)
        same32 = (row >> 5) == (col >> 5)
        gam_all = _dot(lower, gblk, HI)
        rg_all = _dot(upper_s, gblk, HI)
        tot = jnp.sum(gblk, axis=0, keepdims=True)
        for h in range(N_V):
            qk = pl.ds((h // 2) * HEAD_DIM, HEAD_DIM)
            hs = pl.ds(h * HEAD_DIM, HEAD_DIM)
            qh, kh, vh = q_ref[:, qk], k_ref[:, qk], v_ref[:, hs]
            beta_c = _lane_pick(sig, lane, h)
            gam_c, rg_c, tot_h, dm = _head_decay(gam_all, rg_all, tot, lane, h, row, col)
            kk = _dot_nt(_b(kh), _b(kh))
            a = jnp.where(row > col, beta_c * kk * dm, 0.0)
            a_sc[h] = a
            at_sc[h] = jnp.where(same16, a, 0.0).T
            t_sc[h] = eye
            eg = jnp.exp(gam_c)
            attn_ref[0, h] = (_dot_nt(_b(qh), _b(kh)) * dm).astype(attn_ref.dtype)
            qd_ref[:, hs] = (qh * eg).astype(qd_ref.dtype)
            kd_ref[:, hs] = (kh * jnp.exp(rg_c)).astype(kd_ref.dtype)
            cd_ref[0, h] = jnp.broadcast_to(jnp.exp(tot_h), (1, LANES))
            u_ref[:, hs] = vh * beta_c
            rk_sc[:, hs] = kh * (beta_c * eg)

        first_col = (row >> 4) << 4

        def fsub(i, carry):
            for h in range(N_V):
                t = t_sc[h]
                a_col = jnp.sum(jnp.where(col == first_col + i, at_sc[h], 0.0), axis=1, keepdims=True)
                prod = a_col * t
                sums = [jnp.sum(prod[b * SUB:(b + 1) * SUB], axis=0, keepdims=True) for b in range(C // SUB)]
                new = eye - jnp.concatenate([jnp.broadcast_to(s, (SUB, C)) for s in sums], axis=0)
                t_sc[h] = jnp.where(row - first_col == i, new, t)
            return carry

        lax.fori_loop(1, SUB, fsub, 0)
        for h in range(N_V):
            hs = pl.ds(h * HEAD_DIM, HEAD_DIM)
            a = a_sc[h]
            p16 = t_sc[h]
            low16 = jnp.where(jnp.logical_and(same32, jnp.logical_not(same16)), a, 0.0)
            p32 = p16 - _dot3(_dot3(p16, low16), p16)
            tinv = p32 - _dot3(_dot3(p32, jnp.where(same32, 0.0, a)), p32)
            tinv_ref[0, h] = tinv
            u_ref[:, hs] = _dot3(tinv, u_ref[:, hs])
            w_ref[:, hs] = _dot3(tinv, rk_sc[:, hs]).astype(w_ref.dtype)

    big = lambda n: (n, 0)
    return pl.pallas_call(
        body, grid=(N,),
        in_specs=[pl.BlockSpec((C, QK_W), lambda n: (n, 0)), pl.BlockSpec((C, QK_W), lambda n: (n, 1)),
                  pl.BlockSpec((C, V_W), lambda n: (n, 1)), pl.BlockSpec((C, LANES), big),
                  _whole((1, LANES)), _whole((1, LANES))],
        out_specs=[pl.BlockSpec((C, V_W), big)] * 4 + [
            pl.BlockSpec((1, N_V, C, C), lambda n: (n, 0, 0, 0)), pl.BlockSpec((1, N_V, C, C), lambda n: (n, 0, 0, 0)),
            pl.BlockSpec((1, N_V, 1, LANES), lambda n: (n, 0, 0, 0)), pl.BlockSpec((C, LANES), big)],
        out_shape=[SDS((T, V_W), bf16), SDS((T, V_W), bf16), SDS((T, V_W), f32), SDS((T, V_W), bf16),
                   SDS((N, N_V, C, C), bf16), SDS((N, N_V, C, C), f32), SDS((N, N_V, 1, LANES), f32), SDS((T, LANES), f32)],
        scratch_shapes=[pltpu.VMEM((N_V, C, C), f32)] * 3 + [pltpu.VMEM((C, V_W), f32)],
        name=name, compiler_params=_cp("parallel"),
    )(qkv, qkv, qkv, ba, ea_row, dtb_row)


def _delta_scan(qd, kd, u, w, attn, cd, name):
    T = qd.shape[0]
    C = CHUNK_A
    N = T // C

    def body(qd_ref, kd_ref, u_ref, w_ref, attn_ref, cd_ref, o_ref, s_ref, vn_ref, s_sc):
        @pl.when(pl.program_id(0) == 0)
        def _():
            s_sc[...] = jnp.zeros_like(s_sc)

        for h in range(N_V):
            hs = pl.ds(h * HEAD_DIM, HEAD_DIM)
            s = s_sc[h]
            s_ref[0, h] = s
            sb = _b(s)
            vn = u_ref[:, hs] - _dot(w_ref[:, hs], sb)
            vnb = _b(vn)
            vn_ref[:, hs] = vnb
            o_ref[:, hs] = _dot(qd_ref[:, hs], sb) + _dot(attn_ref[0, h], vnb)
            s_sc[h] = s * cd_ref[0, h] + _dot_tn(kd_ref[:, hs], vnb)

    blk = pl.BlockSpec((C, V_W), lambda n: (n, 0))
    per_head = lambda a, b: pl.BlockSpec((1, N_V, a, b), lambda n: (n, 0, 0, 0))
    return pl.pallas_call(
        body, grid=(N,), in_specs=[blk, blk, blk, blk, per_head(C, C), per_head(1, LANES)],
        out_specs=[blk, per_head(HEAD_DIM, HEAD_DIM), blk],
        out_shape=[SDS((T, V_W), f32), SDS((N, N_V, HEAD_DIM, HEAD_DIM), f32), SDS((T, V_W), bf16)],
        scratch_shapes=[pltpu.VMEM((N_V, HEAD_DIM, HEAD_DIM), f32)], name=name, compiler_params=_cp("arbitrary"),
    )(qd, kd, u, w, attn, cd)


def _delta_scan_bwd(do, qd, kd, w, attn, cd, s_all, vn, name):
    T = qd.shape[0]
    C = CHUNK_A
    N = T // C

    def body(do_ref, qd_ref, kd_ref, w_ref, attn_ref, cd_ref, s_ref, vn_ref,
             dqd_ref, dkd_ref, du_ref, dw_ref, dattn_ref, dcd_ref, ds_sc):
        @pl.when(pl.program_id(0) == 0)
        def _():
            ds_sc[...] = jnp.zeros_like(ds_sc)

        row, col = _tri_masks(C)
        for h in range(N_V):
            hs = pl.ds(h * HEAD_DIM, HEAD_DIM)
            dsn = ds_sc[h]
            s = s_ref[0, h]
            dob, sb, dsb = _b(do_ref[:, hs]), _b(s), _b(dsn)
            vnb = vn_ref[:, hs]
            dqd_ref[:, hs] = _dot_nt(dob, sb)
            dattn_ref[0, h] = jnp.where(row >= col, _dot_nt(dob, vnb), 0.0)
            dvn = _dot_tn(attn_ref[0, h], dob) + _dot(kd_ref[:, hs], dsb)
            dvnb = _b(dvn)
            dkd_ref[:, hs] = _dot_nt(vnb, dsb)
            dcd = jnp.sum(jnp.sum(s * dsn, axis=1, keepdims=True), axis=0, keepdims=True)
            dcd_ref[0, h] = jnp.broadcast_to(dcd, (1, LANES))
            du_ref[:, hs] = dvn
            dw_ref[:, hs] = -_dot_nt(dvnb, sb)
            ds_sc[h] = dsn * cd_ref[0, h] + _dot_tn(qd_ref[:, hs], dob) - _dot_tn(w_ref[:, hs], dvnb)

    blk = pl.BlockSpec((C, V_W), lambda n: (N - 1 - n, 0))
    per_head = lambda a, b: pl.BlockSpec((1, N_V, a, b), lambda n: (N - 1 - n, 0, 0, 0))
    return pl.pallas_call(
        body, grid=(N,),
        in_specs=[blk, blk, blk, blk, per_head(C, C), per_head(1, LANES), per_head(HEAD_DIM, HEAD_DIM), blk],
        out_specs=[blk, blk, blk, blk, per_head(C, C), per_head(1, LANES)],
        out_shape=[SDS((T, V_W), f32)] * 4 + [SDS((N, N_V, C, C), f32), SDS((N, N_V, 1, LANES), f32)],
        scratch_shapes=[pltpu.VMEM((N_V, HEAD_DIM, HEAD_DIM), f32)], name=name, compiler_params=_cp("arbitrary"),
    )(do, qd, kd, w, attn, cd, s_all, vn)


def _delta_prep_bwd(qkv, ba, ea_row, dtb_row, tinv, u, w, dqd, dkd, du, dw, dattn, dcd, name):
    T = qkv.shape[0]
    C = CHUNK_A
    N = T // C

    def body(q_ref, k_ref, v_ref, ba_ref, ea_ref, dtb_ref, tinv_ref, u_ref, w_ref,
             dqd_ref, dkd_ref, du_ref, dw_ref, dattn_ref, dcd_ref,
             dqkv_ref, dba_ref, dalog_ref, ddtb_ref):
        row, col = _tri_masks(C)
        lane = lax.broadcasted_iota(jnp.int32, (C, LANES), 1)
        rowc = lax.broadcasted_iota(jnp.int32, (C, 1), 0)
        ba_v, ea, dtb = ba_ref[...], ea_ref[...], dtb_ref[...]
        sig, gblk = _gate_block(ba_v, ea, dtb, lane)
        lower = (row >= col).astype(f32)
        upper_s = (col > row).astype(f32)
        upper = (col >= row).astype(f32)
        gam_all = _dot(lower, gblk, HI)
        rg_all = _dot(upper_s, gblk, HI)
        tot = jnp.sum(gblk, axis=0, keepdims=True)
        dbeta_blk = jnp.zeros((C, LANES), f32)
        dgam_blk = jnp.zeros((C, LANES), f32)
        for j in range(N_QK):
            qk = pl.ds(j * HEAD_DIM, HEAD_DIM)
            qh, kh = q_ref[:, qk], k_ref[:, qk]
            qhb, khb = _b(qh), _b(kh)
            kk = _dot_nt(khb, khb)
            qkm = _dot_nt(qhb, khb)
            dq_j = jnp.zeros((C, HEAD_DIM), f32)
            dk_j = jnp.zeros((C, HEAD_DIM), f32)
            for h in (2 * j, 2 * j + 1):
                hs = pl.ds(h * HEAD_DIM, HEAD_DIM)
                vh = v_ref[:, hs]
                beta_c = _lane_pick(sig, lane, h)
                gam_c, rg_c, tot_h, dm = _head_decay(gam_all, rg_all, tot, lane, h, row, col)
                eg, er, cdh = jnp.exp(gam_c), jnp.exp(rg_c), jnp.exp(tot_h)
                tinv_h = tinv_ref[0, h]
                d_rv = _dot3(tinv_h, du_ref[:, hs], _TN)
                d_rk = _dot3(tinv_h, dw_ref[:, hs], _TN)
                da = -jnp.where(row > col, _dot_nt(_b(d_rv), _b(u_ref[:, hs])) + _dot_nt(_b(d_rk), _b(w_ref[:, hs])), 0.0)
                dqkv_ref[:, pl.ds(2 * QK_W + h * HEAD_DIM, HEAD_DIM)] = beta_c * d_rv
                dbeta = jnp.sum(d_rv * vh + d_rk * (eg * kh), axis=1, keepdims=True)
                dk_h = (beta_c * eg) * d_rk
                d_eg = jnp.sum(d_rk * kh, axis=1, keepdims=True) * beta_c
                bkd = da * dm
                dbeta = dbeta + jnp.sum(bkd * kk, axis=1, keepdims=True)
                dkk = bkd * beta_c
                ddm = da * beta_c * kk
                dattn_h = dattn_ref[0, h]
                dqk = dattn_h * dm
                ddm = ddm + dattn_h * qkm
                dqd_h, dkd_h = dqd_ref[:, hs], dkd_ref[:, hs]
                dq_j = dq_j + _dot(_b(dqk), khb) + eg * dqd_h
                dk_h = dk_h + _dot_tn(_b(dqk), qhb) + _dot(_b(dkk + dkk.T), khb) + er * dkd_h
                dk_j = dk_j + dk_h
                d_eg = d_eg + jnp.sum(dqd_h * qh, axis=1, keepdims=True)
                d_er = jnp.sum(dkd_h * kh, axis=1, keepdims=True)
                e = ddm * dm
                dgam = jnp.sum(e, axis=1, keepdims=True) - jnp.sum(e.T, axis=1, keepdims=True)
                dgam = dgam + d_eg * eg - d_er * er
                extra = jnp.sum(d_er * er, axis=0, keepdims=True) + jnp.max(dcd_ref[0, h], axis=1, keepdims=True) * cdh
                dgam = dgam + jnp.where(rowc == C - 1, extra, 0.0)
                dbeta_blk = jnp.where(lane == h, dbeta, dbeta_blk)
                dgam_blk = jnp.where(lane == N_V + h, dgam, dgam_blk)
            dqkv_ref[:, qk] = dq_j
            dqkv_ref[:, pl.ds(QK_W + j * HEAD_DIM, HEAD_DIM)] = dk_j
        dg_all = _dot(upper, dgam_blk, HI)
        dsp = dg_all * (-ea) * _sig(ba_v + dtb)
        dba_ref[...] = jnp.where(lane < N_V, dbeta_blk * sig * (1.0 - sig), dsp)
        part_alog = jnp.sum(dg_all * gblk, axis=0, keepdims=True)
        part_dtb = jnp.sum(dsp, axis=0, keepdims=True)

        @pl.when(pl.program_id(0) == 0)
        def _():
            dalog_ref[...] = part_alog
            ddtb_ref[...] = part_dtb

        @pl.when(pl.program_id(0) > 0)
        def _():
            dalog_ref[...] += part_alog
            ddtb_ref[...] += part_dtb

    big = lambda n: (n, 0)
    wide = pl.BlockSpec((C, V_W), big)
    sq = pl.BlockSpec((1, N_V, C, C), lambda n: (n, 0, 0, 0))
    return pl.pallas_call(
        body, grid=(N,),
        in_specs=[pl.BlockSpec((C, QK_W), lambda n: (n, 0)), pl.BlockSpec((C, QK_W), lambda n: (n, 1)),
                  pl.BlockSpec((C, V_W), lambda n: (n, 1)), pl.BlockSpec((C, LANES), big),
                  _whole((1, LANES)), _whole((1, LANES)), sq, wide, wide, wide, wide, wide, wide, sq,
                  pl.BlockSpec((1, N_V, 1, LANES), lambda n: (n, 0, 0, 0))],
        out_specs=[pl.BlockSpec((C, 2 * QK_W + V_W), big), pl.BlockSpec((C, LANES), big),
                   _whole((1, LANES)), _whole((1, LANES))],
        out_shape=[SDS((T, 2 * QK_W + V_W), f32), SDS((T, LANES), f32), SDS((1, LANES), f32), SDS((1, LANES), f32)],
        name=name, compiler_params=_cp("arbitrary"),
    )(qkv, qkv, qkv, ba, ea_row, dtb_row, tinv, u, w, dqd, dkd, du, dw, dattn, dcd)


def _onorm_fwd(o, proj, hg, name):
    T = o.shape[0]
    tr = _divisor(T, 256, 16)

    def body(o_ref, z_ref, g_ref, out_ref):
        g = g_ref[...]
        for h in range(N_V):
            hs = pl.ds(h * HEAD_DIM, HEAD_DIM)
            oh = o_ref[:, hs]
            r = lax.rsqrt(jnp.mean(oh * oh, axis=-1, keepdims=True) + EPS)
            out_ref[:, hs] = (oh * r * g * _silu(z_ref[:, hs])).astype(out_ref.dtype)

    return pl.pallas_call(
        body, grid=(T // tr,), in_specs=[_rows(tr, V_W), _rows(tr, V_W, C_Z // V_W), _whole((1, HEAD_DIM))],
        out_specs=_rows(tr, V_W), out_shape=SDS((T, V_W), bf16), name=name, compiler_params=_cp("parallel"),
    )(o, proj, hg)


def _onorm_bwd(don, o, proj, hg, name):
    T = o.shape[0]
    tr = _divisor(T, 256, 16)

    def body(don_ref, o_ref, z_ref, g_ref, do_ref, dz_ref, dg_ref):
        g = g_ref[...]
        dg = jnp.zeros((1, HEAD_DIM), f32)
        for h in range(N_V):
            hs = pl.ds(h * HEAD_DIM, HEAD_DIM)
            oh, zh, dh = o_ref[:, hs], z_ref[:, hs], don_ref[:, hs]
            r = lax.rsqrt(jnp.mean(oh * oh, axis=-1, keepdims=True) + EPS)
            d_n = dh * _silu(zh)
            dz_ref[:, hs] = (dh * (oh * r * g) * _dsilu(zh)).astype(dz_ref.dtype)
            dy = d_n * g
            m = jnp.mean(dy * oh, axis=-1, keepdims=True)
            do_ref[:, hs] = r * dy - oh * (r * r * r * m)
            dg = dg + jnp.sum(d_n * oh * r, axis=0, keepdims=True)

        @pl.when(pl.program_id(0) == 0)
        def _():
            dg_ref[...] = dg

        @pl.when(pl.program_id(0) > 0)
        def _():
            dg_ref[...] += dg

    return pl.pallas_call(
        body, grid=(T // tr,),
        in_specs=[_rows(tr, V_W), _rows(tr, V_W), _rows(tr, V_W, C_Z // V_W), _whole((1, HEAD_DIM))],
        out_specs=[_rows(tr, V_W), _rows(tr, V_W), _whole((1, HEAD_DIM))],
        out_shape=[SDS((T, V_W), f32), SDS((T, V_W), bf16), SDS((1, HEAD_DIM), f32)],
        name=name, compiler_params=_cp("arbitrary"),
    )(don, o, proj, hg)


def _sgu_parts(ub, vb, gain):
    gv = _gelu(vb)
    r = lax.rsqrt(jnp.mean(gv * gv, axis=-1, keepdims=True) + EPS)
    return _gelu(ub), gv, r, gv * r * gain


def _sgu_fwd(proj, gain, w_s, b_bc, name):
    T = proj.shape[0]
    C = CHUNK_B

    def body(ub_ref, vb_ref, g_ref, w_ref, b_ref, o_ref):
        row, col = _tri_masks(C)
        u, _, _, vn = _sgu_parts(ub_ref[...], vb_ref[...], g_ref[...])
        for g in range(N_GROUPS):
            gs = pl.ds(g * GROUP_DIM, GROUP_DIM)
            wg = jnp.where(row >= col, w_ref[g], 0.0)
            mixed = _dot(_b(wg), _b(vn[:, g * GROUP_DIM:(g + 1) * GROUP_DIM])) + b_ref[g]
            o_ref[:, gs] = (u[:, g * GROUP_DIM:(g + 1) * GROUP_DIM] * mixed).astype(o_ref.dtype)

    return pl.pallas_call(
        body, grid=(T // C,),
        in_specs=[_rows(C, WIDTH_B, C_UB // WIDTH_B), _rows(C, WIDTH_B, C_VB // WIDTH_B), _whole((1, WIDTH_B)),
                  _whole((N_GROUPS, C, C)), _whole((N_GROUPS, C, GROUP_DIM))],
        out_specs=_rows(C, WIDTH_B), out_shape=SDS((T, WIDTH_B), bf16), name=name, compiler_params=_cp("parallel"),
    )(proj, proj, gain, w_s, b_bc)


def _sgu_bwd(dsgu, proj, gain, w_s, b_bc, name):
    T = proj.shape[0]
    C = CHUNK_B

    def body(d_ref, ub_ref, vb_ref, g_ref, w_ref, b_ref, dub_ref, dvb_ref, dw_ref, db_ref, dg_ref):
        first = pl.program_id(0) == 0
        row, col = _tri_masks(C)
        ub, vb, gain_v = ub_ref[...], vb_ref[...], g_ref[...]
        u, gv, r, vn = _sgu_parts(ub, vb, gain_v)
        d = d_ref[...]
        dvn_parts = []
        for g in range(N_GROUPS):
            sl = slice(g * GROUP_DIM, (g + 1) * GROUP_DIM)
            wg = jnp.where(row >= col, w_ref[g], 0.0)
            vng = _b(vn[:, sl])
            mixed = _dot(_b(wg), vng) + b_ref[g]
            dub_ref[:, pl.ds(g * GROUP_DIM, GROUP_DIM)] = (d[:, sl] * mixed * _dgelu(ub[:, sl])).astype(dub_ref.dtype)
            dmix = d[:, sl] * u[:, sl]
            dmb = _b(dmix)
            dwg = jnp.where(row >= col, _dot_nt(dmb, vng), 0.0)
            dbg = jnp.sum(dmix, axis=1, keepdims=True)

            @pl.when(first)
            def _():
                dw_ref[g] = dwg
                db_ref[g] = dbg

            @pl.when(jnp.logical_not(first))
            def _():
                dw_ref[g] += dwg
                db_ref[g] += dbg

            dvn_parts.append(_dot_tn(_b(wg), dmb))
        dvn = jnp.concatenate(dvn_parts, axis=1)
        dy = dvn * gain_v
        m = jnp.mean(dy * gv, axis=-1, keepdims=True)
        dgv = r * dy - gv * (r * r * r * m)
        dvb_ref[...] = (dgv * _dgelu(vb)).astype(dvb_ref.dtype)
        dgain = jnp.sum(dvn * gv * r, axis=0, keepdims=True)

        @pl.when(first)
        def _():
            dg_ref[...] = dgain

        @pl.when(jnp.logical_not(first))
        def _():
            dg_ref[...] += dgain

    return pl.pallas_call(
        body, grid=(T // C,),
        in_specs=[_rows(C, WIDTH_B), _rows(C, WIDTH_B, C_UB // WIDTH_B), _rows(C, WIDTH_B, C_VB // WIDTH_B),
                  _whole((1, WIDTH_B)), _whole((N_GROUPS, C, C)), _whole((N_GROUPS, C, GROUP_DIM))],
        out_specs=[_rows(C, WIDTH_B), _rows(C, WIDTH_B), _whole((N_GROUPS, C, C)), _whole((N_GROUPS, C, 1)),
                   _whole((1, WIDTH_B))],
        out_shape=[SDS((T, WIDTH_B), bf16), SDS((T, WIDTH_B), bf16), SDS((N_GROUPS, C, C), f32),
                   SDS((N_GROUPS, C, 1), f32), SDS((1, WIDTH_B), f32)],
        name=name, compiler_params=_cp("arbitrary"),
    )(dsgu, proj, proj, gain, w_s, b_bc)


def _merge_fwd(proj, ya, yb, name):
    T = proj.shape[0]
    tr = _divisor(T, 256, 16)

    def body(ga_ref, gb_ref, ya_ref, yb_ref, o_ref):
        o_ref[...] = (_sig(ga_ref[...]) * ya_ref[...] + _sig(gb_ref[...]) * yb_ref[...]).astype(o_ref.dtype)

    return pl.pallas_call(
        body, grid=(T // tr,),
        in_specs=[_rows(tr, D_MODEL, C_GA // D_MODEL), _rows(tr, D_MODEL, C_GB // D_MODEL), _rows(tr, D_MODEL), _rows(tr, D_MODEL)],
        out_specs=_rows(tr, D_MODEL), out_shape=SDS((T, D_MODEL), bf16), name=name, compiler_params=_cp("parallel"),
    )(proj, proj, ya, yb)


def _merge_bwd(dm, proj, ya, yb, name):
    T = proj.shape[0]
    tr = _divisor(T, 256, 16)

    def body(dm_ref, ga_ref, gb_ref, ya_ref, yb_ref, dya_ref, dyb_ref, dga_ref, dgb_ref):
        d = dm_ref[...]
        sa, sb = _sig(ga_ref[...]), _sig(gb_ref[...])
        dya_ref[...] = (d * sa).astype(bf16)
        dyb_ref[...] = (d * sb).astype(bf16)
        dga_ref[...] = (d * ya_ref[...] * sa * (1.0 - sa)).astype(bf16)
        dgb_ref[...] = (d * yb_ref[...] * sb * (1.0 - sb)).astype(bf16)

    return pl.pallas_call(
        body, grid=(T // tr,),
        in_specs=[_rows(tr, D_MODEL), _rows(tr, D_MODEL, C_GA // D_MODEL), _rows(tr, D_MODEL, C_GB // D_MODEL),
                  _rows(tr, D_MODEL), _rows(tr, D_MODEL)],
        out_specs=[_rows(tr, D_MODEL)] * 4, out_shape=[SDS((T, D_MODEL), bf16)] * 4, name=name,
        compiler_params=_cp("parallel"),
    )(dm, proj, proj, ya, yb)


def _ffnconv_fwd(upg, upv, wg, wv, bg, bv, name):
    T, F = upg.shape
    R = min(ROW_CHUNK, T)
    n_chunks = T // R

    def body(g_ref, v_ref, wg_ref, wv_ref, bg_ref, bv_ref, o_ref):
        wgv, wvv = _w_rows(wg_ref, 3), _w_rows(wv_ref, 3)

        def chunk(ci, carry):
            r0 = pl.multiple_of(ci * R, R)
            cg = _causal_conv(g_ref[pl.ds(r0, R), :], _prev8(g_ref, r0, ci), wgv, 3) + bg_ref[...]
            cv = _causal_conv(v_ref[pl.ds(r0, R), :], _prev8(v_ref, r0, ci), wvv, 3) + bv_ref[...]
            o_ref[pl.ds(r0, R), :] = (_silu(cg) * cv).astype(o_ref.dtype)
            return carry

        lax.fori_loop(0, n_chunks, chunk, 0)

    col = lambda p: (0, p)
    return pl.pallas_call(
        body, grid=(F // LANES,),
        in_specs=[pl.BlockSpec((T, LANES), col)] * 2 + [pl.BlockSpec((3, LANES), col)] * 2 + [pl.BlockSpec((1, LANES), col)] * 2,
        out_specs=pl.BlockSpec((T, LANES), col), out_shape=SDS((T, F), bf16), name=name, compiler_params=_cp("parallel"),
    )(upg, upv, wg, wv, bg, bv)


def _ffnconv_bwd(dact, upg, upv, wg, wv, bg, bv, name):
    T, F = upg.shape
    R = min(ROW_CHUNK, T)
    n_chunks = T // R

    def body(d_ref, g_ref, v_ref, wg_ref, wv_ref, bg_ref, bv_ref,
             dg_ref, dv_ref, dwg_ref, dwv_ref, dbg_ref, dbv_ref, sg, sv):
        wgv, wvv = _w_rows(wg_ref, 3), _w_rows(wv_ref, 3)

        def phase1(ci, carry):
            dwg, dwv = carry
            r0 = pl.multiple_of(ci * R, R)
            gcur, vcur = g_ref[pl.ds(r0, R), :], v_ref[pl.ds(r0, R), :]
            gp, vp = _prev8(g_ref, r0, ci), _prev8(v_ref, r0, ci)
            gsh = [gcur] + [_shift_down(gcur, gp, s) for s in (1, 2)]
            vsh = [vcur] + [_shift_down(vcur, vp, s) for s in (1, 2)]
            cg = gsh[0] * wgv[2] + gsh[1] * wgv[1] + gsh[2] * wgv[0] + bg_ref[...]
            cv = vsh[0] * wvv[2] + vsh[1] * wvv[1] + vsh[2] * wvv[0] + bv_ref[...]
            d = d_ref[pl.ds(r0, R), :]
            dcv = d * _silu(cg)
            dcg = d * cv * _dsilu(cg)
            sg[pl.ds(r0, R), :] = dcg
            sv[pl.ds(r0, R), :] = dcv
            rg = [jnp.sum(dcg * gsh[2 - j], axis=0, keepdims=True) for j in range(3)] + [jnp.sum(dcg, axis=0, keepdims=True)]
            rv = [jnp.sum(dcv * vsh[2 - j], axis=0, keepdims=True) for j in range(3)] + [jnp.sum(dcv, axis=0, keepdims=True)]
            return tuple(a + b for a, b in zip(dwg, rg)), tuple(a + b for a, b in zip(dwv, rv))

        z4 = tuple(jnp.zeros((1, LANES), f32) for _ in range(4))
        dwg, dwv = lax.fori_loop(0, n_chunks, phase1, (z4, z4))
        for j in range(3):
            dwg_ref[j:j + 1, :] = dwg[j]
            dwv_ref[j:j + 1, :] = dwv[j]
        dbg_ref[...] = dwg[3]
        dbv_ref[...] = dwv[3]

        def phase2(ci, carry):
            r0 = pl.multiple_of(ci * R, R)
            for sc, wv_, out in ((sg, wgv, dg_ref), (sv, wvv, dv_ref)):
                cur = sc[pl.ds(r0, R), :]
                n8 = _next8(sc, r0, R, ci, n_chunks)
                acc = cur * wv_[2] + _shift_up(cur, n8, 1) * wv_[1] + _shift_up(cur, n8, 2) * wv_[0]
                out[pl.ds(r0, R), :] = acc.astype(out.dtype)
            return carry

        lax.fori_loop(0, n_chunks, phase2, 0)

    col = lambda p: (0, p)
    big, w3, b1 = pl.BlockSpec((T, LANES), col), pl.BlockSpec((3, LANES), col), pl.BlockSpec((1, LANES), col)
    return pl.pallas_call(
        body, grid=(F // LANES,), in_specs=[big, big, big, w3, w3, b1, b1], out_specs=[big, big, w3, w3, b1, b1],
        out_shape=[SDS((T, F), bf16), SDS((T, F), bf16), SDS((3, F), f32), SDS((3, F), f32), SDS((1, F), f32), SDS((1, F), f32)],
        scratch_shapes=[pltpu.VMEM((T, LANES), f32), pltpu.VMEM((T, LANES), f32)], name=name, compiler_params=_cp("parallel"),
    )(dact, upg, upv, wg, wv, bg, bv)


def _loss_head(x, gain, target, name):
    T, Dm = x.shape
    tr = _divisor(T, 256, 16)

    def body(x_ref, g_ref, t_ref, l_ref, dx_ref, dg_ref):
        xv, g = x_ref[...], g_ref[...]
        r = lax.rsqrt(jnp.mean(xv * xv, axis=-1, keepdims=True) + EPS)
        err = xv * r * g - t_ref[...]
        part_l = 0.5 * jnp.sum(jnp.mean(err * err, axis=-1, keepdims=True), axis=0, keepdims=True)
        dy = err * (1.0 / Dm)
        dyg = dy * g
        m = jnp.mean(dyg * xv, axis=-1, keepdims=True)
        dx_ref[...] = r * dyg - xv * (r * r * r * m)
        part_g = jnp.sum(dy * xv * r, axis=0, keepdims=True)
        part_l = jnp.broadcast_to(part_l, (1, LANES))

        @pl.when(pl.program_id(0) == 0)
        def _():
            l_ref[...] = part_l
            dg_ref[...] = part_g

        @pl.when(pl.program_id(0) > 0)
        def _():
            l_ref[...] += part_l
            dg_ref[...] += part_g

    return pl.pallas_call(
        body, grid=(T // tr,), in_specs=[_rows(tr, Dm), _whole((1, Dm)), _rows(tr, Dm)],
        out_specs=[_whole((1, LANES)), _rows(tr, Dm), _whole((1, Dm))],
        out_shape=[SDS((1, LANES), f32), SDS((T, Dm), f32), SDS((1, Dm), f32)], name=name, compiler_params=_cp("arbitrary"),
    )(x, gain, target)


def _lane_row(vec, offset):
    return jnp.pad(vec.astype(f32), (offset, LANES - offset - vec.shape[0]))[None]


def _layer_fwd(x, p_i, W, S, li):
    nm = lambda s: f"{s}_l{li}"
    sv = {"x0": x}
    h1 = _rms_fwd(x, S["norm_mix"], nm("rms_mix"))
    proj = _mm(h1, W["in_main"], "nn", nm("proj_main"))
    ba = _mm(h1, W["in_ba"], "nn", nm("proj_ba"))
    qkv = _qkvconv_fwd(proj, S["conv_qkv"], nm("qkvconv"))
    qd, kd, u, w, attn, tinv, cd, _ = _delta_prep(qkv, ba, S["ea_row"], S["dtb_row"], nm("delta_prep"))
    o, s_all, vn = _delta_scan(qd, kd, u, w, attn, cd, nm("delta_scan"))
    on = _onorm_fwd(o, proj, S["head_norm"], nm("onorm"))
    ya = _mm(on, W["branch_a"], "nn", nm("branch_a"))
    sgu = _sgu_fwd(proj, S["sgu_norm"], S["w_spatial"], S["b_bc"], nm("sgu"))
    yb = _mm(sgu, W["branch_b"], "nn", nm("branch_b"))
    merged = _merge_fwd(proj, ya, yb, nm("merge"))
    x1 = _mm(merged, W["out"], "nn", nm("out_proj"), c=x)
    h2 = _rms_fwd(x1, S["norm_ffn"], nm("rms_ffn"))
    upg = _mm(h2, W["up_g"], "nn", nm("ffn_up_g"))
    upv = _mm(h2, W["up_v"], "nn", nm("ffn_up_v"))
    act = _ffnconv_fwd(upg, upv, S["conv_g"], S["conv_v"], S["bias_g"], S["bias_v"], nm("ffnconv"))
    x2 = _mm(act, W["down"], "nn", nm("ffn_down"), c=x1, tk=2816)
    h3 = _rms_fwd(x2, S["norm_ple"], nm("rms_ple"))
    gl = _mm(h3, W["ple_gate"], "nn", nm("ple_gate"))
    pp = _mm(p_i, W["ple_proj"], "nn", nm("ple_proj"))
    (x3,) = _ew(lambda a, g, q: (a + _sig(g) * q,), [x2, gl, pp], [f32], nm("ple_mix"))
    sv.update(h1=h1, proj=proj, ba=ba, qkv=qkv, qd=qd, kd=kd, u=u, w=w, attn=attn, tinv=tinv, cd=cd, o=o, s_all=s_all,
              vn=vn, on=on, ya=ya, sgu=sgu, yb=yb, merged=merged, x1=x1, h2=h2, upg=upg, upv=upv, act=act, x2=x2,
              h3=h3, gl=gl, pp=pp, p=p_i)
    return x3, sv


def _layer_bwd(dx3, sv, W, S, li):
    nm = lambda s: f"{s}_l{li}"
    G = {}
    dgl, dpp = _ew(lambda d, g, q: ((lambda s: (d * q * s * (1.0 - s), d * s))(_sig(g))),
                   [dx3, sv["gl"], sv["pp"]], [bf16, bf16], nm("ple_mix_bwd"))
    G["w_ple_gate"] = _mm(sv["h3"], dgl, "tn", nm("d_ple_gate"))
    G["w_ple_proj"] = _mm(sv["p"], dpp, "tn", nm("d_ple_proj"))
    dh3 = _mm(dgl, W["ple_gate"], "nt", nm("dh_ple"))
    dx2, G["norm_ple"] = _rms_bwd(sv["x2"], S["norm_ple"], dh3, dx3, nm("rms_ple_bwd"))
    dact = _mm(dx2, W["down"], "nt", nm("d_act"))
    G["w_ffn_down"] = _mm(sv["act"], dx2, "tn", nm("d_ffn_down"))
    dupg, dupv, dcg, dcv, dbg, dbv = _ffnconv_bwd(dact, sv["upg"], sv["upv"], S["conv_g"], S["conv_v"], S["bias_g"],
                                                  S["bias_v"], nm("ffnconv_bwd"))
    G["conv_ffn"] = jnp.concatenate([dcg, dcv], axis=1)
    G["b_conv_ffn"] = jnp.concatenate([dbg, dbv], axis=1)
    G["up_g"] = _mm(sv["h2"], dupg, "tn", nm("d_ffn_up_g"))
    G["up_v"] = _mm(sv["h2"], dupv, "tn", nm("d_ffn_up_v"))
    dh2 = _mm(dupg, W["up_g"], "nt", nm("dh_ffn_g"), tk=2816)
    dh2 = _mm(dupv, W["up_v"], "nt", nm("dh_ffn_v"), c=dh2, tk=2816)
    dx1, G["norm_ffn"] = _rms_bwd(sv["x1"], S["norm_ffn"], dh2, dx2, nm("rms_ffn_bwd"))
    dmerged = _mm(dx1, W["out"], "nt", nm("d_merged"))
    G["w_out"] = _mm(sv["merged"], dx1, "tn", nm("d_w_out"))
    dya, dyb, dga, dgb = _merge_bwd(dmerged, sv["proj"], sv["ya"], sv["yb"], nm("merge_bwd"))
    G["w_branch_a"] = _mm(sv["on"], dya, "tn", nm("d_branch_a"))
    G["w_branch_b"] = _mm(sv["sgu"], dyb, "tn", nm("d_branch_b"))
    don = _mm(dya, W["branch_a"], "nt", nm("d_on"))
    dsgu = _mm(dyb, W["branch_b"], "nt", nm("d_sgu"))
    dub, dvb, G["w_spatial"], db_s, G["sgu_norm"] = _sgu_bwd(dsgu, sv["proj"], S["sgu_norm"], S["w_spatial"], S["b_bc"], nm("sgu_bwd"))
    G["b_spatial"] = db_s.reshape(N_GROUPS, CHUNK_B)
    do, dz, G["head_norm"] = _onorm_bwd(don, sv["o"], sv["proj"], S["head_norm"], nm("onorm_bwd"))
    dqd, dkd, du, dw, dattn, dcd = _delta_scan_bwd(do, sv["qd"], sv["kd"], sv["w"], sv["attn"], sv["cd"], sv["s_all"], sv["vn"],
                                                   nm("delta_scan_bwd"))
    dqkv_n, dba, dalog_row, ddtb_row = _delta_prep_bwd(sv["qkv"], sv["ba"], S["ea_row"], S["dtb_row"], sv["tinv"], sv["u"], sv["w"],
                                                       dqd, dkd, du, dw, dattn, dcd, nm("delta_prep_bwd"))
    G["a_log"] = dalog_row[0, N_V:2 * N_V]
    G["dt_bias"] = ddtb_row[0, N_V:2 * N_V]
    dqkv, G["conv_qkv"] = _qkvconv_bwd(sv["proj"], S["conv_qkv"], dqkv_n, nm("qkvconv_bwd"))
    dproj = jnp.concatenate([dqkv, dz, dub, dvb, dga, dgb], axis=1)
    G["in_main"] = _mm(sv["h1"], dproj, "tn", nm("d_in_main"))
    G["in_ba"] = _mm(sv["h1"], dba, "tn", nm("d_in_ba"))
    dh1 = _mm(dba, W["in_ba"], "nt", nm("dh_mix_ba"))
    dh1 = _mm(dproj, W["in_main"], "nt", nm("dh_mix"), c=dh1)
    dx0, G["norm_mix"] = _rms_bwd(sv["x0"], S["norm_mix"], dh1, dx1, nm("rms_mix_bwd"))
    return dx0, G


_BA0 = C_UB
_BA1 = C_UB + 2 * N_V


def _cols(segments, lo, hi):
    out = []
    for start, a in segments:
        s_lo, s_hi = max(lo, start), min(hi, start + a.shape[1])
        if s_lo < s_hi:
            out.append(a[:, s_lo - start:s_hi - start])
    return out


def _big_weights(sh):
    n_in = N_IN // N_CHIPS
    w_in = [(j * n_in, a) for j, a in enumerate(sh["w_in"])]
    rows = lambda k: jnp.concatenate(sh[k], axis=0)
    cols = lambda parts: jnp.concatenate(parts, axis=1)
    ba = jnp.pad(cols(_cols(w_in, _BA0, _BA1)), ((0, 0), (0, LANES - 2 * N_V)))
    return dict(
        in_main=cols(_cols(w_in, 0, _BA0) + _cols(w_in, _BA1, N_IN)), in_ba=ba,
        branch_a=rows("w_branch_a"), branch_b=cols(sh["w_branch_b"]), out=rows("w_out"),
        up_g=cols(sh["w_ffn_up"][:2]), up_v=cols(sh["w_ffn_up"][2:]), down=rows("w_ffn_down"),
        ple_gate=rows("w_ple_gate"), ple_proj=cols(sh["w_ple_proj"]))


def _grads_by_chip(G):
    n_in = N_IN // N_CHIPS
    w_in = [(0, G["in_main"][:, :_BA0]), (_BA0, G["in_ba"][:, :2 * N_V]), (_BA1, G["in_main"][:, _BA0:])]
    split = lambda g: jnp.stack(jnp.split(g, N_CHIPS, axis=1))
    by_rows = lambda g: g.reshape(N_CHIPS, -1, g.shape[1])
    return dict(
        w_in=jnp.stack([jnp.concatenate(_cols(w_in, j * n_in, (j + 1) * n_in), axis=1) for j in range(N_CHIPS)]),
        w_branch_a=by_rows(G["w_branch_a"]), w_branch_b=split(G["w_branch_b"]), w_out=by_rows(G["w_out"]),
        w_ffn_up=jnp.stack(jnp.split(G["up_g"], 2, axis=1) + jnp.split(G["up_v"], 2, axis=1)),
        w_ffn_down=by_rows(G["w_ffn_down"]), w_ple_gate=by_rows(G["w_ple_gate"]), w_ple_proj=split(G["w_ple_proj"]))


def _small_params(sm, i):
    return dict(
        norm_mix=sm["norm_mix"][i][None], conv_qkv=sm["conv_qkv"][i], ea_row=_lane_row(jnp.exp(sm["a_log"][i]), N_V),
        dtb_row=_lane_row(sm["dt_bias"][i], N_V), head_norm=sm["head_norm"][i][None], sgu_norm=sm["sgu_norm"][i][None],
        w_spatial=sm["w_spatial"][i],
        b_bc=jnp.broadcast_to(sm["b_spatial"][i][:, :, None], (N_GROUPS, CHUNK_B, GROUP_DIM)),
        norm_ffn=sm["norm_ffn"][i][None], conv_g=sm["conv_ffn"][i][:, :D_FF], conv_v=sm["conv_ffn"][i][:, D_FF:],
        bias_g=sm["b_conv_ffn"][i][None, :D_FF], bias_v=sm["b_conv_ffn"][i][None, D_FF:], norm_ple=sm["norm_ple"][i][None])


MESH = pl.DeviceIdType.MESH
N_CHIPS = 4
ANY = pl.BlockSpec(memory_space=pl.ANY)
ROW_ALIGN = 32


def _place():
    x, y, c = lax.axis_index("x"), lax.axis_index("y"), lax.axis_index("c")
    chips = [(1 - x, y), (x, 1 - y), (1 - x, 1 - y)]
    return x, y, c, 2 * x + y, chips, (x, y, 1 - c)


def _halves(rows, c):
    h = rows // 2
    return pl.ds(pl.multiple_of(c * h, 16), h), pl.ds(pl.multiple_of((1 - c) * h, 16), h)


def _remote(src, dst, send_sems, recv_sems, k, dev):
    return pltpu.make_async_remote_copy(src_ref=src, dst_ref=dst, send_sem=send_sems.at[k], recv_sem=recv_sems.at[k],
                                        device_id=dev, device_id_type=MESH)


def _gather_shards(bufs, name):
    n = len(bufs)

    def body(*refs):
        ins, outs = refs[:n], refs[n:2 * n]
        send_sems, recv_sems = refs[2 * n:]
        x, y, c, me, chips, sib = _place()
        for i in range(n):
            mine, _ = _halves(ins[i].shape[0], c)
            for k, (px, py) in enumerate(chips):
                _remote(ins[i].at[mine], outs[i].at[me, mine], send_sems, recv_sems, 6 * i + k, (px, py, c)).start()
        for i in range(n):
            mine, _ = _halves(ins[i].shape[0], c)
            for k, (px, py) in enumerate(chips):
                pc = 2 * px + py
                _remote(ins[i].at[mine], outs[i].at[pc, mine], send_sems, recv_sems, 6 * i + k, (px, py, c)).wait_recv()
                _remote(outs[i].at[pc, mine], outs[i].at[pc, mine], send_sems, recv_sems, 6 * i + 3 + k, sib).start()
        for i in range(n):
            mine, other = _halves(ins[i].shape[0], c)
            for k, (px, py) in enumerate(chips):
                pc = 2 * px + py
                _remote(outs[i].at[pc, mine], outs[i].at[pc, other], send_sems, recv_sems, 6 * i + 3 + k, sib).wait_recv()
        for i in range(n):
            mine, _ = _halves(ins[i].shape[0], c)
            for k, (px, py) in enumerate(chips):
                pc = 2 * px + py
                _remote(ins[i].at[mine], outs[i].at[me, mine], send_sems, recv_sems, 6 * i + k, (px, py, c)).wait_send()
                _remote(outs[i].at[pc, mine], outs[i].at[pc, mine], send_sems, recv_sems, 6 * i + 3 + k, sib).wait_send()

    return pl.pallas_call(
        body, in_specs=[ANY] * n, out_specs=[ANY] * n,
        out_shape=[SDS((N_CHIPS,) + b.shape, b.dtype) for b in bufs],
        scratch_shapes=[pltpu.SemaphoreType.DMA((6 * n,)), pltpu.SemaphoreType.DMA((6 * n,))],
        name=name,
    )(*bufs)


def _exchange(srcs_of, out_shapes, n_sems, name):
    n = len(out_shapes)

    def body(*refs):
        n_in = len(refs) - n - 2
        ins, outs = refs[:n_in], refs[n_in:n_in + n]
        send_sems, recv_sems = refs[-2:]
        copies = [_remote(s, d, send_sems, recv_sems, k, dev) for s, d, k, dev in srcs_of(ins, outs, _place())]
        for cp in copies:
            cp.start()
        for cp in copies:
            cp.wait()

    def call(*arrs):
        return pl.pallas_call(
            body, in_specs=[ANY] * len(arrs), out_specs=[ANY] * n, out_shape=out_shapes,
            scratch_shapes=[pltpu.SemaphoreType.DMA((n_sems,)), pltpu.SemaphoreType.DMA((n_sems,))], name=name,
        )(*arrs)

    return call


HBM = pl.BlockSpec(memory_space=pltpu.HBM)
SEM = pl.BlockSpec(memory_space=pltpu.SEMAPHORE)
_SIDE_EFFECT = pltpu.SideEffectType.DATAFLOW_SIDE_EFFECTING


def _exchange_start(copies_of, srcs, land_shapes, n_sems, name):
    ns, nl = len(srcs), len(land_shapes)

    def body(*refs):
        ins = refs[:ns + nl]
        send_sems, recv_sems = refs[ns + nl], refs[ns + nl + 1]
        token = refs[-1]
        for s, d, k, dev in copies_of(ins[:ns], ins[ns:], _place()):
            _remote(s, d, send_sems, recv_sems, k, dev).start()
        token[...] = jnp.zeros_like(token)

    lands = [lax.empty(s.shape, s.dtype) for s in land_shapes]
    operands = [pltpu.with_memory_space_constraint(a, pltpu.HBM) for a in list(srcs) + lands]
    outs = pl.pallas_call(
        body, name=name,
        out_shape=(pltpu.SemaphoreType.DMA((n_sems,)), pltpu.SemaphoreType.DMA((n_sems,)),
                   *[pltpu.HBM(a.shape, a.dtype) for a in operands], SDS((8, LANES), f32)),
        in_specs=[HBM] * (ns + nl), out_specs=(SEM, SEM, *[HBM] * (ns + nl), pl.BlockSpec(memory_space=pltpu.VMEM)),
        input_output_aliases={i: 2 + i for i in range(ns + nl)},
        compiler_params=pltpu.CompilerParams(has_side_effects=_SIDE_EFFECT),
    )(*operands)
    return dict(send=outs[0], recv=outs[1], bufs=list(outs[2:2 + ns + nl]), token=outs[-1], ns=ns, copies_of=copies_of)


def _exchange_wait(handle, after, name):
    ns, bufs, copies_of = handle["ns"], handle["bufs"], handle["copies_of"]
    nb = len(bufs)

    def body(*refs):
        ins = refs[:nb]
        send_sems, recv_sems = refs[nb], refs[nb + 1]
        for s, d, k, dev in copies_of(ins[:ns], ins[ns:], _place()):
            cp = _remote(s, d, send_sems, recv_sems, k, dev)
            cp.wait_send()
            cp.wait_recv()

    outs = pl.pallas_call(
        body, name=name, out_shape=tuple(pltpu.HBM(a.shape, a.dtype) for a in bufs),
        in_specs=[HBM] * nb + [SEM, SEM, ANY], out_specs=tuple([HBM] * nb),
        input_output_aliases={i: i for i in range(nb)},
        compiler_params=pltpu.CompilerParams(has_side_effects=_SIDE_EFFECT),
    )(*bufs, handle["send"], handle["recv"], after)
    return list(outs[:ns]), list(outs[ns:])


def _pick(idx, parts):
    out = parts[-1]
    for j in reversed(range(len(parts) - 1)):
        out = jnp.where(idx == j, parts[j], out)
    return out


def _rs_pair(gs, name):
    def copies(ins, outs, place):
        x, y, c, me, chips, sib = place
        return [(g.at[:, _halves(g.shape[1], c)[1]], o, i, sib) for i, (g, o) in enumerate(zip(ins, outs))]

    shapes = [SDS((N_CHIPS, g.shape[1] // 2, g.shape[2]), g.dtype) for g in gs]
    return _exchange(copies, shapes, len(gs), name)(*gs)


def _gather_copies(ins, outs, place):
    x, y, c, me, chips, sib = place
    out = []
    for i, (s, o) in enumerate(zip(ins, outs)):
        mine, _ = _halves(s.shape[0], c)
        out += [(s.at[mine], o.at[me, mine], 3 * i + k, (px, py, c)) for k, (px, py) in enumerate(chips)]
    return out


def _gather_start(shards, name):
    return _exchange_start(_gather_copies, shards, [SDS((N_CHIPS,) + s.shape, s.dtype) for s in shards], 3 * len(shards), name)


def _gather_forward(lands, name):
    n = len(lands)

    def body(*refs):
        ins, outs = refs[:n], refs[n:2 * n]
        send_sems, recv_sems = refs[2 * n:]
        x, y, c, me, chips, sib = _place()
        copies = []
        for i in range(n):
            mine, _ = _halves(ins[i].shape[1], c)
            for k, (px, py) in enumerate(chips):
                pc = 2 * px + py
                copies.append(_remote(ins[i].at[pc, mine], outs[i].at[pc, mine], send_sems, recv_sems, 3 * i + k, sib))
        for cp in copies:
            cp.start()
        for cp in copies:
            cp.wait()

    return pl.pallas_call(
        body, in_specs=[ANY] * n, out_specs=[ANY] * n, out_shape=[SDS(a.shape, a.dtype) for a in lands],
        input_output_aliases={i: i for i in range(n)},
        scratch_shapes=[pltpu.SemaphoreType.DMA((3 * n,)), pltpu.SemaphoreType.DMA((3 * n,))], name=name,
    )(*lands)


def _rs_chips_copies(ins, outs, place):
    x, y, c, me, chips, sib = place
    return [(p.at[2 * px + py], o.at[k], 3 * i + k, (px, py, c))
            for i, (p, o) in enumerate(zip(ins, outs)) for k, (px, py) in enumerate(chips)]


def _rs_tile(h, n_cols):
    return _divisor(h, max(16, (1 << 20) // (4 * n_cols)), 16)


def _rs_add2(g, got, name):
    _, K, Nc = g.shape
    h = K // 2
    tr = _rs_tile(h, Nc)
    nb = h // tr

    def body(g_ref, got_ref, o_ref):
        o_ref[...] = (g_ref[...] + got_ref[...]).astype(o_ref.dtype)

    blk = lambda j, i: (j, i, 0)
    return pl.pallas_call(
        body, name=name, out_shape=SDS((N_CHIPS, h, Nc), bf16), grid=(N_CHIPS, nb),
        in_specs=[pl.BlockSpec((1, tr, Nc), lambda j, i: (j, lax.axis_index("c") * nb + i, 0)), pl.BlockSpec((1, tr, Nc), blk)],
        out_specs=pl.BlockSpec((1, tr, Nc), blk), compiler_params=_cp("parallel", "parallel"),
    )(g, got)


def _rs_add4(p, parts, name):
    _, h, Nc = p.shape
    tr = _rs_tile(h, Nc)
    nb = h // tr

    def body(own_ref, a_ref, b_ref, c_ref, o_ref):
        up = lambda r: r[0].astype(f32)
        o_ref[...] = (up(own_ref) + up(b_ref)) + (up(a_ref) + up(c_ref))

    part = lambda k: pl.BlockSpec((1, tr, Nc), lambda i, k=k: (k, i, 0))
    me = lambda: 2 * lax.axis_index("x") + lax.axis_index("y")
    return pl.pallas_call(
        body, name=name, out_shape=SDS((2 * h, Nc), f32), grid=(nb,),
        in_specs=[pl.BlockSpec((1, tr, Nc), lambda i: (me(), i, 0)), part(0), part(1), part(2)],
        out_specs=pl.BlockSpec((tr, Nc), lambda i: (lax.axis_index("c") * nb + i, 0)), compiler_params=_cp("parallel"),
    )(p, parts, parts, parts)


def _rs_join(fs, name):
    n = len(fs)

    def body(*refs):
        ins, outs = refs[:n], refs[n:2 * n]
        send_sems, recv_sems = refs[2 * n:]
        x, y, c, me, chips, sib = _place()
        copies = []
        for i in range(n):
            mine, _ = _halves(ins[i].shape[0], c)
            copies.append(_remote(ins[i].at[mine], outs[i].at[mine], send_sems, recv_sems, i, sib))
        for cp in copies:
            cp.start()
        for cp in copies:
            cp.wait()

    return pl.pallas_call(
        body, in_specs=[ANY] * n, out_specs=[ANY] * n, out_shape=[SDS(a.shape, a.dtype) for a in fs],
        input_output_aliases={i: i for i in range(n)},
        scratch_shapes=[pltpu.SemaphoreType.DMA((n,)), pltpu.SemaphoreType.DMA((n,))], name=name,
    )(*fs)


def _rs_begin(gs, names, tag):
    gots = _rs_pair(gs, f"rs_pair_{tag}")
    ps = [_rs_add2(g, got, f"rs_add2_{k}_{tag}") for g, got, k in zip(gs, gots, names)]
    lands = [SDS((3,) + p.shape[1:], p.dtype) for p in ps]
    return _exchange_start(_rs_chips_copies, ps, lands, 3 * len(ps), f"rs_chips_start_{tag}")


def _rs_end(handle, after, names, tag):
    ps, parts = _exchange_wait(handle, after, f"rs_chips_wait_{tag}")
    fs = [_rs_add4(p, part, f"rs_add4_{k}_{tag}") for p, part, k in zip(ps, parts, names)]
    return _rs_join(fs, f"rs_join_{tag}")


def _allreduce_small(buf, name):
    R, L = buf.shape

    def body(x_ref, o_ref, r0, s1, r1, send_sems, recv_sems):
        x, y, c, me, chips, sib = _place()
        cp = _remote(x_ref, r0, send_sems, recv_sems, 0, sib)
        cp.start()
        cp.wait()
        s1[...] = x_ref[...] + r0[...]
        cps = []
        for k, (px, py) in enumerate(chips):
            cp = _remote(s1, r1.at[k], send_sems, recv_sems, 1 + k, (px, py, c))
            cp.start()
            cps.append(cp)
        for cp in cps:
            cp.wait()
        o_ref[...] = (s1[...] + r1[1]) + (r1[0] + r1[2])

    vm = pl.BlockSpec(memory_space=pltpu.VMEM)
    return pl.pallas_call(
        body, in_specs=[vm], out_specs=vm, out_shape=SDS((R, L), f32),
        scratch_shapes=[pltpu.VMEM((R, L), f32), pltpu.VMEM((R, L), f32), pltpu.VMEM((3, R, L), f32),
                        pltpu.SemaphoreType.DMA((4,)), pltpu.SemaphoreType.DMA((4,))],
        name=name, compiler_params=pltpu.CompilerParams(vmem_limit_bytes=VMEM_LIMIT_BYTES),
    )(buf)


BIG = ("w_in", "w_branch_a", "w_branch_b", "w_out", "w_ffn_up", "w_ffn_down", "w_ple_gate", "w_ple_proj")
SMALL_REPL = ("norm_mix", "a_log", "dt_bias", "head_norm", "sgu_norm", "w_spatial", "b_spatial", "norm_ffn", "b_conv_ffn",
              "norm_ple", "norm_final")
SMALL_COLS = ("conv_qkv", "conv_ffn")
WEIGHTS = ("norm_mix", "w_in", "conv_qkv", "a_log", "dt_bias", "head_norm", "sgu_norm", "w_spatial", "b_spatial", "w_branch_a",
           "w_branch_b", "w_out", "norm_ffn", "w_ffn_up", "conv_ffn", "b_conv_ffn", "w_ffn_down", "norm_ple", "w_ple_gate",
           "w_ple_proj", "norm_final")


def _flat(arrs, dtype):
    cat = jnp.concatenate([a.astype(dtype).reshape(-1) for a in arrs])
    unit = LANES * ROW_ALIGN
    cat = jnp.pad(cat, (0, -cat.shape[0] % unit))
    return cat.reshape(-1, LANES)


def _unflat(buf, shapes):
    flat = buf.reshape(-1)
    out, off = [], 0
    for s in shapes:
        n = math.prod(s)
        out.append(flat[off:off + n].reshape(s))
        off += n
    return out


def _adamw(w, g, m, v):
    m2 = ADAM_B1 * m + (1.0 - ADAM_B1) * g
    v2 = ADAM_B2 * v + (1.0 - ADAM_B2) * (g * g)
    m_hat = m2 * (1.0 / (1.0 - ADAM_B1 ** ADAM_STEP))
    v_hat = v2 * (1.0 / (1.0 - ADAM_B2 ** ADAM_STEP))
    delta = -ADAM_LR * (m_hat / (jnp.sqrt(v_hat) + ADAM_EPS) + ADAM_WD * w)
    return delta, m2, v2


def kernel(x, p, norm_mix, w_in, conv_qkv, a_log, dt_bias, head_norm, sgu_norm, w_spatial, b_spatial, w_branch_a, w_branch_b, w_out, norm_ffn, w_ffn_up, conv_ffn, b_conv_ffn, w_ffn_down, norm_ple, w_ple_gate, w_ple_proj, norm_final, loss_target, m_norm_mix, m_w_in, m_conv_qkv, m_a_log, m_dt_bias, m_head_norm, m_sgu_norm, m_w_spatial, m_b_spatial, m_w_branch_a, m_w_branch_b, m_w_out, m_norm_ffn, m_w_ffn_up, m_conv_ffn, m_b_conv_ffn, m_w_ffn_down, m_norm_ple, m_w_ple_gate, m_w_ple_proj, m_norm_final, v_norm_mix, v_w_in, v_conv_qkv, v_a_log, v_dt_bias, v_head_norm, v_sgu_norm, v_w_spatial, v_b_spatial, v_w_branch_a, v_w_branch_b, v_w_out, v_norm_ffn, v_w_ffn_up, v_conv_ffn, v_b_conv_ffn, v_w_ffn_down, v_norm_ple, v_w_ple_gate, v_w_ple_proj, v_norm_final):
    w = dict(norm_mix=norm_mix, w_in=w_in, conv_qkv=conv_qkv, a_log=a_log, dt_bias=dt_bias, head_norm=head_norm, sgu_norm=sgu_norm,
             w_spatial=w_spatial, b_spatial=b_spatial, w_branch_a=w_branch_a, w_branch_b=w_branch_b, w_out=w_out, norm_ffn=norm_ffn,
             w_ffn_up=w_ffn_up, conv_ffn=conv_ffn, b_conv_ffn=b_conv_ffn, w_ffn_down=w_ffn_down, norm_ple=norm_ple,
             w_ple_gate=w_ple_gate, w_ple_proj=w_ple_proj, norm_final=norm_final)
    m = dict(norm_mix=m_norm_mix, w_in=m_w_in, conv_qkv=m_conv_qkv, a_log=m_a_log, dt_bias=m_dt_bias, head_norm=m_head_norm,
             sgu_norm=m_sgu_norm, w_spatial=m_w_spatial, b_spatial=m_b_spatial, w_branch_a=m_w_branch_a, w_branch_b=m_w_branch_b,
             w_out=m_w_out, norm_ffn=m_norm_ffn, w_ffn_up=m_w_ffn_up, conv_ffn=m_conv_ffn, b_conv_ffn=m_b_conv_ffn,
             w_ffn_down=m_w_ffn_down, norm_ple=m_norm_ple, w_ple_gate=m_w_ple_gate, w_ple_proj=m_w_ple_proj, norm_final=m_norm_final)
    v = dict(norm_mix=v_norm_mix, w_in=v_w_in, conv_qkv=v_conv_qkv, a_log=v_a_log, dt_bias=v_dt_bias, head_norm=v_head_norm,
             sgu_norm=v_sgu_norm, w_spatial=v_w_spatial, b_spatial=v_b_spatial, w_branch_a=v_w_branch_a, w_branch_b=v_w_branch_b,
             w_out=v_w_out, norm_ffn=v_norm_ffn, w_ffn_up=v_w_ffn_up, conv_ffn=v_conv_ffn, b_conv_ffn=v_b_conv_ffn,
             w_ffn_down=v_w_ffn_down, norm_ple=v_norm_ple, w_ple_gate=v_w_ple_gate, w_ple_proj=v_w_ple_proj, norm_final=v_norm_final)
    chip = 2 * lax.axis_index("x") + lax.axis_index("y")
    core = lax.axis_index("c")

    def by_chip(own, gathered):
        return [jnp.where(chip == j, own, gathered[j]) for j in range(N_CHIPS)]

    conv_buf = _flat([w[k] for k in SMALL_COLS], f32)
    (conv_all,) = _gather_shards([conv_buf], "gather_conv")
    sm = {k: w[k] for k in SMALL_REPL}
    conv_parts = [_unflat(b, [w[k].shape for k in SMALL_COLS]) for b in by_chip(conv_buf, conv_all)]
    for n, k in enumerate(SMALL_COLS):
        sm[k] = jnp.concatenate([conv_parts[j][n] for j in range(N_CHIPS)], axis=-1)
    Ss = [_small_params(sm, i) for i in range(DEPTH)]
    def start_gather(i, behind):
        own = [w[k][i].astype(bf16) for k in BIG]
        own[-1] = own[-1] + behind.astype(bf16)
        return _gather_start(own, f"gather_start_l{i}")

    h = x[0]
    saved, Ws = [], []
    gather = start_gather(0, jnp.zeros((), f32))
    for i in range(DEPTH):
        own, lands = _exchange_wait(gather, gather["token"] if i == 0 else h, f"gather_wait_l{i}")
        S_i = Ss[i]
        if i + 1 < DEPTH:
            gather = start_gather(i + 1, own[-1][0, 0] * 0)
            S_i = dict(S_i, norm_mix=S_i["norm_mix"] + gather["token"][0, 0])
        lands = _gather_forward(lands, f"gather_forward_l{i}")
        Ws.append(_big_weights({k: by_chip(o, g) for k, o, g in zip(BIG, own, lands)}))
        h, sv = _layer_fwd(h, p[i, 0], Ws[i], S_i, i)
        saved.append(sv)
    loss_row, dh, g_final = _loss_head(h, sm["norm_final"][None], loss_target[0], "loss_head")

    gsm = {k: [None] * DEPTH for k in SMALL_REPL + SMALL_COLS if k != "norm_final"}
    gbig = {k: [None] * DEPTH for k in BIG}
    pending = None

    def finish(after):
        j, handle = pending
        for k, g in zip(BIG, _rs_end(handle, after, BIG, f"l{j}")):
            gbig[k][j] = g

    for i in reversed(range(DEPTH)):
        S_i = Ss[i] if pending is None else dict(Ss[i], norm_ple=Ss[i]["norm_ple"] + pending[1]["token"][0, 0])
        dh, G = _layer_bwd(dh, saved[i], Ws[i], S_i, i)
        if pending is not None:
            finish(dh)
        parts = _grads_by_chip(G)
        pending = (i, _rs_begin([parts[k] for k in BIG], BIG, f"l{i}"))
        for k in gsm:
            gsm[k][i] = G[k].reshape(w[k].shape[1:-1] + (-1,)) if k in SMALL_COLS else G[k].reshape(w[k].shape[1:])

    small_names = [k for k in SMALL_REPL + SMALL_COLS if k != "norm_final"]
    small_local = [jnp.stack(gsm[k]) for k in small_names] + [g_final.reshape(-1), loss_row[0, :1]]
    small_shapes = [a.shape for a in small_local]
    small_buf = _allreduce_small(_flat(small_local, f32), "allreduce_small")
    finish(small_buf)
    small_sum = _unflat(small_buf, small_shapes)
    gs = dict(zip(small_names + ["norm_final"], small_sum[:-1]))
    loss = small_sum[-1][0]
    for k in SMALL_COLS:
        n = w[k].shape[-1]
        gs[k] = _pick(chip, [gs[k][..., j * n:(j + 1) * n] for j in range(N_CHIPS)])

    grads, deltas, new_m, new_v = {}, {}, {}, {}
    for k in BIG:
        g = jnp.stack(gbig[k])
        two = lambda a: a.reshape(-1, a.shape[-1])
        d, m2, v2 = _ew(_adamw, [two(w[k]), two(g), two(m[k]), two(v[k])], [f32, f32, f32], f"adamw_{k}")
        grads[k], deltas[k], new_m[k], new_v[k] = g, d.reshape(g.shape), m2.reshape(g.shape), v2.reshape(g.shape)
    small_all = [k for k in WEIGHTS if k not in BIG]
    shapes = [w[k].shape for k in small_all]
    d, m2, v2 = _ew(_adamw, [_flat([w[k] for k in small_all], f32), _flat([gs[k] for k in small_all], f32),
                             _flat([m[k] for k in small_all], f32), _flat([v[k] for k in small_all], f32)],
                    [f32, f32, f32], "adamw_small")
    for k, a, b, c_ in zip(small_all, _unflat(d, shapes), _unflat(m2, shapes), _unflat(v2, shapes)):
        grads[k], deltas[k], new_m[k], new_v[k] = gs[k], a, b, c_
    return (loss, dh[None], *[grads[k] for k in WEIGHTS], *[deltas[k] for k in WEIGHTS],
            *[new_m[k] for k in WEIGHTS], *[new_v[k] for k in WEIGHTS])
```

```python
import functools
import math

import jax
import jax.numpy as jnp
from jax import lax
from jax.experimental import pallas as pl
from jax.experimental.pallas import tpu as pltpu

f32 = jnp.float32
bf16 = jnp.bfloat16
HI = lax.Precision.HIGHEST
SDS = jax.ShapeDtypeStruct

D_MODEL = 2048
DEPTH = 4
HEAD_DIM = 128
N_QK = 8
N_V = 16
QK_W = N_QK * HEAD_DIM
V_W = N_V * HEAD_DIM
CHUNK_A = 64
N_GROUPS = 8
GROUP_DIM = 128
WIDTH_B = N_GROUPS * GROUP_DIM
CHUNK_B = 128
D_FF = 5632
PLE_DIM = 256
EPS = 1e-6
N_IN = 12320
ADAM_LR, ADAM_B1, ADAM_B2, ADAM_EPS, ADAM_WD, ADAM_STEP = 0.001, 0.9, 0.999, 1e-08, 0.01, 10

C_Q, C_K, C_V, C_Z, C_UB, C_VB, C_GA, C_GB, PM = 0, 1024, 2048, 4096, 6144, 7168, 8192, 10240, 12288
LANES = 128
VMEM_LIMIT_BYTES = 48 * 1024 * 1024
ROW_CHUNK = 256


def _cp(*sem):
    return pltpu.CompilerParams(dimension_semantics=sem if sem else None, vmem_limit_bytes=VMEM_LIMIT_BYTES)


def _dot(a, b, prec=None):
    return jnp.dot(a, b, preferred_element_type=f32, precision=prec)


def _dot_nt(a, b, prec=None):
    return lax.dot_general(a, b, (((1,), (1,)), ((), ())), preferred_element_type=f32, precision=prec)


def _dot_tn(a, b, prec=None):
    return lax.dot_general(a, b, (((0,), (0,)), ((), ())), preferred_element_type=f32, precision=prec)


def _b(x):
    return x.astype(bf16)


def _dot3(a, b, dims=(((1,), (0,)), ((), ()))):
    ah, bh = _b(a), _b(b)
    al, bl = _b(a - ah.astype(f32)), _b(b - bh.astype(f32))
    dg = lambda p, q: lax.dot_general(p, q, dims, preferred_element_type=f32)
    return dg(ah, bh) + (dg(ah, bl) + dg(al, bh))


_TN = (((0,), (0,)), ((), ()))


def _sig(x):
    return 1.0 / (1.0 + jnp.exp(-x))


def _silu(x):
    return x * _sig(x)


def _dsilu(x):
    s = _sig(x)
    return s * (1.0 + x * (1.0 - s))


_GELU_C = 0.7978845608028654
_GELU_A = 0.044715


def _gelu(x):
    return 0.5 * x * (1.0 + jnp.tanh(_GELU_C * (x + _GELU_A * x * x * x)))


def _dgelu(x):
    t = jnp.tanh(_GELU_C * (x + _GELU_A * x * x * x))
    return 0.5 * (1.0 + t) + 0.5 * x * (1.0 - t * t) * _GELU_C * (1.0 + 3.0 * _GELU_A * x * x)


def _softplus(x):
    return jnp.maximum(x, 0.0) + jnp.log(1.0 + jnp.exp(-jnp.abs(x)))


def _divisor(n, cap, mult):
    best = None
    d = mult
    while d <= min(n, cap):
        if n % d == 0:
            best = d
        d += mult
    return best if best is not None else n


def _rows(tr, c, j=0):
    return pl.BlockSpec((tr, c), lambda i, j=j: (i, j))


def _whole(shape):
    nd = len(shape)
    return pl.BlockSpec(shape, lambda *_: (0,) * nd)


def _mm(a, b, mode, name, out_dtype=f32, c=None, tm=1024, tn=1024, tk=2048):
    if mode == "nn":
        (M, K), (K2, N) = a.shape, b.shape
    elif mode == "nt":
        (M, K), (N, K2) = a.shape, b.shape
    else:
        (K, M), (K2, N) = a.shape, b.shape
    assert K == K2, (a.shape, b.shape, mode)
    tm = _divisor(M, tm, LANES if mode == "tn" else 16)
    tn = _divisor(N, tn, LANES)
    tk = _divisor(K, tk, LANES if mode != "tn" else 16)
    nk = K // tk
    if mode == "nn":
        a_spec = pl.BlockSpec((tm, tk), lambda i, j, k: (i, k))
        b_spec = pl.BlockSpec((tk, tn), lambda i, j, k: (k, j))
        dn = (((1,), (0,)), ((), ()))
    elif mode == "nt":
        a_spec = pl.BlockSpec((tm, tk), lambda i, j, k: (i, k))
        b_spec = pl.BlockSpec((tn, tk), lambda i, j, k: (j, k))
        dn = (((1,), (1,)), ((), ()))
    else:
        a_spec = pl.BlockSpec((tk, tm), lambda i, j, k: (k, i))
        b_spec = pl.BlockSpec((tk, tn), lambda i, j, k: (k, j))
        dn = (((0,), (0,)), ((), ()))
    o_spec = pl.BlockSpec((tm, tn), lambda i, j, k: (i, j))
    has_c = c is not None

    def body(*refs):
        a_ref, b_ref = refs[0], refs[1]
        c_ref = refs[2] if has_c else None
        o_ref = refs[3] if has_c else refs[2]
        part = lax.dot_general(_b(a_ref[...]), _b(b_ref[...]), dn, preferred_element_type=f32)
        if nk == 1:
            if has_c:
                part = part + c_ref[...]
            o_ref[...] = part.astype(o_ref.dtype)
        else:
            acc = refs[-1]
            k = pl.program_id(2)

            @pl.when(k == 0)
            def _():
                acc[...] = part

            @pl.when(k > 0)
            def _():
                acc[...] += part

            @pl.when(k == nk - 1)
            def _():
                r = acc[...]
                if has_c:
                    r = r + c_ref[...]
                o_ref[...] = r.astype(o_ref.dtype)

    ins = [a, b] + ([c] if has_c else [])
    in_specs = [a_spec, b_spec] + ([o_spec] if has_c else [])
    return pl.pallas_call(
        body, grid=(M // tm, N // tn, nk), in_specs=in_specs, out_specs=o_spec, out_shape=SDS((M, N), out_dtype),
        scratch_shapes=[pltpu.VMEM((tm, tn), f32)] if nk > 1 else [], name=name,
        compiler_params=_cp("parallel", "parallel", "arbitrary"),
    )(*ins)


def _ew(fn, ins, out_dtypes, name, tile_bytes=2 * 1024 * 1024):
    R, C = max((x.shape for x in ins), key=lambda s: s[0])
    n_in = len(ins)
    row_bytes = 4 * C * (len(ins) + len(out_dtypes))
    tr = _divisor(R, max(16, tile_bytes // row_bytes), 16)
    in_specs = [_rows(tr, C) if x.shape[0] == R else _whole((1, C)) for x in ins]

    def body(*refs):
        res = fn(*[r[...] for r in refs[:n_in]])
        for o_ref, r in zip(refs[n_in:], res):
            o_ref[...] = r.astype(o_ref.dtype)

    outs = pl.pallas_call(
        body, grid=(R // tr,), in_specs=in_specs, out_specs=[_rows(tr, C) for _ in out_dtypes],
        out_shape=[SDS((R, C), dt) for dt in out_dtypes], name=name, compiler_params=_cp("parallel"),
    )(*ins)
    return outs


def _rms_fwd(x, gain, name):
    T, Dm = x.shape
    tr = _divisor(T, 256, 16)

    def body(x_ref, g_ref, o_ref):
        xv = x_ref[...]
        r = lax.rsqrt(jnp.mean(xv * xv, axis=-1, keepdims=True) + EPS)
        o_ref[...] = (xv * r * g_ref[...]).astype(o_ref.dtype)

    return pl.pallas_call(
        body, grid=(T // tr,), in_specs=[_rows(tr, Dm), _whole((1, Dm))], out_specs=_rows(tr, Dm),
        out_shape=SDS((T, Dm), bf16), name=name, compiler_params=_cp("parallel"),
    )(x, gain)


def _rms_bwd(x, gain, dh, dres, name):
    T, Dm = x.shape
    tr = _divisor(T, 256, 16)

    def body(x_ref, g_ref, dh_ref, dres_ref, dx_ref, dg_ref):
        xv = x_ref[...]
        dhv = dh_ref[...]
        r = lax.rsqrt(jnp.mean(xv * xv, axis=-1, keepdims=True) + EPS)
        dy = dhv * g_ref[...]
        m = jnp.mean(dy * xv, axis=-1, keepdims=True)
        dx_ref[...] = dres_ref[...] + r * dy - xv * (r * r * r * m)
        part = jnp.sum(dhv * xv * r, axis=0, keepdims=True)

        @pl.when(pl.program_id(0) == 0)
        def _():
            dg_ref[...] = part

        @pl.when(pl.program_id(0) > 0)
        def _():
            dg_ref[...] += part

    return pl.pallas_call(
        body, grid=(T // tr,), in_specs=[_rows(tr, Dm), _whole((1, Dm)), _rows(tr, Dm), _rows(tr, Dm)],
        out_specs=[_rows(tr, Dm), _whole((1, Dm))], out_shape=[SDS((T, Dm), f32), SDS((1, Dm), f32)],
        name=name, compiler_params=_cp("arbitrary"),
    )(x, gain, dh, dres)


def _shift_down(cur, prev8, s):
    rolled = pltpu.roll(cur, s, 0)
    rp = pltpu.roll(prev8, s, 0)
    row8 = lax.broadcasted_iota(jnp.int32, prev8.shape, 0)
    first = jnp.where(row8 < s, rp, rolled[:8])
    return jnp.concatenate([first, rolled[8:]], axis=0)


def _shift_up(cur, next8, s):
    R = cur.shape[0]
    rolled = pltpu.roll(cur, R - s, 0)
    rn = pltpu.roll(next8, 8 - s, 0)
    row8 = lax.broadcasted_iota(jnp.int32, next8.shape, 0)
    last = jnp.where(row8 >= 8 - s, rn, rolled[R - 8:])
    return jnp.concatenate([rolled[: R - 8], last], axis=0)


def _prev8(ref, r0, ci):
    rows = ref[pl.ds(pl.multiple_of(jnp.maximum(r0 - 8, 0), 8), 8), :]
    return jnp.where(ci > 0, rows, 0.0)


def _next8(ref, r0, R, ci, n_chunks):
    start = jnp.minimum(r0 + R, (n_chunks - 1) * R + R - 8)
    rows = ref[pl.ds(pl.multiple_of(start, 8), 8), :]
    return jnp.where(ci < n_chunks - 1, rows, 0.0)


def _w_rows(w_ref, k):
    return [w_ref[j:j + 1, :] for j in range(k)]


def _causal_conv(cur, prev8, w, k):
    acc = cur * w[k - 1]
    for s in range(1, k):
        acc = acc + _shift_down(cur, prev8, s) * w[k - 1 - s]
    return acc


def _qkvconv_fwd(proj, convw, name):
    T = proj.shape[0]
    R = min(ROW_CHUNK, T)
    n_chunks = T // R
    n_blk = (2 * QK_W + V_W) // HEAD_DIM

    def body(x_ref, w_ref, o_ref):
        p = pl.program_id(0)
        is_qk = p < 2 * N_QK
        scale = jnp.where(p < N_QK, HEAD_DIM ** -0.5, 1.0).astype(f32)
        w = _w_rows(w_ref, 4)

        def chunk(ci, carry):
            r0 = pl.multiple_of(ci * R, R)
            cur = x_ref[pl.ds(r0, R), :]
            y = _silu(_causal_conv(cur, _prev8(x_ref, r0, ci), w, 4))
            ss = jnp.sum(y * y, axis=-1, keepdims=True)
            nrm = jnp.where(is_qk, lax.rsqrt(ss + EPS) * scale, 1.0)
            o_ref[pl.ds(r0, R), :] = y * nrm
            return carry

        lax.fori_loop(0, n_chunks, chunk, 0)

    return pl.pallas_call(
        body, grid=(n_blk,),
        in_specs=[pl.BlockSpec((T, HEAD_DIM), lambda p: (0, p)), pl.BlockSpec((4, HEAD_DIM), lambda p: (0, p))],
        out_specs=pl.BlockSpec((T, HEAD_DIM), lambda p: (0, p)), out_shape=SDS((T, n_blk * HEAD_DIM), f32),
        name=name, compiler_params=_cp("parallel"),
    )(proj, convw)


def _qkvconv_bwd(proj, convw, dqkv, name):
    T = proj.shape[0]
    R = min(ROW_CHUNK, T)
    n_chunks = T // R
    n_blk = (2 * QK_W + V_W) // HEAD_DIM

    def body(x_ref, w_ref, do_ref, dx_ref, dw_ref, dc_sc):
        p = pl.program_id(0)
        is_qk = p < 2 * N_QK
        scale = jnp.where(p < N_QK, HEAD_DIM ** -0.5, 1.0).astype(f32)
        w = _w_rows(w_ref, 4)

        def phase1(ci, dw):
            r0 = pl.multiple_of(ci * R, R)
            cur = x_ref[pl.ds(r0, R), :]
            p8 = _prev8(x_ref, r0, ci)
            shifted = [cur] + [_shift_down(cur, p8, s) for s in range(1, 4)]
            c = shifted[0] * w[3]
            for s in range(1, 4):
                c = c + shifted[s] * w[3 - s]
            y = _silu(c)
            dout = do_ref[pl.ds(r0, R), :]
            n = lax.rsqrt(jnp.sum(y * y, axis=-1, keepdims=True) + EPS)
            dot_ = jnp.sum(dout * y, axis=-1, keepdims=True)
            dy = jnp.where(is_qk, scale * (n * dout - y * (n * n * n * dot_)), dout)
            dc = dy * _dsilu(c)
            dc_sc[pl.ds(r0, R), :] = dc
            return tuple(dw[j] + jnp.sum(dc * shifted[3 - j], axis=0, keepdims=True) for j in range(4))

        dw = lax.fori_loop(0, n_chunks, phase1, tuple(jnp.zeros((1, HEAD_DIM), f32) for _ in range(4)))
        for j in range(4):
            dw_ref[j:j + 1, :] = dw[j]

        def phase2(ci, carry):
            r0 = pl.multiple_of(ci * R, R)
            cur = dc_sc[pl.ds(r0, R), :]
            n8 = _next8(dc_sc, r0, R, ci, n_chunks)
            acc = cur * w[3]
            for s in range(1, 4):
                acc = acc + _shift_up(cur, n8, s) * w[3 - s]
            dx_ref[pl.ds(r0, R), :] = acc.astype(dx_ref.dtype)
            return carry

        lax.fori_loop(0, n_chunks, phase2, 0)

    col = lambda p: (0, p)
    return pl.pallas_call(
        body, grid=(n_blk,),
        in_specs=[pl.BlockSpec((T, HEAD_DIM), col), pl.BlockSpec((4, HEAD_DIM), col), pl.BlockSpec((T, HEAD_DIM), col)],
        out_specs=[pl.BlockSpec((T, HEAD_DIM), col), pl.BlockSpec((4, HEAD_DIM), col)],
        out_shape=[SDS((T, n_blk * HEAD_DIM), bf16), SDS((4, n_blk * HEAD_DIM), f32)],
        scratch_shapes=[pltpu.VMEM((T, HEAD_DIM), f32)], name=name, compiler_params=_cp("parallel"),
    )(proj, convw, dqkv)


def _tri_masks(C):
    row = lax.broadcasted_iota(jnp.int32, (C, C), 0)
    col = lax.broadcasted_iota(jnp.int32, (C, C), 1)
    return row, col


def _lane_pick(blk, lane, idx):
    return jnp.sum(jnp.where(lane == idx, blk, 0.0), axis=1, keepdims=True)


def _gate_block(ba, ea, dtb, lane):
    sig = _sig(ba)
    gblk = -ea * _softplus(ba + dtb)
    return sig, gblk


def _head_decay(gam_all, rg_all, tot, lane, h, row, col):
    C = row.shape[0]
    gam_c = _lane_pick(gam_all, lane, N_V + h)
    rg_c = _lane_pick(rg_all, lane, N_V + h)
    lane1 = lax.broadcasted_iota(jnp.int32, (1, LANES), 1)
    tot_h = jnp.sum(jnp.where(lane1 == N_V + h, tot, 0.0), axis=1, keepdims=True)
    gcb = jnp.broadcast_to(gam_c, (C, C))
    dlt = gcb - gcb.T
    dm = jnp.where(row >= col, jnp.exp(jnp.minimum(dlt, 0.0)), 0.0)
    return gam_c, rg_c, tot_h, dm


def _delta_prep(qkv, ba, ea_row, dtb_row, name):
    T = qkv.shape[0]
    C = CHUNK_A
    N = T // C

    SUB = 16

    def body(q_ref, k_ref, v_ref, ba_ref, ea_ref, dtb_ref,
             qd_ref, kd_ref, u_ref, w_ref, attn_ref, tinv_ref, cd_ref, bg_ref, at_sc, t_sc, a_sc, rk_sc):
        row, col = _tri_masks(C)
        lane = lax.broadcasted_iota(jnp.int32, (C, LANES), 1)
        sig, gblk = _gate_block(ba_ref[...], ea_ref[...], dtb_ref[...], lane)
        bg_ref[...] = jnp.where(lane < N_V, sig, gblk)
        lower = (row >= col).astype(f32)
        upper_s = (col > row).astype(f32)
        eye = (row == col).astype(f32)
        same16 = (row >> 4) == (col >> 4)
        same32 = (row >> 5) == (col >> 5)
        gam_all = _dot(lower, gblk, HI)
        rg_all = _dot(upper_s, gblk, HI)
        tot = jnp.sum(gblk, axis=0, keepdims=True)
        for h in range(N_V):
            qk = pl.ds((h // 2) * HEAD_DIM, HEAD_DIM)
            hs = pl.ds(h * HEAD_DIM, HEAD_DIM)
            qh, kh, vh = q_ref[:, qk], k_ref[:, qk], v_ref[:, hs]
            beta_c = _lane_pick(sig, lane, h)
            gam_c, rg_c, tot_h, dm = _head_decay(gam_all, rg_all, tot, lane, h, row, col)
            kk = _dot_nt(_b(kh), _b(kh))
            a = jnp.where(row > col, beta_c * kk * dm, 0.0)
            a_sc[h] = a
            at_sc[h] = jnp.where(same16, a, 0.0).T
            t_sc[h] = eye
            eg = jnp.exp(gam_c)
            attn_ref[0, h] = (_dot_nt(_b(qh), _b(kh)) * dm).astype(attn_ref.dtype)
            qd_ref[:, hs] = (qh * eg).astype(qd_ref.dtype)
            kd_ref[:, hs] = (kh * jnp.exp(rg_c)).astype(kd_ref.dtype)
            cd_ref[0, h] = jnp.broadcast_to(jnp.exp(tot_h), (1, LANES))
            u_ref[:, hs] = vh * beta_c
            rk_sc[:, hs] = kh * (beta_c * eg)

        first_col = (row >> 4) << 4

        def fsub(i, carry):
            for h in range(N_V):
                t = t_sc[h]
                a_col = jnp.sum(jnp.where(col == first_col + i, at_sc[h], 0.0), axis=1, keepdims=True)
                prod = a_col * t
                sums = [jnp.sum(prod[b * SUB:(b + 1) * SUB], axis=0, keepdims=True) for b in range(C // SUB)]
                new = eye - jnp.concatenate([jnp.broadcast_to(s, (SUB, C)) for s in sums], axis=0)
                t_sc[h] = jnp.where(row - first_col == i, new, t)
            return carry

        lax.fori_loop(1, SUB, fsub, 0)
        pair16 = jnp.logical_and(same32, jnp.logical_not(same16))
        for h in range(N_V):
            at_sc[h] = _dot3(t_sc[h], jnp.where(pair16, a_sc[h], 0.0))
        for h in range(N_V):
            p16 = t_sc[h]
            t_sc[h] = p16 - _dot3(at_sc[h], p16)
        for h in range(N_V):
            at_sc[h] = _dot3(t_sc[h], jnp.where(same32, 0.0, a_sc[h]))
        for h in range(N_V):
            p32 = t_sc[h]
            tinv_ref[0, h] = p32 - _dot3(at_sc[h], p32)
        for h in range(N_V):
            hs = pl.ds(h * HEAD_DIM, HEAD_DIM)
            u_ref[:, hs] = _dot3(tinv_ref[0, h], u_ref[:, hs])
        for h in range(N_V):
            hs = pl.ds(h * HEAD_DIM, HEAD_DIM)
            w_ref[:, hs] = _dot3(tinv_ref[0, h], rk_sc[:, hs]).astype(w_ref.dtype)

    big = lambda n: (n, 0)
    return pl.pallas_call(
        body, grid=(N,),
        in_specs=[pl.BlockSpec((C, QK_W), lambda n: (n, 0)), pl.BlockSpec((C, QK_W), lambda n: (n, 1)),
                  pl.BlockSpec((C, V_W), lambda n: (n, 1)), pl.BlockSpec((C, LANES), big),
                  _whole((1, LANES)), _whole((1, LANES))],
        out_specs=[pl.BlockSpec((C, V_W), big)] * 4 + [
            pl.BlockSpec((1, N_V, C, C), lambda n: (n, 0, 0, 0)), pl.BlockSpec((1, N_V, C, C), lambda n: (n, 0, 0, 0)),
            pl.BlockSpec((1, N_V, 1, LANES), lambda n: (n, 0, 0, 0)), pl.BlockSpec((C, LANES), big)],
        out_shape=[SDS((T, V_W), bf16), SDS((T, V_W), bf16), SDS((T, V_W), f32), SDS((T, V_W), bf16),
                   SDS((N, N_V, C, C), bf16), SDS((N, N_V, C, C), f32), SDS((N, N_V, 1, LANES), f32), SDS((T, LANES), f32)],
        scratch_shapes=[pltpu.VMEM((N_V, C, C), f32)] * 3 + [pltpu.VMEM((C, V_W), f32)],
        name=name, compiler_params=_cp("parallel"),
    )(qkv, qkv, qkv, ba, ea_row, dtb_row)


def _delta_scan(qd, kd, u, w, attn, cd, name):
    T = qd.shape[0]
    C = CHUNK_A
    N = T // C

    def body(qd_ref, kd_ref, u_ref, w_ref, attn_ref, cd_ref, o_ref, s_ref, vn_ref, s_sc):
        @pl.when(pl.program_id(0) == 0)
        def _():
            s_sc[...] = jnp.zeros_like(s_sc)

        for h in range(N_V):
            hs = pl.ds(h * HEAD_DIM, HEAD_DIM)
            s = s_sc[h]
            s_ref[0, h] = s
            sb = _b(s)
            vn = u_ref[:, hs] - _dot(w_ref[:, hs], sb)
            vnb = _b(vn)
            vn_ref[:, hs] = vnb
            o_ref[:, hs] = _dot(qd_ref[:, hs], sb) + _dot(attn_ref[0, h], vnb)
            s_sc[h] = s * cd_ref[0, h] + _dot_tn(kd_ref[:, hs], vnb)

    blk = pl.BlockSpec((C, V_W), lambda n: (n, 0))
    per_head = lambda a, b: pl.BlockSpec((1, N_V, a, b), lambda n: (n, 0, 0, 0))
    return pl.pallas_call(
        body, grid=(N,), in_specs=[blk, blk, blk, blk, per_head(C, C), per_head(1, LANES)],
        out_specs=[blk, per_head(HEAD_DIM, HEAD_DIM), blk],
        out_shape=[SDS((T, V_W), f32), SDS((N, N_V, HEAD_DIM, HEAD_DIM), f32), SDS((T, V_W), bf16)],
        scratch_shapes=[pltpu.VMEM((N_V, HEAD_DIM, HEAD_DIM), f32)], name=name, compiler_params=_cp("arbitrary"),
    )(qd, kd, u, w, attn, cd)


def _delta_scan_bwd(do, qd, kd, w, attn, cd, s_all, vn, name):
    T = qd.shape[0]
    C = CHUNK_A
    N = T // C

    def body(do_ref, qd_ref, kd_ref, w_ref, attn_ref, cd_ref, s_ref, vn_ref,
             dqd_ref, dkd_ref, du_ref, dw_ref, dattn_ref, dcd_ref, ds_sc):
        @pl.when(pl.program_id(0) == 0)
        def _():
            ds_sc[...] = jnp.zeros_like(ds_sc)

        row, col = _tri_masks(C)
        for h in range(N_V):
            hs = pl.ds(h * HEAD_DIM, HEAD_DIM)
            dsn = ds_sc[h]
            s = s_ref[0, h]
            dob, sb, dsb = _b(do_ref[:, hs]), _b(s), _b(dsn)
            vnb = vn_ref[:, hs]
            dqd_ref[:, hs] = _dot_nt(dob, sb)
            dattn_ref[0, h] = jnp.where(row >= col, _dot_nt(dob, vnb), 0.0)
            dvn = _dot_tn(attn_ref[0, h], dob) + _dot(kd_ref[:, hs], dsb)
            dvnb = _b(dvn)
            dkd_ref[:, hs] = _dot_nt(vnb, dsb)
            dcd = jnp.sum(jnp.sum(s * dsn, axis=1, keepdims=True), axis=0, keepdims=True)
            dcd_ref[0, h] = jnp.broadcast_to(dcd, (1, LANES))
            du_ref[:, hs] = dvn
            dw_ref[:, hs] = -_dot_nt(dvnb, sb)
            ds_sc[h] = dsn * cd_ref[0, h] + _dot_tn(qd_ref[:, hs], dob) - _dot_tn(w_ref[:, hs], dvnb)

    blk = pl.BlockSpec((C, V_W), lambda n: (N - 1 - n, 0))
    per_head = lambda a, b: pl.BlockSpec((1, N_V, a, b), lambda n: (N - 1 - n, 0, 0, 0))
    return pl.pallas_call(
        body, grid=(N,),
        in_specs=[blk, blk, blk, blk, per_head(C, C), per_head(1, LANES), per_head(HEAD_DIM, HEAD_DIM), blk],
        out_specs=[blk, blk, blk, blk, per_head(C, C), per_head(1, LANES)],
        out_shape=[SDS((T, V_W), f32)] * 4 + [SDS((N, N_V, C, C), f32), SDS((N, N_V, 1, LANES), f32)],
        scratch_shapes=[pltpu.VMEM((N_V, HEAD_DIM, HEAD_DIM), f32)], name=name, compiler_params=_cp("arbitrary"),
    )(do, qd, kd, w, attn, cd, s_all, vn)


def _delta_prep_bwd(qkv, ba, ea_row, dtb_row, tinv, u, w, dqd, dkd, du, dw, dattn, dcd, name):
    T = qkv.shape[0]
    C = CHUNK_A
    N = T // C

    def body(q_ref, k_ref, v_ref, ba_ref, ea_ref, dtb_ref, tinv_ref, u_ref, w_ref,
             dqd_ref, dkd_ref, du_ref, dw_ref, dattn_ref, dcd_ref,
             dqkv_ref, dba_ref, dalog_ref, ddtb_ref):
        row, col = _tri_masks(C)
        lane = lax.broadcasted_iota(jnp.int32, (C, LANES), 1)
        rowc = lax.broadcasted_iota(jnp.int32, (C, 1), 0)
        ba_v, ea, dtb = ba_ref[...], ea_ref[...], dtb_ref[...]
        sig, gblk = _gate_block(ba_v, ea, dtb, lane)
        lower = (row >= col).astype(f32)
        upper_s = (col > row).astype(f32)
        upper = (col >= row).astype(f32)
        gam_all = _dot(lower, gblk, HI)
        rg_all = _dot(upper_s, gblk, HI)
        tot = jnp.sum(gblk, axis=0, keepdims=True)
        dbeta_blk = jnp.zeros((C, LANES), f32)
        dgam_blk = jnp.zeros((C, LANES), f32)
        for j in range(N_QK):
            qk = pl.ds(j * HEAD_DIM, HEAD_DIM)
            qh, kh = q_ref[:, qk], k_ref[:, qk]
            qhb, khb = _b(qh), _b(kh)
            kk = _dot_nt(khb, khb)
            qkm = _dot_nt(qhb, khb)
            dq_j = jnp.zeros((C, HEAD_DIM), f32)
            dk_j = jnp.zeros((C, HEAD_DIM), f32)
            for h in (2 * j, 2 * j + 1):
                hs = pl.ds(h * HEAD_DIM, HEAD_DIM)
                vh = v_ref[:, hs]
                beta_c = _lane_pick(sig, lane, h)
                gam_c, rg_c, tot_h, dm = _head_decay(gam_all, rg_all, tot, lane, h, row, col)
                eg, er, cdh = jnp.exp(gam_c), jnp.exp(rg_c), jnp.exp(tot_h)
                tinv_h = tinv_ref[0, h]
                d_rv = _dot3(tinv_h, du_ref[:, hs], _TN)
                d_rk = _dot3(tinv_h, dw_ref[:, hs], _TN)
                da = -jnp.where(row > col, _dot_nt(_b(d_rv), _b(u_ref[:, hs])) + _dot_nt(_b(d_rk), _b(w_ref[:, hs])), 0.0)
                dqkv_ref[:, pl.ds(2 * QK_W + h * HEAD_DIM, HEAD_DIM)] = beta_c * d_rv
                dbeta = jnp.sum(d_rv * vh + d_rk * (eg * kh), axis=1, keepdims=True)
                dk_h = (beta_c * eg) * d_rk
                d_eg = jnp.sum(d_rk * kh, axis=1, keepdims=True) * beta_c
                bkd = da * dm
                dbeta = dbeta + jnp.sum(bkd * kk, axis=1, keepdims=True)
                dkk = bkd * beta_c
                ddm = da * beta_c * kk
                dattn_h = dattn_ref[0, h]
                dqk = dattn_h * dm
                ddm = ddm + dattn_h * qkm
                dqd_h, dkd_h = dqd_ref[:, hs], dkd_ref[:, hs]
                dq_j = dq_j + _dot(_b(dqk), khb) + eg * dqd_h
                dk_h = dk_h + _dot_tn(_b(dqk), qhb) + _dot(_b(dkk + dkk.T), khb) + er * dkd_h
                dk_j = dk_j + dk_h
                d_eg = d_eg + jnp.sum(dqd_h * qh, axis=1, keepdims=True)
                d_er = jnp.sum(dkd_h * kh, axis=1, keepdims=True)
                e = ddm * dm
                dgam = jnp.sum(e, axis=1, keepdims=True) - jnp.sum(e.T, axis=1, keepdims=True)
                dgam = dgam + d_eg * eg - d_er * er
                extra = jnp.sum(d_er * er, axis=0, keepdims=True) + jnp.max(dcd_ref[0, h], axis=1, keepdims=True) * cdh
                dgam = dgam + jnp.where(rowc == C - 1, extra, 0.0)
                dbeta_blk = jnp.where(lane == h, dbeta, dbeta_blk)
                dgam_blk = jnp.where(lane == N_V + h, dgam, dgam_blk)
            dqkv_ref[:, qk] = dq_j
            dqkv_ref[:, pl.ds(QK_W + j * HEAD_DIM, HEAD_DIM)] = dk_j
        dg_all = _dot(upper, dgam_blk, HI)
        dsp = dg_all * (-ea) * _sig(ba_v + dtb)
        dba_ref[...] = jnp.where(lane < N_V, dbeta_blk * sig * (1.0 - sig), dsp)
        part_alog = jnp.sum(dg_all * gblk, axis=0, keepdims=True)
        part_dtb = jnp.sum(dsp, axis=0, keepdims=True)

        @pl.when(pl.program_id(0) == 0)
        def _():
            dalog_ref[...] = part_alog
            ddtb_ref[...] = part_dtb

        @pl.when(pl.program_id(0) > 0)
        def _():
            dalog_ref[...] += part_alog
            ddtb_ref[...] += part_dtb

    big = lambda n: (n, 0)
    wide = pl.BlockSpec((C, V_W), big)
    sq = pl.BlockSpec((1, N_V, C, C), lambda n: (n, 0, 0, 0))
    return pl.pallas_call(
        body, grid=(N,),
        in_specs=[pl.BlockSpec((C, QK_W), lambda n: (n, 0)), pl.BlockSpec((C, QK_W), lambda n: (n, 1)),
                  pl.BlockSpec((C, V_W), lambda n: (n, 1)), pl.BlockSpec((C, LANES), big),
                  _whole((1, LANES)), _whole((1, LANES)), sq, wide, wide, wide, wide, wide, wide, sq,
                  pl.BlockSpec((1, N_V, 1, LANES), lambda n: (n, 0, 0, 0))],
        out_specs=[pl.BlockSpec((C, 2 * QK_W + V_W), big), pl.BlockSpec((C, LANES), big),
                   _whole((1, LANES)), _whole((1, LANES))],
        out_shape=[SDS((T, 2 * QK_W + V_W), f32), SDS((T, LANES), f32), SDS((1, LANES), f32), SDS((1, LANES), f32)],
        name=name, compiler_params=_cp("arbitrary"),
    )(qkv, qkv, qkv, ba, ea_row, dtb_row, tinv, u, w, dqd, dkd, du, dw, dattn, dcd)


def _onorm_fwd(o, proj, hg, name):
    T = o.shape[0]
    tr = _divisor(T, 256, 16)

    def body(o_ref, z_ref, g_ref, out_ref):
        g = g_ref[...]
        for h in range(N_V):
            hs = pl.ds(h * HEAD_DIM, HEAD_DIM)
            oh = o_ref[:, hs]
            r = lax.rsqrt(jnp.mean(oh * oh, axis=-1, keepdims=True) + EPS)
            out_ref[:, hs] = (oh * r * g * _silu(z_ref[:, hs])).astype(out_ref.dtype)

    return pl.pallas_call(
        body, grid=(T // tr,), in_specs=[_rows(tr, V_W), _rows(tr, V_W, C_Z // V_W), _whole((1, HEAD_DIM))],
        out_specs=_rows(tr, V_W), out_shape=SDS((T, V_W), bf16), name=name, compiler_params=_cp("parallel"),
    )(o, proj, hg)


def _onorm_bwd(don, o, proj, hg, name):
    T = o.shape[0]
    tr = _divisor(T, 256, 16)

    def body(don_ref, o_ref, z_ref, g_ref, do_ref, dz_ref, dg_ref):
        g = g_ref[...]
        dg = jnp.zeros((1, HEAD_DIM), f32)
        for h in range(N_V):
            hs = pl.ds(h * HEAD_DIM, HEAD_DIM)
            oh, zh, dh = o_ref[:, hs], z_ref[:, hs], don_ref[:, hs]
            r = lax.rsqrt(jnp.mean(oh * oh, axis=-1, keepdims=True) + EPS)
            d_n = dh * _silu(zh)
            dz_ref[:, hs] = (dh * (oh * r * g) * _dsilu(zh)).astype(dz_ref.dtype)
            dy = d_n * g
            m = jnp.mean(dy * oh, axis=-1, keepdims=True)
            do_ref[:, hs] = r * dy - oh * (r * r * r * m)
            dg = dg + jnp.sum(d_n * oh * r, axis=0, keepdims=True)

        @pl.when(pl.program_id(0) == 0)
        def _():
            dg_ref[...] = dg

        @pl.when(pl.program_id(0) > 0)
        def _():
            dg_ref[...] += dg

    return pl.pallas_call(
        body, grid=(T // tr,),
        in_specs=[_rows(tr, V_W), _rows(tr, V_W), _rows(tr, V_W, C_Z // V_W), _whole((1, HEAD_DIM))],
        out_specs=[_rows(tr, V_W), _rows(tr, V_W), _whole((1, HEAD_DIM))],
        out_shape=[SDS((T, V_W), f32), SDS((T, V_W), bf16), SDS((1, HEAD_DIM), f32)],
        name=name, compiler_params=_cp("arbitrary"),
    )(don, o, proj, hg)


def _sgu_parts(ub, vb, gain):
    gv = _gelu(vb)
    r = lax.rsqrt(jnp.mean(gv * gv, axis=-1, keepdims=True) + EPS)
    return _gelu(ub), gv, r, gv * r * gain


def _sgu_fwd(proj, gain, w_s, b_bc, name):
    T = proj.shape[0]
    C = CHUNK_B

    def body(ub_ref, vb_ref, g_ref, w_ref, b_ref, o_ref):
        row, col = _tri_masks(C)
        u, _, _, vn = _sgu_parts(ub_ref[...], vb_ref[...], g_ref[...])
        for g in range(N_GROUPS):
            gs = pl.ds(g * GROUP_DIM, GROUP_DIM)
            wg = jnp.where(row >= col, w_ref[g], 0.0)
            mixed = _dot(_b(wg), _b(vn[:, g * GROUP_DIM:(g + 1) * GROUP_DIM])) + b_ref[g]
            o_ref[:, gs] = (u[:, g * GROUP_DIM:(g + 1) * GROUP_DIM] * mixed).astype(o_ref.dtype)

    return pl.pallas_call(
        body, grid=(T // C,),
        in_specs=[_rows(C, WIDTH_B, C_UB // WIDTH_B), _rows(C, WIDTH_B, C_VB // WIDTH_B), _whole((1, WIDTH_B)),
                  _whole((N_GROUPS, C, C)), _whole((N_GROUPS, C, GROUP_DIM))],
        out_specs=_rows(C, WIDTH_B), out_shape=SDS((T, WIDTH_B), bf16), name=name, compiler_params=_cp("parallel"),
    )(proj, proj, gain, w_s, b_bc)


def _sgu_bwd(dsgu, proj, gain, w_s, b_bc, name):
    T = proj.shape[0]
    C = CHUNK_B

    def body(d_ref, ub_ref, vb_ref, g_ref, w_ref, b_ref, dub_ref, dvb_ref, dw_ref, db_ref, dg_ref):
        first = pl.program_id(0) == 0
        row, col = _tri_masks(C)
        ub, vb, gain_v = ub_ref[...], vb_ref[...], g_ref[...]
        u, gv, r, vn = _sgu_parts(ub, vb, gain_v)
        d = d_ref[...]
        dvn_parts = []
        for g in range(N_GROUPS):
            sl = slice(g * GROUP_DIM, (g + 1) * GROUP_DIM)
            wg = jnp.where(row >= col, w_ref[g], 0.0)
            vng = _b(vn[:, sl])
            mixed = _dot(_b(wg), vng) + b_ref[g]
            dub_ref[:, pl.ds(g * GROUP_DIM, GROUP_DIM)] = (d[:, sl] * mixed * _dgelu(ub[:, sl])).astype(dub_ref.dtype)
            dmix = d[:, sl] * u[:, sl]
            dmb = _b(dmix)
            dwg = jnp.where(row >= col, _dot_nt(dmb, vng), 0.0)
            dbg = jnp.sum(dmix, axis=1, keepdims=True)

            @pl.when(first)
            def _():
                dw_ref[g] = dwg
                db_ref[g] = dbg

            @pl.when(jnp.logical_not(first))
            def _():
                dw_ref[g] += dwg
                db_ref[g] += dbg

            dvn_parts.append(_dot_tn(_b(wg), dmb))
        dvn = jnp.concatenate(dvn_parts, axis=1)
        dy = dvn * gain_v
        m = jnp.mean(dy * gv, axis=-1, keepdims=True)
        dgv = r * dy - gv * (r * r * r * m)
        dvb_ref[...] = (dgv * _dgelu(vb)).astype(dvb_ref.dtype)
        dgain = jnp.sum(dvn * gv * r, axis=0, keepdims=True)

        @pl.when(first)
        def _():
            dg_ref[...] = dgain

        @pl.when(jnp.logical_not(first))
        def _():
            dg_ref[...] += dgain

    return pl.pallas_call(
        body, grid=(T // C,),
        in_specs=[_rows(C, WIDTH_B), _rows(C, WIDTH_B, C_UB // WIDTH_B), _rows(C, WIDTH_B, C_VB // WIDTH_B),
                  _whole((1, WIDTH_B)), _whole((N_GROUPS, C, C)), _whole((N_GROUPS, C, GROUP_DIM))],
        out_specs=[_rows(C, WIDTH_B), _rows(C, WIDTH_B), _whole((N_GROUPS, C, C)), _whole((N_GROUPS, C, 1)),
                   _whole((1, WIDTH_B))],
        out_shape=[SDS((T, WIDTH_B), bf16), SDS((T, WIDTH_B), bf16), SDS((N_GROUPS, C, C), f32),
                   SDS((N_GROUPS, C, 1), f32), SDS((1, WIDTH_B), f32)],
        name=name, compiler_params=_cp("arbitrary"),
    )(dsgu, proj, proj, gain, w_s, b_bc)


def _merge_fwd(proj, ya, yb, name):
    T = proj.shape[0]
    tr = _divisor(T, 256, 16)

    def body(ga_ref, gb_ref, ya_ref, yb_ref, o_ref):
        o_ref[...] = (_sig(ga_ref[...]) * ya_ref[...] + _sig(gb_ref[...]) * yb_ref[...]).astype(o_ref.dtype)

    return pl.pallas_call(
        body, grid=(T // tr,),
        in_specs=[_rows(tr, D_MODEL, C_GA // D_MODEL), _rows(tr, D_MODEL, C_GB // D_MODEL), _rows(tr, D_MODEL), _rows(tr, D_MODEL)],
        out_specs=_rows(tr, D_MODEL), out_shape=SDS((T, D_MODEL), bf16), name=name, compiler_params=_cp("parallel"),
    )(proj, proj, ya, yb)


def _merge_bwd(dm, proj, ya, yb, name):
    T = proj.shape[0]
    tr = _divisor(T, 256, 16)

    def body(dm_ref, ga_ref, gb_ref, ya_ref, yb_ref, dya_ref, dyb_ref, dga_ref, dgb_ref):
        d = dm_ref[...]
        sa, sb = _sig(ga_ref[...]), _sig(gb_ref[...])
        dya_ref[...] = (d * sa).astype(bf16)
        dyb_ref[...] = (d * sb).astype(bf16)
        dga_ref[...] = (d * ya_ref[...] * sa * (1.0 - sa)).astype(bf16)
        dgb_ref[...] = (d * yb_ref[...] * sb * (1.0 - sb)).astype(bf16)

    return pl.pallas_call(
        body, grid=(T // tr,),
        in_specs=[_rows(tr, D_MODEL), _rows(tr, D_MODEL, C_GA // D_MODEL), _rows(tr, D_MODEL, C_GB // D_MODEL),
                  _rows(tr, D_MODEL), _rows(tr, D_MODEL)],
        out_specs=[_rows(tr, D_MODEL)] * 4, out_shape=[SDS((T, D_MODEL), bf16)] * 4, name=name,
        compiler_params=_cp("parallel"),
    )(dm, proj, proj, ya, yb)


def _ffnconv_fwd(upg, upv, wg, wv, bg, bv, name):
    T, F = upg.shape
    R = min(ROW_CHUNK, T)
    n_chunks = T // R

    def body(g_ref, v_ref, wg_ref, wv_ref, bg_ref, bv_ref, o_ref):
        wgv, wvv = _w_rows(wg_ref, 3), _w_rows(wv_ref, 3)

        def chunk(ci, carry):
            r0 = pl.multiple_of(ci * R, R)
            cg = _causal_conv(g_ref[pl.ds(r0, R), :], _prev8(g_ref, r0, ci), wgv, 3) + bg_ref[...]
            cv = _causal_conv(v_ref[pl.ds(r0, R), :], _prev8(v_ref, r0, ci), wvv, 3) + bv_ref[...]
            o_ref[pl.ds(r0, R), :] = (_silu(cg) * cv).astype(o_ref.dtype)
            return carry

        lax.fori_loop(0, n_chunks, chunk, 0)

    col = lambda p: (0, p)
    return pl.pallas_call(
        body, grid=(F // LANES,),
        in_specs=[pl.BlockSpec((T, LANES), col)] * 2 + [pl.BlockSpec((3, LANES), col)] * 2 + [pl.BlockSpec((1, LANES), col)] * 2,
        out_specs=pl.BlockSpec((T, LANES), col), out_shape=SDS((T, F), bf16), name=name, compiler_params=_cp("parallel"),
    )(upg, upv, wg, wv, bg, bv)


def _ffnconv_bwd(dact, upg, upv, wg, wv, bg, bv, name):
    T, F = upg.shape
    R = min(ROW_CHUNK, T)
    n_chunks = T // R

    def body(d_ref, g_ref, v_ref, wg_ref, wv_ref, bg_ref, bv_ref,
             dg_ref, dv_ref, dwg_ref, dwv_ref, dbg_ref, dbv_ref, sg, sv):
        wgv, wvv = _w_rows(wg_ref, 3), _w_rows(wv_ref, 3)

        def phase1(ci, carry):
            dwg, dwv = carry
            r0 = pl.multiple_of(ci * R, R)
            gcur, vcur = g_ref[pl.ds(r0, R), :], v_ref[pl.ds(r0, R), :]
            gp, vp = _prev8(g_ref, r0, ci), _prev8(v_ref, r0, ci)
            gsh = [gcur] + [_shift_down(gcur, gp, s) for s in (1, 2)]
            vsh = [vcur] + [_shift_down(vcur, vp, s) for s in (1, 2)]
            cg = gsh[0] * wgv[2] + gsh[1] * wgv[1] + gsh[2] * wgv[0] + bg_ref[...]
            cv = vsh[0] * wvv[2] + vsh[1] * wvv[1] + vsh[2] * wvv[0] + bv_ref[...]
            d = d_ref[pl.ds(r0, R), :]
            dcv = d * _silu(cg)
            dcg = d * cv * _dsilu(cg)
            sg[pl.ds(r0, R), :] = dcg
            sv[pl.ds(r0, R), :] = dcv
            rg = [jnp.sum(dcg * gsh[2 - j], axis=0, keepdims=True) for j in range(3)] + [jnp.sum(dcg, axis=0, keepdims=True)]
            rv = [jnp.sum(dcv * vsh[2 - j], axis=0, keepdims=True) for j in range(3)] + [jnp.sum(dcv, axis=0, keepdims=True)]
            return tuple(a + b for a, b in zip(dwg, rg)), tuple(a + b for a, b in zip(dwv, rv))

        z4 = tuple(jnp.zeros((1, LANES), f32) for _ in range(4))
        dwg, dwv = lax.fori_loop(0, n_chunks, phase1, (z4, z4))
        for j in range(3):
            dwg_ref[j:j + 1, :] = dwg[j]
            dwv_ref[j:j + 1, :] = dwv[j]
        dbg_ref[...] = dwg[3]
        dbv_ref[...] = dwv[3]

        def phase2(ci, carry):
            r0 = pl.multiple_of(ci * R, R)
            for sc, wv_, out in ((sg, wgv, dg_ref), (sv, wvv, dv_ref)):
                cur = sc[pl.ds(r0, R), :]
                n8 = _next8(sc, r0, R, ci, n_chunks)
                acc = cur * wv_[2] + _shift_up(cur, n8, 1) * wv_[1] + _shift_up(cur, n8, 2) * wv_[0]
                out[pl.ds(r0, R), :] = acc.astype(out.dtype)
            return carry

        lax.fori_loop(0, n_chunks, phase2, 0)

    col = lambda p: (0, p)
    big, w3, b1 = pl.BlockSpec((T, LANES), col), pl.BlockSpec((3, LANES), col), pl.BlockSpec((1, LANES), col)
    return pl.pallas_call(
        body, grid=(F // LANES,), in_specs=[big, big, big, w3, w3, b1, b1], out_specs=[big, big, w3, w3, b1, b1],
        out_shape=[SDS((T, F), bf16), SDS((T, F), bf16), SDS((3, F), f32), SDS((3, F), f32), SDS((1, F), f32), SDS((1, F), f32)],
        scratch_shapes=[pltpu.VMEM((T, LANES), f32), pltpu.VMEM((T, LANES), f32)], name=name, compiler_params=_cp("parallel"),
    )(dact, upg, upv, wg, wv, bg, bv)


def _loss_head(x, gain, target, name):
    T, Dm = x.shape
    tr = _divisor(T, 256, 16)

    def body(x_ref, g_ref, t_ref, l_ref, dx_ref, dg_ref):
        xv, g = x_ref[...], g_ref[...]
        r = lax.rsqrt(jnp.mean(xv * xv, axis=-1, keepdims=True) + EPS)
        err = xv * r * g - t_ref[...]
        part_l = 0.5 * jnp.sum(jnp.mean(err * err, axis=-1, keepdims=True), axis=0, keepdims=True)
        dy = err * (1.0 / Dm)
        dyg = dy * g
        m = jnp.mean(dyg * xv, axis=-1, keepdims=True)
        dx_ref[...] = r * dyg - xv * (r * r * r * m)
        part_g = jnp.sum(dy * xv * r, axis=0, keepdims=True)
        part_l = jnp.broadcast_to(part_l, (1, LANES))

        @pl.when(pl.program_id(0) == 0)
        def _():
            l_ref[...] = part_l
            dg_ref[...] = part_g

        @pl.when(pl.program_id(0) > 0)
        def _():
            l_ref[...] += part_l
            dg_ref[...] += part_g

    return pl.pallas_call(
        body, grid=(T // tr,), in_specs=[_rows(tr, Dm), _whole((1, Dm)), _rows(tr, Dm)],
        out_specs=[_whole((1, LANES)), _rows(tr, Dm), _whole((1, Dm))],
        out_shape=[SDS((1, LANES), f32), SDS((T, Dm), f32), SDS((1, Dm), f32)], name=name, compiler_params=_cp("arbitrary"),
    )(x, gain, target)


def _lane_row(vec, offset):
    return jnp.pad(vec.astype(f32), (offset, LANES - offset - vec.shape[0]))[None]


def _layer_fwd(x, p_i, W, S, li):
    nm = lambda s: f"{s}_l{li}"
    sv = {"x0": x}
    h1 = _rms_fwd(x, S["norm_mix"], nm("rms_mix"))
    proj = _mm(h1, W["in_main"], "nn", nm("proj_main"))
    ba = _mm(h1, W["in_ba"], "nn", nm("proj_ba"))
    qkv = _qkvconv_fwd(proj, S["conv_qkv"], nm("qkvconv"))
    qd, kd, u, w, attn, tinv, cd, _ = _delta_prep(qkv, ba, S["ea_row"], S["dtb_row"], nm("delta_prep"))
    o, s_all, vn = _delta_scan(qd, kd, u, w, attn, cd, nm("delta_scan"))
    on = _onorm_fwd(o, proj, S["head_norm"], nm("onorm"))
    ya = _mm(on, W["branch_a"], "nn", nm("branch_a"))
    sgu = _sgu_fwd(proj, S["sgu_norm"], S["w_spatial"], S["b_bc"], nm("sgu"))
    yb = _mm(sgu, W["branch_b"], "nn", nm("branch_b"))
    merged = _merge_fwd(proj, ya, yb, nm("merge"))
    x1 = _mm(merged, W["out"], "nn", nm("out_proj"), c=x)
    h2 = _rms_fwd(x1, S["norm_ffn"], nm("rms_ffn"))
    upg = _mm(h2, W["up_g"], "nn", nm("ffn_up_g"))
    upv = _mm(h2, W["up_v"], "nn", nm("ffn_up_v"))
    act = _ffnconv_fwd(upg, upv, S["conv_g"], S["conv_v"], S["bias_g"], S["bias_v"], nm("ffnconv"))
    x2 = _mm(act, W["down"], "nn", nm("ffn_down"), c=x1, tk=2816)
    h3 = _rms_fwd(x2, S["norm_ple"], nm("rms_ple"))
    gl = _mm(h3, W["ple_gate"], "nn", nm("ple_gate"))
    pp = _mm(p_i, W["ple_proj"], "nn", nm("ple_proj"))
    (x3,) = _ew(lambda a, g, q: (a + _sig(g) * q,), [x2, gl, pp], [f32], nm("ple_mix"))
    sv.update(h1=h1, proj=proj, ba=ba, qkv=qkv, qd=qd, kd=kd, u=u, w=w, attn=attn, tinv=tinv, cd=cd, o=o, s_all=s_all,
              vn=vn, on=on, ya=ya, sgu=sgu, yb=yb, merged=merged, x1=x1, h2=h2, upg=upg, upv=upv, act=act, x2=x2,
              h3=h3, gl=gl, pp=pp, p=p_i)
    return x3, sv


def _layer_bwd(dx3, sv, W, S, li):
    nm = lambda s: f"{s}_l{li}"
    G = {}
    dgl, dpp = _ew(lambda d, g, q: ((lambda s: (d * q * s * (1.0 - s), d * s))(_sig(g))),
                   [dx3, sv["gl"], sv["pp"]], [bf16, bf16], nm("ple_mix_bwd"))
    G["w_ple_gate"] = _mm(sv["h3"], dgl, "tn", nm("d_ple_gate"))
    G["w_ple_proj"] = _mm(sv["p"], dpp, "tn", nm("d_ple_proj"))
    dh3 = _mm(dgl, W["ple_gate"], "nt", nm("dh_ple"))
    dx2, G["norm_ple"] = _rms_bwd(sv["x2"], S["norm_ple"], dh3, dx3, nm("rms_ple_bwd"))
    dact = _mm(dx2, W["down"], "nt", nm("d_act"))
    G["w_ffn_down"] = _mm(sv["act"], dx2, "tn", nm("d_ffn_down"))
    dupg, dupv, dcg, dcv, dbg, dbv = _ffnconv_bwd(dact, sv["upg"], sv["upv"], S["conv_g"], S["conv_v"], S["bias_g"],
                                                  S["bias_v"], nm("ffnconv_bwd"))
    G["conv_ffn"] = jnp.concatenate([dcg, dcv], axis=1)
    G["b_conv_ffn"] = jnp.concatenate([dbg, dbv], axis=1)
    G["up_g"] = _mm(sv["h2"], dupg, "tn", nm("d_ffn_up_g"))
    G["up_v"] = _mm(sv["h2"], dupv, "tn", nm("d_ffn_up_v"))
    dh2 = _mm(dupg, W["up_g"], "nt", nm("dh_ffn_g"), tk=2816)
    dh2 = _mm(dupv, W["up_v"], "nt", nm("dh_ffn_v"), c=dh2, tk=2816)
    dx1, G["norm_ffn"] = _rms_bwd(sv["x1"], S["norm_ffn"], dh2, dx2, nm("rms_ffn_bwd"))
    dmerged = _mm(dx1, W["out"], "nt", nm("d_merged"))
    G["w_out"] = _mm(sv["merged"], dx1, "tn", nm("d_w_out"))
    dya, dyb, dga, dgb = _merge_bwd(dmerged, sv["proj"], sv["ya"], sv["yb"], nm("merge_bwd"))
    G["w_branch_a"] = _mm(sv["on"], dya, "tn", nm("d_branch_a"))
    G["w_branch_b"] = _mm(sv["sgu"], dyb, "tn", nm("d_branch_b"))
    don = _mm(dya, W["branch_a"], "nt", nm("d_on"))
    dsgu = _mm(dyb, W["branch_b"], "nt", nm("d_sgu"))
    dub, dvb, G["w_spatial"], db_s, G["sgu_norm"] = _sgu_bwd(dsgu, sv["proj"], S["sgu_norm"], S["w_spatial"], S["b_bc"], nm("sgu_bwd"))
    G["b_spatial"] = db_s.reshape(N_GROUPS, CHUNK_B)
    do, dz, G["head_norm"] = _onorm_bwd(don, sv["o"], sv["proj"], S["head_norm"], nm("onorm_bwd"))
    dqd, dkd, du, dw, dattn, dcd = _delta_scan_bwd(do, sv["qd"], sv["kd"], sv["w"], sv["attn"], sv["cd"], sv["s_all"], sv["vn"],
                                                   nm("delta_scan_bwd"))
    dqkv_n, dba, dalog_row, ddtb_row = _delta_prep_bwd(sv["qkv"], sv["ba"], S["ea_row"], S["dtb_row"], sv["tinv"], sv["u"], sv["w"],
                                                       dqd, dkd, du, dw, dattn, dcd, nm("delta_prep_bwd"))
    G["a_log"] = dalog_row[0, N_V:2 * N_V]
    G["dt_bias"] = ddtb_row[0, N_V:2 * N_V]
    dqkv, G["conv_qkv"] = _qkvconv_bwd(sv["proj"], S["conv_qkv"], dqkv_n, nm("qkvconv_bwd"))
    dproj = jnp.concatenate([dqkv, dz, dub, dvb, dga, dgb], axis=1)
    G["in_main"] = _mm(sv["h1"], dproj, "tn", nm("d_in_main"))
    G["in_ba"] = _mm(sv["h1"], dba, "tn", nm("d_in_ba"))
    dh1 = _mm(dba, W["in_ba"], "nt", nm("dh_mix_ba"))
    dh1 = _mm(dproj, W["in_main"], "nt", nm("dh_mix"), c=dh1)
    dx0, G["norm_mix"] = _rms_bwd(sv["x0"], S["norm_mix"], dh1, dx1, nm("rms_mix_bwd"))
    return dx0, G


_BA0 = C_UB
_BA1 = C_UB + 2 * N_V


def _cols(segments, lo, hi):
    out = []
    for start, a in segments:
        s_lo, s_hi = max(lo, start), min(hi, start + a.shape[1])
        if s_lo < s_hi:
            out.append(a[:, s_lo - start:s_hi - start])
    return out


def _big_weights(sh):
    n_in = N_IN // N_CHIPS
    w_in = [(j * n_in, a) for j, a in enumerate(sh["w_in"])]
    rows = lambda k: jnp.concatenate(sh[k], axis=0)
    cols = lambda parts: jnp.concatenate(parts, axis=1)
    ba = jnp.pad(cols(_cols(w_in, _BA0, _BA1)), ((0, 0), (0, LANES - 2 * N_V)))
    return dict(
        in_main=cols(_cols(w_in, 0, _BA0) + _cols(w_in, _BA1, N_IN)), in_ba=ba,
        branch_a=rows("w_branch_a"), branch_b=cols(sh["w_branch_b"]), out=rows("w_out"),
        up_g=cols(sh["w_ffn_up"][:2]), up_v=cols(sh["w_ffn_up"][2:]), down=rows("w_ffn_down"),
        ple_gate=rows("w_ple_gate"), ple_proj=cols(sh["w_ple_proj"]))


def _grads_by_chip(G):
    n_in = N_IN // N_CHIPS
    w_in = [(0, G["in_main"][:, :_BA0]), (_BA0, G["in_ba"][:, :2 * N_V]), (_BA1, G["in_main"][:, _BA0:])]
    split = lambda g: jnp.stack(jnp.split(g, N_CHIPS, axis=1))
    by_rows = lambda g: g.reshape(N_CHIPS, -1, g.shape[1])
    return dict(
        w_in=jnp.stack([jnp.concatenate(_cols(w_in, j * n_in, (j + 1) * n_in), axis=1) for j in range(N_CHIPS)]),
        w_branch_a=by_rows(G["w_branch_a"]), w_branch_b=split(G["w_branch_b"]), w_out=by_rows(G["w_out"]),
        w_ffn_up=jnp.stack(jnp.split(G["up_g"], 2, axis=1) + jnp.split(G["up_v"], 2, axis=1)),
        w_ffn_down=by_rows(G["w_ffn_down"]), w_ple_gate=by_rows(G["w_ple_gate"]), w_ple_proj=split(G["w_ple_proj"]))


def _small_params(sm, i):
    return dict(
        norm_mix=sm["norm_mix"][i][None], conv_qkv=sm["conv_qkv"][i], ea_row=_lane_row(jnp.exp(sm["a_log"][i]), N_V),
        dtb_row=_lane_row(sm["dt_bias"][i], N_V), head_norm=sm["head_norm"][i][None], sgu_norm=sm["sgu_norm"][i][None],
        w_spatial=sm["w_spatial"][i],
        b_bc=jnp.broadcast_to(sm["b_spatial"][i][:, :, None], (N_GROUPS, CHUNK_B, GROUP_DIM)),
        norm_ffn=sm["norm_ffn"][i][None], conv_g=sm["conv_ffn"][i][:, :D_FF], conv_v=sm["conv_ffn"][i][:, D_FF:],
        bias_g=sm["b_conv_ffn"][i][None, :D_FF], bias_v=sm["b_conv_ffn"][i][None, D_FF:], norm_ple=sm["norm_ple"][i][None])


MESH = pl.DeviceIdType.MESH
N_CHIPS = 4
ANY = pl.BlockSpec(memory_space=pl.ANY)
ROW_ALIGN = 32


def _place():
    x, y, c = lax.axis_index("x"), lax.axis_index("y"), lax.axis_index("c")
    chips = [(1 - x, y), (x, 1 - y), (1 - x, 1 - y)]
    return x, y, c, 2 * x + y, chips, (x, y, 1 - c)


def _halves(rows, c):
    h = rows // 2
    return pl.ds(pl.multiple_of(c * h, 16), h), pl.ds(pl.multiple_of((1 - c) * h, 16), h)


def _remote(src, dst, send_sems, recv_sems, k, dev):
    return pltpu.make_async_remote_copy(src_ref=src, dst_ref=dst, send_sem=send_sems.at[k], recv_sem=recv_sems.at[k],
                                        device_id=dev, device_id_type=MESH)


def _gather_shards(bufs, name):
    n = len(bufs)

    def body(*refs):
        ins, outs = refs[:n], refs[n:2 * n]
        send_sems, recv_sems = refs[2 * n:]
        x, y, c, me, chips, sib = _place()
        for i in range(n):
            mine, _ = _halves(ins[i].shape[0], c)
            for k, (px, py) in enumerate(chips):
                _remote(ins[i].at[mine], outs[i].at[me, mine], send_sems, recv_sems, 6 * i + k, (px, py, c)).start()
        for i in range(n):
            mine, _ = _halves(ins[i].shape[0], c)
            for k, (px, py) in enumerate(chips):
                pc = 2 * px + py
                _remote(ins[i].at[mine], outs[i].at[pc, mine], send_sems, recv_sems, 6 * i + k, (px, py, c)).wait_recv()
                _remote(outs[i].at[pc, mine], outs[i].at[pc, mine], send_sems, recv_sems, 6 * i + 3 + k, sib).start()
        for i in range(n):
            mine, other = _halves(ins[i].shape[0], c)
            for k, (px, py) in enumerate(chips):
                pc = 2 * px + py
                _remote(outs[i].at[pc, mine], outs[i].at[pc, other], send_sems, recv_sems, 6 * i + 3 + k, sib).wait_recv()
        for i in range(n):
            mine, _ = _halves(ins[i].shape[0], c)
            for k, (px, py) in enumerate(chips):
                pc = 2 * px + py
                _remote(ins[i].at[mine], outs[i].at[me, mine], send_sems, recv_sems, 6 * i + k, (px, py, c)).wait_send()
                _remote(outs[i].at[pc, mine], outs[i].at[pc, mine], send_sems, recv_sems, 6 * i + 3 + k, sib).wait_send()

    return pl.pallas_call(
        body, in_specs=[ANY] * n, out_specs=[ANY] * n,
        out_shape=[SDS((N_CHIPS,) + b.shape, b.dtype) for b in bufs],
        scratch_shapes=[pltpu.SemaphoreType.DMA((6 * n,)), pltpu.SemaphoreType.DMA((6 * n,))],
        name=name,
    )(*bufs)


def _exchange(srcs_of, out_shapes, n_sems, name):
    n = len(out_shapes)

    def body(*refs):
        n_in = len(refs) - n - 2
        ins, outs = refs[:n_in], refs[n_in:n_in + n]
        send_sems, recv_sems = refs[-2:]
        copies = [_remote(s, d, send_sems, recv_sems, k, dev) for s, d, k, dev in srcs_of(ins, outs, _place())]
        for cp in copies:
            cp.start()
        for cp in copies:
            cp.wait()

    def call(*arrs):
        return pl.pallas_call(
            body, in_specs=[ANY] * len(arrs), out_specs=[ANY] * n, out_shape=out_shapes,
            scratch_shapes=[pltpu.SemaphoreType.DMA((n_sems,)), pltpu.SemaphoreType.DMA((n_sems,))], name=name,
        )(*arrs)

    return call


HBM = pl.BlockSpec(memory_space=pltpu.HBM)
SEM = pl.BlockSpec(memory_space=pltpu.SEMAPHORE)
_SIDE_EFFECT = pltpu.SideEffectType.DATAFLOW_SIDE_EFFECTING


def _exchange_start(copies_of, srcs, land_shapes, n_sems, name):
    ns, nl = len(srcs), len(land_shapes)

    def body(*refs):
        ins = refs[:ns + nl]
        send_sems, recv_sems = refs[ns + nl], refs[ns + nl + 1]
        token = refs[-1]
        for s, d, k, dev in copies_of(ins[:ns], ins[ns:], _place()):
            _remote(s, d, send_sems, recv_sems, k, dev).start()
        token[...] = jnp.zeros_like(token)

    lands = [lax.empty(s.shape, s.dtype) for s in land_shapes]
    operands = [pltpu.with_memory_space_constraint(a, pltpu.HBM) for a in list(srcs) + lands]
    outs = pl.pallas_call(
        body, name=name,
        out_shape=(pltpu.SemaphoreType.DMA((n_sems,)), pltpu.SemaphoreType.DMA((n_sems,)),
                   *[pltpu.HBM(a.shape, a.dtype) for a in operands], SDS((8, LANES), f32)),
        in_specs=[HBM] * (ns + nl), out_specs=(SEM, SEM, *[HBM] * (ns + nl), pl.BlockSpec(memory_space=pltpu.VMEM)),
        input_output_aliases={i: 2 + i for i in range(ns + nl)},
        compiler_params=pltpu.CompilerParams(has_side_effects=_SIDE_EFFECT),
    )(*operands)
    return dict(send=outs[0], recv=outs[1], bufs=list(outs[2:2 + ns + nl]), token=outs[-1], ns=ns, copies_of=copies_of)


def _exchange_wait(handle, after, name):
    ns, bufs, copies_of = handle["ns"], handle["bufs"], handle["copies_of"]
    nb = len(bufs)

    def body(*refs):
        ins = refs[:nb]
        send_sems, recv_sems = refs[nb], refs[nb + 1]
        for s, d, k, dev in copies_of(ins[:ns], ins[ns:], _place()):
            cp = _remote(s, d, send_sems, recv_sems, k, dev)
            cp.wait_send()
            cp.wait_recv()

    outs = pl.pallas_call(
        body, name=name, out_shape=tuple(pltpu.HBM(a.shape, a.dtype) for a in bufs),
        in_specs=[HBM] * nb + [SEM, SEM, ANY], out_specs=tuple([HBM] * nb),
        input_output_aliases={i: i for i in range(nb)},
        compiler_params=pltpu.CompilerParams(has_side_effects=_SIDE_EFFECT),
    )(*bufs, handle["send"], handle["recv"], after)
    return list(outs[:ns]), list(outs[ns:])


def _pick(idx, parts):
    out = parts[-1]
    for j in reversed(range(len(parts) - 1)):
        out = jnp.where(idx == j, parts[j], out)
    return out


def _rs_pair(gs, name):
    def copies(ins, outs, place):
        x, y, c, me, chips, sib = place
        return [(g.at[:, _halves(g.shape[1], c)[1]], o, i, sib) for i, (g, o) in enumerate(zip(ins, outs))]

    shapes = [SDS((N_CHIPS, g.shape[1] // 2, g.shape[2]), g.dtype) for g in gs]
    return _exchange(copies, shapes, len(gs), name)(*gs)


def _gather_copies(ins, outs, place):
    x, y, c, me, chips, sib = place
    out = []
    for i, (s, o) in enumerate(zip(ins, outs)):
        mine, _ = _halves(s.shape[0], c)
        out += [(s.at[mine], o.at[me, mine], 3 * i + k, (px, py, c)) for k, (px, py) in enumerate(chips)]
    return out


def _gather_start(shards, name):
    return _exchange_start(_gather_copies, shards, [SDS((N_CHIPS,) + s.shape, s.dtype) for s in shards], 3 * len(shards), name)


def _gather_forward(lands, name):
    n = len(lands)

    def body(*refs):
        ins, outs = refs[:n], refs[n:2 * n]
        send_sems, recv_sems = refs[2 * n:]
        x, y, c, me, chips, sib = _place()
        copies = []
        for i in range(n):
            mine, _ = _halves(ins[i].shape[1], c)
            for k, (px, py) in enumerate(chips):
                pc = 2 * px + py
                copies.append(_remote(ins[i].at[pc, mine], outs[i].at[pc, mine], send_sems, recv_sems, 3 * i + k, sib))
        for cp in copies:
            cp.start()
        for cp in copies:
            cp.wait()

    return pl.pallas_call(
        body, in_specs=[ANY] * n, out_specs=[ANY] * n, out_shape=[SDS(a.shape, a.dtype) for a in lands],
        input_output_aliases={i: i for i in range(n)},
        scratch_shapes=[pltpu.SemaphoreType.DMA((3 * n,)), pltpu.SemaphoreType.DMA((3 * n,))], name=name,
    )(*lands)


def _rs_chips_copies(ins, outs, place):
    x, y, c, me, chips, sib = place
    return [(p.at[2 * px + py], o.at[k], 3 * i + k, (px, py, c))
            for i, (p, o) in enumerate(zip(ins, outs)) for k, (px, py) in enumerate(chips)]


def _rs_tile(h, n_cols):
    return _divisor(h, max(16, (1 << 20) // (4 * n_cols)), 16)


def _rs_add2(g, got, name):
    _, K, Nc = g.shape
    h = K // 2
    tr = _rs_tile(h, Nc)
    nb = h // tr

    def body(g_ref, got_ref, o_ref):
        o_ref[...] = (g_ref[...] + got_ref[...]).astype(o_ref.dtype)

    blk = lambda j, i: (j, i, 0)
    return pl.pallas_call(
        body, name=name, out_shape=SDS((N_CHIPS, h, Nc), bf16), grid=(N_CHIPS, nb),
        in_specs=[pl.BlockSpec((1, tr, Nc), lambda j, i: (j, lax.axis_index("c") * nb + i, 0)), pl.BlockSpec((1, tr, Nc), blk)],
        out_specs=pl.BlockSpec((1, tr, Nc), blk), compiler_params=_cp("parallel", "parallel"),
    )(g, got)


def _rs_add4(p, parts, name):
    _, h, Nc = p.shape
    tr = _rs_tile(h, Nc)
    nb = h // tr

    def body(own_ref, a_ref, b_ref, c_ref, o_ref):
        up = lambda r: r[0].astype(f32)
        o_ref[...] = (up(own_ref) + up(b_ref)) + (up(a_ref) + up(c_ref))

    part = lambda k: pl.BlockSpec((1, tr, Nc), lambda i, k=k: (k, i, 0))
    me = lambda: 2 * lax.axis_index("x") + lax.axis_index("y")
    return pl.pallas_call(
        body, name=name, out_shape=SDS((2 * h, Nc), f32), grid=(nb,),
        in_specs=[pl.BlockSpec((1, tr, Nc), lambda i: (me(), i, 0)), part(0), part(1), part(2)],
        out_specs=pl.BlockSpec((tr, Nc), lambda i: (lax.axis_index("c") * nb + i, 0)), compiler_params=_cp("parallel"),
    )(p, parts, parts, parts)


def _rs_join(fs, name):
    n = len(fs)

    def body(*refs):
        ins, outs = refs[:n], refs[n:2 * n]
        send_sems, recv_sems = refs[2 * n:]
        x, y, c, me, chips, sib = _place()
        copies = []
        for i in range(n):
            mine, _ = _halves(ins[i].shape[0], c)
            copies.append(_remote(ins[i].at[mine], outs[i].at[mine], send_sems, recv_sems, i, sib))
        for cp in copies:
            cp.start()
        for cp in copies:
            cp.wait()

    return pl.pallas_call(
        body, in_specs=[ANY] * n, out_specs=[ANY] * n, out_shape=[SDS(a.shape, a.dtype) for a in fs],
        input_output_aliases={i: i for i in range(n)},
        scratch_shapes=[pltpu.SemaphoreType.DMA((n,)), pltpu.SemaphoreType.DMA((n,))], name=name,
    )(*fs)


def _rs_begin(gs, names, tag):
    gots = _rs_pair(gs, f"rs_pair_{tag}")
    ps = [_rs_add2(g, got, f"rs_add2_{k}_{tag}") for g, got, k in zip(gs, gots, names)]
    lands = [SDS((3,) + p.shape[1:], p.dtype) for p in ps]
    return _exchange_start(_rs_chips_copies, ps, lands, 3 * len(ps), f"rs_chips_start_{tag}")


def _rs_end(handle, after, names, tag):
    ps, parts = _exchange_wait(handle, after, f"rs_chips_wait_{tag}")
    fs = [_rs_add4(p, part, f"rs_add4_{k}_{tag}") for p, part, k in zip(ps, parts, names)]
    return _rs_join(fs, f"rs_join_{tag}")


def _allreduce_small(buf, name):
    R, L = buf.shape

    def body(x_ref, o_ref, r0, s1, r1, send_sems, recv_sems):
        x, y, c, me, chips, sib = _place()
        cp = _remote(x_ref, r0, send_sems, recv_sems, 0, sib)
        cp.start()
        cp.wait()
        s1[...] = x_ref[...] + r0[...]
        cps = []
        for k, (px, py) in enumerate(chips):
            cp = _remote(s1, r1.at[k], send_sems, recv_sems, 1 + k, (px, py, c))
            cp.start()
            cps.append(cp)
        for cp in cps:
            cp.wait()
        o_ref[...] = (s1[...] + r1[1]) + (r1[0] + r1[2])

    vm = pl.BlockSpec(memory_space=pltpu.VMEM)
    return pl.pallas_call(
        body, in_specs=[vm], out_specs=vm, out_shape=SDS((R, L), f32),
        scratch_shapes=[pltpu.VMEM((R, L), f32), pltpu.VMEM((R, L), f32), pltpu.VMEM((3, R, L), f32),
                        pltpu.SemaphoreType.DMA((4,)), pltpu.SemaphoreType.DMA((4,))],
        name=name, compiler_params=pltpu.CompilerParams(vmem_limit_bytes=VMEM_LIMIT_BYTES),
    )(buf)


BIG = ("w_in", "w_branch_a", "w_branch_b", "w_out", "w_ffn_up", "w_ffn_down", "w_ple_gate", "w_ple_proj")
SMALL_REPL = ("norm_mix", "a_log", "dt_bias", "head_norm", "sgu_norm", "w_spatial", "b_spatial", "norm_ffn", "b_conv_ffn",
              "norm_ple", "norm_final")
SMALL_COLS = ("conv_qkv", "conv_ffn")
WEIGHTS = ("norm_mix", "w_in", "conv_qkv", "a_log", "dt_bias", "head_norm", "sgu_norm", "w_spatial", "b_spatial", "w_branch_a",
           "w_branch_b", "w_out", "norm_ffn", "w_ffn_up", "conv_ffn", "b_conv_ffn", "w_ffn_down", "norm_ple", "w_ple_gate",
           "w_ple_proj", "norm_final")


def _flat(arrs, dtype):
    cat = jnp.concatenate([a.astype(dtype).reshape(-1) for a in arrs])
    unit = LANES * ROW_ALIGN
    cat = jnp.pad(cat, (0, -cat.shape[0] % unit))
    return cat.reshape(-1, LANES)


def _unflat(buf, shapes):
    flat = buf.reshape(-1)
    out, off = [], 0
    for s in shapes:
        n = math.prod(s)
        out.append(flat[off:off + n].reshape(s))
        off += n
    return out


def _adamw(w, g, m, v):
    m2 = ADAM_B1 * m + (1.0 - ADAM_B1) * g
    v2 = ADAM_B2 * v + (1.0 - ADAM_B2) * (g * g)
    m_hat = m2 * (1.0 / (1.0 - ADAM_B1 ** ADAM_STEP))
    v_hat = v2 * (1.0 / (1.0 - ADAM_B2 ** ADAM_STEP))
    delta = -ADAM_LR * (m_hat / (jnp.sqrt(v_hat) + ADAM_EPS) + ADAM_WD * w)
    return delta, m2, v2


def _adamw_layer(w, g, m, v, prev, layer, name):
    K, Nc = g.shape
    tr = _divisor(K, max(16, (1 << 20) // (4 * Nc)), 16)
    nb = K // tr
    rows = pl.BlockSpec((tr, Nc), lambda i: (layer * nb + i, 0))
    n_prev = 0 if prev is None else 4

    def body(w_ref, g_ref, m_ref, v_ref, *refs):
        go_ref, d_ref, mo_ref, vo_ref = refs[n_prev:]
        gv = g_ref[...]
        d, m2, v2 = _adamw(w_ref[...], gv, m_ref[...], v_ref[...])
        go_ref[...] = gv
        d_ref[...] = d
        mo_ref[...] = m2
        vo_ref[...] = v2

    return pl.pallas_call(
        body, grid=(nb,), in_specs=[rows, pl.BlockSpec((tr, Nc), lambda i: (i, 0)), rows, rows] + [ANY] * n_prev,
        out_specs=[rows] * 4, out_shape=[SDS(w.shape, f32)] * 4,
        input_output_aliases={4 + j: j for j in range(n_prev)}, name=name, compiler_params=_cp("parallel"),
    )(w, g, m, v, *(prev or ()))


def kernel(x, p, norm_mix, w_in, conv_qkv, a_log, dt_bias, head_norm, sgu_norm, w_spatial, b_spatial, w_branch_a, w_branch_b, w_out, norm_ffn, w_ffn_up, conv_ffn, b_conv_ffn, w_ffn_down, norm_ple, w_ple_gate, w_ple_proj, norm_final, loss_target, m_norm_mix, m_w_in, m_conv_qkv, m_a_log, m_dt_bias, m_head_norm, m_sgu_norm, m_w_spatial, m_b_spatial, m_w_branch_a, m_w_branch_b, m_w_out, m_norm_ffn, m_w_ffn_up, m_conv_ffn, m_b_conv_ffn, m_w_ffn_down, m_norm_ple, m_w_ple_gate, m_w_ple_proj, m_norm_final, v_norm_mix, v_w_in, v_conv_qkv, v_a_log, v_dt_bias, v_head_norm, v_sgu_norm, v_w_spatial, v_b_spatial, v_w_branch_a, v_w_branch_b, v_w_out, v_norm_ffn, v_w_ffn_up, v_conv_ffn, v_b_conv_ffn, v_w_ffn_down, v_norm_ple, v_w_ple_gate, v_w_ple_proj, v_norm_final):
    w = dict(norm_mix=norm_mix, w_in=w_in, conv_qkv=conv_qkv, a_log=a_log, dt_bias=dt_bias, head_norm=head_norm, sgu_norm=sgu_norm,
             w_spatial=w_spatial, b_spatial=b_spatial, w_branch_a=w_branch_a, w_branch_b=w_branch_b, w_out=w_out, norm_ffn=norm_ffn,
             w_ffn_up=w_ffn_up, conv_ffn=conv_ffn, b_conv_ffn=b_conv_ffn, w_ffn_down=w_ffn_down, norm_ple=norm_ple,
             w_ple_gate=w_ple_gate, w_ple_proj=w_ple_proj, norm_final=norm_final)
    m = dict(norm_mix=m_norm_mix, w_in=m_w_in, conv_qkv=m_conv_qkv, a_log=m_a_log, dt_bias=m_dt_bias, head_norm=m_head_norm,
             sgu_norm=m_sgu_norm, w_spatial=m_w_spatial, b_spatial=m_b_spatial, w_branch_a=m_w_branch_a, w_branch_b=m_w_branch_b,
             w_out=m_w_out, norm_ffn=m_norm_ffn, w_ffn_up=m_w_ffn_up, conv_ffn=m_conv_ffn, b_conv_ffn=m_b_conv_ffn,
             w_ffn_down=m_w_ffn_down, norm_ple=m_norm_ple, w_ple_gate=m_w_ple_gate, w_ple_proj=m_w_ple_proj, norm_final=m_norm_final)
    v = dict(norm_mix=v_norm_mix, w_in=v_w_in, conv_qkv=v_conv_qkv, a_log=v_a_log, dt_bias=v_dt_bias, head_norm=v_head_norm,
             sgu_norm=v_sgu_norm, w_spatial=v_w_spatial, b_spatial=v_b_spatial, w_branch_a=v_w_branch_a, w_branch_b=v_w_branch_b,
             w_out=v_w_out, norm_ffn=v_norm_ffn, w_ffn_up=v_w_ffn_up, conv_ffn=v_conv_ffn, b_conv_ffn=v_b_conv_ffn,
             w_ffn_down=v_w_ffn_down, norm_ple=v_norm_ple, w_ple_gate=v_w_ple_gate, w_ple_proj=v_w_ple_proj, norm_final=v_norm_final)
    chip = 2 * lax.axis_index("x") + lax.axis_index("y")
    core = lax.axis_index("c")

    def by_chip(own, gathered):
        return [jnp.where(chip == j, own, gathered[j]) for j in range(N_CHIPS)]

    conv_buf = _flat([w[k] for k in SMALL_COLS], f32)
    (conv_all,) = _gather_shards([conv_buf], "gather_conv")
    sm = {k: w[k] for k in SMALL_REPL}
    conv_parts = [_unflat(b, [w[k].shape for k in SMALL_COLS]) for b in by_chip(conv_buf, conv_all)]
    for n, k in enumerate(SMALL_COLS):
        sm[k] = jnp.concatenate([conv_parts[j][n] for j in range(N_CHIPS)], axis=-1)
    Ss = [_small_params(sm, i) for i in range(DEPTH)]
    def start_gather(i, behind):
        own = [w[k][i].astype(bf16) for k in BIG]
        own[-1] = own[-1] + behind.astype(bf16)
        return _gather_start(own, f"gather_start_l{i}")

    h = x[0]
    saved, Ws = [], []
    gather = start_gather(0, jnp.zeros((), f32))
    for i in range(DEPTH):
        own, lands = _exchange_wait(gather, gather["token"] if i == 0 else h, f"gather_wait_l{i}")
        S_i = Ss[i]
        if i + 1 < DEPTH:
            gather = start_gather(i + 1, own[-1][0, 0] * 0)
            S_i = dict(S_i, norm_mix=S_i["norm_mix"] + gather["token"][0, 0])
        lands = _gather_forward(lands, f"gather_forward_l{i}")
        Ws.append(_big_weights({k: by_chip(o, g) for k, o, g in zip(BIG, own, lands)}))
        h, sv = _layer_fwd(h, p[i, 0], Ws[i], S_i, i)
        saved.append(sv)
    loss_row, dh, g_final = _loss_head(h, sm["norm_final"][None], loss_target[0], "loss_head")

    gsm = {k: [None] * DEPTH for k in SMALL_REPL + SMALL_COLS if k != "norm_final"}
    gbig = {k: [None] * DEPTH for k in BIG}
    pending = None

    def finish(job, after):
        j, handle = job
        for k, g in zip(BIG, _rs_end(handle, after, BIG, f"l{j}")):
            gbig[k][j] = g

    for i in reversed(range(DEPTH)):
        S_i = Ss[i] if pending is None else dict(Ss[i], norm_ple=Ss[i]["norm_ple"] + pending[1]["token"][0, 0])
        dh, G = _layer_bwd(dh, saved[i], Ws[i], S_i, i)
        parts = _grads_by_chip(G)
        begun = (i, _rs_begin([parts[k] for k in BIG], BIG, f"l{i}"))
        if pending is not None:
            finish(pending, dh)
        pending = begun
        for k in gsm:
            gsm[k][i] = G[k].reshape(w[k].shape[1:-1] + (-1,)) if k in SMALL_COLS else G[k].reshape(w[k].shape[1:])

    small_names = [k for k in SMALL_REPL + SMALL_COLS if k != "norm_final"]
    small_local = [jnp.stack(gsm[k]) for k in small_names] + [g_final.reshape(-1), loss_row[0, :1]]
    small_shapes = [a.shape for a in small_local]
    small_buf = _allreduce_small(_flat(small_local, f32), "allreduce_small")

    two = lambda a: a.reshape(-1, a.shape[-1])
    big_out = {k: None for k in BIG}

    def update_layer(j):
        for k in BIG:
            big_out[k] = _adamw_layer(two(w[k]), gbig[k][j], two(m[k]), two(v[k]), big_out[k], j, f"adamw_{k}_l{j}")

    for j in reversed(range(1, DEPTH)):
        update_layer(j)
    finish(pending, big_out[BIG[-1]][1])
    update_layer(0)
    small_sum = _unflat(small_buf, small_shapes)
    gs = dict(zip(small_names + ["norm_final"], small_sum[:-1]))
    loss = small_sum[-1][0]
    for k in SMALL_COLS:
        n = w[k].shape[-1]
        gs[k] = _pick(chip, [gs[k][..., j * n:(j + 1) * n] for j in range(N_CHIPS)])

    grads, deltas, new_m, new_v = {}, {}, {}, {}
    for k in BIG:
        grads[k], deltas[k], new_m[k], new_v[k] = [a.reshape(w[k].shape) for a in big_out[k]]
    small_all = [k for k in WEIGHTS if k not in BIG]
    shapes = [w[k].shape for k in small_all]
    d, m2, v2 = _ew(_adamw, [_flat([w[k] for k in small_all], f32), _flat([gs[k] for k in small_all], f32),
                             _flat([m[k] for k in small_all], f32), _flat([v[k] for k in small_all], f32)],
                    [f32, f32, f32], "adamw_small")
    for k, a, b, c_ in zip(small_all, _unflat(d, shapes), _unflat(m2, shapes), _unflat(v2, shapes)):
        grads[k], deltas[k], new_m[k], new_v[k] = gs[k], a, b, c_
    return (loss, dh[None], *[grads[k] for k in WEIGHTS], *[deltas[k] for k in WEIGHTS],
            *[new_m[k] for k in WEIGHTS], *[new_v[k] for k in WEIGHTS])
```

```python
import functools
import math

import jax
import jax.numpy as jnp
from jax import lax
from jax.experimental import pallas as pl
from jax.experimental.pallas import tpu as pltpu

f32 = jnp.float32
bf16 = jnp.bfloat16
HI = lax.Precision.HIGHEST
SDS = jax.ShapeDtypeStruct

D_MODEL = 2048
DEPTH = 4
HEAD_DIM = 128
N_QK = 8
N_V = 16
QK_W = N_QK * HEAD_DIM
V_W = N_V * HEAD_DIM
CHUNK_A = 64
N_GROUPS = 8
GROUP_DIM = 128
WIDTH_B = N_GROUPS * GROUP_DIM
CHUNK_B = 128
D_FF = 5632
PLE_DIM = 256
EPS = 1e-6
N_IN = 12320
ADAM_LR, ADAM_B1, ADAM_B2, ADAM_EPS, ADAM_WD, ADAM_STEP = 0.001, 0.9, 0.999, 1e-08, 0.01, 10

C_Q, C_K, C_V, C_Z, C_UB, C_VB, C_GA, C_GB, PM = 0, 1024, 2048, 4096, 6144, 7168, 8192, 10240, 12288
LANES = 128
VMEM_LIMIT_BYTES = 48 * 1024 * 1024
ROW_CHUNK = 256


def _cp(*sem):
    return pltpu.CompilerParams(dimension_semantics=sem if sem else None, vmem_limit_bytes=VMEM_LIMIT_BYTES)


def _dot(a, b, prec=None):
    return jnp.dot(a, b, preferred_element_type=f32, precision=prec)


def _dot_nt(a, b, prec=None):
    return lax.dot_general(a, b, (((1,), (1,)), ((), ())), preferred_element_type=f32, precision=prec)


def _dot_tn(a, b, prec=None):
    return lax.dot_general(a, b, (((0,), (0,)), ((), ())), preferred_element_type=f32, precision=prec)


def _b(x):
    return x.astype(bf16)


def _dot3(a, b, dims=(((1,), (0,)), ((), ()))):
    ah, bh = _b(a), _b(b)
    al, bl = _b(a - ah.astype(f32)), _b(b - bh.astype(f32))
    dg = lambda p, q: lax.dot_general(p, q, dims, preferred_element_type=f32)
    return dg(ah, bh) + (dg(ah, bl) + dg(al, bh))


_TN = (((0,), (0,)), ((), ()))


def _sig(x):
    return 1.0 / (1.0 + jnp.exp(-x))


def _silu(x):
    return x * _sig(x)


def _dsilu(x):
    s = _sig(x)
    return s * (1.0 + x * (1.0 - s))


_GELU_C = 0.7978845608028654
_GELU_A = 0.044715


def _gelu(x):
    return 0.5 * x * (1.0 + jnp.tanh(_GELU_C * (x + _GELU_A * x * x * x)))


def _dgelu(x):
    t = jnp.tanh(_GELU_C * (x + _GELU_A * x * x * x))
    return 0.5 * (1.0 + t) + 0.5 * x * (1.0 - t * t) * _GELU_C * (1.0 + 3.0 * _GELU_A * x * x)


def _softplus(x):
    return jnp.maximum(x, 0.0) + jnp.log(1.0 + jnp.exp(-jnp.abs(x)))


def _divisor(n, cap, mult):
    best = None
    d = mult
    while d <= min(n, cap):
        if n % d == 0:
            best = d
        d += mult
    return best if best is not None else n


def _rows(tr, c, j=0):
    return pl.BlockSpec((tr, c), lambda i, j=j: (i, j))


def _whole(shape):
    nd = len(shape)
    return pl.BlockSpec(shape, lambda *_: (0,) * nd)


def _mm(a, b, mode, name, out_dtype=f32, c=None, tm=1024, tn=1024, tk=2048):
    if mode == "nn":
        (M, K), (K2, N) = a.shape, b.shape
    elif mode == "nt":
        (M, K), (N, K2) = a.shape, b.shape
    else:
        (K, M), (K2, N) = a.shape, b.shape
    assert K == K2, (a.shape, b.shape, mode)
    tm = _divisor(M, tm, LANES if mode == "tn" else 16)
    tn = _divisor(N, tn, LANES)
    tk = _divisor(K, tk, LANES if mode != "tn" else 16)
    nk = K // tk
    if mode == "nn":
        a_spec = pl.BlockSpec((tm, tk), lambda i, j, k: (i, k))
        b_spec = pl.BlockSpec((tk, tn), lambda i, j, k: (k, j))
        dn = (((1,), (0,)), ((), ()))
    elif mode == "nt":
        a_spec = pl.BlockSpec((tm, tk), lambda i, j, k: (i, k))
        b_spec = pl.BlockSpec((tn, tk), lambda i, j, k: (j, k))
        dn = (((1,), (1,)), ((), ()))
    else:
        a_spec = pl.BlockSpec((tk, tm), lambda i, j, k: (k, i))
        b_spec = pl.BlockSpec((tk, tn), lambda i, j, k: (k, j))
        dn = (((0,), (0,)), ((), ()))
    o_spec = pl.BlockSpec((tm, tn), lambda i, j, k: (i, j))
    has_c = c is not None

    def body(*refs):
        a_ref, b_ref = refs[0], refs[1]
        c_ref = refs[2] if has_c else None
        o_ref = refs[3] if has_c else refs[2]
        part = lax.dot_general(_b(a_ref[...]), _b(b_ref[...]), dn, preferred_element_type=f32)
        if nk == 1:
            if has_c:
                part = part + c_ref[...]
            o_ref[...] = part.astype(o_ref.dtype)
        else:
            acc = refs[-1]
            k = pl.program_id(2)

            @pl.when(k == 0)
            def _():
                acc[...] = part

            @pl.when(k > 0)
            def _():
                acc[...] += part

            @pl.when(k == nk - 1)
            def _():
                r = acc[...]
                if has_c:
                    r = r + c_ref[...]
                o_ref[...] = r.astype(o_ref.dtype)

    ins = [a, b] + ([c] if has_c else [])
    in_specs = [a_spec, b_spec] + ([o_spec] if has_c else [])
    return pl.pallas_call(
        body, grid=(M // tm, N // tn, nk), in_specs=in_specs, out_specs=o_spec, out_shape=SDS((M, N), out_dtype),
        scratch_shapes=[pltpu.VMEM((tm, tn), f32)] if nk > 1 else [], name=name,
        compiler_params=_cp("parallel", "parallel", "arbitrary"),
    )(*ins)


def _ew(fn, ins, out_dtypes, name, tile_bytes=2 * 1024 * 1024):
    R, C = max((x.shape for x in ins), key=lambda s: s[0])
    n_in = len(ins)
    row_bytes = 4 * C * (len(ins) + len(out_dtypes))
    tr = _divisor(R, max(16, tile_bytes // row_bytes), 16)
    in_specs = [_rows(tr, C) if x.shape[0] == R else _whole((1, C)) for x in ins]

    def body(*refs):
        res = fn(*[r[...] for r in refs[:n_in]])
        for o_ref, r in zip(refs[n_in:], res):
            o_ref[...] = r.astype(o_ref.dtype)

    outs = pl.pallas_call(
        body, grid=(R // tr,), in_specs=in_specs, out_specs=[_rows(tr, C) for _ in out_dtypes],
        out_shape=[SDS((R, C), dt) for dt in out_dtypes], name=name, compiler_params=_cp("parallel"),
    )(*ins)
    return outs


def _rms_fwd(x, gain, name):
    T, Dm = x.shape
    tr = _divisor(T, 256, 16)

    def body(x_ref, g_ref, o_ref):
        xv = x_ref[...]
        r = lax.rsqrt(jnp.mean(xv * xv, axis=-1, keepdims=True) + EPS)
        o_ref[...] = (xv * r * g_ref[...]).astype(o_ref.dtype)

    return pl.pallas_call(
        body, grid=(T // tr,), in_specs=[_rows(tr, Dm), _whole((1, Dm))], out_specs=_rows(tr, Dm),
        out_shape=SDS((T, Dm), bf16), name=name, compiler_params=_cp("parallel"),
    )(x, gain)


def _rms_bwd(x, gain, dh, dres, name):
    T, Dm = x.shape
    tr = _divisor(T, 256, 16)

    def body(x_ref, g_ref, dh_ref, dres_ref, dx_ref, dg_ref):
        xv = x_ref[...]
        dhv = dh_ref[...]
        r = lax.rsqrt(jnp.mean(xv * xv, axis=-1, keepdims=True) + EPS)
        dy = dhv * g_ref[...]
        m = jnp.mean(dy * xv, axis=-1, keepdims=True)
        dx_ref[...] = dres_ref[...] + r * dy - xv * (r * r * r * m)
        part = jnp.sum(dhv * xv * r, axis=0, keepdims=True)

        @pl.when(pl.program_id(0) == 0)
        def _():
            dg_ref[...] = part

        @pl.when(pl.program_id(0) > 0)
        def _():
            dg_ref[...] += part

    return pl.pallas_call(
        body, grid=(T // tr,), in_specs=[_rows(tr, Dm), _whole((1, Dm)), _rows(tr, Dm), _rows(tr, Dm)],
        out_specs=[_rows(tr, Dm), _whole((1, Dm))], out_shape=[SDS((T, Dm), f32), SDS((1, Dm), f32)],
        name=name, compiler_params=_cp("arbitrary"),
    )(x, gain, dh, dres)


def _shift_down(cur, prev8, s):
    rolled = pltpu.roll(cur, s, 0)
    rp = pltpu.roll(prev8, s, 0)
    row8 = lax.broadcasted_iota(jnp.int32, prev8.shape, 0)
    first = jnp.where(row8 < s, rp, rolled[:8])
    return jnp.concatenate([first, rolled[8:]], axis=0)


def _shift_up(cur, next8, s):
    R = cur.shape[0]
    rolled = pltpu.roll(cur, R - s, 0)
    rn = pltpu.roll(next8, 8 - s, 0)
    row8 = lax.broadcasted_iota(jnp.int32, next8.shape, 0)
    last = jnp.where(row8 >= 8 - s, rn, rolled[R - 8:])
    return jnp.concatenate([rolled[: R - 8], last], axis=0)


def _prev8(ref, r0, ci):
    rows = ref[pl.ds(pl.multiple_of(jnp.maximum(r0 - 8, 0), 8), 8), :]
    return jnp.where(ci > 0, rows, 0.0)


def _next8(ref, r0, R, ci, n_chunks):
    start = jnp.minimum(r0 + R, (n_chunks - 1) * R + R - 8)
    rows = ref[pl.ds(pl.multiple_of(start, 8), 8), :]
    return jnp.where(ci < n_chunks - 1, rows, 0.0)


def _w_rows(w_ref, k):
    return [w_ref[j:j + 1, :] for j in range(k)]


def _causal_conv(cur, prev8, w, k):
    acc = cur * w[k - 1]
    for s in range(1, k):
        acc = acc + _shift_down(cur, prev8, s) * w[k - 1 - s]
    return acc


def _qkvconv_fwd(proj, convw, name):
    T = proj.shape[0]
    R = min(ROW_CHUNK, T)
    n_chunks = T // R
    n_blk = (2 * QK_W + V_W) // HEAD_DIM

    def body(x_ref, w_ref, o_ref):
        p = pl.program_id(0)
        is_qk = p < 2 * N_QK
        scale = jnp.where(p < N_QK, HEAD_DIM ** -0.5, 1.0).astype(f32)
        w = _w_rows(w_ref, 4)

        def chunk(ci, carry):
            r0 = pl.multiple_of(ci * R, R)
            cur = x_ref[pl.ds(r0, R), :]
            y = _silu(_causal_conv(cur, _prev8(x_ref, r0, ci), w, 4))
            ss = jnp.sum(y * y, axis=-1, keepdims=True)
            nrm = jnp.where(is_qk, lax.rsqrt(ss + EPS) * scale, 1.0)
            o_ref[pl.ds(r0, R), :] = y * nrm
            return carry

        lax.fori_loop(0, n_chunks, chunk, 0)

    return pl.pallas_call(
        body, grid=(n_blk,),
        in_specs=[pl.BlockSpec((T, HEAD_DIM), lambda p: (0, p)), pl.BlockSpec((4, HEAD_DIM), lambda p: (0, p))],
        out_specs=pl.BlockSpec((T, HEAD_DIM), lambda p: (0, p)), out_shape=SDS((T, n_blk * HEAD_DIM), f32),
        name=name, compiler_params=_cp("parallel"),
    )(proj, convw)


def _qkvconv_bwd(proj, convw, dqkv, dproj, name):
    T = proj.shape[0]
    R = min(ROW_CHUNK, T)
    n_chunks = T // R
    n_blk = (2 * QK_W + V_W) // HEAD_DIM

    def body(x_ref, w_ref, do_ref, dproj_ref, dx_ref, dw_ref, dc_sc):
        p = pl.program_id(0)
        is_qk = p < 2 * N_QK
        scale = jnp.where(p < N_QK, HEAD_DIM ** -0.5, 1.0).astype(f32)
        w = _w_rows(w_ref, 4)

        def phase1(ci, dw):
            r0 = pl.multiple_of(ci * R, R)
            cur = x_ref[pl.ds(r0, R), :]
            p8 = _prev8(x_ref, r0, ci)
            shifted = [cur] + [_shift_down(cur, p8, s) for s in range(1, 4)]
            c = shifted[0] * w[3]
            for s in range(1, 4):
                c = c + shifted[s] * w[3 - s]
            y = _silu(c)
            dout = do_ref[pl.ds(r0, R), :]
            n = lax.rsqrt(jnp.sum(y * y, axis=-1, keepdims=True) + EPS)
            dot_ = jnp.sum(dout * y, axis=-1, keepdims=True)
            dy = jnp.where(is_qk, scale * (n * dout - y * (n * n * n * dot_)), dout)
            dc = dy * _dsilu(c)
            dc_sc[pl.ds(r0, R), :] = dc
            return tuple(dw[j] + jnp.sum(dc * shifted[3 - j], axis=0, keepdims=True) for j in range(4))

        dw = lax.fori_loop(0, n_chunks, phase1, tuple(jnp.zeros((1, HEAD_DIM), f32) for _ in range(4)))
        for j in range(4):
            dw_ref[j:j + 1, :] = dw[j]

        def phase2(ci, carry):
            r0 = pl.multiple_of(ci * R, R)
            cur = dc_sc[pl.ds(r0, R), :]
            n8 = _next8(dc_sc, r0, R, ci, n_chunks)
            acc = cur * w[3]
            for s in range(1, 4):
                acc = acc + _shift_up(cur, n8, s) * w[3 - s]
            dx_ref[pl.ds(r0, R), :] = acc.astype(dx_ref.dtype)
            return carry

        lax.fori_loop(0, n_chunks, phase2, 0)

    col = lambda p: (0, p)
    return pl.pallas_call(
        body, grid=(n_blk,),
        in_specs=[pl.BlockSpec((T, HEAD_DIM), col), pl.BlockSpec((4, HEAD_DIM), col), pl.BlockSpec((T, HEAD_DIM), col), ANY],
        out_specs=[pl.BlockSpec((T, HEAD_DIM), col), pl.BlockSpec((4, HEAD_DIM), col)],
        out_shape=[SDS(dproj.shape, bf16), SDS((4, n_blk * HEAD_DIM), f32)], input_output_aliases={3: 0},
        scratch_shapes=[pltpu.VMEM((T, HEAD_DIM), f32)], name=name, compiler_params=_cp("parallel"),
    )(proj, convw, dqkv, dproj)


def _tri_masks(C):
    row = lax.broadcasted_iota(jnp.int32, (C, C), 0)
    col = lax.broadcasted_iota(jnp.int32, (C, C), 1)
    return row, col


def _lane_pick(blk, lane, idx):
    return jnp.sum(jnp.where(lane == idx, blk, 0.0), axis=1, keepdims=True)


def _gate_block(ba, ea, dtb, lane):
    sig = _sig(ba)
    gblk = -ea * _softplus(ba + dtb)
    return sig, gblk


def _head_decay(gam_all, rg_all, tot, lane, h, row, col):
    C = row.shape[0]
    gam_c = _lane_pick(gam_all, lane, N_V + h)
    rg_c = _lane_pick(rg_all, lane, N_V + h)
    lane1 = lax.broadcasted_iota(jnp.int32, (1, LANES), 1)
    tot_h = jnp.sum(jnp.where(lane1 == N_V + h, tot, 0.0), axis=1, keepdims=True)
    gcb = jnp.broadcast_to(gam_c, (C, C))
    dlt = gcb - gcb.T
    dm = jnp.where(row >= col, jnp.exp(jnp.minimum(dlt, 0.0)), 0.0)
    return gam_c, rg_c, tot_h, dm


def _delta_prep(qkv, ba, ea_row, dtb_row, name):
    T = qkv.shape[0]
    C = CHUNK_A
    N = T // C

    SUB = 16

    def body(q_ref, k_ref, v_ref, ba_ref, ea_ref, dtb_ref,
             qd_ref, kd_ref, u_ref, w_ref, attn_ref, tinv_ref, cd_ref, bg_ref, at_sc, t_sc, a_sc, rk_sc):
        row, col = _tri_masks(C)
        lane = lax.broadcasted_iota(jnp.int32, (C, LANES), 1)
        sig, gblk = _gate_block(ba_ref[...], ea_ref[...], dtb_ref[...], lane)
        bg_ref[...] = jnp.where(lane < N_V, sig, gblk)
        lower = (row >= col).astype(f32)
        upper_s = (col > row).astype(f32)
        eye = (row == col).astype(f32)
        same16 = (row >> 4) == (col >> 4)
        same32 = (row >> 5) == (col >> 5)
        gam_all = _dot(lower, gblk, HI)
        rg_all = _dot(upper_s, gblk, HI)
        tot = jnp.sum(gblk, axis=0, keepdims=True)
        for h in range(N_V):
            qk = pl.ds((h // 2) * HEAD_DIM, HEAD_DIM)
            hs = pl.ds(h * HEAD_DIM, HEAD_DIM)
            qh, kh, vh = q_ref[:, qk], k_ref[:, qk], v_ref[:, hs]
            beta_c = _lane_pick(sig, lane, h)
            gam_c, rg_c, tot_h, dm = _head_decay(gam_all, rg_all, tot, lane, h, row, col)
            kk = _dot_nt(_b(kh), _b(kh))
            a = jnp.where(row > col, beta_c * kk * dm, 0.0)
            a_sc[h] = a
            at_sc[h] = jnp.where(same16, a, 0.0).T
            t_sc[h] = eye
            eg = jnp.exp(gam_c)
            attn_ref[0, h] = (_dot_nt(_b(qh), _b(kh)) * dm).astype(attn_ref.dtype)
            qd_ref[:, hs] = (qh * eg).astype(qd_ref.dtype)
            kd_ref[:, hs] = (kh * jnp.exp(rg_c)).astype(kd_ref.dtype)
            cd_ref[0, h] = jnp.broadcast_to(jnp.exp(tot_h), (1, LANES))
            u_ref[:, hs] = vh * beta_c
            rk_sc[:, hs] = kh * (beta_c * eg)

        first_col = (row >> 4) << 4

        def fsub(i, carry):
            for h in range(N_V):
                t = t_sc[h]
                a_col = jnp.sum(jnp.where(col == first_col + i, at_sc[h], 0.0), axis=1, keepdims=True)
                prod = a_col * t
                sums = [jnp.sum(prod[b * SUB:(b + 1) * SUB], axis=0, keepdims=True) for b in range(C // SUB)]
                new = eye - jnp.concatenate([jnp.broadcast_to(s, (SUB, C)) for s in sums], axis=0)
                t_sc[h] = jnp.where(row - first_col == i, new, t)
            return carry

        lax.fori_loop(1, SUB, fsub, 0)
        pair16 = jnp.logical_and(same32, jnp.logical_not(same16))
        for h in range(N_V):
            at_sc[h] = _dot3(t_sc[h], jnp.where(pair16, a_sc[h], 0.0))
        for h in range(N_V):
            p16 = t_sc[h]
            t_sc[h] = p16 - _dot3(at_sc[h], p16)
        for h in range(N_V):
            at_sc[h] = _dot3(t_sc[h], jnp.where(same32, 0.0, a_sc[h]))
        for h in range(N_V):
            p32 = t_sc[h]
            tinv_ref[0, h] = p32 - _dot3(at_sc[h], p32)
        for h in range(N_V):
            hs = pl.ds(h * HEAD_DIM, HEAD_DIM)
            u_ref[:, hs] = _dot3(tinv_ref[0, h], u_ref[:, hs])
        for h in range(N_V):
            hs = pl.ds(h * HEAD_DIM, HEAD_DIM)
            w_ref[:, hs] = _dot3(tinv_ref[0, h], rk_sc[:, hs]).astype(w_ref.dtype)

    big = lambda n: (n, 0)
    return pl.pallas_call(
        body, grid=(N,),
        in_specs=[pl.BlockSpec((C, QK_W), lambda n: (n, 0)), pl.BlockSpec((C, QK_W), lambda n: (n, 1)),
                  pl.BlockSpec((C, V_W), lambda n: (n, 1)), pl.BlockSpec((C, LANES), big),
                  _whole((1, LANES)), _whole((1, LANES))],
        out_specs=[pl.BlockSpec((C, V_W), big)] * 4 + [
            pl.BlockSpec((1, N_V, C, C), lambda n: (n, 0, 0, 0)), pl.BlockSpec((1, N_V, C, C), lambda n: (n, 0, 0, 0)),
            pl.BlockSpec((1, N_V, 1, LANES), lambda n: (n, 0, 0, 0)), pl.BlockSpec((C, LANES), big)],
        out_shape=[SDS((T, V_W), bf16), SDS((T, V_W), bf16), SDS((T, V_W), f32), SDS((T, V_W), bf16),
                   SDS((N, N_V, C, C), bf16), SDS((N, N_V, C, C), f32), SDS((N, N_V, 1, LANES), f32), SDS((T, LANES), f32)],
        scratch_shapes=[pltpu.VMEM((N_V, C, C), f32)] * 3 + [pltpu.VMEM((C, V_W), f32)],
        name=name, compiler_params=_cp("parallel"),
    )(qkv, qkv, qkv, ba, ea_row, dtb_row)


def _delta_scan(qd, kd, u, w, attn, cd, name):
    T = qd.shape[0]
    C = CHUNK_A
    N = T // C

    def body(qd_ref, kd_ref, u_ref, w_ref, attn_ref, cd_ref, o_ref, s_ref, vn_ref, s_sc):
        @pl.when(pl.program_id(0) == 0)
        def _():
            s_sc[...] = jnp.zeros_like(s_sc)

        for h in range(N_V):
            hs = pl.ds(h * HEAD_DIM, HEAD_DIM)
            s = s_sc[h]
            s_ref[0, h] = s
            sb = _b(s)
            vn = u_ref[:, hs] - _dot(w_ref[:, hs], sb)
            vnb = _b(vn)
            vn_ref[:, hs] = vnb
            o_ref[:, hs] = _dot(qd_ref[:, hs], sb) + _dot(attn_ref[0, h], vnb)
            s_sc[h] = s * cd_ref[0, h] + _dot_tn(kd_ref[:, hs], vnb)

    blk = pl.BlockSpec((C, V_W), lambda n: (n, 0))
    per_head = lambda a, b: pl.BlockSpec((1, N_V, a, b), lambda n: (n, 0, 0, 0))
    return pl.pallas_call(
        body, grid=(N,), in_specs=[blk, blk, blk, blk, per_head(C, C), per_head(1, LANES)],
        out_specs=[blk, per_head(HEAD_DIM, HEAD_DIM), blk],
        out_shape=[SDS((T, V_W), f32), SDS((N, N_V, HEAD_DIM, HEAD_DIM), f32), SDS((T, V_W), bf16)],
        scratch_shapes=[pltpu.VMEM((N_V, HEAD_DIM, HEAD_DIM), f32)], name=name, compiler_params=_cp("arbitrary"),
    )(qd, kd, u, w, attn, cd)


def _delta_scan_bwd(do, qd, kd, w, attn, cd, s_all, vn, name):
    T = qd.shape[0]
    C = CHUNK_A
    N = T // C

    def body(do_ref, qd_ref, kd_ref, w_ref, attn_ref, cd_ref, s_ref, vn_ref,
             dqd_ref, dkd_ref, du_ref, dw_ref, dattn_ref, dcd_ref, ds_sc):
        @pl.when(pl.program_id(0) == 0)
        def _():
            ds_sc[...] = jnp.zeros_like(ds_sc)

        row, col = _tri_masks(C)
        for h in range(N_V):
            hs = pl.ds(h * HEAD_DIM, HEAD_DIM)
            dsn = ds_sc[h]
            s = s_ref[0, h]
            dob, sb, dsb = _b(do_ref[:, hs]), _b(s), _b(dsn)
            vnb = vn_ref[:, hs]
            dqd_ref[:, hs] = _dot_nt(dob, sb)
            dattn_ref[0, h] = jnp.where(row >= col, _dot_nt(dob, vnb), 0.0)
            dvn = _dot_tn(attn_ref[0, h], dob) + _dot(kd_ref[:, hs], dsb)
            dvnb = _b(dvn)
            dkd_ref[:, hs] = _dot_nt(vnb, dsb)
            dcd = jnp.sum(jnp.sum(s * dsn, axis=1, keepdims=True), axis=0, keepdims=True)
            dcd_ref[0, h] = jnp.broadcast_to(dcd, (1, LANES))
            du_ref[:, hs] = dvn
            dw_ref[:, hs] = -_dot_nt(dvnb, sb)
            ds_sc[h] = dsn * cd_ref[0, h] + _dot_tn(qd_ref[:, hs], dob) - _dot_tn(w_ref[:, hs], dvnb)

    blk = pl.BlockSpec((C, V_W), lambda n: (N - 1 - n, 0))
    per_head = lambda a, b: pl.BlockSpec((1, N_V, a, b), lambda n: (N - 1 - n, 0, 0, 0))
    return pl.pallas_call(
        body, grid=(N,),
        in_specs=[blk, blk, blk, blk, per_head(C, C), per_head(1, LANES), per_head(HEAD_DIM, HEAD_DIM), blk],
        out_specs=[blk, blk, blk, blk, per_head(C, C), per_head(1, LANES)],
        out_shape=[SDS((T, V_W), f32)] * 4 + [SDS((N, N_V, C, C), f32), SDS((N, N_V, 1, LANES), f32)],
        scratch_shapes=[pltpu.VMEM((N_V, HEAD_DIM, HEAD_DIM), f32)], name=name, compiler_params=_cp("arbitrary"),
    )(do, qd, kd, w, attn, cd, s_all, vn)


def _delta_prep_bwd(qkv, ba, ea_row, dtb_row, tinv, u, w, dqd, dkd, du, dw, dattn, dcd, name):
    T = qkv.shape[0]
    C = CHUNK_A
    N = T // C

    def body(q_ref, k_ref, v_ref, ba_ref, ea_ref, dtb_ref, tinv_ref, u_ref, w_ref,
             dqd_ref, dkd_ref, du_ref, dw_ref, dattn_ref, dcd_ref,
             dqkv_ref, dba_ref, dalog_ref, ddtb_ref, drv_sc, drk_sc, da_sc):
        row, col = _tri_masks(C)
        lane = lax.broadcasted_iota(jnp.int32, (C, LANES), 1)
        rowc = lax.broadcasted_iota(jnp.int32, (C, 1), 0)
        for h in range(N_V):
            hs = pl.ds(h * HEAD_DIM, HEAD_DIM)
            drv_sc[:, hs] = _dot3(tinv_ref[0, h], du_ref[:, hs], _TN)
        for h in range(N_V):
            hs = pl.ds(h * HEAD_DIM, HEAD_DIM)
            drk_sc[:, hs] = _dot3(tinv_ref[0, h], dw_ref[:, hs], _TN)
        for h in range(N_V):
            hs = pl.ds(h * HEAD_DIM, HEAD_DIM)
            da_sc[h] = -jnp.where(row > col, _dot_nt(_b(drv_sc[:, hs]), _b(u_ref[:, hs]))
                                  + _dot_nt(_b(drk_sc[:, hs]), w_ref[:, hs]), 0.0)
        ba_v, ea, dtb = ba_ref[...], ea_ref[...], dtb_ref[...]
        sig, gblk = _gate_block(ba_v, ea, dtb, lane)
        lower = (row >= col).astype(f32)
        upper_s = (col > row).astype(f32)
        upper = (col >= row).astype(f32)
        gam_all = _dot(lower, gblk, HI)
        rg_all = _dot(upper_s, gblk, HI)
        tot = jnp.sum(gblk, axis=0, keepdims=True)
        dbeta_blk = jnp.zeros((C, LANES), f32)
        dgam_blk = jnp.zeros((C, LANES), f32)
        for j in range(N_QK):
            qk = pl.ds(j * HEAD_DIM, HEAD_DIM)
            qh, kh = q_ref[:, qk], k_ref[:, qk]
            qhb, khb = _b(qh), _b(kh)
            kk = _dot_nt(khb, khb)
            qkm = _dot_nt(qhb, khb)
            dq_j = jnp.zeros((C, HEAD_DIM), f32)
            dk_j = jnp.zeros((C, HEAD_DIM), f32)
            for h in (2 * j, 2 * j + 1):
                hs = pl.ds(h * HEAD_DIM, HEAD_DIM)
                vh = v_ref[:, hs]
                beta_c = _lane_pick(sig, lane, h)
                gam_c, rg_c, tot_h, dm = _head_decay(gam_all, rg_all, tot, lane, h, row, col)
                eg, er, cdh = jnp.exp(gam_c), jnp.exp(rg_c), jnp.exp(tot_h)
                d_rv, d_rk, da = drv_sc[:, hs], drk_sc[:, hs], da_sc[h]
                dqkv_ref[:, pl.ds(2 * QK_W + h * HEAD_DIM, HEAD_DIM)] = beta_c * d_rv
                dbeta = jnp.sum(d_rv * vh + d_rk * (eg * kh), axis=1, keepdims=True)
                dk_h = (beta_c * eg) * d_rk
                d_eg = jnp.sum(d_rk * kh, axis=1, keepdims=True) * beta_c
                bkd = da * dm
                dbeta = dbeta + jnp.sum(bkd * kk, axis=1, keepdims=True)
                dkk = bkd * beta_c
                ddm = da * beta_c * kk
                dattn_h = dattn_ref[0, h]
                dqk = dattn_h * dm
                ddm = ddm + dattn_h * qkm
                dqd_h, dkd_h = dqd_ref[:, hs], dkd_ref[:, hs]
                dq_j = dq_j + _dot(_b(dqk), khb) + eg * dqd_h
                dk_h = dk_h + _dot_tn(_b(dqk), qhb) + _dot(_b(dkk + dkk.T), khb) + er * dkd_h
                dk_j = dk_j + dk_h
                d_eg = d_eg + jnp.sum(dqd_h * qh, axis=1, keepdims=True)
                d_er = jnp.sum(dkd_h * kh, axis=1, keepdims=True)
                e = ddm * dm
                dgam = jnp.sum(e, axis=1, keepdims=True) - jnp.sum(e.T, axis=1, keepdims=True)
                dgam = dgam + d_eg * eg - d_er * er
                extra = jnp.sum(d_er * er, axis=0, keepdims=True) + jnp.max(dcd_ref[0, h], axis=1, keepdims=True) * cdh
                dgam = dgam + jnp.where(rowc == C - 1, extra, 0.0)
                dbeta_blk = jnp.where(lane == h, dbeta, dbeta_blk)
                dgam_blk = jnp.where(lane == N_V + h, dgam, dgam_blk)
            dqkv_ref[:, qk] = dq_j
            dqkv_ref[:, pl.ds(QK_W + j * HEAD_DIM, HEAD_DIM)] = dk_j
        dg_all = _dot(upper, dgam_blk, HI)
        dsp = dg_all * (-ea) * _sig(ba_v + dtb)
        dba_ref[...] = jnp.where(lane < N_V, dbeta_blk * sig * (1.0 - sig), dsp)
        part_alog = jnp.sum(dg_all * gblk, axis=0, keepdims=True)
        part_dtb = jnp.sum(dsp, axis=0, keepdims=True)

        @pl.when(pl.program_id(0) == 0)
        def _():
            dalog_ref[...] = part_alog
            ddtb_ref[...] = part_dtb

        @pl.when(pl.program_id(0) > 0)
        def _():
            dalog_ref[...] += part_alog
            ddtb_ref[...] += part_dtb

    big = lambda n: (n, 0)
    wide = pl.BlockSpec((C, V_W), big)
    sq = pl.BlockSpec((1, N_V, C, C), lambda n: (n, 0, 0, 0))
    return pl.pallas_call(
        body, grid=(N,),
        in_specs=[pl.BlockSpec((C, QK_W), lambda n: (n, 0)), pl.BlockSpec((C, QK_W), lambda n: (n, 1)),
                  pl.BlockSpec((C, V_W), lambda n: (n, 1)), pl.BlockSpec((C, LANES), big),
                  _whole((1, LANES)), _whole((1, LANES)), sq, wide, wide, wide, wide, wide, wide, sq,
                  pl.BlockSpec((1, N_V, 1, LANES), lambda n: (n, 0, 0, 0))],
        out_specs=[pl.BlockSpec((C, 2 * QK_W + V_W), big), pl.BlockSpec((C, LANES), big),
                   _whole((1, LANES)), _whole((1, LANES))],
        out_shape=[SDS((T, 2 * QK_W + V_W), f32), SDS((T, LANES), f32), SDS((1, LANES), f32), SDS((1, LANES), f32)],
        scratch_shapes=[pltpu.VMEM((C, V_W), f32), pltpu.VMEM((C, V_W), f32), pltpu.VMEM((N_V, C, C), f32)],
        name=name, compiler_params=_cp("arbitrary"),
    )(qkv, qkv, qkv, ba, ea_row, dtb_row, tinv, u, w, dqd, dkd, du, dw, dattn, dcd)


def _onorm_fwd(o, proj, hg, name):
    T = o.shape[0]
    tr = _divisor(T, 256, 16)

    def body(o_ref, z_ref, g_ref, out_ref):
        g = g_ref[...]
        for h in range(N_V):
            hs = pl.ds(h * HEAD_DIM, HEAD_DIM)
            oh = o_ref[:, hs]
            r = lax.rsqrt(jnp.mean(oh * oh, axis=-1, keepdims=True) + EPS)
            out_ref[:, hs] = (oh * r * g * _silu(z_ref[:, hs])).astype(out_ref.dtype)

    return pl.pallas_call(
        body, grid=(T // tr,), in_specs=[_rows(tr, V_W), _rows(tr, V_W, C_Z // V_W), _whole((1, HEAD_DIM))],
        out_specs=_rows(tr, V_W), out_shape=SDS((T, V_W), bf16), name=name, compiler_params=_cp("parallel"),
    )(o, proj, hg)


def _onorm_bwd(don, o, proj, hg, dproj, name):
    T = o.shape[0]
    tr = _divisor(T, 256, 16)

    def body(don_ref, o_ref, z_ref, g_ref, dproj_ref, do_ref, dz_ref, dg_ref):
        g = g_ref[...]
        dg = jnp.zeros((1, HEAD_DIM), f32)
        for h in range(N_V):
            hs = pl.ds(h * HEAD_DIM, HEAD_DIM)
            oh, zh, dh = o_ref[:, hs], z_ref[:, hs], don_ref[:, hs]
            r = lax.rsqrt(jnp.mean(oh * oh, axis=-1, keepdims=True) + EPS)
            d_n = dh * _silu(zh)
            dz_ref[:, hs] = (dh * (oh * r * g) * _dsilu(zh)).astype(dz_ref.dtype)
            dy = d_n * g
            m = jnp.mean(dy * oh, axis=-1, keepdims=True)
            do_ref[:, hs] = r * dy - oh * (r * r * r * m)
            dg = dg + jnp.sum(d_n * oh * r, axis=0, keepdims=True)

        @pl.when(pl.program_id(0) == 0)
        def _():
            dg_ref[...] = dg

        @pl.when(pl.program_id(0) > 0)
        def _():
            dg_ref[...] += dg

    return pl.pallas_call(
        body, grid=(T // tr,),
        in_specs=[_rows(tr, V_W), _rows(tr, V_W), _rows(tr, V_W, C_Z // V_W), _whole((1, HEAD_DIM)), ANY],
        out_specs=[_rows(tr, V_W), _rows(tr, V_W, C_Z // V_W), _whole((1, HEAD_DIM))],
        out_shape=[SDS((T, V_W), f32), SDS(dproj.shape, bf16), SDS((1, HEAD_DIM), f32)], input_output_aliases={4: 1},
        name=name, compiler_params=_cp("arbitrary"),
    )(don, o, proj, hg, dproj)


def _sgu_parts(ub, vb, gain):
    gv = _gelu(vb)
    r = lax.rsqrt(jnp.mean(gv * gv, axis=-1, keepdims=True) + EPS)
    return _gelu(ub), gv, r, gv * r * gain


def _sgu_fwd(proj, gain, w_s, b_bc, name):
    T = proj.shape[0]
    C = CHUNK_B

    def body(ub_ref, vb_ref, g_ref, w_ref, b_ref, o_ref):
        row, col = _tri_masks(C)
        u, _, _, vn = _sgu_parts(ub_ref[...], vb_ref[...], g_ref[...])
        for g in range(N_GROUPS):
            gs = pl.ds(g * GROUP_DIM, GROUP_DIM)
            wg = jnp.where(row >= col, w_ref[g], 0.0)
            mixed = _dot(_b(wg), _b(vn[:, g * GROUP_DIM:(g + 1) * GROUP_DIM])) + b_ref[g]
            o_ref[:, gs] = (u[:, g * GROUP_DIM:(g + 1) * GROUP_DIM] * mixed).astype(o_ref.dtype)

    return pl.pallas_call(
        body, grid=(T // C,),
        in_specs=[_rows(C, WIDTH_B, C_UB // WIDTH_B), _rows(C, WIDTH_B, C_VB // WIDTH_B), _whole((1, WIDTH_B)),
                  _whole((N_GROUPS, C, C)), _whole((N_GROUPS, C, GROUP_DIM))],
        out_specs=_rows(C, WIDTH_B), out_shape=SDS((T, WIDTH_B), bf16), name=name, compiler_params=_cp("parallel"),
    )(proj, proj, gain, w_s, b_bc)


def _sgu_bwd(dsgu, proj, gain, w_s, b_bc, dproj, name):
    T = proj.shape[0]
    C = CHUNK_B

    def body(d_ref, ub_ref, vb_ref, g_ref, w_ref, b_ref, dproj_ref, duv_ref, dw_ref, db_ref, dg_ref):
        dub_ref = duv_ref.at[:, pl.ds(0, WIDTH_B)]
        dvb_ref = duv_ref.at[:, pl.ds(WIDTH_B, WIDTH_B)]
        first = pl.program_id(0) == 0
        row, col = _tri_masks(C)
        ub, vb, gain_v = ub_ref[...], vb_ref[...], g_ref[...]
        u, gv, r, vn = _sgu_parts(ub, vb, gain_v)
        d = d_ref[...]
        dvn_parts = []
        for g in range(N_GROUPS):
            sl = slice(g * GROUP_DIM, (g + 1) * GROUP_DIM)
            wg = jnp.where(row >= col, w_ref[g], 0.0)
            vng = _b(vn[:, sl])
            mixed = _dot(_b(wg), vng) + b_ref[g]
            dub_ref[:, pl.ds(g * GROUP_DIM, GROUP_DIM)] = (d[:, sl] * mixed * _dgelu(ub[:, sl])).astype(dub_ref.dtype)
            dmix = d[:, sl] * u[:, sl]
            dmb = _b(dmix)
            dwg = jnp.where(row >= col, _dot_nt(dmb, vng), 0.0)
            dbg = jnp.sum(dmix, axis=1, keepdims=True)

            @pl.when(first)
            def _():
                dw_ref[g] = dwg
                db_ref[g] = dbg

            @pl.when(jnp.logical_not(first))
            def _():
                dw_ref[g] += dwg
                db_ref[g] += dbg

            dvn_parts.append(_dot_tn(_b(wg), dmb))
        dvn = jnp.concatenate(dvn_parts, axis=1)
        dy = dvn * gain_v
        m = jnp.mean(dy * gv, axis=-1, keepdims=True)
        dgv = r * dy - gv * (r * r * r * m)
        dvb_ref[...] = (dgv * _dgelu(vb)).astype(dvb_ref.dtype)
        dgain = jnp.sum(dvn * gv * r, axis=0, keepdims=True)

        @pl.when(first)
        def _():
            dg_ref[...] = dgain

        @pl.when(jnp.logical_not(first))
        def _():
            dg_ref[...] += dgain

    return pl.pallas_call(
        body, grid=(T // C,),
        in_specs=[_rows(C, WIDTH_B), _rows(C, WIDTH_B, C_UB // WIDTH_B), _rows(C, WIDTH_B, C_VB // WIDTH_B),
                  _whole((1, WIDTH_B)), _whole((N_GROUPS, C, C)), _whole((N_GROUPS, C, GROUP_DIM)), ANY],
        out_specs=[_rows(C, 2 * WIDTH_B, C_UB // (2 * WIDTH_B)), _whole((N_GROUPS, C, C)), _whole((N_GROUPS, C, 1)),
                   _whole((1, WIDTH_B))],
        out_shape=[SDS(dproj.shape, bf16), SDS((N_GROUPS, C, C), f32), SDS((N_GROUPS, C, 1), f32), SDS((1, WIDTH_B), f32)],
        input_output_aliases={6: 0}, name=name, compiler_params=_cp("arbitrary"),
    )(dsgu, proj, proj, gain, w_s, b_bc, dproj)


def _merge_fwd(proj, ya, yb, name):
    T = proj.shape[0]
    tr = _divisor(T, 256, 16)

    def body(ga_ref, gb_ref, ya_ref, yb_ref, o_ref):
        o_ref[...] = (_sig(ga_ref[...]) * ya_ref[...] + _sig(gb_ref[...]) * yb_ref[...]).astype(o_ref.dtype)

    return pl.pallas_call(
        body, grid=(T // tr,),
        in_specs=[_rows(tr, D_MODEL, C_GA // D_MODEL), _rows(tr, D_MODEL, C_GB // D_MODEL), _rows(tr, D_MODEL), _rows(tr, D_MODEL)],
        out_specs=_rows(tr, D_MODEL), out_shape=SDS((T, D_MODEL), bf16), name=name, compiler_params=_cp("parallel"),
    )(proj, proj, ya, yb)


def _merge_bwd(dm, proj, ya, yb, name):
    T = proj.shape[0]
    tr = _divisor(T, 256, 16)

    def body(dm_ref, ga_ref, gb_ref, ya_ref, yb_ref, dya_ref, dyb_ref, dg_ref):
        d = dm_ref[...]
        sa, sb = _sig(ga_ref[...]), _sig(gb_ref[...])
        dya_ref[...] = (d * sa).astype(bf16)
        dyb_ref[...] = (d * sb).astype(bf16)
        dg_ref[:, :D_MODEL] = (d * ya_ref[...] * sa * (1.0 - sa)).astype(bf16)
        dg_ref[:, D_MODEL:] = (d * yb_ref[...] * sb * (1.0 - sb)).astype(bf16)

    return pl.pallas_call(
        body, grid=(T // tr,),
        in_specs=[_rows(tr, D_MODEL), _rows(tr, D_MODEL, C_GA // D_MODEL), _rows(tr, D_MODEL, C_GB // D_MODEL),
                  _rows(tr, D_MODEL), _rows(tr, D_MODEL)],
        out_specs=[_rows(tr, D_MODEL), _rows(tr, D_MODEL), _rows(tr, 2 * D_MODEL, C_GA // (2 * D_MODEL))],
        out_shape=[SDS((T, D_MODEL), bf16), SDS((T, D_MODEL), bf16), SDS((T, PM), bf16)], name=name,
        compiler_params=_cp("parallel"),
    )(dm, proj, proj, ya, yb)


def _ffnconv_fwd(upg, upv, wg, wv, bg, bv, name):
    T, F = upg.shape
    R = min(ROW_CHUNK, T)
    n_chunks = T // R

    def body(g_ref, v_ref, wg_ref, wv_ref, bg_ref, bv_ref, o_ref):
        wgv, wvv = _w_rows(wg_ref, 3), _w_rows(wv_ref, 3)

        def chunk(ci, carry):
            r0 = pl.multiple_of(ci * R, R)
            cg = _causal_conv(g_ref[pl.ds(r0, R), :], _prev8(g_ref, r0, ci), wgv, 3) + bg_ref[...]
            cv = _causal_conv(v_ref[pl.ds(r0, R), :], _prev8(v_ref, r0, ci), wvv, 3) + bv_ref[...]
            o_ref[pl.ds(r0, R), :] = (_silu(cg) * cv).astype(o_ref.dtype)
            return carry

        lax.fori_loop(0, n_chunks, chunk, 0)

    col = lambda p: (0, p)
    return pl.pallas_call(
        body, grid=(F // LANES,),
        in_specs=[pl.BlockSpec((T, LANES), col)] * 2 + [pl.BlockSpec((3, LANES), col)] * 2 + [pl.BlockSpec((1, LANES), col)] * 2,
        out_specs=pl.BlockSpec((T, LANES), col), out_shape=SDS((T, F), bf16), name=name, compiler_params=_cp("parallel"),
    )(upg, upv, wg, wv, bg, bv)


def _ffnconv_bwd(dact, upg, upv, wg, wv, bg, bv, name):
    T, F = upg.shape
    R = min(ROW_CHUNK, T)
    n_chunks = T // R

    def body(d_ref, g_ref, v_ref, wg_ref, wv_ref, bg_ref, bv_ref,
             dg_ref, dv_ref, dwg_ref, dwv_ref, dbg_ref, dbv_ref, sg, sv):
        wgv, wvv = _w_rows(wg_ref, 3), _w_rows(wv_ref, 3)

        def phase1(ci, carry):
            dwg, dwv = carry
            r0 = pl.multiple_of(ci * R, R)
            gcur, vcur = g_ref[pl.ds(r0, R), :], v_ref[pl.ds(r0, R), :]
            gp, vp = _prev8(g_ref, r0, ci), _prev8(v_ref, r0, ci)
            gsh = [gcur] + [_shift_down(gcur, gp, s) for s in (1, 2)]
            vsh = [vcur] + [_shift_down(vcur, vp, s) for s in (1, 2)]
            cg = gsh[0] * wgv[2] + gsh[1] * wgv[1] + gsh[2] * wgv[0] + bg_ref[...]
            cv = vsh[0] * wvv[2] + vsh[1] * wvv[1] + vsh[2] * wvv[0] + bv_ref[...]
            d = d_ref[pl.ds(r0, R), :]
            dcv = d * _silu(cg)
            dcg = d * cv * _dsilu(cg)
            sg[pl.ds(r0, R), :] = dcg
            sv[pl.ds(r0, R), :] = dcv
            rg = [jnp.sum(dcg * gsh[2 - j], axis=0, keepdims=True) for j in range(3)] + [jnp.sum(dcg, axis=0, keepdims=True)]
            rv = [jnp.sum(dcv * vsh[2 - j], axis=0, keepdims=True) for j in range(3)] + [jnp.sum(dcv, axis=0, keepdims=True)]
            return tuple(a + b for a, b in zip(dwg, rg)), tuple(a + b for a, b in zip(dwv, rv))

        z4 = tuple(jnp.zeros((1, LANES), f32) for _ in range(4))
        dwg, dwv = lax.fori_loop(0, n_chunks, phase1, (z4, z4))
        for j in range(3):
            dwg_ref[j:j + 1, :] = dwg[j]
            dwv_ref[j:j + 1, :] = dwv[j]
        dbg_ref[...] = dwg[3]
        dbv_ref[...] = dwv[3]

        def phase2(ci, carry):
            r0 = pl.multiple_of(ci * R, R)
            for sc, wv_, out in ((sg, wgv, dg_ref), (sv, wvv, dv_ref)):
                cur = sc[pl.ds(r0, R), :]
                n8 = _next8(sc, r0, R, ci, n_chunks)
                acc = cur * wv_[2] + _shift_up(cur, n8, 1) * wv_[1] + _shift_up(cur, n8, 2) * wv_[0]
                out[pl.ds(r0, R), :] = acc.astype(out.dtype)
            return carry

        lax.fori_loop(0, n_chunks, phase2, 0)

    col = lambda p: (0, p)
    big, w3, b1 = pl.BlockSpec((T, LANES), col), pl.BlockSpec((3, LANES), col), pl.BlockSpec((1, LANES), col)
    return pl.pallas_call(
        body, grid=(F // LANES,), in_specs=[big, big, big, w3, w3, b1, b1], out_specs=[big, big, w3, w3, b1, b1],
        out_shape=[SDS((T, F), bf16), SDS((T, F), bf16), SDS((3, F), f32), SDS((3, F), f32), SDS((1, F), f32), SDS((1, F), f32)],
        scratch_shapes=[pltpu.VMEM((T, LANES), f32), pltpu.VMEM((T, LANES), f32)], name=name, compiler_params=_cp("parallel"),
    )(dact, upg, upv, wg, wv, bg, bv)


def _loss_head(x, gain, target, name):
    T, Dm = x.shape
    tr = _divisor(T, 256, 16)

    def body(x_ref, g_ref, t_ref, l_ref, dx_ref, dg_ref):
        xv, g = x_ref[...], g_ref[...]
        r = lax.rsqrt(jnp.mean(xv * xv, axis=-1, keepdims=True) + EPS)
        err = xv * r * g - t_ref[...]
        part_l = 0.5 * jnp.sum(jnp.mean(err * err, axis=-1, keepdims=True), axis=0, keepdims=True)
        dy = err * (1.0 / Dm)
        dyg = dy * g
        m = jnp.mean(dyg * xv, axis=-1, keepdims=True)
        dx_ref[...] = r * dyg - xv * (r * r * r * m)
        part_g = jnp.sum(dy * xv * r, axis=0, keepdims=True)
        part_l = jnp.broadcast_to(part_l, (1, LANES))

        @pl.when(pl.program_id(0) == 0)
        def _():
            l_ref[...] = part_l
            dg_ref[...] = part_g

        @pl.when(pl.program_id(0) > 0)
        def _():
            l_ref[...] += part_l
            dg_ref[...] += part_g

    return pl.pallas_call(
        body, grid=(T // tr,), in_specs=[_rows(tr, Dm), _whole((1, Dm)), _rows(tr, Dm)],
        out_specs=[_whole((1, LANES)), _rows(tr, Dm), _whole((1, Dm))],
        out_shape=[SDS((1, LANES), f32), SDS((T, Dm), f32), SDS((1, Dm), f32)], name=name, compiler_params=_cp("arbitrary"),
    )(x, gain, target)


def _lane_row(vec, offset):
    return jnp.pad(vec.astype(f32), (offset, LANES - offset - vec.shape[0]))[None]


def _layer_fwd(x, p_i, W, S, li):
    nm = lambda s: f"{s}_l{li}"
    sv = {"x0": x}
    h1 = _rms_fwd(x, S["norm_mix"], nm("rms_mix"))
    proj = _mm(h1, W["in_main"], "nn", nm("proj_main"))
    ba = _mm(h1, W["in_ba"], "nn", nm("proj_ba"))
    qkv = _qkvconv_fwd(proj, S["conv_qkv"], nm("qkvconv"))
    qd, kd, u, w, attn, tinv, cd, _ = _delta_prep(qkv, ba, S["ea_row"], S["dtb_row"], nm("delta_prep"))
    o, s_all, vn = _delta_scan(qd, kd, u, w, attn, cd, nm("delta_scan"))
    on = _onorm_fwd(o, proj, S["head_norm"], nm("onorm"))
    ya = _mm(on, W["branch_a"], "nn", nm("branch_a"))
    sgu = _sgu_fwd(proj, S["sgu_norm"], S["w_spatial"], S["b_bc"], nm("sgu"))
    yb = _mm(sgu, W["branch_b"], "nn", nm("branch_b"))
    merged = _merge_fwd(proj, ya, yb, nm("merge"))
    x1 = _mm(merged, W["out"], "nn", nm("out_proj"), c=x)
    h2 = _rms_fwd(x1, S["norm_ffn"], nm("rms_ffn"))
    upg = _mm(h2, W["up_g"], "nn", nm("ffn_up_g"))
    upv = _mm(h2, W["up_v"], "nn", nm("ffn_up_v"))
    act = _ffnconv_fwd(upg, upv, S["conv_g"], S["conv_v"], S["bias_g"], S["bias_v"], nm("ffnconv"))
    x2 = _mm(act, W["down"], "nn", nm("ffn_down"), c=x1, tk=2816)
    h3 = _rms_fwd(x2, S["norm_ple"], nm("rms_ple"))
    gl = _mm(h3, W["ple_gate"], "nn", nm("ple_gate"))
    pp = _mm(p_i, W["ple_proj"], "nn", nm("ple_proj"))
    (x3,) = _ew(lambda a, g, q: (a + _sig(g) * q,), [x2, gl, pp], [f32], nm("ple_mix"))
    sv.update(h1=h1, proj=proj, ba=ba, qkv=qkv, qd=qd, kd=kd, u=u, w=w, attn=attn, tinv=tinv, cd=cd, o=o, s_all=s_all,
              vn=vn, on=on, ya=ya, sgu=sgu, yb=yb, merged=merged, x1=x1, h2=h2, upg=upg, upv=upv, act=act, x2=x2,
              h3=h3, gl=gl, pp=pp, p=p_i)
    return x3, sv


def _layer_bwd(dx3, sv, W, S, li):
    nm = lambda s: f"{s}_l{li}"
    G = {}
    dgl, dpp = _ew(lambda d, g, q: ((lambda s: (d * q * s * (1.0 - s), d * s))(_sig(g))),
                   [dx3, sv["gl"], sv["pp"]], [bf16, bf16], nm("ple_mix_bwd"))
    G["w_ple_gate"] = _mm(sv["h3"], dgl, "tn", nm("d_ple_gate"))
    G["w_ple_proj"] = _mm(sv["p"], dpp, "tn", nm("d_ple_proj"))
    dh3 = _mm(dgl, W["ple_gate"], "nt", nm("dh_ple"))
    dx2, G["norm_ple"] = _rms_bwd(sv["x2"], S["norm_ple"], dh3, dx3, nm("rms_ple_bwd"))
    dact = _mm(dx2, W["down"], "nt", nm("d_act"))
    G["w_ffn_down"] = _mm(sv["act"], dx2, "tn", nm("d_ffn_down"))
    dupg, dupv, dcg, dcv, dbg, dbv = _ffnconv_bwd(dact, sv["upg"], sv["upv"], S["conv_g"], S["conv_v"], S["bias_g"],
                                                  S["bias_v"], nm("ffnconv_bwd"))
    G["conv_ffn"] = jnp.concatenate([dcg, dcv], axis=1)
    G["b_conv_ffn"] = jnp.concatenate([dbg, dbv], axis=1)
    G["up_g"] = _mm(sv["h2"], dupg, "tn", nm("d_ffn_up_g"))
    G["up_v"] = _mm(sv["h2"], dupv, "tn", nm("d_ffn_up_v"))
    dh2 = _mm(dupg, W["up_g"], "nt", nm("dh_ffn_g"), tk=2816)
    dh2 = _mm(dupv, W["up_v"], "nt", nm("dh_ffn_v"), c=dh2, tk=2816)
    dx1, G["norm_ffn"] = _rms_bwd(sv["x1"], S["norm_ffn"], dh2, dx2, nm("rms_ffn_bwd"))
    dmerged = _mm(dx1, W["out"], "nt", nm("d_merged"))
    G["w_out"] = _mm(sv["merged"], dx1, "tn", nm("d_w_out"))
    dya, dyb, dproj = _merge_bwd(dmerged, sv["proj"], sv["ya"], sv["yb"], nm("merge_bwd"))
    G["w_branch_a"] = _mm(sv["on"], dya, "tn", nm("d_branch_a"))
    G["w_branch_b"] = _mm(sv["sgu"], dyb, "tn", nm("d_branch_b"))
    don = _mm(dya, W["branch_a"], "nt", nm("d_on"))
    dsgu = _mm(dyb, W["branch_b"], "nt", nm("d_sgu"))
    dproj, G["w_spatial"], db_s, G["sgu_norm"] = _sgu_bwd(dsgu, sv["proj"], S["sgu_norm"], S["w_spatial"], S["b_bc"], dproj,
                                                          nm("sgu_bwd"))
    G["b_spatial"] = db_s.reshape(N_GROUPS, CHUNK_B)
    do, dproj, G["head_norm"] = _onorm_bwd(don, sv["o"], sv["proj"], S["head_norm"], dproj, nm("onorm_bwd"))
    dqd, dkd, du, dw, dattn, dcd = _delta_scan_bwd(do, sv["qd"], sv["kd"], sv["w"], sv["attn"], sv["cd"], sv["s_all"], sv["vn"],
                                                   nm("delta_scan_bwd"))
    dqkv_n, dba, dalog_row, ddtb_row = _delta_prep_bwd(sv["qkv"], sv["ba"], S["ea_row"], S["dtb_row"], sv["tinv"], sv["u"], sv["w"],
                                                       dqd, dkd, du, dw, dattn, dcd, nm("delta_prep_bwd"))
    G["a_log"] = dalog_row[0, N_V:2 * N_V]
    G["dt_bias"] = ddtb_row[0, N_V:2 * N_V]
    dproj, G["conv_qkv"] = _qkvconv_bwd(sv["proj"], S["conv_qkv"], dqkv_n, dproj, nm("qkvconv_bwd"))
    G["in_main"] = _mm(sv["h1"], dproj, "tn", nm("d_in_main"))
    G["in_ba"] = _mm(sv["h1"], dba, "tn", nm("d_in_ba"))
    dh1 = _mm(dba, W["in_ba"], "nt", nm("dh_mix_ba"))
    dh1 = _mm(dproj, W["in_main"], "nt", nm("dh_mix"), c=dh1)
    dx0, G["norm_mix"] = _rms_bwd(sv["x0"], S["norm_mix"], dh1, dx1, nm("rms_mix_bwd"))
    return dx0, G


_BA0 = C_UB
_BA1 = C_UB + 2 * N_V


def _cols(segments, lo, hi):
    out = []
    for start, a in segments:
        s_lo, s_hi = max(lo, start), min(hi, start + a.shape[1])
        if s_lo < s_hi:
            out.append(a[:, s_lo - start:s_hi - start])
    return out


def _big_weights(sh):
    n_in = N_IN // N_CHIPS
    w_in = [(j * n_in, a) for j, a in enumerate(sh["w_in"])]
    rows = lambda k: jnp.concatenate(sh[k], axis=0)
    cols = lambda parts: jnp.concatenate(parts, axis=1)
    ba = jnp.pad(cols(_cols(w_in, _BA0, _BA1)), ((0, 0), (0, LANES - 2 * N_V)))
    return dict(
        in_main=cols(_cols(w_in, 0, _BA0) + _cols(w_in, _BA1, N_IN)), in_ba=ba,
        branch_a=rows("w_branch_a"), branch_b=cols(sh["w_branch_b"]), out=rows("w_out"),
        up_g=cols(sh["w_ffn_up"][:2]), up_v=cols(sh["w_ffn_up"][2:]), down=rows("w_ffn_down"),
        ple_gate=rows("w_ple_gate"), ple_proj=cols(sh["w_ple_proj"]))


def _grads_by_chip(G):
    n_in = N_IN // N_CHIPS
    w_in = [(0, G["in_main"][:, :_BA0]), (_BA0, G["in_ba"][:, :2 * N_V]), (_BA1, G["in_main"][:, _BA0:])]
    split = lambda g: jnp.stack(jnp.split(g, N_CHIPS, axis=1))
    by_rows = lambda g: g.reshape(N_CHIPS, -1, g.shape[1])
    return dict(
        w_in=jnp.stack([jnp.concatenate(_cols(w_in, j * n_in, (j + 1) * n_in), axis=1) for j in range(N_CHIPS)]),
        w_branch_a=by_rows(G["w_branch_a"]), w_branch_b=split(G["w_branch_b"]), w_out=by_rows(G["w_out"]),
        w_ffn_up=jnp.stack(jnp.split(G["up_g"], 2, axis=1) + jnp.split(G["up_v"], 2, axis=1)),
        w_ffn_down=by_rows(G["w_ffn_down"]), w_ple_gate=by_rows(G["w_ple_gate"]), w_ple_proj=split(G["w_ple_proj"]))


def _small_params(sm, i):
    return dict(
        norm_mix=sm["norm_mix"][i][None], conv_qkv=sm["conv_qkv"][i], ea_row=_lane_row(jnp.exp(sm["a_log"][i]), N_V),
        dtb_row=_lane_row(sm["dt_bias"][i], N_V), head_norm=sm["head_norm"][i][None], sgu_norm=sm["sgu_norm"][i][None],
        w_spatial=sm["w_spatial"][i],
        b_bc=jnp.broadcast_to(sm["b_spatial"][i][:, :, None], (N_GROUPS, CHUNK_B, GROUP_DIM)),
        norm_ffn=sm["norm_ffn"][i][None], conv_g=sm["conv_ffn"][i][:, :D_FF], conv_v=sm["conv_ffn"][i][:, D_FF:],
        bias_g=sm["b_conv_ffn"][i][None, :D_FF], bias_v=sm["b_conv_ffn"][i][None, D_FF:], norm_ple=sm["norm_ple"][i][None])


MESH = pl.DeviceIdType.MESH
N_CHIPS = 4
ANY = pl.BlockSpec(memory_space=pl.ANY)
ROW_ALIGN = 32


def _place():
    x, y, c = lax.axis_index("x"), lax.axis_index("y"), lax.axis_index("c")
    chips = [(1 - x, y), (x, 1 - y), (1 - x, 1 - y)]
    return x, y, c, 2 * x + y, chips, (x, y, 1 - c)


def _halves(rows, c):
    h = rows // 2
    return pl.ds(pl.multiple_of(c * h, 16), h), pl.ds(pl.multiple_of((1 - c) * h, 16), h)


def _remote(src, dst, send_sems, recv_sems, k, dev):
    return pltpu.make_async_remote_copy(src_ref=src, dst_ref=dst, send_sem=send_sems.at[k], recv_sem=recv_sems.at[k],
                                        device_id=dev, device_id_type=MESH)


def _gather_shards(bufs, name):
    n = len(bufs)

    def body(*refs):
        ins, outs = refs[:n], refs[n:2 * n]
        send_sems, recv_sems = refs[2 * n:]
        x, y, c, me, chips, sib = _place()
        for i in range(n):
            mine, _ = _halves(ins[i].shape[0], c)
            for k, (px, py) in enumerate(chips):
                _remote(ins[i].at[mine], outs[i].at[me, mine], send_sems, recv_sems, 6 * i + k, (px, py, c)).start()
        for i in range(n):
            mine, _ = _halves(ins[i].shape[0], c)
            for k, (px, py) in enumerate(chips):
                pc = 2 * px + py
                _remote(ins[i].at[mine], outs[i].at[pc, mine], send_sems, recv_sems, 6 * i + k, (px, py, c)).wait_recv()
                _remote(outs[i].at[pc, mine], outs[i].at[pc, mine], send_sems, recv_sems, 6 * i + 3 + k, sib).start()
        for i in range(n):
            mine, other = _halves(ins[i].shape[0], c)
            for k, (px, py) in enumerate(chips):
                pc = 2 * px + py
                _remote(outs[i].at[pc, mine], outs[i].at[pc, other], send_sems, recv_sems, 6 * i + 3 + k, sib).wait_recv()
        for i in range(n):
            mine, _ = _halves(ins[i].shape[0], c)
            for k, (px, py) in enumerate(chips):
                pc = 2 * px + py
                _remote(ins[i].at[mine], outs[i].at[me, mine], send_sems, recv_sems, 6 * i + k, (px, py, c)).wait_send()
                _remote(outs[i].at[pc, mine], outs[i].at[pc, mine], send_sems, recv_sems, 6 * i + 3 + k, sib).wait_send()

    return pl.pallas_call(
        body, in_specs=[ANY] * n, out_specs=[ANY] * n,
        out_shape=[SDS((N_CHIPS,) + b.shape, b.dtype) for b in bufs],
        scratch_shapes=[pltpu.SemaphoreType.DMA((6 * n,)), pltpu.SemaphoreType.DMA((6 * n,))],
        name=name,
    )(*bufs)


def _exchange(srcs_of, out_shapes, n_sems, name):
    n = len(out_shapes)

    def body(*refs):
        n_in = len(refs) - n - 2
        ins, outs = refs[:n_in], refs[n_in:n_in + n]
        send_sems, recv_sems = refs[-2:]
        copies = [_remote(s, d, send_sems, recv_sems, k, dev) for s, d, k, dev in srcs_of(ins, outs, _place())]
        for cp in copies:
            cp.start()
        for cp in copies:
            cp.wait()

    def call(*arrs):
        return pl.pallas_call(
            body, in_specs=[ANY] * len(arrs), out_specs=[ANY] * n, out_shape=out_shapes,
            scratch_shapes=[pltpu.SemaphoreType.DMA((n_sems,)), pltpu.SemaphoreType.DMA((n_sems,))], name=name,
        )(*arrs)

    return call


HBM = pl.BlockSpec(memory_space=pltpu.HBM)
SEM = pl.BlockSpec(memory_space=pltpu.SEMAPHORE)
_SIDE_EFFECT = pltpu.SideEffectType.DATAFLOW_SIDE_EFFECTING


def _exchange_start(copies_of, srcs, land_shapes, n_sems, name):
    ns, nl = len(srcs), len(land_shapes)

    def body(*refs):
        ins = refs[:ns + nl]
        send_sems, recv_sems = refs[ns + nl], refs[ns + nl + 1]
        token = refs[-1]
        for s, d, k, dev in copies_of(ins[:ns], ins[ns:], _place()):
            _remote(s, d, send_sems, recv_sems, k, dev).start()
        token[...] = jnp.zeros_like(token)

    lands = [lax.empty(s.shape, s.dtype) for s in land_shapes]
    operands = [pltpu.with_memory_space_constraint(a, pltpu.HBM) for a in list(srcs) + lands]
    outs = pl.pallas_call(
        body, name=name,
        out_shape=(pltpu.SemaphoreType.DMA((n_sems,)), pltpu.SemaphoreType.DMA((n_sems,)),
                   *[pltpu.HBM(a.shape, a.dtype) for a in operands], SDS((8, LANES), f32)),
        in_specs=[HBM] * (ns + nl), out_specs=(SEM, SEM, *[HBM] * (ns + nl), pl.BlockSpec(memory_space=pltpu.VMEM)),
        input_output_aliases={i: 2 + i for i in range(ns + nl)},
        compiler_params=pltpu.CompilerParams(has_side_effects=_SIDE_EFFECT),
    )(*operands)
    return dict(send=outs[0], recv=outs[1], bufs=list(outs[2:2 + ns + nl]), token=outs[-1], ns=ns, copies_of=copies_of)


def _exchange_wait(handle, after, name):
    ns, bufs, copies_of = handle["ns"], handle["bufs"], handle["copies_of"]
    nb = len(bufs)

    def body(*refs):
        ins = refs[:nb]
        send_sems, recv_sems = refs[nb], refs[nb + 1]
        for s, d, k, dev in copies_of(ins[:ns], ins[ns:], _place()):
            cp = _remote(s, d, send_sems, recv_sems, k, dev)
            cp.wait_send()
            cp.wait_recv()

    outs = pl.pallas_call(
        body, name=name, out_shape=tuple(pltpu.HBM(a.shape, a.dtype) for a in bufs),
        in_specs=[HBM] * nb + [SEM, SEM, ANY], out_specs=tuple([HBM] * nb),
        input_output_aliases={i: i for i in range(nb)},
        compiler_params=pltpu.CompilerParams(has_side_effects=_SIDE_EFFECT),
    )(*bufs, handle["send"], handle["recv"], after)
    return list(outs[:ns]), list(outs[ns:])


def _pick(idx, parts):
    out = parts[-1]
    for j in reversed(range(len(parts) - 1)):
        out = jnp.where(idx == j, parts[j], out)
    return out


def _rs_pair(gs, name):
    def copies(ins, outs, place):
        x, y, c, me, chips, sib = place
        return [(g.at[:, _halves(g.shape[1], c)[1]], o, i, sib) for i, (g, o) in enumerate(zip(ins, outs))]

    shapes = [SDS((N_CHIPS, g.shape[1] // 2, g.shape[2]), g.dtype) for g in gs]
    return _exchange(copies, shapes, len(gs), name)(*gs)


def _gather_copies(ins, outs, place):
    x, y, c, me, chips, sib = place
    out = []
    for i, (s, o) in enumerate(zip(ins, outs)):
        mine, _ = _halves(s.shape[0], c)
        out += [(s.at[mine], o.at[me, mine], 3 * i + k, (px, py, c)) for k, (px, py) in enumerate(chips)]
    return out


def _gather_start(shards, name):
    return _exchange_start(_gather_copies, shards, [SDS((N_CHIPS,) + s.shape, s.dtype) for s in shards], 3 * len(shards), name)


def _gather_forward(lands, name):
    n = len(lands)

    def body(*refs):
        ins, outs = refs[:n], refs[n:2 * n]
        send_sems, recv_sems = refs[2 * n:]
        x, y, c, me, chips, sib = _place()
        copies = []
        for i in range(n):
            mine, _ = _halves(ins[i].shape[1], c)
            for k, (px, py) in enumerate(chips):
                pc = 2 * px + py
                copies.append(_remote(ins[i].at[pc, mine], outs[i].at[pc, mine], send_sems, recv_sems, 3 * i + k, sib))
        for cp in copies:
            cp.start()
        for cp in copies:
            cp.wait()

    return pl.pallas_call(
        body, in_specs=[ANY] * n, out_specs=[ANY] * n, out_shape=[SDS(a.shape, a.dtype) for a in lands],
        input_output_aliases={i: i for i in range(n)},
        scratch_shapes=[pltpu.SemaphoreType.DMA((3 * n,)), pltpu.SemaphoreType.DMA((3 * n,))], name=name,
    )(*lands)


def _rs_chips_copies(ins, outs, place):
    x, y, c, me, chips, sib = place
    return [(p.at[2 * px + py], o.at[k], 3 * i + k, (px, py, c))
            for i, (p, o) in enumerate(zip(ins, outs)) for k, (px, py) in enumerate(chips)]


def _rs_tile(h, n_cols):
    return _divisor(h, max(16, (1 << 20) // (4 * n_cols)), 16)


def _rs_add2(g, got, name):
    _, K, Nc = g.shape
    h = K // 2
    tr = _rs_tile(h, Nc)
    nb = h // tr

    def body(g_ref, got_ref, o_ref):
        o_ref[...] = (g_ref[...] + got_ref[...]).astype(o_ref.dtype)

    blk = lambda j, i: (j, i, 0)
    return pl.pallas_call(
        body, name=name, out_shape=SDS((N_CHIPS, h, Nc), bf16), grid=(N_CHIPS, nb),
        in_specs=[pl.BlockSpec((1, tr, Nc), lambda j, i: (j, lax.axis_index("c") * nb + i, 0)), pl.BlockSpec((1, tr, Nc), blk)],
        out_specs=pl.BlockSpec((1, tr, Nc), blk), compiler_params=_cp("parallel", "parallel"),
    )(g, got)


def _rs_add4(p, parts, name):
    _, h, Nc = p.shape
    tr = _rs_tile(h, Nc)
    nb = h // tr

    def body(own_ref, a_ref, b_ref, c_ref, o_ref):
        up = lambda r: r[0].astype(f32)
        o_ref[...] = (up(own_ref) + up(b_ref)) + (up(a_ref) + up(c_ref))

    part = lambda k: pl.BlockSpec((1, tr, Nc), lambda i, k=k: (k, i, 0))
    me = lambda: 2 * lax.axis_index("x") + lax.axis_index("y")
    return pl.pallas_call(
        body, name=name, out_shape=SDS((2 * h, Nc), f32), grid=(nb,),
        in_specs=[pl.BlockSpec((1, tr, Nc), lambda i: (me(), i, 0)), part(0), part(1), part(2)],
        out_specs=pl.BlockSpec((tr, Nc), lambda i: (lax.axis_index("c") * nb + i, 0)), compiler_params=_cp("parallel"),
    )(p, parts, parts, parts)


def _rs_join(fs, name):
    n = len(fs)

    def body(*refs):
        ins, outs = refs[:n], refs[n:2 * n]
        send_sems, recv_sems = refs[2 * n:]
        x, y, c, me, chips, sib = _place()
        copies = []
        for i in range(n):
            mine, _ = _halves(ins[i].shape[0], c)
            copies.append(_remote(ins[i].at[mine], outs[i].at[mine], send_sems, recv_sems, i, sib))
        for cp in copies:
            cp.start()
        for cp in copies:
            cp.wait()

    return pl.pallas_call(
        body, in_specs=[ANY] * n, out_specs=[ANY] * n, out_shape=[SDS(a.shape, a.dtype) for a in fs],
        input_output_aliases={i: i for i in range(n)},
        scratch_shapes=[pltpu.SemaphoreType.DMA((n,)), pltpu.SemaphoreType.DMA((n,))], name=name,
    )(*fs)


def _rs_begin(gs, names, tag):
    gots = _rs_pair(gs, f"rs_pair_{tag}")
    ps = [_rs_add2(g, got, f"rs_add2_{k}_{tag}") for g, got, k in zip(gs, gots, names)]
    lands = [SDS((3,) + p.shape[1:], p.dtype) for p in ps]
    return _exchange_start(_rs_chips_copies, ps, lands, 3 * len(ps), f"rs_chips_start_{tag}")


def _rs_end(handle, after, names, tag):
    ps, parts = _exchange_wait(handle, after, f"rs_chips_wait_{tag}")
    fs = [_rs_add4(p, part, f"rs_add4_{k}_{tag}") for p, part, k in zip(ps, parts, names)]
    return _rs_join(fs, f"rs_join_{tag}")


def _allreduce_small(buf, name):
    R, L = buf.shape

    def body(x_ref, o_ref, r0, s1, r1, send_sems, recv_sems):
        x, y, c, me, chips, sib = _place()
        cp = _remote(x_ref, r0, send_sems, recv_sems, 0, sib)
        cp.start()
        cp.wait()
        s1[...] = x_ref[...] + r0[...]
        cps = []
        for k, (px, py) in enumerate(chips):
            cp = _remote(s1, r1.at[k], send_sems, recv_sems, 1 + k, (px, py, c))
            cp.start()
            cps.append(cp)
        for cp in cps:
            cp.wait()
        o_ref[...] = (s1[...] + r1[1]) + (r1[0] + r1[2])

    vm = pl.BlockSpec(memory_space=pltpu.VMEM)
    return pl.pallas_call(
        body, in_specs=[vm], out_specs=vm, out_shape=SDS((R, L), f32),
        scratch_shapes=[pltpu.VMEM((R, L), f32), pltpu.VMEM((R, L), f32), pltpu.VMEM((3, R, L), f32),
                        pltpu.SemaphoreType.DMA((4,)), pltpu.SemaphoreType.DMA((4,))],
        name=name, compiler_params=pltpu.CompilerParams(vmem_limit_bytes=VMEM_LIMIT_BYTES),
    )(buf)


BIG = ("w_in", "w_branch_a", "w_branch_b", "w_out", "w_ffn_up", "w_ffn_down", "w_ple_gate", "w_ple_proj")
SMALL_REPL = ("norm_mix", "a_log", "dt_bias", "head_norm", "sgu_norm", "w_spatial", "b_spatial", "norm_ffn", "b_conv_ffn",
              "norm_ple", "norm_final")
SMALL_COLS = ("conv_qkv", "conv_ffn")
WEIGHTS = ("norm_mix", "w_in", "conv_qkv", "a_log", "dt_bias", "head_norm", "sgu_norm", "w_spatial", "b_spatial", "w_branch_a",
           "w_branch_b", "w_out", "norm_ffn", "w_ffn_up", "conv_ffn", "b_conv_ffn", "w_ffn_down", "norm_ple", "w_ple_gate",
           "w_ple_proj", "norm_final")


def _flat(arrs, dtype):
    cat = jnp.concatenate([a.astype(dtype).reshape(-1) for a in arrs])
    unit = LANES * ROW_ALIGN
    cat = jnp.pad(cat, (0, -cat.shape[0] % unit))
    return cat.reshape(-1, LANES)


def _unflat(buf, shapes):
    flat = buf.reshape(-1)
    out, off = [], 0
    for s in shapes:
        n = math.prod(s)
        out.append(flat[off:off + n].reshape(s))
        off += n
    return out


def _adamw(w, g, m, v):
    m2 = ADAM_B1 * m + (1.0 - ADAM_B1) * g
    v2 = ADAM_B2 * v + (1.0 - ADAM_B2) * (g * g)
    m_hat = m2 * (1.0 / (1.0 - ADAM_B1 ** ADAM_STEP))
    v_hat = v2 * (1.0 / (1.0 - ADAM_B2 ** ADAM_STEP))
    delta = -ADAM_LR * (m_hat / (jnp.sqrt(v_hat) + ADAM_EPS) + ADAM_WD * w)
    return delta, m2, v2


def _adamw_layer(w, g, m, v, prev, behind, layer, name):
    K, Nc = g.shape
    tr = _divisor(K, max(16, (1 << 20) // (4 * Nc)), 16)
    nb = K // tr
    rows = pl.BlockSpec((tr, Nc), lambda i: (layer * nb + i, 0))
    n_prev = 0 if prev is None else 4

    def body(w_ref, g_ref, m_ref, v_ref, behind_ref, *refs):
        go_ref, d_ref, mo_ref, vo_ref = refs[n_prev:]
        gv = g_ref[...]
        d, m2, v2 = _adamw(w_ref[...], gv, m_ref[...], v_ref[...])
        go_ref[...] = gv
        d_ref[...] = d
        mo_ref[...] = m2
        vo_ref[...] = v2

    return pl.pallas_call(
        body, grid=(nb,), in_specs=[rows, pl.BlockSpec((tr, Nc), lambda i: (i, 0)), rows, rows, ANY] + [ANY] * n_prev,
        out_specs=[rows] * 4, out_shape=[SDS(w.shape, f32)] * 4,
        input_output_aliases={5 + j: j for j in range(n_prev)}, name=name, compiler_params=_cp("parallel"),
    )(w, g, m, v, behind, *(prev or ()))


def kernel(x, p, norm_mix, w_in, conv_qkv, a_log, dt_bias, head_norm, sgu_norm, w_spatial, b_spatial, w_branch_a, w_branch_b, w_out, norm_ffn, w_ffn_up, conv_ffn, b_conv_ffn, w_ffn_down, norm_ple, w_ple_gate, w_ple_proj, norm_final, loss_target, m_norm_mix, m_w_in, m_conv_qkv, m_a_log, m_dt_bias, m_head_norm, m_sgu_norm, m_w_spatial, m_b_spatial, m_w_branch_a, m_w_branch_b, m_w_out, m_norm_ffn, m_w_ffn_up, m_conv_ffn, m_b_conv_ffn, m_w_ffn_down, m_norm_ple, m_w_ple_gate, m_w_ple_proj, m_norm_final, v_norm_mix, v_w_in, v_conv_qkv, v_a_log, v_dt_bias, v_head_norm, v_sgu_norm, v_w_spatial, v_b_spatial, v_w_branch_a, v_w_branch_b, v_w_out, v_norm_ffn, v_w_ffn_up, v_conv_ffn, v_b_conv_ffn, v_w_ffn_down, v_norm_ple, v_w_ple_gate, v_w_ple_proj, v_norm_final):
    w = dict(norm_mix=norm_mix, w_in=w_in, conv_qkv=conv_qkv, a_log=a_log, dt_bias=dt_bias, head_norm=head_norm, sgu_norm=sgu_norm,
             w_spatial=w_spatial, b_spatial=b_spatial, w_branch_a=w_branch_a, w_branch_b=w_branch_b, w_out=w_out, norm_ffn=norm_ffn,
             w_ffn_up=w_ffn_up, conv_ffn=conv_ffn, b_conv_ffn=b_conv_ffn, w_ffn_down=w_ffn_down, norm_ple=norm_ple,
             w_ple_gate=w_ple_gate, w_ple_proj=w_ple_proj, norm_final=norm_final)
    m = dict(norm_mix=m_norm_mix, w_in=m_w_in, conv_qkv=m_conv_qkv, a_log=m_a_log, dt_bias=m_dt_bias, head_norm=m_head_norm,
             sgu_norm=m_sgu_norm, w_spatial=m_w_spatial, b_spatial=m_b_spatial, w_branch_a=m_w_branch_a, w_branch_b=m_w_branch_b,
             w_out=m_w_out, norm_ffn=m_norm_ffn, w_ffn_up=m_w_ffn_up, conv_ffn=m_conv_ffn, b_conv_ffn=m_b_conv_ffn,
             w_ffn_down=m_w_ffn_down, norm_ple=m_norm_ple, w_ple_gate=m_w_ple_gate, w_ple_proj=m_w_ple_proj, norm_final=m_norm_final)
    v = dict(norm_mix=v_norm_mix, w_in=v_w_in, conv_qkv=v_conv_qkv, a_log=v_a_log, dt_bias=v_dt_bias, head_norm=v_head_norm,
             sgu_norm=v_sgu_norm, w_spatial=v_w_spatial, b_spatial=v_b_spatial, w_branch_a=v_w_branch_a, w_branch_b=v_w_branch_b,
             w_out=v_w_out, norm_ffn=v_norm_ffn, w_ffn_up=v_w_ffn_up, conv_ffn=v_conv_ffn, b_conv_ffn=v_b_conv_ffn,
             w_ffn_down=v_w_ffn_down, norm_ple=v_norm_ple, w_ple_gate=v_w_ple_gate, w_ple_proj=v_w_ple_proj, norm_final=v_norm_final)
    chip = 2 * lax.axis_index("x") + lax.axis_index("y")
    core = lax.axis_index("c")

    def by_chip(own, gathered):
        return [jnp.where(chip == j, own, gathered[j]) for j in range(N_CHIPS)]

    conv_buf = _flat([w[k] for k in SMALL_COLS], f32)
    (conv_all,) = _gather_shards([conv_buf], "gather_conv")
    sm = {k: w[k] for k in SMALL_REPL}
    conv_parts = [_unflat(b, [w[k].shape for k in SMALL_COLS]) for b in by_chip(conv_buf, conv_all)]
    for n, k in enumerate(SMALL_COLS):
        sm[k] = jnp.concatenate([conv_parts[j][n] for j in range(N_CHIPS)], axis=-1)
    Ss = [_small_params(sm, i) for i in range(DEPTH)]
    def start_gather(i, behind):
        own = [w[k][i].astype(bf16) for k in BIG]
        own[-1] = own[-1] + behind.astype(bf16)
        return _gather_start(own, f"gather_start_l{i}")

    h = x[0]
    saved, Ws = [], []
    gather = start_gather(0, jnp.zeros((), f32))
    for i in range(DEPTH):
        own, lands = _exchange_wait(gather, gather["token"] if i == 0 else h, f"gather_wait_l{i}")
        S_i = Ss[i]
        if i + 1 < DEPTH:
            gather = start_gather(i + 1, own[-1][0, 0] * 0)
            S_i = dict(S_i, norm_mix=S_i["norm_mix"] + gather["token"][0, 0])
        lands = _gather_forward(lands, f"gather_forward_l{i}")
        Ws.append(_big_weights({k: by_chip(o, g) for k, o, g in zip(BIG, own, lands)}))
        h, sv = _layer_fwd(h, p[i, 0], Ws[i], S_i, i)
        saved.append(sv)
    loss_row, dh, g_final = _loss_head(h, sm["norm_final"][None], loss_target[0], "loss_head")

    gsm = {k: [None] * DEPTH for k in SMALL_REPL + SMALL_COLS if k != "norm_final"}
    gbig = {k: [None] * DEPTH for k in BIG}
    pending = None

    def finish(job, after):
        j, handle = job
        for k, g in zip(BIG, _rs_end(handle, after, BIG, f"l{j}")):
            gbig[k][j] = g

    for i in reversed(range(DEPTH)):
        S_i = Ss[i] if pending is None else dict(Ss[i], norm_ple=Ss[i]["norm_ple"] + pending[1]["token"][0, 0])
        dh, G = _layer_bwd(dh, saved[i], Ws[i], S_i, i)
        parts = _grads_by_chip(G)
        begun = (i, _rs_begin([parts[k] for k in BIG], BIG, f"l{i}"))
        if pending is not None:
            finish(pending, dh)
        pending = begun
        for k in gsm:
            gsm[k][i] = G[k].reshape(w[k].shape[1:-1] + (-1,)) if k in SMALL_COLS else G[k].reshape(w[k].shape[1:])

    small_names = [k for k in SMALL_REPL + SMALL_COLS if k != "norm_final"]
    small_local = [jnp.stack(gsm[k]) for k in small_names] + [g_final.reshape(-1), loss_row[0, :1]]
    small_shapes = [a.shape for a in small_local]
    small_buf = _allreduce_small(_flat(small_local, f32), "allreduce_small")

    two = lambda a: a.reshape(-1, a.shape[-1])
    big_out = {k: None for k in BIG}

    def update_layer(j):
        for k in BIG:
            big_out[k] = _adamw_layer(two(w[k]), gbig[k][j], two(m[k]), two(v[k]), big_out[k], pending[1]["token"], j,
                                      f"adamw_{k}_l{j}")

    for j in reversed(range(1, DEPTH)):
        update_layer(j)
    finish(pending, big_out[BIG[-1]][1])
    update_layer(0)
    small_sum = _unflat(small_buf, small_shapes)
    gs = dict(zip(small_names + ["norm_final"], small_sum[:-1]))
    loss = small_sum[-1][0]
    for k in SMALL_COLS:
        n = w[k].shape[-1]
        gs[k] = _pick(chip, [gs[k][..., j * n:(j + 1) * n] for j in range(N_CHIPS)])

    grads, deltas, new_m, new_v = {}, {}, {}, {}
    for k in BIG:
        grads[k], deltas[k], new_m[k], new_v[k] = [a.reshape(w[k].shape) for a in big_out[k]]
    small_all = [k for k in WEIGHTS if k not in BIG]
    shapes = [w[k].shape for k in small_all]
    d, m2, v2 = _ew(_adamw, [_flat([w[k] for k in small_all], f32), _flat([gs[k] for k in small_all], f32),
                             _flat([m[k] for k in small_all], f32), _flat([v[k] for k in small_all], f32)],
                    [f32, f32, f32], "adamw_small")
    for k, a, b, c_ in zip(small_all, _unflat(d, shapes), _unflat(m2, shapes), _unflat(v2, shapes)):
        grads[k], deltas[k], new_m[k], new_v[k] = gs[k], a, b, c_
    return (loss, dh[None], *[grads[k] for k in WEIGHTS], *[deltas[k] for k in WEIGHTS],
            *[new_m[k] for k in WEIGHTS], *[new_v[k] for k in WEIGHTS])
```

```python
import functools
import math

import jax
import jax.numpy as jnp
from jax import lax
from jax.experimental import pallas as pl
from jax.experimental.pallas import tpu as pltpu

f32 = jnp.float32
bf16 = jnp.bfloat16
HI = lax.Precision.HIGHEST
SDS = jax.ShapeDtypeStruct

D_MODEL = 2048
DEPTH = 4
HEAD_DIM = 128
N_QK = 8
N_V = 16
QK_W = N_QK * HEAD_DIM
V_W = N_V * HEAD_DIM
CHUNK_A = 64
N_GROUPS = 8
GROUP_DIM = 128
WIDTH_B = N_GROUPS * GROUP_DIM
CHUNK_B = 128
D_FF = 5632
PLE_DIM = 256
EPS = 1e-6
N_IN = 12320
ADAM_LR, ADAM_B1, ADAM_B2, ADAM_EPS, ADAM_WD, ADAM_STEP = 0.001, 0.9, 0.999, 1e-08, 0.01, 10

C_Q, C_K, C_V, C_Z, C_UB, C_VB, C_GA, C_GB, PM = 0, 1024, 2048, 4096, 6144, 7168, 8192, 10240, 12288
LANES = 128
VMEM_LIMIT_BYTES = 48 * 1024 * 1024
ROW_CHUNK = 256


def _cp(*sem):
    return pltpu.CompilerParams(dimension_semantics=sem if sem else None, vmem_limit_bytes=VMEM_LIMIT_BYTES)


def _dot(a, b, prec=None):
    return jnp.dot(a, b, preferred_element_type=f32, precision=prec)


def _dot_nt(a, b, prec=None):
    return lax.dot_general(a, b, (((1,), (1,)), ((), ())), preferred_element_type=f32, precision=prec)


def _dot_tn(a, b, prec=None):
    return lax.dot_general(a, b, (((0,), (0,)), ((), ())), preferred_element_type=f32, precision=prec)


def _b(x):
    return x.astype(bf16)


def _dot3(a, b, dims=(((1,), (0,)), ((), ()))):
    ah, bh = _b(a), _b(b)
    al, bl = _b(a - ah.astype(f32)), _b(b - bh.astype(f32))
    dg = lambda p, q: lax.dot_general(p, q, dims, preferred_element_type=f32)
    return dg(ah, bh) + (dg(ah, bl) + dg(al, bh))


_TN = (((0,), (0,)), ((), ()))


def _sig(x):
    return 1.0 / (1.0 + jnp.exp(-x))


def _silu(x):
    return x * _sig(x)


def _dsilu(x):
    s = _sig(x)
    return s * (1.0 + x * (1.0 - s))


_GELU_C = 0.7978845608028654
_GELU_A = 0.044715


def _gelu(x):
    return 0.5 * x * (1.0 + jnp.tanh(_GELU_C * (x + _GELU_A * x * x * x)))


def _dgelu(x):
    t = jnp.tanh(_GELU_C * (x + _GELU_A * x * x * x))
    return 0.5 * (1.0 + t) + 0.5 * x * (1.0 - t * t) * _GELU_C * (1.0 + 3.0 * _GELU_A * x * x)


def _softplus(x):
    return jnp.maximum(x, 0.0) + jnp.log(1.0 + jnp.exp(-jnp.abs(x)))


def _divisor(n, cap, mult):
    best = None
    d = mult
    while d <= min(n, cap):
        if n % d == 0:
            best = d
        d += mult
    return best if best is not None else n


def _rows(tr, c, j=0):
    return pl.BlockSpec((tr, c), lambda i, j=j: (i, j))


def _whole(shape):
    nd = len(shape)
    return pl.BlockSpec(shape, lambda *_: (0,) * nd)


def _mm(a, b, mode, name, c=None, tm=1024, tn=1024, tk=2048):
    out_dtype = bf16 if mode == "tn" else f32
    if mode == "nn":
        (M, K), (K2, N) = a.shape, b.shape
    elif mode == "nt":
        (M, K), (N, K2) = a.shape, b.shape
    else:
        (K, M), (K2, N) = a.shape, b.shape
    assert K == K2, (a.shape, b.shape, mode)
    tm = _divisor(M, tm, LANES if mode == "tn" else 16)
    tn = _divisor(N, tn, LANES)
    tk = _divisor(K, tk, LANES if mode != "tn" else 16)
    nk = K // tk
    if mode == "nn":
        a_spec = pl.BlockSpec((tm, tk), lambda i, j, k: (i, k))
        b_spec = pl.BlockSpec((tk, tn), lambda i, j, k: (k, j))
        dn = (((1,), (0,)), ((), ()))
    elif mode == "nt":
        a_spec = pl.BlockSpec((tm, tk), lambda i, j, k: (i, k))
        b_spec = pl.BlockSpec((tn, tk), lambda i, j, k: (j, k))
        dn = (((1,), (1,)), ((), ()))
    else:
        a_spec = pl.BlockSpec((tk, tm), lambda i, j, k: (k, i))
        b_spec = pl.BlockSpec((tk, tn), lambda i, j, k: (k, j))
        dn = (((0,), (0,)), ((), ()))
    o_spec = pl.BlockSpec((tm, tn), lambda i, j, k: (i, j))
    has_c = c is not None

    def body(*refs):
        a_ref, b_ref = refs[0], refs[1]
        c_ref = refs[2] if has_c else None
        o_ref = refs[3] if has_c else refs[2]
        part = lax.dot_general(_b(a_ref[...]), _b(b_ref[...]), dn, preferred_element_type=f32)
        if nk == 1:
            if has_c:
                part = part + c_ref[...]
            o_ref[...] = part.astype(o_ref.dtype)
        else:
            acc = refs[-1]
            k = pl.program_id(2)

            @pl.when(k == 0)
            def _():
                acc[...] = part

            @pl.when(k > 0)
            def _():
                acc[...] += part

            @pl.when(k == nk - 1)
            def _():
                r = acc[...]
                if has_c:
                    r = r + c_ref[...]
                o_ref[...] = r.astype(o_ref.dtype)

    ins = [a, b] + ([c] if has_c else [])
    in_specs = [a_spec, b_spec] + ([o_spec] if has_c else [])
    return pl.pallas_call(
        body, grid=(M // tm, N // tn, nk), in_specs=in_specs, out_specs=o_spec, out_shape=SDS((M, N), out_dtype),
        scratch_shapes=[pltpu.VMEM((tm, tn), f32)] if nk > 1 else [], name=name,
        compiler_params=_cp("parallel", "parallel", "arbitrary"),
    )(*ins)


def _ew(fn, ins, out_dtypes, name, tile_bytes=2 * 1024 * 1024):
    R, C = max((x.shape for x in ins), key=lambda s: s[0])
    n_in = len(ins)
    row_bytes = 4 * C * (len(ins) + len(out_dtypes))
    tr = _divisor(R, max(16, tile_bytes // row_bytes), 16)
    in_specs = [_rows(tr, C) if x.shape[0] == R else _whole((1, C)) for x in ins]

    def body(*refs):
        res = fn(*[r[...] for r in refs[:n_in]])
        for o_ref, r in zip(refs[n_in:], res):
            o_ref[...] = r.astype(o_ref.dtype)

    outs = pl.pallas_call(
        body, grid=(R // tr,), in_specs=in_specs, out_specs=[_rows(tr, C) for _ in out_dtypes],
        out_shape=[SDS((R, C), dt) for dt in out_dtypes], name=name, compiler_params=_cp("parallel"),
    )(*ins)
    return outs


def _rms_fwd(x, gain, name):
    T, Dm = x.shape
    tr = _divisor(T, 256, 16)

    def body(x_ref, g_ref, o_ref):
        xv = x_ref[...]
        r = lax.rsqrt(jnp.mean(xv * xv, axis=-1, keepdims=True) + EPS)
        o_ref[...] = (xv * r * g_ref[...]).astype(o_ref.dtype)

    return pl.pallas_call(
        body, grid=(T // tr,), in_specs=[_rows(tr, Dm), _whole((1, Dm))], out_specs=_rows(tr, Dm),
        out_shape=SDS((T, Dm), bf16), name=name, compiler_params=_cp("parallel"),
    )(x, gain)


def _rms_bwd(x, gain, dh, dres, name):
    T, Dm = x.shape
    tr = _divisor(T, 256, 16)

    def body(x_ref, g_ref, dh_ref, dres_ref, dx_ref, dg_ref):
        xv = x_ref[...]
        dhv = dh_ref[...]
        r = lax.rsqrt(jnp.mean(xv * xv, axis=-1, keepdims=True) + EPS)
        dy = dhv * g_ref[...]
        m = jnp.mean(dy * xv, axis=-1, keepdims=True)
        dx_ref[...] = dres_ref[...] + r * dy - xv * (r * r * r * m)
        part = jnp.sum(dhv * xv * r, axis=0, keepdims=True)

        @pl.when(pl.program_id(0) == 0)
        def _():
            dg_ref[...] = part

        @pl.when(pl.program_id(0) > 0)
        def _():
            dg_ref[...] += part

    return pl.pallas_call(
        body, grid=(T // tr,), in_specs=[_rows(tr, Dm), _whole((1, Dm)), _rows(tr, Dm), _rows(tr, Dm)],
        out_specs=[_rows(tr, Dm), _whole((1, Dm))], out_shape=[SDS((T, Dm), f32), SDS((1, Dm), f32)],
        name=name, compiler_params=_cp("arbitrary"),
    )(x, gain, dh, dres)


def _shift_down(cur, prev8, s):
    rolled = pltpu.roll(cur, s, 0)
    rp = pltpu.roll(prev8, s, 0)
    row8 = lax.broadcasted_iota(jnp.int32, prev8.shape, 0)
    first = jnp.where(row8 < s, rp, rolled[:8])
    return jnp.concatenate([first, rolled[8:]], axis=0)


def _shift_up(cur, next8, s):
    R = cur.shape[0]
    rolled = pltpu.roll(cur, R - s, 0)
    rn = pltpu.roll(next8, 8 - s, 0)
    row8 = lax.broadcasted_iota(jnp.int32, next8.shape, 0)
    last = jnp.where(row8 >= 8 - s, rn, rolled[R - 8:])
    return jnp.concatenate([rolled[: R - 8], last], axis=0)


def _prev8(ref, r0, ci):
    rows = ref[pl.ds(pl.multiple_of(jnp.maximum(r0 - 8, 0), 8), 8), :]
    return jnp.where(ci > 0, rows, 0.0)


def _next8(ref, r0, R, ci, n_chunks):
    start = jnp.minimum(r0 + R, (n_chunks - 1) * R + R - 8)
    rows = ref[pl.ds(pl.multiple_of(start, 8), 8), :]
    return jnp.where(ci < n_chunks - 1, rows, 0.0)


def _w_rows(w_ref, k):
    return [w_ref[j:j + 1, :] for j in range(k)]


def _causal_conv(cur, prev8, w, k):
    acc = cur * w[k - 1]
    for s in range(1, k):
        acc = acc + _shift_down(cur, prev8, s) * w[k - 1 - s]
    return acc


def _qkvconv_fwd(proj, convw, name):
    T = proj.shape[0]
    R = min(ROW_CHUNK, T)
    n_chunks = T // R
    n_blk = (2 * QK_W + V_W) // HEAD_DIM

    def body(x_ref, w_ref, o_ref):
        p = pl.program_id(0)
        is_qk = p < 2 * N_QK
        scale = jnp.where(p < N_QK, HEAD_DIM ** -0.5, 1.0).astype(f32)
        w = _w_rows(w_ref, 4)

        def chunk(ci, carry):
            r0 = pl.multiple_of(ci * R, R)
            cur = x_ref[pl.ds(r0, R), :]
            y = _silu(_causal_conv(cur, _prev8(x_ref, r0, ci), w, 4))
            ss = jnp.sum(y * y, axis=-1, keepdims=True)
            nrm = jnp.where(is_qk, lax.rsqrt(ss + EPS) * scale, 1.0)
            o_ref[pl.ds(r0, R), :] = y * nrm
            return carry

        lax.fori_loop(0, n_chunks, chunk, 0)

    return pl.pallas_call(
        body, grid=(n_blk,),
        in_specs=[pl.BlockSpec((T, HEAD_DIM), lambda p: (0, p)), pl.BlockSpec((4, HEAD_DIM), lambda p: (0, p))],
        out_specs=pl.BlockSpec((T, HEAD_DIM), lambda p: (0, p)), out_shape=SDS((T, n_blk * HEAD_DIM), f32),
        name=name, compiler_params=_cp("parallel"),
    )(proj, convw)


def _qkvconv_bwd(proj, convw, dqkv, dproj, name):
    T = proj.shape[0]
    R = min(ROW_CHUNK, T)
    n_chunks = T // R
    n_blk = (2 * QK_W + V_W) // HEAD_DIM

    def body(x_ref, w_ref, do_ref, dproj_ref, dx_ref, dw_ref, dc_sc):
        p = pl.program_id(0)
        is_qk = p < 2 * N_QK
        scale = jnp.where(p < N_QK, HEAD_DIM ** -0.5, 1.0).astype(f32)
        w = _w_rows(w_ref, 4)

        def phase1(ci, dw):
            r0 = pl.multiple_of(ci * R, R)
            cur = x_ref[pl.ds(r0, R), :]
            p8 = _prev8(x_ref, r0, ci)
            shifted = [cur] + [_shift_down(cur, p8, s) for s in range(1, 4)]
            c = shifted[0] * w[3]
            for s in range(1, 4):
                c = c + shifted[s] * w[3 - s]
            y = _silu(c)
            dout = do_ref[pl.ds(r0, R), :]
            n = lax.rsqrt(jnp.sum(y * y, axis=-1, keepdims=True) + EPS)
            dot_ = jnp.sum(dout * y, axis=-1, keepdims=True)
            dy = jnp.where(is_qk, scale * (n * dout - y * (n * n * n * dot_)), dout)
            dc = dy * _dsilu(c)
            dc_sc[pl.ds(r0, R), :] = dc
            return tuple(dw[j] + jnp.sum(dc * shifted[3 - j], axis=0, keepdims=True) for j in range(4))

        dw = lax.fori_loop(0, n_chunks, phase1, tuple(jnp.zeros((1, HEAD_DIM), f32) for _ in range(4)))
        for j in range(4):
            dw_ref[j:j + 1, :] = dw[j]

        def phase2(ci, carry):
            r0 = pl.multiple_of(ci * R, R)
            cur = dc_sc[pl.ds(r0, R), :]
            n8 = _next8(dc_sc, r0, R, ci, n_chunks)
            acc = cur * w[3]
            for s in range(1, 4):
                acc = acc + _shift_up(cur, n8, s) * w[3 - s]
            dx_ref[pl.ds(r0, R), :] = acc.astype(dx_ref.dtype)
            return carry

        lax.fori_loop(0, n_chunks, phase2, 0)

    col = lambda p: (0, p)
    return pl.pallas_call(
        body, grid=(n_blk,),
        in_specs=[pl.BlockSpec((T, HEAD_DIM), col), pl.BlockSpec((4, HEAD_DIM), col), pl.BlockSpec((T, HEAD_DIM), col), ANY],
        out_specs=[pl.BlockSpec((T, HEAD_DIM), col), pl.BlockSpec((4, HEAD_DIM), col)],
        out_shape=[SDS(dproj.shape, bf16), SDS((4, n_blk * HEAD_DIM), f32)], input_output_aliases={3: 0},
        scratch_shapes=[pltpu.VMEM((T, HEAD_DIM), f32)], name=name, compiler_params=_cp("parallel"),
    )(proj, convw, dqkv, dproj)


def _tri_masks(C):
    row = lax.broadcasted_iota(jnp.int32, (C, C), 0)
    col = lax.broadcasted_iota(jnp.int32, (C, C), 1)
    return row, col


def _lane_pick(blk, lane, idx):
    return jnp.sum(jnp.where(lane == idx, blk, 0.0), axis=1, keepdims=True)


def _gate_block(ba, ea, dtb, lane):
    sig = _sig(ba)
    gblk = -ea * _softplus(ba + dtb)
    return sig, gblk


def _head_decay(gam_all, rg_all, tot, lane, h, row, col):
    C = row.shape[0]
    gam_c = _lane_pick(gam_all, lane, N_V + h)
    rg_c = _lane_pick(rg_all, lane, N_V + h)
    lane1 = lax.broadcasted_iota(jnp.int32, (1, LANES), 1)
    tot_h = jnp.sum(jnp.where(lane1 == N_V + h, tot, 0.0), axis=1, keepdims=True)
    gcb = jnp.broadcast_to(gam_c, (C, C))
    dlt = gcb - gcb.T
    dm = jnp.where(row >= col, jnp.exp(jnp.minimum(dlt, 0.0)), 0.0)
    return gam_c, rg_c, tot_h, dm


def _delta_prep(qkv, ba, ea_row, dtb_row, name):
    T = qkv.shape[0]
    C = CHUNK_A
    N = T // C

    SUB = 16

    def body(q_ref, k_ref, v_ref, ba_ref, ea_ref, dtb_ref,
             qd_ref, kd_ref, u_ref, w_ref, attn_ref, tinv_ref, cd_ref, bg_ref, at_sc, t_sc, a_sc, rk_sc):
        row, col = _tri_masks(C)
        lane = lax.broadcasted_iota(jnp.int32, (C, LANES), 1)
        sig, gblk = _gate_block(ba_ref[...], ea_ref[...], dtb_ref[...], lane)
        bg_ref[...] = jnp.where(lane < N_V, sig, gblk)
        lower = (row >= col).astype(f32)
        upper_s = (col > row).astype(f32)
        eye = (row == col).astype(f32)
        same16 = (row >> 4) == (col >> 4)
        same32 = (row >> 5) == (col >> 5)
        gam_all = _dot(lower, gblk, HI)
        rg_all = _dot(upper_s, gblk, HI)
        tot = jnp.sum(gblk, axis=0, keepdims=True)
        for h in range(N_V):
            qk = pl.ds((h // 2) * HEAD_DIM, HEAD_DIM)
            hs = pl.ds(h * HEAD_DIM, HEAD_DIM)
            qh, kh, vh = q_ref[:, qk], k_ref[:, qk], v_ref[:, hs]
            beta_c = _lane_pick(sig, lane, h)
            gam_c, rg_c, tot_h, dm = _head_decay(gam_all, rg_all, tot, lane, h, row, col)
            kk = _dot_nt(_b(kh), _b(kh))
            a = jnp.where(row > col, beta_c * kk * dm, 0.0)
            a_sc[h] = a
            at_sc[h] = jnp.where(same16, a, 0.0).T
            t_sc[h] = eye
            eg = jnp.exp(gam_c)
            attn_ref[0, h] = (_dot_nt(_b(qh), _b(kh)) * dm).astype(attn_ref.dtype)
            qd_ref[:, hs] = (qh * eg).astype(qd_ref.dtype)
            kd_ref[:, hs] = (kh * jnp.exp(rg_c)).astype(kd_ref.dtype)
            cd_ref[0, h] = jnp.broadcast_to(jnp.exp(tot_h), (1, LANES))
            u_ref[:, hs] = vh * beta_c
            rk_sc[:, hs] = kh * (beta_c * eg)

        first_col = (row >> 4) << 4

        def fsub(i, carry):
            for h in range(N_V):
                t = t_sc[h]
                a_col = jnp.sum(jnp.where(col == first_col + i, at_sc[h], 0.0), axis=1, keepdims=True)
                prod = a_col * t
                sums = [jnp.sum(prod[b * SUB:(b + 1) * SUB], axis=0, keepdims=True) for b in range(C // SUB)]
                new = eye - jnp.concatenate([jnp.broadcast_to(s, (SUB, C)) for s in sums], axis=0)
                t_sc[h] = jnp.where(row - first_col == i, new, t)
            return carry

        lax.fori_loop(1, SUB, fsub, 0)
        pair16 = jnp.logical_and(same32, jnp.logical_not(same16))
        for h in range(N_V):
            at_sc[h] = _dot3(t_sc[h], jnp.where(pair16, a_sc[h], 0.0))
        for h in range(N_V):
            p16 = t_sc[h]
            t_sc[h] = p16 - _dot3(at_sc[h], p16)
        for h in range(N_V):
            at_sc[h] = _dot3(t_sc[h], jnp.where(same32, 0.0, a_sc[h]))
        for h in range(N_V):
            p32 = t_sc[h]
            tinv_ref[0, h] = p32 - _dot3(at_sc[h], p32)
        for h in range(N_V):
            hs = pl.ds(h * HEAD_DIM, HEAD_DIM)
            u_ref[:, hs] = _dot3(tinv_ref[0, h], u_ref[:, hs])
        for h in range(N_V):
            hs = pl.ds(h * HEAD_DIM, HEAD_DIM)
            w_ref[:, hs] = _dot3(tinv_ref[0, h], rk_sc[:, hs]).astype(w_ref.dtype)

    big = lambda n: (n, 0)
    return pl.pallas_call(
        body, grid=(N,),
        in_specs=[pl.BlockSpec((C, QK_W), lambda n: (n, 0)), pl.BlockSpec((C, QK_W), lambda n: (n, 1)),
                  pl.BlockSpec((C, V_W), lambda n: (n, 1)), pl.BlockSpec((C, LANES), big),
                  _whole((1, LANES)), _whole((1, LANES))],
        out_specs=[pl.BlockSpec((C, V_W), big)] * 4 + [
            pl.BlockSpec((1, N_V, C, C), lambda n: (n, 0, 0, 0)), pl.BlockSpec((1, N_V, C, C), lambda n: (n, 0, 0, 0)),
            pl.BlockSpec((1, N_V, 1, LANES), lambda n: (n, 0, 0, 0)), pl.BlockSpec((C, LANES), big)],
        out_shape=[SDS((T, V_W), bf16), SDS((T, V_W), bf16), SDS((T, V_W), f32), SDS((T, V_W), bf16),
                   SDS((N, N_V, C, C), bf16), SDS((N, N_V, C, C), f32), SDS((N, N_V, 1, LANES), f32), SDS((T, LANES), f32)],
        scratch_shapes=[pltpu.VMEM((N_V, C, C), f32)] * 3 + [pltpu.VMEM((C, V_W), f32)],
        name=name, compiler_params=_cp("parallel"),
    )(qkv, qkv, qkv, ba, ea_row, dtb_row)


def _delta_scan(qd, kd, u, w, attn, cd, name):
    T = qd.shape[0]
    C = CHUNK_A
    N = T // C

    def body(qd_ref, kd_ref, u_ref, w_ref, attn_ref, cd_ref, o_ref, s_ref, vn_ref, s_sc):
        @pl.when(pl.program_id(0) == 0)
        def _():
            s_sc[...] = jnp.zeros_like(s_sc)

        for h in range(N_V):
            hs = pl.ds(h * HEAD_DIM, HEAD_DIM)
            s = s_sc[h]
            s_ref[0, h] = s
            sb = _b(s)
            vn = u_ref[:, hs] - _dot(w_ref[:, hs], sb)
            vnb = _b(vn)
            vn_ref[:, hs] = vnb
            o_ref[:, hs] = _dot(qd_ref[:, hs], sb) + _dot(attn_ref[0, h], vnb)
            s_sc[h] = s * cd_ref[0, h] + _dot_tn(kd_ref[:, hs], vnb)

    blk = pl.BlockSpec((C, V_W), lambda n: (n, 0))
    per_head = lambda a, b: pl.BlockSpec((1, N_V, a, b), lambda n: (n, 0, 0, 0))
    return pl.pallas_call(
        body, grid=(N,), in_specs=[blk, blk, blk, blk, per_head(C, C), per_head(1, LANES)],
        out_specs=[blk, per_head(HEAD_DIM, HEAD_DIM), blk],
        out_shape=[SDS((T, V_W), f32), SDS((N, N_V, HEAD_DIM, HEAD_DIM), f32), SDS((T, V_W), bf16)],
        scratch_shapes=[pltpu.VMEM((N_V, HEAD_DIM, HEAD_DIM), f32)], name=name, compiler_params=_cp("arbitrary"),
    )(qd, kd, u, w, attn, cd)


def _delta_scan_bwd(do, qd, kd, w, attn, cd, s_all, vn, name):
    T = qd.shape[0]
    C = CHUNK_A
    N = T // C

    def body(do_ref, qd_ref, kd_ref, w_ref, attn_ref, cd_ref, s_ref, vn_ref,
             dqd_ref, dkd_ref, du_ref, dw_ref, dattn_ref, dcd_ref, ds_sc):
        @pl.when(pl.program_id(0) == 0)
        def _():
            ds_sc[...] = jnp.zeros_like(ds_sc)

        row, col = _tri_masks(C)
        for h in range(N_V):
            hs = pl.ds(h * HEAD_DIM, HEAD_DIM)
            dsn = ds_sc[h]
            s = s_ref[0, h]
            dob, sb, dsb = _b(do_ref[:, hs]), _b(s), _b(dsn)
            vnb = vn_ref[:, hs]
            dqd_ref[:, hs] = _dot_nt(dob, sb)
            dattn_ref[0, h] = jnp.where(row >= col, _dot_nt(dob, vnb), 0.0)
            dvn = _dot_tn(attn_ref[0, h], dob) + _dot(kd_ref[:, hs], dsb)
            dvnb = _b(dvn)
            dkd_ref[:, hs] = _dot_nt(vnb, dsb)
            dcd = jnp.sum(jnp.sum(s * dsn, axis=1, keepdims=True), axis=0, keepdims=True)
            dcd_ref[0, h] = jnp.broadcast_to(dcd, (1, LANES))
            du_ref[:, hs] = dvn
            dw_ref[:, hs] = -_dot_nt(dvnb, sb)
            ds_sc[h] = dsn * cd_ref[0, h] + _dot_tn(qd_ref[:, hs], dob) - _dot_tn(w_ref[:, hs], dvnb)

    blk = pl.BlockSpec((C, V_W), lambda n: (N - 1 - n, 0))
    per_head = lambda a, b: pl.BlockSpec((1, N_V, a, b), lambda n: (N - 1 - n, 0, 0, 0))
    return pl.pallas_call(
        body, grid=(N,),
        in_specs=[blk, blk, blk, blk, per_head(C, C), per_head(1, LANES), per_head(HEAD_DIM, HEAD_DIM), blk],
        out_specs=[blk, blk, blk, blk, per_head(C, C), per_head(1, LANES)],
        out_shape=[SDS((T, V_W), f32)] * 4 + [SDS((N, N_V, C, C), f32), SDS((N, N_V, 1, LANES), f32)],
        scratch_shapes=[pltpu.VMEM((N_V, HEAD_DIM, HEAD_DIM), f32)], name=name, compiler_params=_cp("arbitrary"),
    )(do, qd, kd, w, attn, cd, s_all, vn)


def _delta_prep_bwd(qkv, ba, ea_row, dtb_row, tinv, u, w, dqd, dkd, du, dw, dattn, dcd, name):
    T = qkv.shape[0]
    C = CHUNK_A
    N = T // C

    def body(q_ref, k_ref, v_ref, ba_ref, ea_ref, dtb_ref, tinv_ref, u_ref, w_ref,
             dqd_ref, dkd_ref, du_ref, dw_ref, dattn_ref, dcd_ref,
             dqkv_ref, dba_ref, dalog_ref, ddtb_ref, drv_sc, drk_sc, da_sc):
        row, col = _tri_masks(C)
        lane = lax.broadcasted_iota(jnp.int32, (C, LANES), 1)
        rowc = lax.broadcasted_iota(jnp.int32, (C, 1), 0)
        for h in range(N_V):
            hs = pl.ds(h * HEAD_DIM, HEAD_DIM)
            drv_sc[:, hs] = _dot3(tinv_ref[0, h], du_ref[:, hs], _TN)
        for h in range(N_V):
            hs = pl.ds(h * HEAD_DIM, HEAD_DIM)
            drk_sc[:, hs] = _dot3(tinv_ref[0, h], dw_ref[:, hs], _TN)
        for h in range(N_V):
            hs = pl.ds(h * HEAD_DIM, HEAD_DIM)
            da_sc[h] = -jnp.where(row > col, _dot_nt(_b(drv_sc[:, hs]), _b(u_ref[:, hs]))
                                  + _dot_nt(_b(drk_sc[:, hs]), w_ref[:, hs]), 0.0)
        ba_v, ea, dtb = ba_ref[...], ea_ref[...], dtb_ref[...]
        sig, gblk = _gate_block(ba_v, ea, dtb, lane)
        lower = (row >= col).astype(f32)
        upper_s = (col > row).astype(f32)
        upper = (col >= row).astype(f32)
        gam_all = _dot(lower, gblk, HI)
        rg_all = _dot(upper_s, gblk, HI)
        tot = jnp.sum(gblk, axis=0, keepdims=True)
        dbeta_blk = jnp.zeros((C, LANES), f32)
        dgam_blk = jnp.zeros((C, LANES), f32)
        for j in range(N_QK):
            qk = pl.ds(j * HEAD_DIM, HEAD_DIM)
            qh, kh = q_ref[:, qk], k_ref[:, qk]
            qhb, khb = _b(qh), _b(kh)
            kk = _dot_nt(khb, khb)
            qkm = _dot_nt(qhb, khb)
            dq_j = jnp.zeros((C, HEAD_DIM), f32)
            dk_j = jnp.zeros((C, HEAD_DIM), f32)
            for h in (2 * j, 2 * j + 1):
                hs = pl.ds(h * HEAD_DIM, HEAD_DIM)
                vh = v_ref[:, hs]
                beta_c = _lane_pick(sig, lane, h)
                gam_c, rg_c, tot_h, dm = _head_decay(gam_all, rg_all, tot, lane, h, row, col)
                eg, er, cdh = jnp.exp(gam_c), jnp.exp(rg_c), jnp.exp(tot_h)
                d_rv, d_rk, da = drv_sc[:, hs], drk_sc[:, hs], da_sc[h]
                dqkv_ref[:, pl.ds(2 * QK_W + h * HEAD_DIM, HEAD_DIM)] = beta_c * d_rv
                dbeta = jnp.sum(d_rv * vh + d_rk * (eg * kh), axis=1, keepdims=True)
                dk_h = (beta_c * eg) * d_rk
                d_eg = jnp.sum(d_rk * kh, axis=1, keepdims=True) * beta_c
                bkd = da * dm
                dbeta = dbeta + jnp.sum(bkd * kk, axis=1, keepdims=True)
                dkk = bkd * beta_c
                ddm = da * beta_c * kk
                dattn_h = dattn_ref[0, h]
                dqk = dattn_h * dm
                ddm = ddm + dattn_h * qkm
                dqd_h, dkd_h = dqd_ref[:, hs], dkd_ref[:, hs]
                dq_j = dq_j + _dot(_b(dqk), khb) + eg * dqd_h
                dk_h = dk_h + _dot_tn(_b(dqk), qhb) + _dot(_b(dkk + dkk.T), khb) + er * dkd_h
                dk_j = dk_j + dk_h
                d_eg = d_eg + jnp.sum(dqd_h * qh, axis=1, keepdims=True)
                d_er = jnp.sum(dkd_h * kh, axis=1, keepdims=True)
                e = ddm * dm
                dgam = jnp.sum(e, axis=1, keepdims=True) - jnp.sum(e.T, axis=1, keepdims=True)
                dgam = dgam + d_eg * eg - d_er * er
                extra = jnp.sum(d_er * er, axis=0, keepdims=True) + jnp.max(dcd_ref[0, h], axis=1, keepdims=True) * cdh
                dgam = dgam + jnp.where(rowc == C - 1, extra, 0.0)
                dbeta_blk = jnp.where(lane == h, dbeta, dbeta_blk)
                dgam_blk = jnp.where(lane == N_V + h, dgam, dgam_blk)
            dqkv_ref[:, qk] = dq_j
            dqkv_ref[:, pl.ds(QK_W + j * HEAD_DIM, HEAD_DIM)] = dk_j
        dg_all = _dot(upper, dgam_blk, HI)
        dsp = dg_all * (-ea) * _sig(ba_v + dtb)
        dba_ref[...] = jnp.where(lane < N_V, dbeta_blk * sig * (1.0 - sig), dsp)
        part_alog = jnp.sum(dg_all * gblk, axis=0, keepdims=True)
        part_dtb = jnp.sum(dsp, axis=0, keepdims=True)

        @pl.when(pl.program_id(0) == 0)
        def _():
            dalog_ref[...] = part_alog
            ddtb_ref[...] = part_dtb

        @pl.when(pl.program_id(0) > 0)
        def _():
            dalog_ref[...] += part_alog
            ddtb_ref[...] += part_dtb

    big = lambda n: (n, 0)
    wide = pl.BlockSpec((C, V_W), big)
    sq = pl.BlockSpec((1, N_V, C, C), lambda n: (n, 0, 0, 0))
    return pl.pallas_call(
        body, grid=(N,),
        in_specs=[pl.BlockSpec((C, QK_W), lambda n: (n, 0)), pl.BlockSpec((C, QK_W), lambda n: (n, 1)),
                  pl.BlockSpec((C, V_W), lambda n: (n, 1)), pl.BlockSpec((C, LANES), big),
                  _whole((1, LANES)), _whole((1, LANES)), sq, wide, wide, wide, wide, wide, wide, sq,
                  pl.BlockSpec((1, N_V, 1, LANES), lambda n: (n, 0, 0, 0))],
        out_specs=[pl.BlockSpec((C, 2 * QK_W + V_W), big), pl.BlockSpec((C, LANES), big),
                   _whole((1, LANES)), _whole((1, LANES))],
        out_shape=[SDS((T, 2 * QK_W + V_W), f32), SDS((T, LANES), f32), SDS((1, LANES), f32), SDS((1, LANES), f32)],
        scratch_shapes=[pltpu.VMEM((C, V_W), f32), pltpu.VMEM((C, V_W), f32), pltpu.VMEM((N_V, C, C), f32)],
        name=name, compiler_params=_cp("arbitrary"),
    )(qkv, qkv, qkv, ba, ea_row, dtb_row, tinv, u, w, dqd, dkd, du, dw, dattn, dcd)


def _onorm_fwd(o, proj, hg, name):
    T = o.shape[0]
    tr = _divisor(T, 256, 16)

    def body(o_ref, z_ref, g_ref, out_ref):
        g = g_ref[...]
        for h in range(N_V):
            hs = pl.ds(h * HEAD_DIM, HEAD_DIM)
            oh = o_ref[:, hs]
            r = lax.rsqrt(jnp.mean(oh * oh, axis=-1, keepdims=True) + EPS)
            out_ref[:, hs] = (oh * r * g * _silu(z_ref[:, hs])).astype(out_ref.dtype)

    return pl.pallas_call(
        body, grid=(T // tr,), in_specs=[_rows(tr, V_W), _rows(tr, V_W, C_Z // V_W), _whole((1, HEAD_DIM))],
        out_specs=_rows(tr, V_W), out_shape=SDS((T, V_W), bf16), name=name, compiler_params=_cp("parallel"),
    )(o, proj, hg)


def _onorm_bwd(don, o, proj, hg, dproj, name):
    T = o.shape[0]
    tr = _divisor(T, 256, 16)

    def body(don_ref, o_ref, z_ref, g_ref, dproj_ref, do_ref, dz_ref, dg_ref):
        g = g_ref[...]
        dg = jnp.zeros((1, HEAD_DIM), f32)
        for h in range(N_V):
            hs = pl.ds(h * HEAD_DIM, HEAD_DIM)
            oh, zh, dh = o_ref[:, hs], z_ref[:, hs], don_ref[:, hs]
            r = lax.rsqrt(jnp.mean(oh * oh, axis=-1, keepdims=True) + EPS)
            d_n = dh * _silu(zh)
            dz_ref[:, hs] = (dh * (oh * r * g) * _dsilu(zh)).astype(dz_ref.dtype)
            dy = d_n * g
            m = jnp.mean(dy * oh, axis=-1, keepdims=True)
            do_ref[:, hs] = r * dy - oh * (r * r * r * m)
            dg = dg + jnp.sum(d_n * oh * r, axis=0, keepdims=True)

        @pl.when(pl.program_id(0) == 0)
        def _():
            dg_ref[...] = dg

        @pl.when(pl.program_id(0) > 0)
        def _():
            dg_ref[...] += dg

    return pl.pallas_call(
        body, grid=(T // tr,),
        in_specs=[_rows(tr, V_W), _rows(tr, V_W), _rows(tr, V_W, C_Z // V_W), _whole((1, HEAD_DIM)), ANY],
        out_specs=[_rows(tr, V_W), _rows(tr, V_W, C_Z // V_W), _whole((1, HEAD_DIM))],
        out_shape=[SDS((T, V_W), f32), SDS(dproj.shape, bf16), SDS((1, HEAD_DIM), f32)], input_output_aliases={4: 1},
        name=name, compiler_params=_cp("arbitrary"),
    )(don, o, proj, hg, dproj)


def _sgu_parts(ub, vb, gain):
    gv = _gelu(vb)
    r = lax.rsqrt(jnp.mean(gv * gv, axis=-1, keepdims=True) + EPS)
    return _gelu(ub), gv, r, gv * r * gain


def _sgu_fwd(proj, gain, w_s, b_bc, name):
    T = proj.shape[0]
    C = CHUNK_B

    def body(ub_ref, vb_ref, g_ref, w_ref, b_ref, o_ref):
        row, col = _tri_masks(C)
        u, _, _, vn = _sgu_parts(ub_ref[...], vb_ref[...], g_ref[...])
        for g in range(N_GROUPS):
            gs = pl.ds(g * GROUP_DIM, GROUP_DIM)
            wg = jnp.where(row >= col, w_ref[g], 0.0)
            mixed = _dot(_b(wg), _b(vn[:, g * GROUP_DIM:(g + 1) * GROUP_DIM])) + b_ref[g]
            o_ref[:, gs] = (u[:, g * GROUP_DIM:(g + 1) * GROUP_DIM] * mixed).astype(o_ref.dtype)

    return pl.pallas_call(
        body, grid=(T // C,),
        in_specs=[_rows(C, WIDTH_B, C_UB // WIDTH_B), _rows(C, WIDTH_B, C_VB // WIDTH_B), _whole((1, WIDTH_B)),
                  _whole((N_GROUPS, C, C)), _whole((N_GROUPS, C, GROUP_DIM))],
        out_specs=_rows(C, WIDTH_B), out_shape=SDS((T, WIDTH_B), bf16), name=name, compiler_params=_cp("parallel"),
    )(proj, proj, gain, w_s, b_bc)


def _sgu_bwd(dsgu, proj, gain, w_s, b_bc, dproj, name):
    T = proj.shape[0]
    C = CHUNK_B

    def body(d_ref, ub_ref, vb_ref, g_ref, w_ref, b_ref, dproj_ref, duv_ref, dw_ref, db_ref, dg_ref):
        dub_ref = duv_ref.at[:, pl.ds(0, WIDTH_B)]
        dvb_ref = duv_ref.at[:, pl.ds(WIDTH_B, WIDTH_B)]
        first = pl.program_id(0) == 0
        row, col = _tri_masks(C)
        ub, vb, gain_v = ub_ref[...], vb_ref[...], g_ref[...]
        u, gv, r, vn = _sgu_parts(ub, vb, gain_v)
        d = d_ref[...]
        dvn_parts = []
        for g in range(N_GROUPS):
            sl = slice(g * GROUP_DIM, (g + 1) * GROUP_DIM)
            wg = jnp.where(row >= col, w_ref[g], 0.0)
            vng = _b(vn[:, sl])
            mixed = _dot(_b(wg), vng) + b_ref[g]
            dub_ref[:, pl.ds(g * GROUP_DIM, GROUP_DIM)] = (d[:, sl] * mixed * _dgelu(ub[:, sl])).astype(dub_ref.dtype)
            dmix = d[:, sl] * u[:, sl]
            dmb = _b(dmix)
            dwg = jnp.where(row >= col, _dot_nt(dmb, vng), 0.0)
            dbg = jnp.sum(dmix, axis=1, keepdims=True)

            @pl.when(first)
            def _():
                dw_ref[g] = dwg
                db_ref[g] = dbg

            @pl.when(jnp.logical_not(first))
            def _():
                dw_ref[g] += dwg
                db_ref[g] += dbg

            dvn_parts.append(_dot_tn(_b(wg), dmb))
        dvn = jnp.concatenate(dvn_parts, axis=1)
        dy = dvn * gain_v
        m = jnp.mean(dy * gv, axis=-1, keepdims=True)
        dgv = r * dy - gv * (r * r * r * m)
        dvb_ref[...] = (dgv * _dgelu(vb)).astype(dvb_ref.dtype)
        dgain = jnp.sum(dvn * gv * r, axis=0, keepdims=True)

        @pl.when(first)
        def _():
            dg_ref[...] = dgain

        @pl.when(jnp.logical_not(first))
        def _():
            dg_ref[...] += dgain

    return pl.pallas_call(
        body, grid=(T // C,),
        in_specs=[_rows(C, WIDTH_B), _rows(C, WIDTH_B, C_UB // WIDTH_B), _rows(C, WIDTH_B, C_VB // WIDTH_B),
                  _whole((1, WIDTH_B)), _whole((N_GROUPS, C, C)), _whole((N_GROUPS, C, GROUP_DIM)), ANY],
        out_specs=[_rows(C, 2 * WIDTH_B, C_UB // (2 * WIDTH_B)), _whole((N_GROUPS, C, C)), _whole((N_GROUPS, C, 1)),
                   _whole((1, WIDTH_B))],
        out_shape=[SDS(dproj.shape, bf16), SDS((N_GROUPS, C, C), f32), SDS((N_GROUPS, C, 1), f32), SDS((1, WIDTH_B), f32)],
        input_output_aliases={6: 0}, name=name, compiler_params=_cp("arbitrary"),
    )(dsgu, proj, proj, gain, w_s, b_bc, dproj)


def _merge_fwd(proj, ya, yb, name):
    T = proj.shape[0]
    tr = _divisor(T, 256, 16)

    def body(ga_ref, gb_ref, ya_ref, yb_ref, o_ref):
        o_ref[...] = (_sig(ga_ref[...]) * ya_ref[...] + _sig(gb_ref[...]) * yb_ref[...]).astype(o_ref.dtype)

    return pl.pallas_call(
        body, grid=(T // tr,),
        in_specs=[_rows(tr, D_MODEL, C_GA // D_MODEL), _rows(tr, D_MODEL, C_GB // D_MODEL), _rows(tr, D_MODEL), _rows(tr, D_MODEL)],
        out_specs=_rows(tr, D_MODEL), out_shape=SDS((T, D_MODEL), bf16), name=name, compiler_params=_cp("parallel"),
    )(proj, proj, ya, yb)


def _merge_bwd(dm, proj, ya, yb, name):
    T = proj.shape[0]
    tr = _divisor(T, 256, 16)

    def body(dm_ref, ga_ref, gb_ref, ya_ref, yb_ref, dya_ref, dyb_ref, dg_ref):
        d = dm_ref[...]
        sa, sb = _sig(ga_ref[...]), _sig(gb_ref[...])
        dya_ref[...] = (d * sa).astype(bf16)
        dyb_ref[...] = (d * sb).astype(bf16)
        dg_ref[:, :D_MODEL] = (d * ya_ref[...] * sa * (1.0 - sa)).astype(bf16)
        dg_ref[:, D_MODEL:] = (d * yb_ref[...] * sb * (1.0 - sb)).astype(bf16)

    return pl.pallas_call(
        body, grid=(T // tr,),
        in_specs=[_rows(tr, D_MODEL), _rows(tr, D_MODEL, C_GA // D_MODEL), _rows(tr, D_MODEL, C_GB // D_MODEL),
                  _rows(tr, D_MODEL), _rows(tr, D_MODEL)],
        out_specs=[_rows(tr, D_MODEL), _rows(tr, D_MODEL), _rows(tr, 2 * D_MODEL, C_GA // (2 * D_MODEL))],
        out_shape=[SDS((T, D_MODEL), bf16), SDS((T, D_MODEL), bf16), SDS((T, PM), bf16)], name=name,
        compiler_params=_cp("parallel"),
    )(dm, proj, proj, ya, yb)


def _ffnconv_fwd(upg, upv, wg, wv, bg, bv, name):
    T, F = upg.shape
    R = min(ROW_CHUNK, T)
    n_chunks = T // R

    def body(g_ref, v_ref, wg_ref, wv_ref, bg_ref, bv_ref, o_ref):
        wgv, wvv = _w_rows(wg_ref, 3), _w_rows(wv_ref, 3)

        def chunk(ci, carry):
            r0 = pl.multiple_of(ci * R, R)
            cg = _causal_conv(g_ref[pl.ds(r0, R), :], _prev8(g_ref, r0, ci), wgv, 3) + bg_ref[...]
            cv = _causal_conv(v_ref[pl.ds(r0, R), :], _prev8(v_ref, r0, ci), wvv, 3) + bv_ref[...]
            o_ref[pl.ds(r0, R), :] = (_silu(cg) * cv).astype(o_ref.dtype)
            return carry

        lax.fori_loop(0, n_chunks, chunk, 0)

    col = lambda p: (0, p)
    return pl.pallas_call(
        body, grid=(F // LANES,),
        in_specs=[pl.BlockSpec((T, LANES), col)] * 2 + [pl.BlockSpec((3, LANES), col)] * 2 + [pl.BlockSpec((1, LANES), col)] * 2,
        out_specs=pl.BlockSpec((T, LANES), col), out_shape=SDS((T, F), bf16), name=name, compiler_params=_cp("parallel"),
    )(upg, upv, wg, wv, bg, bv)


def _ffnconv_bwd(dact, upg, upv, wg, wv, bg, bv, name):
    T, F = upg.shape
    R = min(ROW_CHUNK, T)
    n_chunks = T // R

    def body(d_ref, g_ref, v_ref, wg_ref, wv_ref, bg_ref, bv_ref,
             dg_ref, dv_ref, dwg_ref, dwv_ref, dbg_ref, dbv_ref, sg, sv):
        wgv, wvv = _w_rows(wg_ref, 3), _w_rows(wv_ref, 3)

        def phase1(ci, carry):
            dwg, dwv = carry
            r0 = pl.multiple_of(ci * R, R)
            gcur, vcur = g_ref[pl.ds(r0, R), :], v_ref[pl.ds(r0, R), :]
            gp, vp = _prev8(g_ref, r0, ci), _prev8(v_ref, r0, ci)
            gsh = [gcur] + [_shift_down(gcur, gp, s) for s in (1, 2)]
            vsh = [vcur] + [_shift_down(vcur, vp, s) for s in (1, 2)]
            cg = gsh[0] * wgv[2] + gsh[1] * wgv[1] + gsh[2] * wgv[0] + bg_ref[...]
            cv = vsh[0] * wvv[2] + vsh[1] * wvv[1] + vsh[2] * wvv[0] + bv_ref[...]
            d = d_ref[pl.ds(r0, R), :]
            dcv = d * _silu(cg)
            dcg = d * cv * _dsilu(cg)
            sg[pl.ds(r0, R), :] = dcg
            sv[pl.ds(r0, R), :] = dcv
            rg = [jnp.sum(dcg * gsh[2 - j], axis=0, keepdims=True) for j in range(3)] + [jnp.sum(dcg, axis=0, keepdims=True)]
            rv = [jnp.sum(dcv * vsh[2 - j], axis=0, keepdims=True) for j in range(3)] + [jnp.sum(dcv, axis=0, keepdims=True)]
            return tuple(a + b for a, b in zip(dwg, rg)), tuple(a + b for a, b in zip(dwv, rv))

        z4 = tuple(jnp.zeros((1, LANES), f32) for _ in range(4))
        dwg, dwv = lax.fori_loop(0, n_chunks, phase1, (z4, z4))
        for j in range(3):
            dwg_ref[j:j + 1, :] = dwg[j]
            dwv_ref[j:j + 1, :] = dwv[j]
        dbg_ref[...] = dwg[3]
        dbv_ref[...] = dwv[3]

        def phase2(ci, carry):
            r0 = pl.multiple_of(ci * R, R)
            for sc, wv_, out in ((sg, wgv, dg_ref), (sv, wvv, dv_ref)):
                cur = sc[pl.ds(r0, R), :]
                n8 = _next8(sc, r0, R, ci, n_chunks)
                acc = cur * wv_[2] + _shift_up(cur, n8, 1) * wv_[1] + _shift_up(cur, n8, 2) * wv_[0]
                out[pl.ds(r0, R), :] = acc.astype(out.dtype)
            return carry

        lax.fori_loop(0, n_chunks, phase2, 0)

    col = lambda p: (0, p)
    big, w3, b1 = pl.BlockSpec((T, LANES), col), pl.BlockSpec((3, LANES), col), pl.BlockSpec((1, LANES), col)
    return pl.pallas_call(
        body, grid=(F // LANES,), in_specs=[big, big, big, w3, w3, b1, b1], out_specs=[big, big, w3, w3, b1, b1],
        out_shape=[SDS((T, F), bf16), SDS((T, F), bf16), SDS((3, F), f32), SDS((3, F), f32), SDS((1, F), f32), SDS((1, F), f32)],
        scratch_shapes=[pltpu.VMEM((T, LANES), f32), pltpu.VMEM((T, LANES), f32)], name=name, compiler_params=_cp("parallel"),
    )(dact, upg, upv, wg, wv, bg, bv)


def _loss_head(x, gain, target, name):
    T, Dm = x.shape
    tr = _divisor(T, 256, 16)

    def body(x_ref, g_ref, t_ref, l_ref, dx_ref, dg_ref):
        xv, g = x_ref[...], g_ref[...]
        r = lax.rsqrt(jnp.mean(xv * xv, axis=-1, keepdims=True) + EPS)
        err = xv * r * g - t_ref[...]
        part_l = 0.5 * jnp.sum(jnp.mean(err * err, axis=-1, keepdims=True), axis=0, keepdims=True)
        dy = err * (1.0 / Dm)
        dyg = dy * g
        m = jnp.mean(dyg * xv, axis=-1, keepdims=True)
        dx_ref[...] = r * dyg - xv * (r * r * r * m)
        part_g = jnp.sum(dy * xv * r, axis=0, keepdims=True)
        part_l = jnp.broadcast_to(part_l, (1, LANES))

        @pl.when(pl.program_id(0) == 0)
        def _():
            l_ref[...] = part_l
            dg_ref[...] = part_g

        @pl.when(pl.program_id(0) > 0)
        def _():
            l_ref[...] += part_l
            dg_ref[...] += part_g

    return pl.pallas_call(
        body, grid=(T // tr,), in_specs=[_rows(tr, Dm), _whole((1, Dm)), _rows(tr, Dm)],
        out_specs=[_whole((1, LANES)), _rows(tr, Dm), _whole((1, Dm))],
        out_shape=[SDS((1, LANES), f32), SDS((T, Dm), f32), SDS((1, Dm), f32)], name=name, compiler_params=_cp("arbitrary"),
    )(x, gain, target)


def _lane_row(vec, offset):
    return jnp.pad(vec.astype(f32), (offset, LANES - offset - vec.shape[0]))[None]


def _layer_fwd(x, p_i, W, S, li):
    nm = lambda s: f"{s}_l{li}"
    sv = {"x0": x}
    h1 = _rms_fwd(x, S["norm_mix"], nm("rms_mix"))
    proj = _mm(h1, W["in_main"], "nn", nm("proj_main"))
    ba = _mm(h1, W["in_ba"], "nn", nm("proj_ba"))
    qkv = _qkvconv_fwd(proj, S["conv_qkv"], nm("qkvconv"))
    qd, kd, u, w, attn, tinv, cd, _ = _delta_prep(qkv, ba, S["ea_row"], S["dtb_row"], nm("delta_prep"))
    o, s_all, vn = _delta_scan(qd, kd, u, w, attn, cd, nm("delta_scan"))
    on = _onorm_fwd(o, proj, S["head_norm"], nm("onorm"))
    ya = _mm(on, W["branch_a"], "nn", nm("branch_a"))
    sgu = _sgu_fwd(proj, S["sgu_norm"], S["w_spatial"], S["b_bc"], nm("sgu"))
    yb = _mm(sgu, W["branch_b"], "nn", nm("branch_b"))
    merged = _merge_fwd(proj, ya, yb, nm("merge"))
    x1 = _mm(merged, W["out"], "nn", nm("out_proj"), c=x)
    h2 = _rms_fwd(x1, S["norm_ffn"], nm("rms_ffn"))
    upg = _mm(h2, W["up_g"], "nn", nm("ffn_up_g"))
    upv = _mm(h2, W["up_v"], "nn", nm("ffn_up_v"))
    act = _ffnconv_fwd(upg, upv, S["conv_g"], S["conv_v"], S["bias_g"], S["bias_v"], nm("ffnconv"))
    x2 = _mm(act, W["down"], "nn", nm("ffn_down"), c=x1, tk=2816)
    h3 = _rms_fwd(x2, S["norm_ple"], nm("rms_ple"))
    gl = _mm(h3, W["ple_gate"], "nn", nm("ple_gate"))
    pp = _mm(p_i, W["ple_proj"], "nn", nm("ple_proj"))
    (x3,) = _ew(lambda a, g, q: (a + _sig(g) * q,), [x2, gl, pp], [f32], nm("ple_mix"))
    sv.update(h1=h1, proj=proj, ba=ba, qkv=qkv, qd=qd, kd=kd, u=u, w=w, attn=attn, tinv=tinv, cd=cd, o=o, s_all=s_all,
              vn=vn, on=on, ya=ya, sgu=sgu, yb=yb, merged=merged, x1=x1, h2=h2, upg=upg, upv=upv, act=act, x2=x2,
              h3=h3, gl=gl, pp=pp, p=p_i)
    return x3, sv


def _layer_bwd(dx3, sv, W, S, li):
    nm = lambda s: f"{s}_l{li}"
    G = {}
    dgl, dpp = _ew(lambda d, g, q: ((lambda s: (d * q * s * (1.0 - s), d * s))(_sig(g))),
                   [dx3, sv["gl"], sv["pp"]], [bf16, bf16], nm("ple_mix_bwd"))
    G["w_ple_gate"] = _mm(sv["h3"], dgl, "tn", nm("d_ple_gate"))
    G["w_ple_proj"] = _mm(sv["p"], dpp, "tn", nm("d_ple_proj"))
    dh3 = _mm(dgl, W["ple_gate"], "nt", nm("dh_ple"))
    dx2, G["norm_ple"] = _rms_bwd(sv["x2"], S["norm_ple"], dh3, dx3, nm("rms_ple_bwd"))
    dact = _mm(dx2, W["down"], "nt", nm("d_act"))
    G["w_ffn_down"] = _mm(sv["act"], dx2, "tn", nm("d_ffn_down"))
    dupg, dupv, dcg, dcv, dbg, dbv = _ffnconv_bwd(dact, sv["upg"], sv["upv"], S["conv_g"], S["conv_v"], S["bias_g"],
                                                  S["bias_v"], nm("ffnconv_bwd"))
    G["conv_ffn"] = jnp.concatenate([dcg, dcv], axis=1)
    G["b_conv_ffn"] = jnp.concatenate([dbg, dbv], axis=1)
    G["up_g"] = _mm(sv["h2"], dupg, "tn", nm("d_ffn_up_g"))
    G["up_v"] = _mm(sv["h2"], dupv, "tn", nm("d_ffn_up_v"))
    dh2 = _mm(dupg, W["up_g"], "nt", nm("dh_ffn_g"), tk=2816)
    dh2 = _mm(dupv, W["up_v"], "nt", nm("dh_ffn_v"), c=dh2, tk=2816)
    dx1, G["norm_ffn"] = _rms_bwd(sv["x1"], S["norm_ffn"], dh2, dx2, nm("rms_ffn_bwd"))
    dmerged = _mm(dx1, W["out"], "nt", nm("d_merged"))
    G["w_out"] = _mm(sv["merged"], dx1, "tn", nm("d_w_out"))
    dya, dyb, dproj = _merge_bwd(dmerged, sv["proj"], sv["ya"], sv["yb"], nm("merge_bwd"))
    G["w_branch_a"] = _mm(sv["on"], dya, "tn", nm("d_branch_a"))
    G["w_branch_b"] = _mm(sv["sgu"], dyb, "tn", nm("d_branch_b"))
    don = _mm(dya, W["branch_a"], "nt", nm("d_on"))
    dsgu = _mm(dyb, W["branch_b"], "nt", nm("d_sgu"))
    dproj, G["w_spatial"], db_s, G["sgu_norm"] = _sgu_bwd(dsgu, sv["proj"], S["sgu_norm"], S["w_spatial"], S["b_bc"], dproj,
                                                          nm("sgu_bwd"))
    G["b_spatial"] = db_s.reshape(N_GROUPS, CHUNK_B)
    do, dproj, G["head_norm"] = _onorm_bwd(don, sv["o"], sv["proj"], S["head_norm"], dproj, nm("onorm_bwd"))
    dqd, dkd, du, dw, dattn, dcd = _delta_scan_bwd(do, sv["qd"], sv["kd"], sv["w"], sv["attn"], sv["cd"], sv["s_all"], sv["vn"],
                                                   nm("delta_scan_bwd"))
    dqkv_n, dba, dalog_row, ddtb_row = _delta_prep_bwd(sv["qkv"], sv["ba"], S["ea_row"], S["dtb_row"], sv["tinv"], sv["u"], sv["w"],
                                                       dqd, dkd, du, dw, dattn, dcd, nm("delta_prep_bwd"))
    G["a_log"] = dalog_row[0, N_V:2 * N_V]
    G["dt_bias"] = ddtb_row[0, N_V:2 * N_V]
    dproj, G["conv_qkv"] = _qkvconv_bwd(sv["proj"], S["conv_qkv"], dqkv_n, dproj, nm("qkvconv_bwd"))
    G["in_main"] = _mm(sv["h1"], dproj, "tn", nm("d_in_main"))
    G["in_ba"] = _mm(sv["h1"], dba, "tn", nm("d_in_ba"))
    dh1 = _mm(dba, W["in_ba"], "nt", nm("dh_mix_ba"))
    dh1 = _mm(dproj, W["in_main"], "nt", nm("dh_mix"), c=dh1)
    dx0, G["norm_mix"] = _rms_bwd(sv["x0"], S["norm_mix"], dh1, dx1, nm("rms_mix_bwd"))
    return dx0, G


_BA0 = C_UB
_BA1 = C_UB + 2 * N_V


def _cols(segments, lo, hi):
    out = []
    for start, a in segments:
        s_lo, s_hi = max(lo, start), min(hi, start + a.shape[1])
        if s_lo < s_hi:
            out.append(a[:, s_lo - start:s_hi - start])
    return out


def _big_weights(sh):
    n_in = N_IN // N_CHIPS
    w_in = [(j * n_in, a) for j, a in enumerate(sh["w_in"])]
    rows = lambda k: jnp.concatenate(sh[k], axis=0)
    cols = lambda parts: jnp.concatenate(parts, axis=1)
    ba = jnp.pad(cols(_cols(w_in, _BA0, _BA1)), ((0, 0), (0, LANES - 2 * N_V)))
    return dict(
        in_main=cols(_cols(w_in, 0, _BA0) + _cols(w_in, _BA1, N_IN)), in_ba=ba,
        branch_a=rows("w_branch_a"), branch_b=cols(sh["w_branch_b"]), out=rows("w_out"),
        up_g=cols(sh["w_ffn_up"][:2]), up_v=cols(sh["w_ffn_up"][2:]), down=rows("w_ffn_down"),
        ple_gate=rows("w_ple_gate"), ple_proj=cols(sh["w_ple_proj"]))


def _grads_by_chip(G):
    n_in = N_IN // N_CHIPS
    w_in = [(0, G["in_main"][:, :_BA0]), (_BA0, G["in_ba"][:, :2 * N_V]), (_BA1, G["in_main"][:, _BA0:])]
    split = lambda g: jnp.stack(jnp.split(g, N_CHIPS, axis=1))
    by_rows = lambda g: g.reshape(N_CHIPS, -1, g.shape[1])
    return dict(
        w_in=jnp.stack([jnp.concatenate(_cols(w_in, j * n_in, (j + 1) * n_in), axis=1) for j in range(N_CHIPS)]),
        w_branch_a=by_rows(G["w_branch_a"]), w_branch_b=split(G["w_branch_b"]), w_out=by_rows(G["w_out"]),
        w_ffn_up=jnp.stack(jnp.split(G["up_g"], 2, axis=1) + jnp.split(G["up_v"], 2, axis=1)),
        w_ffn_down=by_rows(G["w_ffn_down"]), w_ple_gate=by_rows(G["w_ple_gate"]), w_ple_proj=split(G["w_ple_proj"]))


def _small_params(sm, i):
    return dict(
        norm_mix=sm["norm_mix"][i][None], conv_qkv=sm["conv_qkv"][i], ea_row=_lane_row(jnp.exp(sm["a_log"][i]), N_V),
        dtb_row=_lane_row(sm["dt_bias"][i], N_V), head_norm=sm["head_norm"][i][None], sgu_norm=sm["sgu_norm"][i][None],
        w_spatial=sm["w_spatial"][i],
        b_bc=jnp.broadcast_to(sm["b_spatial"][i][:, :, None], (N_GROUPS, CHUNK_B, GROUP_DIM)),
        norm_ffn=sm["norm_ffn"][i][None], conv_g=sm["conv_ffn"][i][:, :D_FF], conv_v=sm["conv_ffn"][i][:, D_FF:],
        bias_g=sm["b_conv_ffn"][i][None, :D_FF], bias_v=sm["b_conv_ffn"][i][None, D_FF:], norm_ple=sm["norm_ple"][i][None])


MESH = pl.DeviceIdType.MESH
N_CHIPS = 4
ANY = pl.BlockSpec(memory_space=pl.ANY)
ROW_ALIGN = 32


def _place():
    x, y, c = lax.axis_index("x"), lax.axis_index("y"), lax.axis_index("c")
    chips = [(1 - x, y), (x, 1 - y), (1 - x, 1 - y)]
    return x, y, c, 2 * x + y, chips, (x, y, 1 - c)


def _halves(rows, c):
    h = rows // 2
    return pl.ds(pl.multiple_of(c * h, 16), h), pl.ds(pl.multiple_of((1 - c) * h, 16), h)


def _remote(src, dst, send_sems, recv_sems, k, dev):
    return pltpu.make_async_remote_copy(src_ref=src, dst_ref=dst, send_sem=send_sems.at[k], recv_sem=recv_sems.at[k],
                                        device_id=dev, device_id_type=MESH)


def _gather_shards(bufs, name):
    n = len(bufs)

    def body(*refs):
        ins, outs = refs[:n], refs[n:2 * n]
        send_sems, recv_sems = refs[2 * n:]
        x, y, c, me, chips, sib = _place()
        for i in range(n):
            mine, _ = _halves(ins[i].shape[0], c)
            for k, (px, py) in enumerate(chips):
                _remote(ins[i].at[mine], outs[i].at[me, mine], send_sems, recv_sems, 6 * i + k, (px, py, c)).start()
        for i in range(n):
            mine, _ = _halves(ins[i].shape[0], c)
            for k, (px, py) in enumerate(chips):
                pc = 2 * px + py
                _remote(ins[i].at[mine], outs[i].at[pc, mine], send_sems, recv_sems, 6 * i + k, (px, py, c)).wait_recv()
                _remote(outs[i].at[pc, mine], outs[i].at[pc, mine], send_sems, recv_sems, 6 * i + 3 + k, sib).start()
        for i in range(n):
            mine, other = _halves(ins[i].shape[0], c)
            for k, (px, py) in enumerate(chips):
                pc = 2 * px + py
                _remote(outs[i].at[pc, mine], outs[i].at[pc, other], send_sems, recv_sems, 6 * i + 3 + k, sib).wait_recv()
        for i in range(n):
            mine, _ = _halves(ins[i].shape[0], c)
            for k, (px, py) in enumerate(chips):
                pc = 2 * px + py
                _remote(ins[i].at[mine], outs[i].at[me, mine], send_sems, recv_sems, 6 * i + k, (px, py, c)).wait_send()
                _remote(outs[i].at[pc, mine], outs[i].at[pc, mine], send_sems, recv_sems, 6 * i + 3 + k, sib).wait_send()

    return pl.pallas_call(
        body, in_specs=[ANY] * n, out_specs=[ANY] * n,
        out_shape=[SDS((N_CHIPS,) + b.shape, b.dtype) for b in bufs],
        scratch_shapes=[pltpu.SemaphoreType.DMA((6 * n,)), pltpu.SemaphoreType.DMA((6 * n,))],
        name=name,
    )(*bufs)


def _exchange(srcs_of, out_shapes, n_sems, name):
    n = len(out_shapes)

    def body(*refs):
        n_in = len(refs) - n - 2
        ins, outs = refs[:n_in], refs[n_in:n_in + n]
        send_sems, recv_sems = refs[-2:]
        copies = [_remote(s, d, send_sems, recv_sems, k, dev) for s, d, k, dev in srcs_of(ins, outs, _place())]
        for cp in copies:
            cp.start()
        for cp in copies:
            cp.wait()

    def call(*arrs):
        return pl.pallas_call(
            body, in_specs=[ANY] * len(arrs), out_specs=[ANY] * n, out_shape=out_shapes,
            scratch_shapes=[pltpu.SemaphoreType.DMA((n_sems,)), pltpu.SemaphoreType.DMA((n_sems,))], name=name,
        )(*arrs)

    return call


HBM = pl.BlockSpec(memory_space=pltpu.HBM)
SEM = pl.BlockSpec(memory_space=pltpu.SEMAPHORE)
_SIDE_EFFECT = pltpu.SideEffectType.DATAFLOW_SIDE_EFFECTING


def _exchange_start(copies_of, srcs, land_shapes, n_sems, name):
    ns, nl = len(srcs), len(land_shapes)

    def body(*refs):
        ins = refs[:ns + nl]
        send_sems, recv_sems = refs[ns + nl], refs[ns + nl + 1]
        token = refs[-1]
        for s, d, k, dev in copies_of(ins[:ns], ins[ns:], _place()):
            _remote(s, d, send_sems, recv_sems, k, dev).start()
        token[...] = jnp.zeros_like(token)

    lands = [lax.empty(s.shape, s.dtype) for s in land_shapes]
    operands = [pltpu.with_memory_space_constraint(a, pltpu.HBM) for a in list(srcs) + lands]
    outs = pl.pallas_call(
        body, name=name,
        out_shape=(pltpu.SemaphoreType.DMA((n_sems,)), pltpu.SemaphoreType.DMA((n_sems,)),
                   *[pltpu.HBM(a.shape, a.dtype) for a in operands], SDS((8, LANES), f32)),
        in_specs=[HBM] * (ns + nl), out_specs=(SEM, SEM, *[HBM] * (ns + nl), pl.BlockSpec(memory_space=pltpu.VMEM)),
        input_output_aliases={i: 2 + i for i in range(ns + nl)},
        compiler_params=pltpu.CompilerParams(has_side_effects=_SIDE_EFFECT),
    )(*operands)
    return dict(send=outs[0], recv=outs[1], bufs=list(outs[2:2 + ns + nl]), token=outs[-1], ns=ns, copies_of=copies_of)


def _exchange_wait(handle, after, name):
    ns, bufs, copies_of = handle["ns"], handle["bufs"], handle["copies_of"]
    nb = len(bufs)

    def body(*refs):
        ins = refs[:nb]
        send_sems, recv_sems = refs[nb], refs[nb + 1]
        for s, d, k, dev in copies_of(ins[:ns], ins[ns:], _place()):
            cp = _remote(s, d, send_sems, recv_sems, k, dev)
            cp.wait_send()
            cp.wait_recv()

    outs = pl.pallas_call(
        body, name=name, out_shape=tuple(pltpu.HBM(a.shape, a.dtype) for a in bufs),
        in_specs=[HBM] * nb + [SEM, SEM, ANY], out_specs=tuple([HBM] * nb),
        input_output_aliases={i: i for i in range(nb)},
        compiler_params=pltpu.CompilerParams(has_side_effects=_SIDE_EFFECT),
    )(*bufs, handle["send"], handle["recv"], after)
    return list(outs[:ns]), list(outs[ns:])


def _pick(idx, parts):
    out = parts[-1]
    for j in reversed(range(len(parts) - 1)):
        out = jnp.where(idx == j, parts[j], out)
    return out


def _rs_pair(gs, name):
    def copies(ins, outs, place):
        x, y, c, me, chips, sib = place
        return [(g.at[:, _halves(g.shape[1], c)[1]], o, i, sib) for i, (g, o) in enumerate(zip(ins, outs))]

    shapes = [SDS((N_CHIPS, g.shape[1] // 2, g.shape[2]), g.dtype) for g in gs]
    return _exchange(copies, shapes, len(gs), name)(*gs)


def _gather_copies(ins, outs, place):
    x, y, c, me, chips, sib = place
    out = []
    for i, (s, o) in enumerate(zip(ins, outs)):
        mine, _ = _halves(s.shape[0], c)
        out += [(s.at[mine], o.at[me, mine], 3 * i + k, (px, py, c)) for k, (px, py) in enumerate(chips)]
    return out


def _gather_start(shards, name):
    return _exchange_start(_gather_copies, shards, [SDS((N_CHIPS,) + s.shape, s.dtype) for s in shards], 3 * len(shards), name)


def _gather_forward(lands, name):
    n = len(lands)

    def body(*refs):
        ins, outs = refs[:n], refs[n:2 * n]
        send_sems, recv_sems = refs[2 * n:]
        x, y, c, me, chips, sib = _place()
        copies = []
        for i in range(n):
            mine, _ = _halves(ins[i].shape[1], c)
            for k, (px, py) in enumerate(chips):
                pc = 2 * px + py
                copies.append(_remote(ins[i].at[pc, mine], outs[i].at[pc, mine], send_sems, recv_sems, 3 * i + k, sib))
        for cp in copies:
            cp.start()
        for cp in copies:
            cp.wait()

    return pl.pallas_call(
        body, in_specs=[ANY] * n, out_specs=[ANY] * n, out_shape=[SDS(a.shape, a.dtype) for a in lands],
        input_output_aliases={i: i for i in range(n)},
        scratch_shapes=[pltpu.SemaphoreType.DMA((3 * n,)), pltpu.SemaphoreType.DMA((3 * n,))], name=name,
    )(*lands)


def _rs_chips_copies(ins, outs, place):
    x, y, c, me, chips, sib = place
    return [(p.at[2 * px + py], o.at[k], 3 * i + k, (px, py, c))
            for i, (p, o) in enumerate(zip(ins, outs)) for k, (px, py) in enumerate(chips)]


def _rs_tile(h, n_cols):
    return _divisor(h, max(16, (1 << 20) // (4 * n_cols)), 16)


def _rs_add2(g, got, name):
    _, K, Nc = g.shape
    h = K // 2
    tr = _rs_tile(h, Nc)
    nb = h // tr

    def body(g_ref, got_ref, o_ref):
        o_ref[...] = (g_ref[...].astype(f32) + got_ref[...].astype(f32)).astype(o_ref.dtype)

    blk = lambda j, i: (j, i, 0)
    return pl.pallas_call(
        body, name=name, out_shape=SDS((N_CHIPS, h, Nc), bf16), grid=(N_CHIPS, nb),
        in_specs=[pl.BlockSpec((1, tr, Nc), lambda j, i: (j, lax.axis_index("c") * nb + i, 0)), pl.BlockSpec((1, tr, Nc), blk)],
        out_specs=pl.BlockSpec((1, tr, Nc), blk), compiler_params=_cp("parallel", "parallel"),
    )(g, got)


def _rs_add4(p, parts, name):
    _, h, Nc = p.shape
    tr = _rs_tile(h, Nc)
    nb = h // tr

    def body(own_ref, a_ref, b_ref, c_ref, o_ref):
        up = lambda r: r[0].astype(f32)
        o_ref[...] = (up(own_ref) + up(b_ref)) + (up(a_ref) + up(c_ref))

    part = lambda k: pl.BlockSpec((1, tr, Nc), lambda i, k=k: (k, i, 0))
    me = lambda: 2 * lax.axis_index("x") + lax.axis_index("y")
    return pl.pallas_call(
        body, name=name, out_shape=SDS((2 * h, Nc), f32), grid=(nb,),
        in_specs=[pl.BlockSpec((1, tr, Nc), lambda i: (me(), i, 0)), part(0), part(1), part(2)],
        out_specs=pl.BlockSpec((tr, Nc), lambda i: (lax.axis_index("c") * nb + i, 0)), compiler_params=_cp("parallel"),
    )(p, parts, parts, parts)


def _rs_join(fs, name):
    n = len(fs)

    def body(*refs):
        ins, outs = refs[:n], refs[n:2 * n]
        send_sems, recv_sems = refs[2 * n:]
        x, y, c, me, chips, sib = _place()
        copies = []
        for i in range(n):
            mine, _ = _halves(ins[i].shape[0], c)
            copies.append(_remote(ins[i].at[mine], outs[i].at[mine], send_sems, recv_sems, i, sib))
        for cp in copies:
            cp.start()
        for cp in copies:
            cp.wait()

    return pl.pallas_call(
        body, in_specs=[ANY] * n, out_specs=[ANY] * n, out_shape=[SDS(a.shape, a.dtype) for a in fs],
        input_output_aliases={i: i for i in range(n)},
        scratch_shapes=[pltpu.SemaphoreType.DMA((n,)), pltpu.SemaphoreType.DMA((n,))], name=name,
    )(*fs)


def _rs_begin(gs, names, tag):
    gots = _rs_pair(gs, f"rs_pair_{tag}")
    ps = [_rs_add2(g, got, f"rs_add2_{k}_{tag}") for g, got, k in zip(gs, gots, names)]
    lands = [SDS((3,) + p.shape[1:], p.dtype) for p in ps]
    return _exchange_start(_rs_chips_copies, ps, lands, 3 * len(ps), f"rs_chips_start_{tag}")


def _rs_end(handle, after, names, tag):
    ps, parts = _exchange_wait(handle, after, f"rs_chips_wait_{tag}")
    fs = [_rs_add4(p, part, f"rs_add4_{k}_{tag}") for p, part, k in zip(ps, parts, names)]
    return _rs_join(fs, f"rs_join_{tag}")


def _allreduce_small(buf, name):
    R, L = buf.shape

    def body(x_ref, o_ref, r0, s1, r1, send_sems, recv_sems):
        x, y, c, me, chips, sib = _place()
        cp = _remote(x_ref, r0, send_sems, recv_sems, 0, sib)
        cp.start()
        cp.wait()
        s1[...] = x_ref[...] + r0[...]
        cps = []
        for k, (px, py) in enumerate(chips):
            cp = _remote(s1, r1.at[k], send_sems, recv_sems, 1 + k, (px, py, c))
            cp.start()
            cps.append(cp)
        for cp in cps:
            cp.wait()
        o_ref[...] = (s1[...] + r1[1]) + (r1[0] + r1[2])

    vm = pl.BlockSpec(memory_space=pltpu.VMEM)
    return pl.pallas_call(
        body, in_specs=[vm], out_specs=vm, out_shape=SDS((R, L), f32),
        scratch_shapes=[pltpu.VMEM((R, L), f32), pltpu.VMEM((R, L), f32), pltpu.VMEM((3, R, L), f32),
                        pltpu.SemaphoreType.DMA((4,)), pltpu.SemaphoreType.DMA((4,))],
        name=name, compiler_params=pltpu.CompilerParams(vmem_limit_bytes=VMEM_LIMIT_BYTES),
    )(buf)


BIG = ("w_in", "w_branch_a", "w_branch_b", "w_out", "w_ffn_up", "w_ffn_down", "w_ple_gate", "w_ple_proj")
SMALL_REPL = ("norm_mix", "a_log", "dt_bias", "head_norm", "sgu_norm", "w_spatial", "b_spatial", "norm_ffn", "b_conv_ffn",
              "norm_ple", "norm_final")
SMALL_COLS = ("conv_qkv", "conv_ffn")
WEIGHTS = ("norm_mix", "w_in", "conv_qkv", "a_log", "dt_bias", "head_norm", "sgu_norm", "w_spatial", "b_spatial", "w_branch_a",
           "w_branch_b", "w_out", "norm_ffn", "w_ffn_up", "conv_ffn", "b_conv_ffn", "w_ffn_down", "norm_ple", "w_ple_gate",
           "w_ple_proj", "norm_final")


def _flat(arrs, dtype):
    cat = jnp.concatenate([a.astype(dtype).reshape(-1) for a in arrs])
    unit = LANES * ROW_ALIGN
    cat = jnp.pad(cat, (0, -cat.shape[0] % unit))
    return cat.reshape(-1, LANES)


def _unflat(buf, shapes):
    flat = buf.reshape(-1)
    out, off = [], 0
    for s in shapes:
        n = math.prod(s)
        out.append(flat[off:off + n].reshape(s))
        off += n
    return out


def _adamw(w, g, m, v):
    m2 = ADAM_B1 * m + (1.0 - ADAM_B1) * g
    v2 = ADAM_B2 * v + (1.0 - ADAM_B2) * (g * g)
    m_hat = m2 * (1.0 / (1.0 - ADAM_B1 ** ADAM_STEP))
    v_hat = v2 * (1.0 / (1.0 - ADAM_B2 ** ADAM_STEP))
    delta = -ADAM_LR * (m_hat / (jnp.sqrt(v_hat) + ADAM_EPS) + ADAM_WD * w)
    return delta, m2, v2


def _adamw_layer(w, g, m, v, prev, behind, layer, name):
    K, Nc = g.shape
    tr = _divisor(K, max(16, (1 << 20) // (4 * Nc)), 16)
    nb = K // tr
    rows = pl.BlockSpec((tr, Nc), lambda i: (layer * nb + i, 0))
    n_prev = 0 if prev is None else 4

    def body(w_ref, g_ref, m_ref, v_ref, behind_ref, *refs):
        go_ref, d_ref, mo_ref, vo_ref = refs[n_prev:]
        gv = g_ref[...]
        d, m2, v2 = _adamw(w_ref[...], gv, m_ref[...], v_ref[...])
        go_ref[...] = gv
        d_ref[...] = d
        mo_ref[...] = m2
        vo_ref[...] = v2

    return pl.pallas_call(
        body, grid=(nb,), in_specs=[rows, pl.BlockSpec((tr, Nc), lambda i: (i, 0)), rows, rows, ANY] + [ANY] * n_prev,
        out_specs=[rows] * 4, out_shape=[SDS(w.shape, f32)] * 4,
        input_output_aliases={5 + j: j for j in range(n_prev)}, name=name, compiler_params=_cp("parallel"),
    )(w, g, m, v, behind, *(prev or ()))


def kernel(x, p, norm_mix, w_in, conv_qkv, a_log, dt_bias, head_norm, sgu_norm, w_spatial, b_spatial, w_branch_a, w_branch_b, w_out, norm_ffn, w_ffn_up, conv_ffn, b_conv_ffn, w_ffn_down, norm_ple, w_ple_gate, w_ple_proj, norm_final, loss_target, m_norm_mix, m_w_in, m_conv_qkv, m_a_log, m_dt_bias, m_head_norm, m_sgu_norm, m_w_spatial, m_b_spatial, m_w_branch_a, m_w_branch_b, m_w_out, m_norm_ffn, m_w_ffn_up, m_conv_ffn, m_b_conv_ffn, m_w_ffn_down, m_norm_ple, m_w_ple_gate, m_w_ple_proj, m_norm_final, v_norm_mix, v_w_in, v_conv_qkv, v_a_log, v_dt_bias, v_head_norm, v_sgu_norm, v_w_spatial, v_b_spatial, v_w_branch_a, v_w_branch_b, v_w_out, v_norm_ffn, v_w_ffn_up, v_conv_ffn, v_b_conv_ffn, v_w_ffn_down, v_norm_ple, v_w_ple_gate, v_w_ple_proj, v_norm_final):
    w = dict(norm_mix=norm_mix, w_in=w_in, conv_qkv=conv_qkv, a_log=a_log, dt_bias=dt_bias, head_norm=head_norm, sgu_norm=sgu_norm,
             w_spatial=w_spatial, b_spatial=b_spatial, w_branch_a=w_branch_a, w_branch_b=w_branch_b, w_out=w_out, norm_ffn=norm_ffn,
             w_ffn_up=w_ffn_up, conv_ffn=conv_ffn, b_conv_ffn=b_conv_ffn, w_ffn_down=w_ffn_down, norm_ple=norm_ple,
             w_ple_gate=w_ple_gate, w_ple_proj=w_ple_proj, norm_final=norm_final)
    m = dict(norm_mix=m_norm_mix, w_in=m_w_in, conv_qkv=m_conv_qkv, a_log=m_a_log, dt_bias=m_dt_bias, head_norm=m_head_norm,
             sgu_norm=m_sgu_norm, w_spatial=m_w_spatial, b_spatial=m_b_spatial, w_branch_a=m_w_branch_a, w_branch_b=m_w_branch_b,
             w_out=m_w_out, norm_ffn=m_norm_ffn, w_ffn_up=m_w_ffn_up, conv_ffn=m_conv_ffn, b_conv_ffn=m_b_conv_ffn,
             w_ffn_down=m_w_ffn_down, norm_ple=m_norm_ple, w_ple_gate=m_w_ple_gate, w_ple_proj=m_w_ple_proj, norm_final=m_norm_final)
    v = dict(norm_mix=v_norm_mix, w_in=v_w_in, conv_qkv=v_conv_qkv, a_log=v_a_log, dt_bias=v_dt_bias, head_norm=v_head_norm,
             sgu_norm=v_sgu_norm, w_spatial=v_w_spatial, b_spatial=v_b_spatial, w_branch_a=v_w_branch_a, w_branch_b=v_w_branch_b,
             w_out=v_w_out, norm_ffn=v_norm_ffn, w_ffn_up=v_w_ffn_up, conv_ffn=v_conv_ffn, b_conv_ffn=v_b_conv_ffn,
             w_ffn_down=v_w_ffn_down, norm_ple=v_norm_ple, w_ple_gate=v_w_ple_gate, w_ple_proj=v_w_ple_proj, norm_final=v_norm_final)
    chip = 2 * lax.axis_index("x") + lax.axis_index("y")
    core = lax.axis_index("c")

    def by_chip(own, gathered):
        return [jnp.where(chip == j, own, gathered[j]) for j in range(N_CHIPS)]

    conv_buf = _flat([w[k] for k in SMALL_COLS], f32)
    (conv_all,) = _gather_shards([conv_buf], "gather_conv")
    sm = {k: w[k] for k in SMALL_REPL}
    conv_parts = [_unflat(b, [w[k].shape for k in SMALL_COLS]) for b in by_chip(conv_buf, conv_all)]
    for n, k in enumerate(SMALL_COLS):
        sm[k] = jnp.concatenate([conv_parts[j][n] for j in range(N_CHIPS)], axis=-1)
    Ss = [_small_params(sm, i) for i in range(DEPTH)]
    def start_gather(i, behind):
        own = [w[k][i].astype(bf16) for k in BIG]
        own[-1] = own[-1] + behind.astype(bf16)
        return _gather_start(own, f"gather_start_l{i}")

    h = x[0]
    saved, Ws = [], []
    gather = start_gather(0, jnp.zeros((), f32))
    for i in range(DEPTH):
        own, lands = _exchange_wait(gather, gather["token"] if i == 0 else h, f"gather_wait_l{i}")
        S_i = Ss[i]
        if i + 1 < DEPTH:
            gather = start_gather(i + 1, own[-1][0, 0] * 0)
            S_i = dict(S_i, norm_mix=S_i["norm_mix"] + gather["token"][0, 0])
        lands = _gather_forward(lands, f"gather_forward_l{i}")
        Ws.append(_big_weights({k: by_chip(o, g) for k, o, g in zip(BIG, own, lands)}))
        h, sv = _layer_fwd(h, p[i, 0], Ws[i], S_i, i)
        saved.append(sv)
    loss_row, dh, g_final = _loss_head(h, sm["norm_final"][None], loss_target[0], "loss_head")

    gsm = {k: [None] * DEPTH for k in SMALL_REPL + SMALL_COLS if k != "norm_final"}
    gbig = {k: [None] * DEPTH for k in BIG}
    pending = None

    def finish(job, after):
        j, handle = job
        for k, g in zip(BIG, _rs_end(handle, after, BIG, f"l{j}")):
            gbig[k][j] = g

    for i in reversed(range(DEPTH)):
        S_i = Ss[i] if pending is None else dict(Ss[i], norm_ple=Ss[i]["norm_ple"] + pending[1]["token"][0, 0])
        dh, G = _layer_bwd(dh, saved[i], Ws[i], S_i, i)
        parts = _grads_by_chip(G)
        begun = (i, _rs_begin([parts[k] for k in BIG], BIG, f"l{i}"))
        if pending is not None:
            finish(pending, dh)
        pending = begun
        for k in gsm:
            gsm[k][i] = G[k].reshape(w[k].shape[1:-1] + (-1,)) if k in SMALL_COLS else G[k].reshape(w[k].shape[1:])

    small_names = [k for k in SMALL_REPL + SMALL_COLS if k != "norm_final"]
    small_local = [jnp.stack(gsm[k]) for k in small_names] + [g_final.reshape(-1), loss_row[0, :1]]
    small_shapes = [a.shape for a in small_local]
    small_buf = _allreduce_small(_flat(small_local, f32), "allreduce_small")

    two = lambda a: a.reshape(-1, a.shape[-1])
    big_out = {k: None for k in BIG}

    def update_layer(j):
        for k in BIG:
            big_out[k] = _adamw_layer(two(w[k]), gbig[k][j], two(m[k]), two(v[k]), big_out[k], pending[1]["token"], j,
                                      f"adamw_{k}_l{j}")

    for j in reversed(range(1, DEPTH)):
        update_layer(j)
    finish(pending, big_out[BIG[-1]][1])
    update_layer(0)
    small_sum = _unflat(small_buf, small_shapes)
    gs = dict(zip(small_names + ["norm_final"], small_sum[:-1]))
    loss = small_sum[-1][0]
    for k in SMALL_COLS:
        n = w[k].shape[-1]
        gs[k] = _pick(chip, [gs[k][..., j * n:(j + 1) * n] for j in range(N_CHIPS)])

    grads, deltas, new_m, new_v = {}, {}, {}, {}
    for k in BIG:
        grads[k], deltas[k], new_m[k], new_v[k] = [a.reshape(w[k].shape) for a in big_out[k]]
    small_all = [k for k in WEIGHTS if k not in BIG]
    shapes = [w[k].shape for k in small_all]
    d, m2, v2 = _ew(_adamw, [_flat([w[k] for k in small_all], f32), _flat([gs[k] for k in small_all], f32),
                             _flat([m[k] for k in small_all], f32), _flat([v[k] for k in small_all], f32)],
                    [f32, f32, f32], "adamw_small")
    for k, a, b, c_ in zip(small_all, _unflat(d, shapes), _unflat(m2, shapes), _unflat(v2, shapes)):
        grads[k], deltas[k], new_m[k], new_v[k] = gs[k], a, b, c_
    return (loss, dh[None], *[grads[k] for k in WEIGHTS], *[deltas[k] for k in WEIGHTS],
            *[new_m[k] for k in WEIGHTS], *[new_v[k] for k in WEIGHTS])
```

```python
import functools
import math

import jax
import jax.numpy as jnp
from jax import lax
from jax.experimental import pallas as pl
from jax.experimental.pallas import tpu as pltpu

f32 = jnp.float32
bf16 = jnp.bfloat16
HI = lax.Precision.HIGHEST
SDS = jax.ShapeDtypeStruct

D_MODEL = 2048
DEPTH = 4
HEAD_DIM = 128
N_QK = 8
N_V = 16
QK_W = N_QK * HEAD_DIM
V_W = N_V * HEAD_DIM
CHUNK_A = 64
N_GROUPS = 8
GROUP_DIM = 128
WIDTH_B = N_GROUPS * GROUP_DIM
CHUNK_B = 128
D_FF = 5632
PLE_DIM = 256
EPS = 1e-6
N_IN = 12320
ADAM_LR, ADAM_B1, ADAM_B2, ADAM_EPS, ADAM_WD, ADAM_STEP = 0.001, 0.9, 0.999, 1e-08, 0.01, 10

C_Q, C_K, C_V, C_Z, C_UB, C_VB, C_GA, C_GB, PM = 0, 1024, 2048, 4096, 6144, 7168, 8192, 10240, 12288
LANES = 128
VMEM_LIMIT_BYTES = 48 * 1024 * 1024
ROW_CHUNK = 256


def _cp(*sem):
    return pltpu.CompilerParams(dimension_semantics=sem if sem else None, vmem_limit_bytes=VMEM_LIMIT_BYTES)


def _dot(a, b, prec=None):
    return jnp.dot(a, b, preferred_element_type=f32, precision=prec)


def _dot_nt(a, b, prec=None):
    return lax.dot_general(a, b, (((1,), (1,)), ((), ())), preferred_element_type=f32, precision=prec)


def _dot_tn(a, b, prec=None):
    return lax.dot_general(a, b, (((0,), (0,)), ((), ())), preferred_element_type=f32, precision=prec)


def _b(x):
    return x.astype(bf16)


def _dot3(a, b, dims=(((1,), (0,)), ((), ()))):
    ah, bh = _b(a), _b(b)
    al, bl = _b(a - ah.astype(f32)), _b(b - bh.astype(f32))
    dg = lambda p, q: lax.dot_general(p, q, dims, preferred_element_type=f32)
    return dg(ah, bh) + (dg(ah, bl) + dg(al, bh))


_TN = (((0,), (0,)), ((), ()))


def _sig(x):
    return 1.0 / (1.0 + jnp.exp(-x))


def _silu(x):
    return x * _sig(x)


def _dsilu(x):
    s = _sig(x)
    return s * (1.0 + x * (1.0 - s))


_GELU_C = 0.7978845608028654
_GELU_A = 0.044715


def _gelu(x):
    return 0.5 * x * (1.0 + jnp.tanh(_GELU_C * (x + _GELU_A * x * x * x)))


def _dgelu(x):
    t = jnp.tanh(_GELU_C * (x + _GELU_A * x * x * x))
    return 0.5 * (1.0 + t) + 0.5 * x * (1.0 - t * t) * _GELU_C * (1.0 + 3.0 * _GELU_A * x * x)


def _softplus(x):
    return jnp.maximum(x, 0.0) + jnp.log(1.0 + jnp.exp(-jnp.abs(x)))


def _divisor(n, cap, mult):
    best = None
    d = mult
    while d <= min(n, cap):
        if n % d == 0:
            best = d
        d += mult
    return best if best is not None else n


def _rows(tr, c, j=0):
    return pl.BlockSpec((tr, c), lambda i, j=j: (i, j))


def _whole(shape):
    nd = len(shape)
    return pl.BlockSpec(shape, lambda *_: (0,) * nd)


def _mm(a, b, mode, name, c=None, tm=1024, tn=1024, tk=2048):
    out_dtype = bf16 if mode == "tn" else f32
    if mode == "nn":
        (M, K), (K2, N) = a.shape, b.shape
    elif mode == "nt":
        (M, K), (N, K2) = a.shape, b.shape
    else:
        (K, M), (K2, N) = a.shape, b.shape
    assert K == K2, (a.shape, b.shape, mode)
    tm = _divisor(M, tm, LANES if mode == "tn" else 16)
    tn = _divisor(N, tn, LANES)
    tk = _divisor(K, tk, LANES if mode != "tn" else 16)
    nk = K // tk
    if mode == "nn":
        a_spec = pl.BlockSpec((tm, tk), lambda i, j, k: (i, k))
        b_spec = pl.BlockSpec((tk, tn), lambda i, j, k: (k, j))
        dn = (((1,), (0,)), ((), ()))
    elif mode == "nt":
        a_spec = pl.BlockSpec((tm, tk), lambda i, j, k: (i, k))
        b_spec = pl.BlockSpec((tn, tk), lambda i, j, k: (j, k))
        dn = (((1,), (1,)), ((), ()))
    else:
        a_spec = pl.BlockSpec((tk, tm), lambda i, j, k: (k, i))
        b_spec = pl.BlockSpec((tk, tn), lambda i, j, k: (k, j))
        dn = (((0,), (0,)), ((), ()))
    o_spec = pl.BlockSpec((tm, tn), lambda i, j, k: (i, j))
    has_c = c is not None

    def body(*refs):
        a_ref, b_ref = refs[0], refs[1]
        c_ref = refs[2] if has_c else None
        o_ref = refs[3] if has_c else refs[2]
        part = lax.dot_general(_b(a_ref[...]), _b(b_ref[...]), dn, preferred_element_type=f32)
        if nk == 1:
            if has_c:
                part = part + c_ref[...]
            o_ref[...] = part.astype(o_ref.dtype)
        else:
            acc = refs[-1]
            k = pl.program_id(2)

            @pl.when(k == 0)
            def _():
                acc[...] = part

            @pl.when(k > 0)
            def _():
                acc[...] += part

            @pl.when(k == nk - 1)
            def _():
                r = acc[...]
                if has_c:
                    r = r + c_ref[...]
                o_ref[...] = r.astype(o_ref.dtype)

    ins = [a, b] + ([c] if has_c else [])
    in_specs = [a_spec, b_spec] + ([o_spec] if has_c else [])
    return pl.pallas_call(
        body, grid=(M // tm, N // tn, nk), in_specs=in_specs, out_specs=o_spec, out_shape=SDS((M, N), out_dtype),
        scratch_shapes=[pltpu.VMEM((tm, tn), f32)] if nk > 1 else [], name=name,
        compiler_params=_cp("parallel", "parallel", "arbitrary"),
    )(*ins)


def _ew(fn, ins, out_dtypes, name, tile_bytes=2 * 1024 * 1024):
    R, C = max((x.shape for x in ins), key=lambda s: s[0])
    n_in = len(ins)
    row_bytes = 4 * C * (len(ins) + len(out_dtypes))
    tr = _divisor(R, max(16, tile_bytes // row_bytes), 16)
    in_specs = [_rows(tr, C) if x.shape[0] == R else _whole((1, C)) for x in ins]

    def body(*refs):
        res = fn(*[r[...] for r in refs[:n_in]])
        for o_ref, r in zip(refs[n_in:], res):
            o_ref[...] = r.astype(o_ref.dtype)

    outs = pl.pallas_call(
        body, grid=(R // tr,), in_specs=in_specs, out_specs=[_rows(tr, C) for _ in out_dtypes],
        out_shape=[SDS((R, C), dt) for dt in out_dtypes], name=name, compiler_params=_cp("parallel"),
    )(*ins)
    return outs


def _rms_fwd(x, gain, name):
    T, Dm = x.shape
    tr = _divisor(T, 256, 16)

    def body(x_ref, g_ref, o_ref):
        xv = x_ref[...]
        r = lax.rsqrt(jnp.mean(xv * xv, axis=-1, keepdims=True) + EPS)
        o_ref[...] = (xv * r * g_ref[...]).astype(o_ref.dtype)

    return pl.pallas_call(
        body, grid=(T // tr,), in_specs=[_rows(tr, Dm), _whole((1, Dm))], out_specs=_rows(tr, Dm),
        out_shape=SDS((T, Dm), bf16), name=name, compiler_params=_cp("parallel"),
    )(x, gain)


def _rms_bwd(x, gain, dh, dres, name):
    T, Dm = x.shape
    tr = _divisor(T, 256, 16)

    def body(x_ref, g_ref, dh_ref, dres_ref, dx_ref, dg_ref):
        xv = x_ref[...]
        dhv = dh_ref[...]
        r = lax.rsqrt(jnp.mean(xv * xv, axis=-1, keepdims=True) + EPS)
        dy = dhv * g_ref[...]
        m = jnp.mean(dy * xv, axis=-1, keepdims=True)
        dx_ref[...] = dres_ref[...] + r * dy - xv * (r * r * r * m)
        part = jnp.sum(dhv * xv * r, axis=0, keepdims=True)

        @pl.when(pl.program_id(0) == 0)
        def _():
            dg_ref[...] = part

        @pl.when(pl.program_id(0) > 0)
        def _():
            dg_ref[...] += part

    return pl.pallas_call(
        body, grid=(T // tr,), in_specs=[_rows(tr, Dm), _whole((1, Dm)), _rows(tr, Dm), _rows(tr, Dm)],
        out_specs=[_rows(tr, Dm), _whole((1, Dm))], out_shape=[SDS((T, Dm), f32), SDS((1, Dm), f32)],
        name=name, compiler_params=_cp("arbitrary"),
    )(x, gain, dh, dres)


def _shift_down(cur, prev8, s):
    rolled = pltpu.roll(cur, s, 0)
    rp = pltpu.roll(prev8, s, 0)
    row8 = lax.broadcasted_iota(jnp.int32, prev8.shape, 0)
    first = jnp.where(row8 < s, rp, rolled[:8])
    return jnp.concatenate([first, rolled[8:]], axis=0)


def _shift_up(cur, next8, s):
    R = cur.shape[0]
    rolled = pltpu.roll(cur, R - s, 0)
    rn = pltpu.roll(next8, 8 - s, 0)
    row8 = lax.broadcasted_iota(jnp.int32, next8.shape, 0)
    last = jnp.where(row8 >= 8 - s, rn, rolled[R - 8:])
    return jnp.concatenate([rolled[: R - 8], last], axis=0)


def _prev8(ref, r0, ci):
    rows = ref[pl.ds(pl.multiple_of(jnp.maximum(r0 - 8, 0), 8), 8), :]
    return jnp.where(ci > 0, rows, 0.0)


def _next8(ref, r0, R, ci, n_chunks):
    start = jnp.minimum(r0 + R, (n_chunks - 1) * R + R - 8)
    rows = ref[pl.ds(pl.multiple_of(start, 8), 8), :]
    return jnp.where(ci < n_chunks - 1, rows, 0.0)


def _w_rows(w_ref, k):
    return [w_ref[j:j + 1, :] for j in range(k)]


def _causal_conv(cur, prev8, w, k):
    acc = cur * w[k - 1]
    for s in range(1, k):
        acc = acc + _shift_down(cur, prev8, s) * w[k - 1 - s]
    return acc


def _qkvconv_fwd(proj, convw, name):
    T = proj.shape[0]
    R = min(ROW_CHUNK, T)
    n_chunks = T // R
    n_blk = (2 * QK_W + V_W) // HEAD_DIM

    def body(x_ref, w_ref, o_ref):
        p = pl.program_id(0)
        is_qk = p < 2 * N_QK
        scale = jnp.where(p < N_QK, HEAD_DIM ** -0.5, 1.0).astype(f32)
        w = _w_rows(w_ref, 4)

        def chunk(ci, carry):
            r0 = pl.multiple_of(ci * R, R)
            cur = x_ref[pl.ds(r0, R), :]
            y = _silu(_causal_conv(cur, _prev8(x_ref, r0, ci), w, 4))
            ss = jnp.sum(y * y, axis=-1, keepdims=True)
            nrm = jnp.where(is_qk, lax.rsqrt(ss + EPS) * scale, 1.0)
            o_ref[pl.ds(r0, R), :] = y * nrm
            return carry

        lax.fori_loop(0, n_chunks, chunk, 0)

    return pl.pallas_call(
        body, grid=(n_blk,),
        in_specs=[pl.BlockSpec((T, HEAD_DIM), lambda p: (0, p)), pl.BlockSpec((4, HEAD_DIM), lambda p: (0, p))],
        out_specs=pl.BlockSpec((T, HEAD_DIM), lambda p: (0, p)), out_shape=SDS((T, n_blk * HEAD_DIM), f32),
        name=name, compiler_params=_cp("parallel"),
    )(proj, convw)


def _qkvconv_bwd(proj, convw, dqkv, dproj, name):
    T = proj.shape[0]
    R = min(ROW_CHUNK, T)
    n_chunks = T // R
    n_blk = (2 * QK_W + V_W) // HEAD_DIM

    def body(x_ref, w_ref, do_ref, dproj_ref, dx_ref, dw_ref, dc_sc):
        p = pl.program_id(0)
        is_qk = p < 2 * N_QK
        scale = jnp.where(p < N_QK, HEAD_DIM ** -0.5, 1.0).astype(f32)
        w = _w_rows(w_ref, 4)

        def phase1(ci, dw):
            r0 = pl.multiple_of(ci * R, R)
            cur = x_ref[pl.ds(r0, R), :]
            p8 = _prev8(x_ref, r0, ci)
            shifted = [cur] + [_shift_down(cur, p8, s) for s in range(1, 4)]
            c = shifted[0] * w[3]
            for s in range(1, 4):
                c = c + shifted[s] * w[3 - s]
            y = _silu(c)
            dout = do_ref[pl.ds(r0, R), :]
            n = lax.rsqrt(jnp.sum(y * y, axis=-1, keepdims=True) + EPS)
            dot_ = jnp.sum(dout * y, axis=-1, keepdims=True)
            dy = jnp.where(is_qk, scale * (n * dout - y * (n * n * n * dot_)), dout)
            dc = dy * _dsilu(c)
            dc_sc[pl.ds(r0, R), :] = dc
            return tuple(dw[j] + jnp.sum(dc * shifted[3 - j], axis=0, keepdims=True) for j in range(4))

        dw = lax.fori_loop(0, n_chunks, phase1, tuple(jnp.zeros((1, HEAD_DIM), f32) for _ in range(4)))
        for j in range(4):
            dw_ref[j:j + 1, :] = dw[j]

        def phase2(ci, carry):
            r0 = pl.multiple_of(ci * R, R)
            cur = dc_sc[pl.ds(r0, R), :]
            n8 = _next8(dc_sc, r0, R, ci, n_chunks)
            acc = cur * w[3]
            for s in range(1, 4):
                acc = acc + _shift_up(cur, n8, s) * w[3 - s]
            dx_ref[pl.ds(r0, R), :] = acc.astype(dx_ref.dtype)
            return carry

        lax.fori_loop(0, n_chunks, phase2, 0)

    col = lambda p: (0, p)
    return pl.pallas_call(
        body, grid=(n_blk,),
        in_specs=[pl.BlockSpec((T, HEAD_DIM), col), pl.BlockSpec((4, HEAD_DIM), col), pl.BlockSpec((T, HEAD_DIM), col), ANY],
        out_specs=[pl.BlockSpec((T, HEAD_DIM), col), pl.BlockSpec((4, HEAD_DIM), col)],
        out_shape=[SDS(dproj.shape, bf16), SDS((4, n_blk * HEAD_DIM), f32)], input_output_aliases={3: 0},
        scratch_shapes=[pltpu.VMEM((T, HEAD_DIM), f32)], name=name, compiler_params=_cp("parallel"),
    )(proj, convw, dqkv, dproj)


def _tri_masks(C):
    row = lax.broadcasted_iota(jnp.int32, (C, C), 0)
    col = lax.broadcasted_iota(jnp.int32, (C, C), 1)
    return row, col


def _lane_pick(blk, lane, idx):
    return jnp.sum(jnp.where(lane == idx, blk, 0.0), axis=1, keepdims=True)


def _gate_block(ba, ea, dtb, lane):
    sig = _sig(ba)
    gblk = -ea * _softplus(ba + dtb)
    return sig, gblk


def _head_decay(gam_all, rg_all, tot, lane, h, row, col):
    C = row.shape[0]
    gam_c = _lane_pick(gam_all, lane, N_V + h)
    rg_c = _lane_pick(rg_all, lane, N_V + h)
    lane1 = lax.broadcasted_iota(jnp.int32, (1, LANES), 1)
    tot_h = jnp.sum(jnp.where(lane1 == N_V + h, tot, 0.0), axis=1, keepdims=True)
    gcb = jnp.broadcast_to(gam_c, (C, C))
    dlt = gcb - gcb.T
    dm = jnp.where(row >= col, jnp.exp(jnp.minimum(dlt, 0.0)), 0.0)
    return gam_c, rg_c, tot_h, dm


def _delta_prep(qkv, ba, ea_row, dtb_row, name):
    T = qkv.shape[0]
    C = CHUNK_A
    N = T // C

    SUB = 16

    def body(q_ref, k_ref, v_ref, ba_ref, ea_ref, dtb_ref,
             qd_ref, kd_ref, u_ref, w_ref, attn_ref, tinv_ref, cd_ref, bg_ref, at_sc, t_sc, a_sc, rk_sc):
        row, col = _tri_masks(C)
        lane = lax.broadcasted_iota(jnp.int32, (C, LANES), 1)
        sig, gblk = _gate_block(ba_ref[...], ea_ref[...], dtb_ref[...], lane)
        bg_ref[...] = jnp.where(lane < N_V, sig, gblk)
        lower = (row >= col).astype(f32)
        upper_s = (col > row).astype(f32)
        eye = (row == col).astype(f32)
        same16 = (row >> 4) == (col >> 4)
        same32 = (row >> 5) == (col >> 5)
        gam_all = _dot(lower, gblk, HI)
        rg_all = _dot(upper_s, gblk, HI)
        tot = jnp.sum(gblk, axis=0, keepdims=True)
        for h in range(N_V):
            qk = pl.ds((h // 2) * HEAD_DIM, HEAD_DIM)
            hs = pl.ds(h * HEAD_DIM, HEAD_DIM)
            qh, kh, vh = q_ref[:, qk], k_ref[:, qk], v_ref[:, hs]
            beta_c = _lane_pick(sig, lane, h)
            gam_c, rg_c, tot_h, dm = _head_decay(gam_all, rg_all, tot, lane, h, row, col)
            kk = _dot_nt(_b(kh), _b(kh))
            a = jnp.where(row > col, beta_c * kk * dm, 0.0)
            a_sc[h] = a
            at_sc[h] = jnp.where(same16, a, 0.0).T
            t_sc[h] = eye
            eg = jnp.exp(gam_c)
            attn_ref[0, h] = (_dot_nt(_b(qh), _b(kh)) * dm).astype(attn_ref.dtype)
            qd_ref[:, hs] = (qh * eg).astype(qd_ref.dtype)
            kd_ref[:, hs] = (kh * jnp.exp(rg_c)).astype(kd_ref.dtype)
            cd_ref[0, h] = jnp.broadcast_to(jnp.exp(tot_h), (1, LANES))
            u_ref[:, hs] = vh * beta_c
            rk_sc[:, hs] = kh * (beta_c * eg)

        first_col = (row >> 4) << 4

        def fsub(i, carry):
            for h in range(N_V):
                t = t_sc[h]
                a_col = jnp.sum(jnp.where(col == first_col + i, at_sc[h], 0.0), axis=1, keepdims=True)
                prod = a_col * t
                sums = [jnp.sum(prod[b * SUB:(b + 1) * SUB], axis=0, keepdims=True) for b in range(C // SUB)]
                new = eye - jnp.concatenate([jnp.broadcast_to(s, (SUB, C)) for s in sums], axis=0)
                t_sc[h] = jnp.where(row - first_col == i, new, t)
            return carry

        lax.fori_loop(1, SUB, fsub, 0)
        pair16 = jnp.logical_and(same32, jnp.logical_not(same16))
        for h in range(N_V):
            at_sc[h] = _dot3(t_sc[h], jnp.where(pair16, a_sc[h], 0.0))
        for h in range(N_V):
            p16 = t_sc[h]
            t_sc[h] = p16 - _dot3(at_sc[h], p16)
        for h in range(N_V):
            at_sc[h] = _dot3(t_sc[h], jnp.where(same32, 0.0, a_sc[h]))
        for h in range(N_V):
            p32 = t_sc[h]
            tinv_ref[0, h] = p32 - _dot3(at_sc[h], p32)
        for h in range(N_V):
            hs = pl.ds(h * HEAD_DIM, HEAD_DIM)
            u_ref[:, hs] = _dot3(tinv_ref[0, h], u_ref[:, hs])
        for h in range(N_V):
            hs = pl.ds(h * HEAD_DIM, HEAD_DIM)
            w_ref[:, hs] = _dot3(tinv_ref[0, h], rk_sc[:, hs]).astype(w_ref.dtype)

    big = lambda n: (n, 0)
    return pl.pallas_call(
        body, grid=(N,),
        in_specs=[pl.BlockSpec((C, QK_W), lambda n: (n, 0)), pl.BlockSpec((C, QK_W), lambda n: (n, 1)),
                  pl.BlockSpec((C, V_W), lambda n: (n, 1)), pl.BlockSpec((C, LANES), big),
                  _whole((1, LANES)), _whole((1, LANES))],
        out_specs=[pl.BlockSpec((C, V_W), big)] * 4 + [
            pl.BlockSpec((1, N_V, C, C), lambda n: (n, 0, 0, 0)), pl.BlockSpec((1, N_V, C, C), lambda n: (n, 0, 0, 0)),
            pl.BlockSpec((1, N_V, 1, LANES), lambda n: (n, 0, 0, 0)), pl.BlockSpec((C, LANES), big)],
        out_shape=[SDS((T, V_W), bf16), SDS((T, V_W), bf16), SDS((T, V_W), f32), SDS((T, V_W), bf16),
                   SDS((N, N_V, C, C), bf16), SDS((N, N_V, C, C), f32), SDS((N, N_V, 1, LANES), f32), SDS((T, LANES), f32)],
        scratch_shapes=[pltpu.VMEM((N_V, C, C), f32)] * 3 + [pltpu.VMEM((C, V_W), f32)],
        name=name, compiler_params=_cp("parallel"),
    )(qkv, qkv, qkv, ba, ea_row, dtb_row)


def _delta_scan(qd, kd, u, w, attn, cd, name):
    T = qd.shape[0]
    C = CHUNK_A
    N = T // C

    def body(qd_ref, kd_ref, u_ref, w_ref, attn_ref, cd_ref, o_ref, s_ref, vn_ref, s_sc):
        @pl.when(pl.program_id(0) == 0)
        def _():
            s_sc[...] = jnp.zeros_like(s_sc)

        heads = [(h, pl.ds(h * HEAD_DIM, HEAD_DIM)) for h in range(N_V)]
        for h, hs in heads:
            s_ref[0, h] = s_sc[h]
            vn_ref[:, hs] = _b(u_ref[:, hs] - _dot(w_ref[:, hs], _b(s_sc[h])))
        for h, hs in heads:
            o_ref[:, hs] = _dot(qd_ref[:, hs], _b(s_sc[h])) + _dot(attn_ref[0, h], vn_ref[:, hs])
        for h, hs in heads:
            s_sc[h] = s_sc[h] * cd_ref[0, h] + _dot_tn(kd_ref[:, hs], vn_ref[:, hs])

    blk = pl.BlockSpec((C, V_W), lambda n: (n, 0))
    per_head = lambda a, b: pl.BlockSpec((1, N_V, a, b), lambda n: (n, 0, 0, 0))
    return pl.pallas_call(
        body, grid=(N,), in_specs=[blk, blk, blk, blk, per_head(C, C), per_head(1, LANES)],
        out_specs=[blk, per_head(HEAD_DIM, HEAD_DIM), blk],
        out_shape=[SDS((T, V_W), f32), SDS((N, N_V, HEAD_DIM, HEAD_DIM), f32), SDS((T, V_W), bf16)],
        scratch_shapes=[pltpu.VMEM((N_V, HEAD_DIM, HEAD_DIM), f32)], name=name, compiler_params=_cp("arbitrary"),
    )(qd, kd, u, w, attn, cd)


def _delta_scan_bwd(do, qd, kd, w, attn, cd, s_all, vn, name):
    T = qd.shape[0]
    C = CHUNK_A
    N = T // C

    def body(do_ref, qd_ref, kd_ref, w_ref, attn_ref, cd_ref, s_ref, vn_ref,
             dqd_ref, dkd_ref, du_ref, dw_ref, dattn_ref, dcd_ref, ds_sc):
        @pl.when(pl.program_id(0) == 0)
        def _():
            ds_sc[...] = jnp.zeros_like(ds_sc)

        row, col = _tri_masks(C)
        heads = [(h, pl.ds(h * HEAD_DIM, HEAD_DIM)) for h in range(N_V)]
        for h, hs in heads:
            dob = _b(do_ref[:, hs])
            dqd_ref[:, hs] = _dot_nt(dob, _b(s_ref[0, h]))
            dattn_ref[0, h] = jnp.where(row >= col, _dot_nt(dob, vn_ref[:, hs]), 0.0)
        for h, hs in heads:
            dsn = ds_sc[h]
            dkd_ref[:, hs] = _dot_nt(vn_ref[:, hs], _b(dsn))
            dcd = jnp.sum(jnp.sum(s_ref[0, h] * dsn, axis=1, keepdims=True), axis=0, keepdims=True)
            dcd_ref[0, h] = jnp.broadcast_to(dcd, (1, LANES))
        for h, hs in heads:
            du_ref[:, hs] = _dot_tn(attn_ref[0, h], _b(do_ref[:, hs])) + _dot(kd_ref[:, hs], _b(ds_sc[h]))
        for h, hs in heads:
            dw_ref[:, hs] = -_dot_nt(_b(du_ref[:, hs]), _b(s_ref[0, h]))
        for h, hs in heads:
            ds_sc[h] = (ds_sc[h] * cd_ref[0, h] + _dot_tn(qd_ref[:, hs], _b(do_ref[:, hs]))
                        - _dot_tn(w_ref[:, hs], _b(du_ref[:, hs])))

    blk = pl.BlockSpec((C, V_W), lambda n: (N - 1 - n, 0))
    per_head = lambda a, b: pl.BlockSpec((1, N_V, a, b), lambda n: (N - 1 - n, 0, 0, 0))
    return pl.pallas_call(
        body, grid=(N,),
        in_specs=[blk, blk, blk, blk, per_head(C, C), per_head(1, LANES), per_head(HEAD_DIM, HEAD_DIM), blk],
        out_specs=[blk, blk, blk, blk, per_head(C, C), per_head(1, LANES)],
        out_shape=[SDS((T, V_W), f32)] * 4 + [SDS((N, N_V, C, C), f32), SDS((N, N_V, 1, LANES), f32)],
        scratch_shapes=[pltpu.VMEM((N_V, HEAD_DIM, HEAD_DIM), f32)], name=name, compiler_params=_cp("arbitrary"),
    )(do, qd, kd, w, attn, cd, s_all, vn)


def _delta_prep_bwd(qkv, ba, ea_row, dtb_row, tinv, u, w, dqd, dkd, du, dw, dattn, dcd, name):
    T = qkv.shape[0]
    C = CHUNK_A
    N = T // C

    def body(q_ref, k_ref, v_ref, ba_ref, ea_ref, dtb_ref, tinv_ref, u_ref, w_ref,
             dqd_ref, dkd_ref, du_ref, dw_ref, dattn_ref, dcd_ref,
             dqkv_ref, dba_ref, dalog_ref, ddtb_ref, drv_sc, drk_sc, da_sc):
        row, col = _tri_masks(C)
        lane = lax.broadcasted_iota(jnp.int32, (C, LANES), 1)
        rowc = lax.broadcasted_iota(jnp.int32, (C, 1), 0)
        for h in range(N_V):
            hs = pl.ds(h * HEAD_DIM, HEAD_DIM)
            drv_sc[:, hs] = _dot3(tinv_ref[0, h], du_ref[:, hs], _TN)
        for h in range(N_V):
            hs = pl.ds(h * HEAD_DIM, HEAD_DIM)
            drk_sc[:, hs] = _dot3(tinv_ref[0, h], dw_ref[:, hs], _TN)
        for h in range(N_V):
            hs = pl.ds(h * HEAD_DIM, HEAD_DIM)
            da_sc[h] = -jnp.where(row > col, _dot_nt(_b(drv_sc[:, hs]), _b(u_ref[:, hs]))
                                  + _dot_nt(_b(drk_sc[:, hs]), w_ref[:, hs]), 0.0)
        ba_v, ea, dtb = ba_ref[...], ea_ref[...], dtb_ref[...]
        sig, gblk = _gate_block(ba_v, ea, dtb, lane)
        lower = (row >= col).astype(f32)
        upper_s = (col > row).astype(f32)
        upper = (col >= row).astype(f32)
        gam_all = _dot(lower, gblk, HI)
        rg_all = _dot(upper_s, gblk, HI)
        tot = jnp.sum(gblk, axis=0, keepdims=True)
        dbeta_blk = jnp.zeros((C, LANES), f32)
        dgam_blk = jnp.zeros((C, LANES), f32)
        for j in range(N_QK):
            qk = pl.ds(j * HEAD_DIM, HEAD_DIM)
            qh, kh = q_ref[:, qk], k_ref[:, qk]
            qhb, khb = _b(qh), _b(kh)
            kk = _dot_nt(khb, khb)
            qkm = _dot_nt(qhb, khb)
            dq_j = jnp.zeros((C, HEAD_DIM), f32)
            dk_j = jnp.zeros((C, HEAD_DIM), f32)
            for h in (2 * j, 2 * j + 1):
                hs = pl.ds(h * HEAD_DIM, HEAD_DIM)
                vh = v_ref[:, hs]
                beta_c = _lane_pick(sig, lane, h)
                gam_c, rg_c, tot_h, dm = _head_decay(gam_all, rg_all, tot, lane, h, row, col)
                eg, er, cdh = jnp.exp(gam_c), jnp.exp(rg_c), jnp.exp(tot_h)
                d_rv, d_rk, da = drv_sc[:, hs], drk_sc[:, hs], da_sc[h]
                dqkv_ref[:, pl.ds(2 * QK_W + h * HEAD_DIM, HEAD_DIM)] = beta_c * d_rv
                dbeta = jnp.sum(d_rv * vh + d_rk * (eg * kh), axis=1, keepdims=True)
                dk_h = (beta_c * eg) * d_rk
                d_eg = jnp.sum(d_rk * kh, axis=1, keepdims=True) * beta_c
                bkd = da * dm
                dbeta = dbeta + jnp.sum(bkd * kk, axis=1, keepdims=True)
                dkk = bkd * beta_c
                ddm = da * beta_c * kk
                dattn_h = dattn_ref[0, h]
                dqk = dattn_h * dm
                ddm = ddm + dattn_h * qkm
                dqd_h, dkd_h = dqd_ref[:, hs], dkd_ref[:, hs]
                dq_j = dq_j + _dot(_b(dqk), khb) + eg * dqd_h
                dk_h = dk_h + _dot_tn(_b(dqk), qhb) + _dot(_b(dkk + dkk.T), khb) + er * dkd_h
                dk_j = dk_j + dk_h
                d_eg = d_eg + jnp.sum(dqd_h * qh, axis=1, keepdims=True)
                d_er = jnp.sum(dkd_h * kh, axis=1, keepdims=True)
                e = ddm * dm
                dgam = jnp.sum(e, axis=1, keepdims=True) - jnp.sum(e.T, axis=1, keepdims=True)
                dgam = dgam + d_eg * eg - d_er * er
                extra = jnp.sum(d_er * er, axis=0, keepdims=True) + jnp.max(dcd_ref[0, h], axis=1, keepdims=True) * cdh
                dgam = dgam + jnp.where(rowc == C - 1, extra, 0.0)
                dbeta_blk = jnp.where(lane == h, dbeta, dbeta_blk)
                dgam_blk = jnp.where(lane == N_V + h, dgam, dgam_blk)
            dqkv_ref[:, qk] = dq_j
            dqkv_ref[:, pl.ds(QK_W + j * HEAD_DIM, HEAD_DIM)] = dk_j
        dg_all = _dot(upper, dgam_blk, HI)
        dsp = dg_all * (-ea) * _sig(ba_v + dtb)
        dba_ref[...] = jnp.where(lane < N_V, dbeta_blk * sig * (1.0 - sig), dsp)
        part_alog = jnp.sum(dg_all * gblk, axis=0, keepdims=True)
        part_dtb = jnp.sum(dsp, axis=0, keepdims=True)

        @pl.when(pl.program_id(0) == 0)
        def _():
            dalog_ref[...] = part_alog
            ddtb_ref[...] = part_dtb

        @pl.when(pl.program_id(0) > 0)
        def _():
            dalog_ref[...] += part_alog
            ddtb_ref[...] += part_dtb

    big = lambda n: (n, 0)
    wide = pl.BlockSpec((C, V_W), big)
    sq = pl.BlockSpec((1, N_V, C, C), lambda n: (n, 0, 0, 0))
    return pl.pallas_call(
        body, grid=(N,),
        in_specs=[pl.BlockSpec((C, QK_W), lambda n: (n, 0)), pl.BlockSpec((C, QK_W), lambda n: (n, 1)),
                  pl.BlockSpec((C, V_W), lambda n: (n, 1)), pl.BlockSpec((C, LANES), big),
                  _whole((1, LANES)), _whole((1, LANES)), sq, wide, wide, wide, wide, wide, wide, sq,
                  pl.BlockSpec((1, N_V, 1, LANES), lambda n: (n, 0, 0, 0))],
        out_specs=[pl.BlockSpec((C, 2 * QK_W + V_W), big), pl.BlockSpec((C, LANES), big),
                   _whole((1, LANES)), _whole((1, LANES))],
        out_shape=[SDS((T, 2 * QK_W + V_W), f32), SDS((T, LANES), f32), SDS((1, LANES), f32), SDS((1, LANES), f32)],
        scratch_shapes=[pltpu.VMEM((C, V_W), f32), pltpu.VMEM((C, V_W), f32), pltpu.VMEM((N_V, C, C), f32)],
        name=name, compiler_params=_cp("arbitrary"),
    )(qkv, qkv, qkv, ba, ea_row, dtb_row, tinv, u, w, dqd, dkd, du, dw, dattn, dcd)


def _onorm_fwd(o, proj, hg, name):
    T = o.shape[0]
    tr = _divisor(T, 256, 16)

    def body(o_ref, z_ref, g_ref, out_ref):
        g = g_ref[...]
        for h in range(N_V):
            hs = pl.ds(h * HEAD_DIM, HEAD_DIM)
            oh = o_ref[:, hs]
            r = lax.rsqrt(jnp.mean(oh * oh, axis=-1, keepdims=True) + EPS)
            out_ref[:, hs] = (oh * r * g * _silu(z_ref[:, hs])).astype(out_ref.dtype)

    return pl.pallas_call(
        body, grid=(T // tr,), in_specs=[_rows(tr, V_W), _rows(tr, V_W, C_Z // V_W), _whole((1, HEAD_DIM))],
        out_specs=_rows(tr, V_W), out_shape=SDS((T, V_W), bf16), name=name, compiler_params=_cp("parallel"),
    )(o, proj, hg)


def _onorm_bwd(don, o, proj, hg, dproj, name):
    T = o.shape[0]
    tr = _divisor(T, 256, 16)

    def body(don_ref, o_ref, z_ref, g_ref, dproj_ref, do_ref, dz_ref, dg_ref):
        g = g_ref[...]
        dg = jnp.zeros((1, HEAD_DIM), f32)
        for h in range(N_V):
            hs = pl.ds(h * HEAD_DIM, HEAD_DIM)
            oh, zh, dh = o_ref[:, hs], z_ref[:, hs], don_ref[:, hs]
            r = lax.rsqrt(jnp.mean(oh * oh, axis=-1, keepdims=True) + EPS)
            d_n = dh * _silu(zh)
            dz_ref[:, hs] = (dh * (oh * r * g) * _dsilu(zh)).astype(dz_ref.dtype)
            dy = d_n * g
            m = jnp.mean(dy * oh, axis=-1, keepdims=True)
            do_ref[:, hs] = r * dy - oh * (r * r * r * m)
            dg = dg + jnp.sum(d_n * oh * r, axis=0, keepdims=True)

        @pl.when(pl.program_id(0) == 0)
        def _():
            dg_ref[...] = dg

        @pl.when(pl.program_id(0) > 0)
        def _():
            dg_ref[...] += dg

    return pl.pallas_call(
        body, grid=(T // tr,),
        in_specs=[_rows(tr, V_W), _rows(tr, V_W), _rows(tr, V_W, C_Z // V_W), _whole((1, HEAD_DIM)), ANY],
        out_specs=[_rows(tr, V_W), _rows(tr, V_W, C_Z // V_W), _whole((1, HEAD_DIM))],
        out_shape=[SDS((T, V_W), f32), SDS(dproj.shape, bf16), SDS((1, HEAD_DIM), f32)], input_output_aliases={4: 1},
        name=name, compiler_params=_cp("arbitrary"),
    )(don, o, proj, hg, dproj)


def _sgu_parts(ub, vb, gain):
    gv = _gelu(vb)
    r = lax.rsqrt(jnp.mean(gv * gv, axis=-1, keepdims=True) + EPS)
    return _gelu(ub), gv, r, gv * r * gain


def _sgu_fwd(proj, gain, w_s, b_bc, name):
    T = proj.shape[0]
    C = CHUNK_B

    def body(ub_ref, vb_ref, g_ref, w_ref, b_ref, o_ref):
        row, col = _tri_masks(C)
        u, _, _, vn = _sgu_parts(ub_ref[...], vb_ref[...], g_ref[...])
        for g in range(N_GROUPS):
            gs = pl.ds(g * GROUP_DIM, GROUP_DIM)
            wg = jnp.where(row >= col, w_ref[g], 0.0)
            mixed = _dot(_b(wg), _b(vn[:, g * GROUP_DIM:(g + 1) * GROUP_DIM])) + b_ref[g]
            o_ref[:, gs] = (u[:, g * GROUP_DIM:(g + 1) * GROUP_DIM] * mixed).astype(o_ref.dtype)

    return pl.pallas_call(
        body, grid=(T // C,),
        in_specs=[_rows(C, WIDTH_B, C_UB // WIDTH_B), _rows(C, WIDTH_B, C_VB // WIDTH_B), _whole((1, WIDTH_B)),
                  _whole((N_GROUPS, C, C)), _whole((N_GROUPS, C, GROUP_DIM))],
        out_specs=_rows(C, WIDTH_B), out_shape=SDS((T, WIDTH_B), bf16), name=name, compiler_params=_cp("parallel"),
    )(proj, proj, gain, w_s, b_bc)


def _sgu_bwd(dsgu, proj, gain, w_s, b_bc, dproj, name):
    T = proj.shape[0]
    C = CHUNK_B

    def body(d_ref, ub_ref, vb_ref, g_ref, w_ref, b_ref, dproj_ref, duv_ref, dw_ref, db_ref, dg_ref):
        dub_ref = duv_ref.at[:, pl.ds(0, WIDTH_B)]
        dvb_ref = duv_ref.at[:, pl.ds(WIDTH_B, WIDTH_B)]
        first = pl.program_id(0) == 0
        row, col = _tri_masks(C)
        ub, vb, gain_v = ub_ref[...], vb_ref[...], g_ref[...]
        u, gv, r, vn = _sgu_parts(ub, vb, gain_v)
        d = d_ref[...]
        dvn_parts = []
        for g in range(N_GROUPS):
            sl = slice(g * GROUP_DIM, (g + 1) * GROUP_DIM)
            wg = jnp.where(row >= col, w_ref[g], 0.0)
            vng = _b(vn[:, sl])
            mixed = _dot(_b(wg), vng) + b_ref[g]
            dub_ref[:, pl.ds(g * GROUP_DIM, GROUP_DIM)] = (d[:, sl] * mixed * _dgelu(ub[:, sl])).astype(dub_ref.dtype)
            dmix = d[:, sl] * u[:, sl]
            dmb = _b(dmix)
            dwg = jnp.where(row >= col, _dot_nt(dmb, vng), 0.0)
            dbg = jnp.sum(dmix, axis=1, keepdims=True)

            @pl.when(first)
            def _():
                dw_ref[g] = dwg
                db_ref[g] = dbg

            @pl.when(jnp.logical_not(first))
            def _():
                dw_ref[g] += dwg
                db_ref[g] += dbg

            dvn_parts.append(_dot_tn(_b(wg), dmb))
        dvn = jnp.concatenate(dvn_parts, axis=1)
        dy = dvn * gain_v
        m = jnp.mean(dy * gv, axis=-1, keepdims=True)
        dgv = r * dy - gv * (r * r * r * m)
        dvb_ref[...] = (dgv * _dgelu(vb)).astype(dvb_ref.dtype)
        dgain = jnp.sum(dvn * gv * r, axis=0, keepdims=True)

        @pl.when(first)
        def _():
            dg_ref[...] = dgain

        @pl.when(jnp.logical_not(first))
        def _():
            dg_ref[...] += dgain

    return pl.pallas_call(
        body, grid=(T // C,),
        in_specs=[_rows(C, WIDTH_B), _rows(C, WIDTH_B, C_UB // WIDTH_B), _rows(C, WIDTH_B, C_VB // WIDTH_B),
                  _whole((1, WIDTH_B)), _whole((N_GROUPS, C, C)), _whole((N_GROUPS, C, GROUP_DIM)), ANY],
        out_specs=[_rows(C, 2 * WIDTH_B, C_UB // (2 * WIDTH_B)), _whole((N_GROUPS, C, C)), _whole((N_GROUPS, C, 1)),
                   _whole((1, WIDTH_B))],
        out_shape=[SDS(dproj.shape, bf16), SDS((N_GROUPS, C, C), f32), SDS((N_GROUPS, C, 1), f32), SDS((1, WIDTH_B), f32)],
        input_output_aliases={6: 0}, name=name, compiler_params=_cp("arbitrary"),
    )(dsgu, proj, proj, gain, w_s, b_bc, dproj)


def _merge_fwd(proj, ya, yb, name):
    T = proj.shape[0]
    tr = _divisor(T, 256, 16)

    def body(ga_ref, gb_ref, ya_ref, yb_ref, o_ref):
        o_ref[...] = (_sig(ga_ref[...]) * ya_ref[...] + _sig(gb_ref[...]) * yb_ref[...]).astype(o_ref.dtype)

    return pl.pallas_call(
        body, grid=(T // tr,),
        in_specs=[_rows(tr, D_MODEL, C_GA // D_MODEL), _rows(tr, D_MODEL, C_GB // D_MODEL), _rows(tr, D_MODEL), _rows(tr, D_MODEL)],
        out_specs=_rows(tr, D_MODEL), out_shape=SDS((T, D_MODEL), bf16), name=name, compiler_params=_cp("parallel"),
    )(proj, proj, ya, yb)


def _merge_bwd(dm, proj, ya, yb, name):
    T = proj.shape[0]
    tr = _divisor(T, 256, 16)

    def body(dm_ref, ga_ref, gb_ref, ya_ref, yb_ref, dya_ref, dyb_ref, dg_ref):
        d = dm_ref[...]
        sa, sb = _sig(ga_ref[...]), _sig(gb_ref[...])
        dya_ref[...] = (d * sa).astype(bf16)
        dyb_ref[...] = (d * sb).astype(bf16)
        dg_ref[:, :D_MODEL] = (d * ya_ref[...] * sa * (1.0 - sa)).astype(bf16)
        dg_ref[:, D_MODEL:] = (d * yb_ref[...] * sb * (1.0 - sb)).astype(bf16)

    return pl.pallas_call(
        body, grid=(T // tr,),
        in_specs=[_rows(tr, D_MODEL), _rows(tr, D_MODEL, C_GA // D_MODEL), _rows(tr, D_MODEL, C_GB // D_MODEL),
                  _rows(tr, D_MODEL), _rows(tr, D_MODEL)],
        out_specs=[_rows(tr, D_MODEL), _rows(tr, D_MODEL), _rows(tr, 2 * D_MODEL, C_GA // (2 * D_MODEL))],
        out_shape=[SDS((T, D_MODEL), bf16), SDS((T, D_MODEL), bf16), SDS((T, PM), bf16)], name=name,
        compiler_params=_cp("parallel"),
    )(dm, proj, proj, ya, yb)


def _ffnconv_fwd(upg, upv, wg, wv, bg, bv, name):
    T, F = upg.shape
    R = min(ROW_CHUNK, T)
    n_chunks = T // R

    def body(g_ref, v_ref, wg_ref, wv_ref, bg_ref, bv_ref, o_ref):
        wgv, wvv = _w_rows(wg_ref, 3), _w_rows(wv_ref, 3)

        def chunk(ci, carry):
            r0 = pl.multiple_of(ci * R, R)
            cg = _causal_conv(g_ref[pl.ds(r0, R), :], _prev8(g_ref, r0, ci), wgv, 3) + bg_ref[...]
            cv = _causal_conv(v_ref[pl.ds(r0, R), :], _prev8(v_ref, r0, ci), wvv, 3) + bv_ref[...]
            o_ref[pl.ds(r0, R), :] = (_silu(cg) * cv).astype(o_ref.dtype)
            return carry

        lax.fori_loop(0, n_chunks, chunk, 0)

    col = lambda p: (0, p)
    return pl.pallas_call(
        body, grid=(F // LANES,),
        in_specs=[pl.BlockSpec((T, LANES), col)] * 2 + [pl.BlockSpec((3, LANES), col)] * 2 + [pl.BlockSpec((1, LANES), col)] * 2,
        out_specs=pl.BlockSpec((T, LANES), col), out_shape=SDS((T, F), bf16), name=name, compiler_params=_cp("parallel"),
    )(upg, upv, wg, wv, bg, bv)


def _ffnconv_bwd(dact, upg, upv, wg, wv, bg, bv, name):
    T, F = upg.shape
    R = min(ROW_CHUNK, T)
    n_chunks = T // R

    def body(d_ref, g_ref, v_ref, wg_ref, wv_ref, bg_ref, bv_ref,
             dg_ref, dv_ref, dwg_ref, dwv_ref, dbg_ref, dbv_ref, sg, sv):
        wgv, wvv = _w_rows(wg_ref, 3), _w_rows(wv_ref, 3)

        def phase1(ci, carry):
            dwg, dwv = carry
            r0 = pl.multiple_of(ci * R, R)
            gcur, vcur = g_ref[pl.ds(r0, R), :], v_ref[pl.ds(r0, R), :]
            gp, vp = _prev8(g_ref, r0, ci), _prev8(v_ref, r0, ci)
            gsh = [gcur] + [_shift_down(gcur, gp, s) for s in (1, 2)]
            vsh = [vcur] + [_shift_down(vcur, vp, s) for s in (1, 2)]
            cg = gsh[0] * wgv[2] + gsh[1] * wgv[1] + gsh[2] * wgv[0] + bg_ref[...]
            cv = vsh[0] * wvv[2] + vsh[1] * wvv[1] + vsh[2] * wvv[0] + bv_ref[...]
            d = d_ref[pl.ds(r0, R), :]
            dcv = d * _silu(cg)
            dcg = d * cv * _dsilu(cg)
            sg[pl.ds(r0, R), :] = dcg
            sv[pl.ds(r0, R), :] = dcv
            rg = [jnp.sum(dcg * gsh[2 - j], axis=0, keepdims=True) for j in range(3)] + [jnp.sum(dcg, axis=0, keepdims=True)]
            rv = [jnp.sum(dcv * vsh[2 - j], axis=0, keepdims=True) for j in range(3)] + [jnp.sum(dcv, axis=0, keepdims=True)]
            return tuple(a + b for a, b in zip(dwg, rg)), tuple(a + b for a, b in zip(dwv, rv))

        z4 = tuple(jnp.zeros((1, LANES), f32) for _ in range(4))
        dwg, dwv = lax.fori_loop(0, n_chunks, phase1, (z4, z4))
        for j in range(3):
            dwg_ref[j:j + 1, :] = dwg[j]
            dwv_ref[j:j + 1, :] = dwv[j]
        dbg_ref[...] = dwg[3]
        dbv_ref[...] = dwv[3]

        def phase2(ci, carry):
            r0 = pl.multiple_of(ci * R, R)
            for sc, wv_, out in ((sg, wgv, dg_ref), (sv, wvv, dv_ref)):
                cur = sc[pl.ds(r0, R), :]
                n8 = _next8(sc, r0, R, ci, n_chunks)
                acc = cur * wv_[2] + _shift_up(cur, n8, 1) * wv_[1] + _shift_up(cur, n8, 2) * wv_[0]
                out[pl.ds(r0, R), :] = acc.astype(out.dtype)
            return carry

        lax.fori_loop(0, n_chunks, phase2, 0)

    col = lambda p: (0, p)
    big, w3, b1 = pl.BlockSpec((T, LANES), col), pl.BlockSpec((3, LANES), col), pl.BlockSpec((1, LANES), col)
    return pl.pallas_call(
        body, grid=(F // LANES,), in_specs=[big, big, big, w3, w3, b1, b1], out_specs=[big, big, w3, w3, b1, b1],
        out_shape=[SDS((T, F), bf16), SDS((T, F), bf16), SDS((3, F), f32), SDS((3, F), f32), SDS((1, F), f32), SDS((1, F), f32)],
        scratch_shapes=[pltpu.VMEM((T, LANES), f32), pltpu.VMEM((T, LANES), f32)], name=name, compiler_params=_cp("parallel"),
    )(dact, upg, upv, wg, wv, bg, bv)


def _loss_head(x, gain, target, name):
    T, Dm = x.shape
    tr = _divisor(T, 256, 16)

    def body(x_ref, g_ref, t_ref, l_ref, dx_ref, dg_ref):
        xv, g = x_ref[...], g_ref[...]
        r = lax.rsqrt(jnp.mean(xv * xv, axis=-1, keepdims=True) + EPS)
        err = xv * r * g - t_ref[...]
        part_l = 0.5 * jnp.sum(jnp.mean(err * err, axis=-1, keepdims=True), axis=0, keepdims=True)
        dy = err * (1.0 / Dm)
        dyg = dy * g
        m = jnp.mean(dyg * xv, axis=-1, keepdims=True)
        dx_ref[...] = r * dyg - xv * (r * r * r * m)
        part_g = jnp.sum(dy * xv * r, axis=0, keepdims=True)
        part_l = jnp.broadcast_to(part_l, (1, LANES))

        @pl.when(pl.program_id(0) == 0)
        def _():
            l_ref[...] = part_l
            dg_ref[...] = part_g

        @pl.when(pl.program_id(0) > 0)
        def _():
            l_ref[...] += part_l
            dg_ref[...] += part_g

    return pl.pallas_call(
        body, grid=(T // tr,), in_specs=[_rows(tr, Dm), _whole((1, Dm)), _rows(tr, Dm)],
        out_specs=[_whole((1, LANES)), _rows(tr, Dm), _whole((1, Dm))],
        out_shape=[SDS((1, LANES), f32), SDS((T, Dm), f32), SDS((1, Dm), f32)], name=name, compiler_params=_cp("arbitrary"),
    )(x, gain, target)


def _lane_row(vec, offset):
    return jnp.pad(vec.astype(f32), (offset, LANES - offset - vec.shape[0]))[None]


def _layer_fwd(x, p_i, W, S, li):
    nm = lambda s: f"{s}_l{li}"
    sv = {"x0": x}
    h1 = _rms_fwd(x, S["norm_mix"], nm("rms_mix"))
    proj = _mm(h1, W["in_main"], "nn", nm("proj_main"))
    ba = _mm(h1, W["in_ba"], "nn", nm("proj_ba"))
    qkv = _qkvconv_fwd(proj, S["conv_qkv"], nm("qkvconv"))
    qd, kd, u, w, attn, tinv, cd, _ = _delta_prep(qkv, ba, S["ea_row"], S["dtb_row"], nm("delta_prep"))
    o, s_all, vn = _delta_scan(qd, kd, u, w, attn, cd, nm("delta_scan"))
    on = _onorm_fwd(o, proj, S["head_norm"], nm("onorm"))
    ya = _mm(on, W["branch_a"], "nn", nm("branch_a"))
    sgu = _sgu_fwd(proj, S["sgu_norm"], S["w_spatial"], S["b_bc"], nm("sgu"))
    yb = _mm(sgu, W["branch_b"], "nn", nm("branch_b"))
    merged = _merge_fwd(proj, ya, yb, nm("merge"))
    x1 = _mm(merged, W["out"], "nn", nm("out_proj"), c=x)
    h2 = _rms_fwd(x1, S["norm_ffn"], nm("rms_ffn"))
    upg = _mm(h2, W["up_g"], "nn", nm("ffn_up_g"))
    upv = _mm(h2, W["up_v"], "nn", nm("ffn_up_v"))
    act = _ffnconv_fwd(upg, upv, S["conv_g"], S["conv_v"], S["bias_g"], S["bias_v"], nm("ffnconv"))
    x2 = _mm(act, W["down"], "nn", nm("ffn_down"), c=x1, tk=2816)
    h3 = _rms_fwd(x2, S["norm_ple"], nm("rms_ple"))
    gl = _mm(h3, W["ple_gate"], "nn", nm("ple_gate"))
    pp = _mm(p_i, W["ple_proj"], "nn", nm("ple_proj"))
    (x3,) = _ew(lambda a, g, q: (a + _sig(g) * q,), [x2, gl, pp], [f32], nm("ple_mix"))
    sv.update(h1=h1, proj=proj, ba=ba, qkv=qkv, qd=qd, kd=kd, u=u, w=w, attn=attn, tinv=tinv, cd=cd, o=o, s_all=s_all,
              vn=vn, on=on, ya=ya, sgu=sgu, yb=yb, merged=merged, x1=x1, h2=h2, upg=upg, upv=upv, act=act, x2=x2,
              h3=h3, gl=gl, pp=pp, p=p_i)
    return x3, sv


def _layer_bwd(dx3, sv, W, S, li):
    nm = lambda s: f"{s}_l{li}"
    G = {}
    dgl, dpp = _ew(lambda d, g, q: ((lambda s: (d * q * s * (1.0 - s), d * s))(_sig(g))),
                   [dx3, sv["gl"], sv["pp"]], [bf16, bf16], nm("ple_mix_bwd"))
    G["w_ple_gate"] = _mm(sv["h3"], dgl, "tn", nm("d_ple_gate"))
    G["w_ple_proj"] = _mm(sv["p"], dpp, "tn", nm("d_ple_proj"))
    dh3 = _mm(dgl, W["ple_gate"], "nt", nm("dh_ple"))
    dx2, G["norm_ple"] = _rms_bwd(sv["x2"], S["norm_ple"], dh3, dx3, nm("rms_ple_bwd"))
    dact = _mm(dx2, W["down"], "nt", nm("d_act"))
    G["w_ffn_down"] = _mm(sv["act"], dx2, "tn", nm("d_ffn_down"))
    dupg, dupv, dcg, dcv, dbg, dbv = _ffnconv_bwd(dact, sv["upg"], sv["upv"], S["conv_g"], S["conv_v"], S["bias_g"],
                                                  S["bias_v"], nm("ffnconv_bwd"))
    G["conv_ffn"] = jnp.concatenate([dcg, dcv], axis=1)
    G["b_conv_ffn"] = jnp.concatenate([dbg, dbv], axis=1)
    G["up_g"] = _mm(sv["h2"], dupg, "tn", nm("d_ffn_up_g"))
    G["up_v"] = _mm(sv["h2"], dupv, "tn", nm("d_ffn_up_v"))
    dh2 = _mm(dupg, W["up_g"], "nt", nm("dh_ffn_g"), tk=2816)
    dh2 = _mm(dupv, W["up_v"], "nt", nm("dh_ffn_v"), c=dh2, tk=2816)
    dx1, G["norm_ffn"] = _rms_bwd(sv["x1"], S["norm_ffn"], dh2, dx2, nm("rms_ffn_bwd"))
    dmerged = _mm(dx1, W["out"], "nt", nm("d_merged"))
    G["w_out"] = _mm(sv["merged"], dx1, "tn", nm("d_w_out"))
    dya, dyb, dproj = _merge_bwd(dmerged, sv["proj"], sv["ya"], sv["yb"], nm("merge_bwd"))
    G["w_branch_a"] = _mm(sv["on"], dya, "tn", nm("d_branch_a"))
    G["w_branch_b"] = _mm(sv["sgu"], dyb, "tn", nm("d_branch_b"))
    don = _mm(dya, W["branch_a"], "nt", nm("d_on"))
    dsgu = _mm(dyb, W["branch_b"], "nt", nm("d_sgu"))
    dproj, G["w_spatial"], db_s, G["sgu_norm"] = _sgu_bwd(dsgu, sv["proj"], S["sgu_norm"], S["w_spatial"], S["b_bc"], dproj,
                                                          nm("sgu_bwd"))
    G["b_spatial"] = db_s.reshape(N_GROUPS, CHUNK_B)
    do, dproj, G["head_norm"] = _onorm_bwd(don, sv["o"], sv["proj"], S["head_norm"], dproj, nm("onorm_bwd"))
    dqd, dkd, du, dw, dattn, dcd = _delta_scan_bwd(do, sv["qd"], sv["kd"], sv["w"], sv["attn"], sv["cd"], sv["s_all"], sv["vn"],
                                                   nm("delta_scan_bwd"))
    dqkv_n, dba, dalog_row, ddtb_row = _delta_prep_bwd(sv["qkv"], sv["ba"], S["ea_row"], S["dtb_row"], sv["tinv"], sv["u"], sv["w"],
                                                       dqd, dkd, du, dw, dattn, dcd, nm("delta_prep_bwd"))
    G["a_log"] = dalog_row[0, N_V:2 * N_V]
    G["dt_bias"] = ddtb_row[0, N_V:2 * N_V]
    dproj, G["conv_qkv"] = _qkvconv_bwd(sv["proj"], S["conv_qkv"], dqkv_n, dproj, nm("qkvconv_bwd"))
    G["in_main"] = _mm(sv["h1"], dproj, "tn", nm("d_in_main"))
    G["in_ba"] = _mm(sv["h1"], dba, "tn", nm("d_in_ba"))
    dh1 = _mm(dba, W["in_ba"], "nt", nm("dh_mix_ba"))
    dh1 = _mm(dproj, W["in_main"], "nt", nm("dh_mix"), c=dh1)
    dx0, G["norm_mix"] = _rms_bwd(sv["x0"], S["norm_mix"], dh1, dx1, nm("rms_mix_bwd"))
    return dx0, G


_BA0 = C_UB
_BA1 = C_UB + 2 * N_V


def _cols(segments, lo, hi):
    out = []
    for start, a in segments:
        s_lo, s_hi = max(lo, start), min(hi, start + a.shape[1])
        if s_lo < s_hi:
            out.append(a[:, s_lo - start:s_hi - start])
    return out


def _big_weights(sh):
    n_in = N_IN // N_CHIPS
    w_in = [(j * n_in, a) for j, a in enumerate(sh["w_in"])]
    rows = lambda k: jnp.concatenate(sh[k], axis=0)
    cols = lambda parts: jnp.concatenate(parts, axis=1)
    ba = jnp.pad(cols(_cols(w_in, _BA0, _BA1)), ((0, 0), (0, LANES - 2 * N_V)))
    return dict(
        in_main=cols(_cols(w_in, 0, _BA0) + _cols(w_in, _BA1, N_IN)), in_ba=ba,
        branch_a=rows("w_branch_a"), branch_b=cols(sh["w_branch_b"]), out=rows("w_out"),
        up_g=cols(sh["w_ffn_up"][:2]), up_v=cols(sh["w_ffn_up"][2:]), down=rows("w_ffn_down"),
        ple_gate=rows("w_ple_gate"), ple_proj=cols(sh["w_ple_proj"]))


def _grads_by_chip(G):
    n_in = N_IN // N_CHIPS
    w_in = [(0, G["in_main"][:, :_BA0]), (_BA0, G["in_ba"][:, :2 * N_V]), (_BA1, G["in_main"][:, _BA0:])]
    split = lambda g: jnp.stack(jnp.split(g, N_CHIPS, axis=1))
    by_rows = lambda g: g.reshape(N_CHIPS, -1, g.shape[1])
    return dict(
        w_in=jnp.stack([jnp.concatenate(_cols(w_in, j * n_in, (j + 1) * n_in), axis=1) for j in range(N_CHIPS)]),
        w_branch_a=by_rows(G["w_branch_a"]), w_branch_b=split(G["w_branch_b"]), w_out=by_rows(G["w_out"]),
        w_ffn_up=jnp.stack(jnp.split(G["up_g"], 2, axis=1) + jnp.split(G["up_v"], 2, axis=1)),
        w_ffn_down=by_rows(G["w_ffn_down"]), w_ple_gate=by_rows(G["w_ple_gate"]), w_ple_proj=split(G["w_ple_proj"]))


def _small_params(sm, i):
    return dict(
        norm_mix=sm["norm_mix"][i][None], conv_qkv=sm["conv_qkv"][i], ea_row=_lane_row(jnp.exp(sm["a_log"][i]), N_V),
        dtb_row=_lane_row(sm["dt_bias"][i], N_V), head_norm=sm["head_norm"][i][None], sgu_norm=sm["sgu_norm"][i][None],
        w_spatial=sm["w_spatial"][i],
        b_bc=jnp.broadcast_to(sm["b_spatial"][i][:, :, None], (N_GROUPS, CHUNK_B, GROUP_DIM)),
        norm_ffn=sm["norm_ffn"][i][None], conv_g=sm["conv_ffn"][i][:, :D_FF], conv_v=sm["conv_ffn"][i][:, D_FF:],
        bias_g=sm["b_conv_ffn"][i][None, :D_FF], bias_v=sm["b_conv_ffn"][i][None, D_FF:], norm_ple=sm["norm_ple"][i][None])


MESH = pl.DeviceIdType.MESH
N_CHIPS = 4
ANY = pl.BlockSpec(memory_space=pl.ANY)
ROW_ALIGN = 32


def _place():
    x, y, c = lax.axis_index("x"), lax.axis_index("y"), lax.axis_index("c")
    chips = [(1 - x, y), (x, 1 - y), (1 - x, 1 - y)]
    return x, y, c, 2 * x + y, chips, (x, y, 1 - c)


def _halves(rows, c):
    h = rows // 2
    return pl.ds(pl.multiple_of(c * h, 16), h), pl.ds(pl.multiple_of((1 - c) * h, 16), h)


def _remote(src, dst, send_sems, recv_sems, k, dev):
    return pltpu.make_async_remote_copy(src_ref=src, dst_ref=dst, send_sem=send_sems.at[k], recv_sem=recv_sems.at[k],
                                        device_id=dev, device_id_type=MESH)


def _gather_shards(bufs, name):
    n = len(bufs)

    def body(*refs):
        ins, outs = refs[:n], refs[n:2 * n]
        send_sems, recv_sems = refs[2 * n:]
        x, y, c, me, chips, sib = _place()
        for i in range(n):
            mine, _ = _halves(ins[i].shape[0], c)
            for k, (px, py) in enumerate(chips):
                _remote(ins[i].at[mine], outs[i].at[me, mine], send_sems, recv_sems, 6 * i + k, (px, py, c)).start()
        for i in range(n):
            mine, _ = _halves(ins[i].shape[0], c)
            for k, (px, py) in enumerate(chips):
                pc = 2 * px + py
                _remote(ins[i].at[mine], outs[i].at[pc, mine], send_sems, recv_sems, 6 * i + k, (px, py, c)).wait_recv()
                _remote(outs[i].at[pc, mine], outs[i].at[pc, mine], send_sems, recv_sems, 6 * i + 3 + k, sib).start()
        for i in range(n):
            mine, other = _halves(ins[i].shape[0], c)
            for k, (px, py) in enumerate(chips):
                pc = 2 * px + py
                _remote(outs[i].at[pc, mine], outs[i].at[pc, other], send_sems, recv_sems, 6 * i + 3 + k, sib).wait_recv()
        for i in range(n):
            mine, _ = _halves(ins[i].shape[0], c)
            for k, (px, py) in enumerate(chips):
                pc = 2 * px + py
                _remote(ins[i].at[mine], outs[i].at[me, mine], send_sems, recv_sems, 6 * i + k, (px, py, c)).wait_send()
                _remote(outs[i].at[pc, mine], outs[i].at[pc, mine], send_sems, recv_sems, 6 * i + 3 + k, sib).wait_send()

    return pl.pallas_call(
        body, in_specs=[ANY] * n, out_specs=[ANY] * n,
        out_shape=[SDS((N_CHIPS,) + b.shape, b.dtype) for b in bufs],
        scratch_shapes=[pltpu.SemaphoreType.DMA((6 * n,)), pltpu.SemaphoreType.DMA((6 * n,))],
        name=name,
    )(*bufs)


def _exchange(srcs_of, out_shapes, n_sems, name):
    n = len(out_shapes)

    def body(*refs):
        n_in = len(refs) - n - 2
        ins, outs = refs[:n_in], refs[n_in:n_in + n]
        send_sems, recv_sems = refs[-2:]
        copies = [_remote(s, d, send_sems, recv_sems, k, dev) for s, d, k, dev in srcs_of(ins, outs, _place())]
        for cp in copies:
            cp.start()
        for cp in copies:
            cp.wait()

    def call(*arrs):
        return pl.pallas_call(
            body, in_specs=[ANY] * len(arrs), out_specs=[ANY] * n, out_shape=out_shapes,
            scratch_shapes=[pltpu.SemaphoreType.DMA((n_sems,)), pltpu.SemaphoreType.DMA((n_sems,))], name=name,
        )(*arrs)

    return call


HBM = pl.BlockSpec(memory_space=pltpu.HBM)
SEM = pl.BlockSpec(memory_space=pltpu.SEMAPHORE)
_SIDE_EFFECT = pltpu.SideEffectType.DATAFLOW_SIDE_EFFECTING


def _exchange_start(copies_of, srcs, land_shapes, n_sems, name):
    ns, nl = len(srcs), len(land_shapes)

    def body(*refs):
        ins = refs[:ns + nl]
        send_sems, recv_sems = refs[ns + nl], refs[ns + nl + 1]
        token = refs[-1]
        for s, d, k, dev in copies_of(ins[:ns], ins[ns:], _place()):
            _remote(s, d, send_sems, recv_sems, k, dev).start()
        token[...] = jnp.zeros_like(token)

    lands = [lax.empty(s.shape, s.dtype) for s in land_shapes]
    operands = [pltpu.with_memory_space_constraint(a, pltpu.HBM) for a in list(srcs) + lands]
    outs = pl.pallas_call(
        body, name=name,
        out_shape=(pltpu.SemaphoreType.DMA((n_sems,)), pltpu.SemaphoreType.DMA((n_sems,)),
                   *[pltpu.HBM(a.shape, a.dtype) for a in operands], SDS((8, LANES), f32)),
        in_specs=[HBM] * (ns + nl), out_specs=(SEM, SEM, *[HBM] * (ns + nl), pl.BlockSpec(memory_space=pltpu.VMEM)),
        input_output_aliases={i: 2 + i for i in range(ns + nl)},
        compiler_params=pltpu.CompilerParams(has_side_effects=_SIDE_EFFECT),
    )(*operands)
    return dict(send=outs[0], recv=outs[1], bufs=list(outs[2:2 + ns + nl]), token=outs[-1], ns=ns, copies_of=copies_of)


def _exchange_wait(handle, after, name):
    ns, bufs, copies_of = handle["ns"], handle["bufs"], handle["copies_of"]
    nb = len(bufs)

    def body(*refs):
        ins = refs[:nb]
        send_sems, recv_sems = refs[nb], refs[nb + 1]
        for s, d, k, dev in copies_of(ins[:ns], ins[ns:], _place()):
            cp = _remote(s, d, send_sems, recv_sems, k, dev)
            cp.wait_send()
            cp.wait_recv()

    outs = pl.pallas_call(
        body, name=name, out_shape=tuple(pltpu.HBM(a.shape, a.dtype) for a in bufs),
        in_specs=[HBM] * nb + [SEM, SEM, ANY], out_specs=tuple([HBM] * nb),
        input_output_aliases={i: i for i in range(nb)},
        compiler_params=pltpu.CompilerParams(has_side_effects=_SIDE_EFFECT),
    )(*bufs, handle["send"], handle["recv"], after)
    return list(outs[:ns]), list(outs[ns:])


def _pick(idx, parts):
    out = parts[-1]
    for j in reversed(range(len(parts) - 1)):
        out = jnp.where(idx == j, parts[j], out)
    return out


def _rs_pair(gs, name):
    def copies(ins, outs, place):
        x, y, c, me, chips, sib = place
        return [(g.at[:, _halves(g.shape[1], c)[1]], o, i, sib) for i, (g, o) in enumerate(zip(ins, outs))]

    shapes = [SDS((N_CHIPS, g.shape[1] // 2, g.shape[2]), g.dtype) for g in gs]
    return _exchange(copies, shapes, len(gs), name)(*gs)


def _gather_copies(ins, outs, place):
    x, y, c, me, chips, sib = place
    out = []
    for i, (s, o) in enumerate(zip(ins, outs)):
        mine, _ = _halves(s.shape[0], c)
        out += [(s.at[mine], o.at[me, mine], 3 * i + k, (px, py, c)) for k, (px, py) in enumerate(chips)]
    return out


def _gather_start(shards, name):
    return _exchange_start(_gather_copies, shards, [SDS((N_CHIPS,) + s.shape, s.dtype) for s in shards], 3 * len(shards), name)


def _gather_forward(lands, name):
    n = len(lands)

    def body(*refs):
        ins, outs = refs[:n], refs[n:2 * n]
        send_sems, recv_sems = refs[2 * n:]
        x, y, c, me, chips, sib = _place()
        copies = []
        for i in range(n):
            mine, _ = _halves(ins[i].shape[1], c)
            for k, (px, py) in enumerate(chips):
                pc = 2 * px + py
                copies.append(_remote(ins[i].at[pc, mine], outs[i].at[pc, mine], send_sems, recv_sems, 3 * i + k, sib))
        for cp in copies:
            cp.start()
        for cp in copies:
            cp.wait()

    return pl.pallas_call(
        body, in_specs=[ANY] * n, out_specs=[ANY] * n, out_shape=[SDS(a.shape, a.dtype) for a in lands],
        input_output_aliases={i: i for i in range(n)},
        scratch_shapes=[pltpu.SemaphoreType.DMA((3 * n,)), pltpu.SemaphoreType.DMA((3 * n,))], name=name,
    )(*lands)


def _rs_chips_copies(ins, outs, place):
    x, y, c, me, chips, sib = place
    return [(p.at[2 * px + py], o.at[k], 3 * i + k, (px, py, c))
            for i, (p, o) in enumerate(zip(ins, outs)) for k, (px, py) in enumerate(chips)]


def _rs_tile(h, n_cols):
    return _divisor(h, max(16, (1 << 20) // (4 * n_cols)), 16)


def _rs_add2(g, got, name):
    _, K, Nc = g.shape
    h = K // 2
    tr = _rs_tile(h, Nc)
    nb = h // tr

    def body(g_ref, got_ref, o_ref):
        o_ref[...] = (g_ref[...].astype(f32) + got_ref[...].astype(f32)).astype(o_ref.dtype)

    blk = lambda j, i: (j, i, 0)
    return pl.pallas_call(
        body, name=name, out_shape=SDS((N_CHIPS, h, Nc), bf16), grid=(N_CHIPS, nb),
        in_specs=[pl.BlockSpec((1, tr, Nc), lambda j, i: (j, lax.axis_index("c") * nb + i, 0)), pl.BlockSpec((1, tr, Nc), blk)],
        out_specs=pl.BlockSpec((1, tr, Nc), blk), compiler_params=_cp("parallel", "parallel"),
    )(g, got)


def _rs_add4(p, parts, name):
    _, h, Nc = p.shape
    tr = _rs_tile(h, Nc)
    nb = h // tr

    def body(own_ref, a_ref, b_ref, c_ref, o_ref):
        up = lambda r: r[0].astype(f32)
        o_ref[...] = (up(own_ref) + up(b_ref)) + (up(a_ref) + up(c_ref))

    part = lambda k: pl.BlockSpec((1, tr, Nc), lambda i, k=k: (k, i, 0))
    me = lambda: 2 * lax.axis_index("x") + lax.axis_index("y")
    return pl.pallas_call(
        body, name=name, out_shape=SDS((2 * h, Nc), f32), grid=(nb,),
        in_specs=[pl.BlockSpec((1, tr, Nc), lambda i: (me(), i, 0)), part(0), part(1), part(2)],
        out_specs=pl.BlockSpec((tr, Nc), lambda i: (lax.axis_index("c") * nb + i, 0)), compiler_params=_cp("parallel"),
    )(p, parts, parts, parts)


def _rs_join(fs, name):
    n = len(fs)

    def body(*refs):
        ins, outs = refs[:n], refs[n:2 * n]
        send_sems, recv_sems = refs[2 * n:]
        x, y, c, me, chips, sib = _place()
        copies = []
        for i in range(n):
            mine, _ = _halves(ins[i].shape[0], c)
            copies.append(_remote(ins[i].at[mine], outs[i].at[mine], send_sems, recv_sems, i, sib))
        for cp in copies:
            cp.start()
        for cp in copies:
            cp.wait()

    return pl.pallas_call(
        body, in_specs=[ANY] * n, out_specs=[ANY] * n, out_shape=[SDS(a.shape, a.dtype) for a in fs],
        input_output_aliases={i: i for i in range(n)},
        scratch_shapes=[pltpu.SemaphoreType.DMA((n,)), pltpu.SemaphoreType.DMA((n,))], name=name,
    )(*fs)


def _rs_begin(gs, names, tag):
    gots = _rs_pair(gs, f"rs_pair_{tag}")
    ps = [_rs_add2(g, got, f"rs_add2_{k}_{tag}") for g, got, k in zip(gs, gots, names)]
    lands = [SDS((3,) + p.shape[1:], p.dtype) for p in ps]
    return _exchange_start(_rs_chips_copies, ps, lands, 3 * len(ps), f"rs_chips_start_{tag}")


def _rs_end(handle, after, names, tag):
    ps, parts = _exchange_wait(handle, after, f"rs_chips_wait_{tag}")
    fs = [_rs_add4(p, part, f"rs_add4_{k}_{tag}") for p, part, k in zip(ps, parts, names)]
    return _rs_join(fs, f"rs_join_{tag}")


def _allreduce_small(buf, name):
    R, L = buf.shape

    def body(x_ref, o_ref, r0, s1, r1, send_sems, recv_sems):
        x, y, c, me, chips, sib = _place()
        cp = _remote(x_ref, r0, send_sems, recv_sems, 0, sib)
        cp.start()
        cp.wait()
        s1[...] = x_ref[...] + r0[...]
        cps = []
        for k, (px, py) in enumerate(chips):
            cp = _remote(s1, r1.at[k], send_sems, recv_sems, 1 + k, (px, py, c))
            cp.start()
            cps.append(cp)
        for cp in cps:
            cp.wait()
        o_ref[...] = (s1[...] + r1[1]) + (r1[0] + r1[2])

    vm = pl.BlockSpec(memory_space=pltpu.VMEM)
    return pl.pallas_call(
        body, in_specs=[vm], out_specs=vm, out_shape=SDS((R, L), f32),
        scratch_shapes=[pltpu.VMEM((R, L), f32), pltpu.VMEM((R, L), f32), pltpu.VMEM((3, R, L), f32),
                        pltpu.SemaphoreType.DMA((4,)), pltpu.SemaphoreType.DMA((4,))],
        name=name, compiler_params=pltpu.CompilerParams(vmem_limit_bytes=VMEM_LIMIT_BYTES),
    )(buf)


BIG = ("w_in", "w_branch_a", "w_branch_b", "w_out", "w_ffn_up", "w_ffn_down", "w_ple_gate", "w_ple_proj")
SMALL_REPL = ("norm_mix", "a_log", "dt_bias", "head_norm", "sgu_norm", "w_spatial", "b_spatial", "norm_ffn", "b_conv_ffn",
              "norm_ple", "norm_final")
SMALL_COLS = ("conv_qkv", "conv_ffn")
WEIGHTS = ("norm_mix", "w_in", "conv_qkv", "a_log", "dt_bias", "head_norm", "sgu_norm", "w_spatial", "b_spatial", "w_branch_a",
           "w_branch_b", "w_out", "norm_ffn", "w_ffn_up", "conv_ffn", "b_conv_ffn", "w_ffn_down", "norm_ple", "w_ple_gate",
           "w_ple_proj", "norm_final")


def _flat(arrs, dtype):
    cat = jnp.concatenate([a.astype(dtype).reshape(-1) for a in arrs])
    unit = LANES * ROW_ALIGN
    cat = jnp.pad(cat, (0, -cat.shape[0] % unit))
    return cat.reshape(-1, LANES)


def _unflat(buf, shapes):
    flat = buf.reshape(-1)
    out, off = [], 0
    for s in shapes:
        n = math.prod(s)
        out.append(flat[off:off + n].reshape(s))
        off += n
    return out


def _adamw(w, g, m, v):
    m2 = ADAM_B1 * m + (1.0 - ADAM_B1) * g
    v2 = ADAM_B2 * v + (1.0 - ADAM_B2) * (g * g)
    m_hat = m2 * (1.0 / (1.0 - ADAM_B1 ** ADAM_STEP))
    v_hat = v2 * (1.0 / (1.0 - ADAM_B2 ** ADAM_STEP))
    delta = -ADAM_LR * (m_hat / (jnp.sqrt(v_hat) + ADAM_EPS) + ADAM_WD * w)
    return delta, m2, v2


def _adamw_layer(w, g, m, v, prev, behind, layer, name):
    K, Nc = g.shape
    tr = _divisor(K, max(16, (1 << 20) // (4 * Nc)), 16)
    nb = K // tr
    rows = pl.BlockSpec((tr, Nc), lambda i: (layer * nb + i, 0))
    n_prev = 0 if prev is None else 4

    def body(w_ref, g_ref, m_ref, v_ref, behind_ref, *refs):
        go_ref, d_ref, mo_ref, vo_ref = refs[n_prev:]
        gv = g_ref[...]
        d, m2, v2 = _adamw(w_ref[...], gv, m_ref[...], v_ref[...])
        go_ref[...] = gv
        d_ref[...] = d
        mo_ref[...] = m2
        vo_ref[...] = v2

    return pl.pallas_call(
        body, grid=(nb,), in_specs=[rows, pl.BlockSpec((tr, Nc), lambda i: (i, 0)), rows, rows, ANY] + [ANY] * n_prev,
        out_specs=[rows] * 4, out_shape=[SDS(w.shape, f32)] * 4,
        input_output_aliases={5 + j: j for j in range(n_prev)}, name=name, compiler_params=_cp("parallel"),
    )(w, g, m, v, behind, *(prev or ()))


def kernel(x, p, norm_mix, w_in, conv_qkv, a_log, dt_bias, head_norm, sgu_norm, w_spatial, b_spatial, w_branch_a, w_branch_b, w_out, norm_ffn, w_ffn_up, conv_ffn, b_conv_ffn, w_ffn_down, norm_ple, w_ple_gate, w_ple_proj, norm_final, loss_target, m_norm_mix, m_w_in, m_conv_qkv, m_a_log, m_dt_bias, m_head_norm, m_sgu_norm, m_w_spatial, m_b_spatial, m_w_branch_a, m_w_branch_b, m_w_out, m_norm_ffn, m_w_ffn_up, m_conv_ffn, m_b_conv_ffn, m_w_ffn_down, m_norm_ple, m_w_ple_gate, m_w_ple_proj, m_norm_final, v_norm_mix, v_w_in, v_conv_qkv, v_a_log, v_dt_bias, v_head_norm, v_sgu_norm, v_w_spatial, v_b_spatial, v_w_branch_a, v_w_branch_b, v_w_out, v_norm_ffn, v_w_ffn_up, v_conv_ffn, v_b_conv_ffn, v_w_ffn_down, v_norm_ple, v_w_ple_gate, v_w_ple_proj, v_norm_final):
    w = dict(norm_mix=norm_mix, w_in=w_in, conv_qkv=conv_qkv, a_log=a_log, dt_bias=dt_bias, head_norm=head_norm, sgu_norm=sgu_norm,
             w_spatial=w_spatial, b_spatial=b_spatial, w_branch_a=w_branch_a, w_branch_b=w_branch_b, w_out=w_out, norm_ffn=norm_ffn,
             w_ffn_up=w_ffn_up, conv_ffn=conv_ffn, b_conv_ffn=b_conv_ffn, w_ffn_down=w_ffn_down, norm_ple=norm_ple,
             w_ple_gate=w_ple_gate, w_ple_proj=w_ple_proj, norm_final=norm_final)
    m = dict(norm_mix=m_norm_mix, w_in=m_w_in, conv_qkv=m_conv_qkv, a_log=m_a_log, dt_bias=m_dt_bias, head_norm=m_head_norm,
             sgu_norm=m_sgu_norm, w_spatial=m_w_spatial, b_spatial=m_b_spatial, w_branch_a=m_w_branch_a, w_branch_b=m_w_branch_b,
             w_out=m_w_out, norm_ffn=m_norm_ffn, w_ffn_up=m_w_ffn_up, conv_ffn=m_conv_ffn, b_conv_ffn=m_b_conv_ffn,
             w_ffn_down=m_w_ffn_down, norm_ple=m_norm_ple, w_ple_gate=m_w_ple_gate, w_ple_proj=m_w_ple_proj, norm_final=m_norm_final)
    v = dict(norm_mix=v_norm_mix, w_in=v_w_in, conv_qkv=v_conv_qkv, a_log=v_a_log, dt_bias=v_dt_bias, head_norm=v_head_norm,
             sgu_norm=v_sgu_norm, w_spatial=v_w_spatial, b_spatial=v_b_spatial, w_branch_a=v_w_branch_a, w_branch_b=v_w_branch_b,
             w_out=v_w_out, norm_ffn=v_norm_ffn, w_ffn_up=v_w_ffn_up, conv_ffn=v_conv_ffn, b_conv_ffn=v_b_conv_ffn,
             w_ffn_down=v_w_ffn_down, norm_ple=v_norm_ple, w_ple_gate=v_w_ple_gate, w_ple_proj=v_w_ple_proj, norm_final=v_norm_final)
    chip = 2 * lax.axis_index("x") + lax.axis_index("y")
    core = lax.axis_index("c")

    def by_chip(own, gathered):
        return [jnp.where(chip == j, own, gathered[j]) for j in range(N_CHIPS)]

    conv_buf = _flat([w[k] for k in SMALL_COLS], f32)
    (conv_all,) = _gather_shards([conv_buf], "gather_conv")
    sm = {k: w[k] for k in SMALL_REPL}
    conv_parts = [_unflat(b, [w[k].shape for k in SMALL_COLS]) for b in by_chip(conv_buf, conv_all)]
    for n, k in enumerate(SMALL_COLS):
        sm[k] = jnp.concatenate([conv_parts[j][n] for j in range(N_CHIPS)], axis=-1)
    Ss = [_small_params(sm, i) for i in range(DEPTH)]
    def start_gather(i, behind):
        own = [w[k][i].astype(bf16) for k in BIG]
        own[-1] = own[-1] + behind.astype(bf16)
        return _gather_start(own, f"gather_start_l{i}")

    h = x[0]
    saved, Ws = [], []
    gather = start_gather(0, jnp.zeros((), f32))
    for i in range(DEPTH):
        own, lands = _exchange_wait(gather, gather["token"] if i == 0 else h, f"gather_wait_l{i}")
        S_i = Ss[i]
        if i + 1 < DEPTH:
            gather = start_gather(i + 1, own[-1][0, 0] * 0)
            S_i = dict(S_i, norm_mix=S_i["norm_mix"] + gather["token"][0, 0])
        lands = _gather_forward(lands, f"gather_forward_l{i}")
        Ws.append(_big_weights({k: by_chip(o, g) for k, o, g in zip(BIG, own, lands)}))
        h, sv = _layer_fwd(h, p[i, 0], Ws[i], S_i, i)
        saved.append(sv)
    loss_row, dh, g_final = _loss_head(h, sm["norm_final"][None], loss_target[0], "loss_head")

    gsm = {k: [None] * DEPTH for k in SMALL_REPL + SMALL_COLS if k != "norm_final"}
    gbig = {k: [None] * DEPTH for k in BIG}
    pending = None

    def finish(job, after):
        j, handle = job
        for k, g in zip(BIG, _rs_end(handle, after, BIG, f"l{j}")):
            gbig[k][j] = g

    for i in reversed(range(DEPTH)):
        S_i = Ss[i] if pending is None else dict(Ss[i], norm_ple=Ss[i]["norm_ple"] + pending[1]["token"][0, 0])
        dh, G = _layer_bwd(dh, saved[i], Ws[i], S_i, i)
        parts = _grads_by_chip(G)
        begun = (i, _rs_begin([parts[k] for k in BIG], BIG, f"l{i}"))
        if pending is not None:
            finish(pending, dh)
        pending = begun
        for k in gsm:
            gsm[k][i] = G[k].reshape(w[k].shape[1:-1] + (-1,)) if k in SMALL_COLS else G[k].reshape(w[k].shape[1:])

    small_names = [k for k in SMALL_REPL + SMALL_COLS if k != "norm_final"]
    small_local = [jnp.stack(gsm[k]) for k in small_names] + [g_final.reshape(-1), loss_row[0, :1]]
    small_shapes = [a.shape for a in small_local]
    small_buf = _allreduce_small(_flat(small_local, f32), "allreduce_small")

    two = lambda a: a.reshape(-1, a.shape[-1])
    big_out = {k: None for k in BIG}

    def update_layer(j):
        for k in BIG:
            big_out[k] = _adamw_layer(two(w[k]), gbig[k][j], two(m[k]), two(v[k]), big_out[k], pending[1]["token"], j,
                                      f"adamw_{k}_l{j}")

    for j in reversed(range(1, DEPTH)):
        update_layer(j)
    finish(pending, big_out[BIG[-1]][1])
    update_layer(0)
    small_sum = _unflat(small_buf, small_shapes)
    gs = dict(zip(small_names + ["norm_final"], small_sum[:-1]))
    loss = small_sum[-1][0]
    for k in SMALL_COLS:
        n = w[k].shape[-1]
        gs[k] = _pick(chip, [gs[k][..., j * n:(j + 1) * n] for j in range(N_CHIPS)])

    grads, deltas, new_m, new_v = {}, {}, {}, {}
    for k in BIG:
        grads[k], deltas[k], new_m[k], new_v[k] = [a.reshape(w[k].shape) for a in big_out[k]]
    small_all = [k for k in WEIGHTS if k not in BIG]
    shapes = [w[k].shape for k in small_all]
    d, m2, v2 = _ew(_adamw, [_flat([w[k] for k in small_all], f32), _flat([gs[k] for k in small_all], f32),
                             _flat([m[k] for k in small_all], f32), _flat([v[k] for k in small_all], f32)],
                    [f32, f32, f32], "adamw_small")
    for k, a, b, c_ in zip(small_all, _unflat(d, shapes), _unflat(m2, shapes), _unflat(v2, shapes)):
        grads[k], deltas[k], new_m[k], new_v[k] = gs[k], a, b, c_
    return (loss, dh[None], *[grads[k] for k in WEIGHTS], *[deltas[k] for k in WEIGHTS],
            *[new_m[k] for k in WEIGHTS], *[new_v[k] for k in WEIGHTS])
```

```python
import functools
import math

import jax
import jax.numpy as jnp
from jax import lax
from jax.experimental import pallas as pl
from jax.experimental.pallas import tpu as pltpu

f32 = jnp.float32
bf16 = jnp.bfloat16
HI = lax.Precision.HIGHEST
SDS = jax.ShapeDtypeStruct

D_MODEL = 2048
DEPTH = 4
HEAD_DIM = 128
N_QK = 8
N_V = 16
QK_W = N_QK * HEAD_DIM
V_W = N_V * HEAD_DIM
CHUNK_A = 64
N_GROUPS = 8
GROUP_DIM = 128
WIDTH_B = N_GROUPS * GROUP_DIM
CHUNK_B = 128
D_FF = 5632
PLE_DIM = 256
EPS = 1e-6
N_IN = 12320
ADAM_LR, ADAM_B1, ADAM_B2, ADAM_EPS, ADAM_WD, ADAM_STEP = 0.001, 0.9, 0.999, 1e-08, 0.01, 10

C_Q, C_K, C_V, C_Z, C_UB, C_VB, C_GA, C_GB, PM = 0, 1024, 2048, 4096, 6144, 7168, 8192, 10240, 12288
LANES = 128
VMEM_LIMIT_BYTES = 48 * 1024 * 1024
ROW_CHUNK = 256


def _cp(*sem):
    return pltpu.CompilerParams(dimension_semantics=sem if sem else None, vmem_limit_bytes=VMEM_LIMIT_BYTES)


def _dot(a, b, prec=None):
    return jnp.dot(a, b, preferred_element_type=f32, precision=prec)


def _dot_nt(a, b, prec=None):
    return lax.dot_general(a, b, (((1,), (1,)), ((), ())), preferred_element_type=f32, precision=prec)


def _dot_tn(a, b, prec=None):
    return lax.dot_general(a, b, (((0,), (0,)), ((), ())), preferred_element_type=f32, precision=prec)


def _b(x):
    return x.astype(bf16)


def _dot3(a, b, dims=(((1,), (0,)), ((), ()))):
    ah, bh = _b(a), _b(b)
    al, bl = _b(a - ah.astype(f32)), _b(b - bh.astype(f32))
    dg = lambda p, q: lax.dot_general(p, q, dims, preferred_element_type=f32)
    return dg(ah, bh) + (dg(ah, bl) + dg(al, bh))


_TN = (((0,), (0,)), ((), ()))


def _sig(x):
    return 1.0 / (1.0 + jnp.exp(-x))


def _silu(x):
    return x * _sig(x)


def _dsilu(x):
    s = _sig(x)
    return s * (1.0 + x * (1.0 - s))


_GELU_C = 0.7978845608028654
_GELU_A = 0.044715


def _gelu(x):
    return 0.5 * x * (1.0 + jnp.tanh(_GELU_C * (x + _GELU_A * x * x * x)))


def _dgelu(x):
    t = jnp.tanh(_GELU_C * (x + _GELU_A * x * x * x))
    return 0.5 * (1.0 + t) + 0.5 * x * (1.0 - t * t) * _GELU_C * (1.0 + 3.0 * _GELU_A * x * x)


def _softplus(x):
    return jnp.maximum(x, 0.0) + jnp.log(1.0 + jnp.exp(-jnp.abs(x)))


def _divisor(n, cap, mult):
    best = None
    d = mult
    while d <= min(n, cap):
        if n % d == 0:
            best = d
        d += mult
    return best if best is not None else n


def _rows(tr, c, j=0):
    return pl.BlockSpec((tr, c), lambda i, j=j: (i, j))


def _whole(shape):
    nd = len(shape)
    return pl.BlockSpec(shape, lambda *_: (0,) * nd)


def _mm(a, b, mode, name, c=None, tm=1024, tn=1024, tk=2048):
    out_dtype = bf16 if mode == "tn" else f32
    if mode == "nn":
        (M, K), (K2, N) = a.shape, b.shape
    elif mode == "nt":
        (M, K), (N, K2) = a.shape, b.shape
    else:
        (K, M), (K2, N) = a.shape, b.shape
    assert K == K2, (a.shape, b.shape, mode)
    tm = _divisor(M, tm, LANES if mode == "tn" else 16)
    tn = _divisor(N, tn, LANES)
    tk = _divisor(K, tk, LANES if mode != "tn" else 16)
    nk = K // tk
    if mode == "nn":
        a_spec = pl.BlockSpec((tm, tk), lambda i, j, k: (i, k))
        b_spec = pl.BlockSpec((tk, tn), lambda i, j, k: (k, j))
        dn = (((1,), (0,)), ((), ()))
    elif mode == "nt":
        a_spec = pl.BlockSpec((tm, tk), lambda i, j, k: (i, k))
        b_spec = pl.BlockSpec((tn, tk), lambda i, j, k: (j, k))
        dn = (((1,), (1,)), ((), ()))
    else:
        a_spec = pl.BlockSpec((tk, tm), lambda i, j, k: (k, i))
        b_spec = pl.BlockSpec((tk, tn), lambda i, j, k: (k, j))
        dn = (((0,), (0,)), ((), ()))
    o_spec = pl.BlockSpec((tm, tn), lambda i, j, k: (i, j))
    has_c = c is not None

    def body(*refs):
        a_ref, b_ref = refs[0], refs[1]
        c_ref = refs[2] if has_c else None
        o_ref = refs[3] if has_c else refs[2]
        part = lax.dot_general(_b(a_ref[...]), _b(b_ref[...]), dn, preferred_element_type=f32)
        if nk == 1:
            if has_c:
                part = part + c_ref[...]
            o_ref[...] = part.astype(o_ref.dtype)
        else:
            acc = refs[-1]
            k = pl.program_id(2)

            @pl.when(k == 0)
            def _():
                acc[...] = part

            @pl.when(k > 0)
            def _():
                acc[...] += part

            @pl.when(k == nk - 1)
            def _():
                r = acc[...]
                if has_c:
                    r = r + c_ref[...]
                o_ref[...] = r.astype(o_ref.dtype)

    ins = [a, b] + ([c] if has_c else [])
    in_specs = [a_spec, b_spec] + ([o_spec] if has_c else [])
    return pl.pallas_call(
        body, grid=(M // tm, N // tn, nk), in_specs=in_specs, out_specs=o_spec, out_shape=SDS((M, N), out_dtype),
        scratch_shapes=[pltpu.VMEM((tm, tn), f32)] if nk > 1 else [], name=name,
        compiler_params=_cp("parallel", "parallel", "arbitrary"),
    )(*ins)


def _ew(fn, ins, out_dtypes, name, tile_bytes=2 * 1024 * 1024):
    R, C = max((x.shape for x in ins), key=lambda s: s[0])
    n_in = len(ins)
    row_bytes = 4 * C * (len(ins) + len(out_dtypes))
    tr = _divisor(R, max(16, tile_bytes // row_bytes), 16)
    in_specs = [_rows(tr, C) if x.shape[0] == R else _whole((1, C)) for x in ins]

    def body(*refs):
        res = fn(*[r[...] for r in refs[:n_in]])
        for o_ref, r in zip(refs[n_in:], res):
            o_ref[...] = r.astype(o_ref.dtype)

    outs = pl.pallas_call(
        body, grid=(R // tr,), in_specs=in_specs, out_specs=[_rows(tr, C) for _ in out_dtypes],
        out_shape=[SDS((R, C), dt) for dt in out_dtypes], name=name, compiler_params=_cp("parallel"),
    )(*ins)
    return outs


def _rms_fwd(x, gain, name):
    T, Dm = x.shape
    tr = _divisor(T, 256, 16)

    def body(x_ref, g_ref, o_ref):
        xv = x_ref[...]
        r = lax.rsqrt(jnp.mean(xv * xv, axis=-1, keepdims=True) + EPS)
        o_ref[...] = (xv * r * g_ref[...]).astype(o_ref.dtype)

    return pl.pallas_call(
        body, grid=(T // tr,), in_specs=[_rows(tr, Dm), _whole((1, Dm))], out_specs=_rows(tr, Dm),
        out_shape=SDS((T, Dm), bf16), name=name, compiler_params=_cp("parallel"),
    )(x, gain)


def _rms_bwd(x, gain, dh, dres, name):
    T, Dm = x.shape
    tr = _divisor(T, 256, 16)

    def body(x_ref, g_ref, dh_ref, dres_ref, dx_ref, dg_ref):
        xv = x_ref[...]
        dhv = dh_ref[...]
        r = lax.rsqrt(jnp.mean(xv * xv, axis=-1, keepdims=True) + EPS)
        dy = dhv * g_ref[...]
        m = jnp.mean(dy * xv, axis=-1, keepdims=True)
        dx_ref[...] = dres_ref[...] + r * dy - xv * (r * r * r * m)
        part = jnp.sum(dhv * xv * r, axis=0, keepdims=True)

        @pl.when(pl.program_id(0) == 0)
        def _():
            dg_ref[...] = part

        @pl.when(pl.program_id(0) > 0)
        def _():
            dg_ref[...] += part

    return pl.pallas_call(
        body, grid=(T // tr,), in_specs=[_rows(tr, Dm), _whole((1, Dm)), _rows(tr, Dm), _rows(tr, Dm)],
        out_specs=[_rows(tr, Dm), _whole((1, Dm))], out_shape=[SDS((T, Dm), f32), SDS((1, Dm), f32)],
        name=name, compiler_params=_cp("arbitrary"),
    )(x, gain, dh, dres)


def _shift_down(cur, prev8, s):
    rolled = pltpu.roll(cur, s, 0)
    rp = pltpu.roll(prev8, s, 0)
    row8 = lax.broadcasted_iota(jnp.int32, prev8.shape, 0)
    first = jnp.where(row8 < s, rp, rolled[:8])
    return jnp.concatenate([first, rolled[8:]], axis=0)


def _shift_up(cur, next8, s):
    R = cur.shape[0]
    rolled = pltpu.roll(cur, R - s, 0)
    rn = pltpu.roll(next8, 8 - s, 0)
    row8 = lax.broadcasted_iota(jnp.int32, next8.shape, 0)
    last = jnp.where(row8 >= 8 - s, rn, rolled[R - 8:])
    return jnp.concatenate([rolled[: R - 8], last], axis=0)


def _prev8(ref, r0, ci):
    rows = ref[pl.ds(pl.multiple_of(jnp.maximum(r0 - 8, 0), 8), 8), :]
    return jnp.where(ci > 0, rows, 0.0)


def _next8(ref, r0, R, ci, n_chunks):
    start = jnp.minimum(r0 + R, (n_chunks - 1) * R + R - 8)
    rows = ref[pl.ds(pl.multiple_of(start, 8), 8), :]
    return jnp.where(ci < n_chunks - 1, rows, 0.0)


def _w_rows(w_ref, k):
    return [w_ref[j:j + 1, :] for j in range(k)]


def _causal_conv(cur, prev8, w, k):
    acc = cur * w[k - 1]
    for s in range(1, k):
        acc = acc + _shift_down(cur, prev8, s) * w[k - 1 - s]
    return acc


def _qkvconv_fwd(proj, convw, name):
    T = proj.shape[0]
    R = min(ROW_CHUNK, T)
    n_chunks = T // R
    n_blk = (2 * QK_W + V_W) // HEAD_DIM

    def body(x_ref, w_ref, o_ref):
        p = pl.program_id(0)
        is_qk = p < 2 * N_QK
        scale = jnp.where(p < N_QK, HEAD_DIM ** -0.5, 1.0).astype(f32)
        w = _w_rows(w_ref, 4)

        def chunk(ci, carry):
            r0 = pl.multiple_of(ci * R, R)
            cur = x_ref[pl.ds(r0, R), :]
            y = _silu(_causal_conv(cur, _prev8(x_ref, r0, ci), w, 4))
            ss = jnp.sum(y * y, axis=-1, keepdims=True)
            nrm = jnp.where(is_qk, lax.rsqrt(ss + EPS) * scale, 1.0)
            o_ref[pl.ds(r0, R), :] = y * nrm
            return carry

        lax.fori_loop(0, n_chunks, chunk, 0)

    return pl.pallas_call(
        body, grid=(n_blk,),
        in_specs=[pl.BlockSpec((T, HEAD_DIM), lambda p: (0, p)), pl.BlockSpec((4, HEAD_DIM), lambda p: (0, p))],
        out_specs=pl.BlockSpec((T, HEAD_DIM), lambda p: (0, p)), out_shape=SDS((T, n_blk * HEAD_DIM), f32),
        name=name, compiler_params=_cp("parallel"),
    )(proj, convw)


def _qkvconv_bwd(proj, convw, dqkv, dproj, name):
    T = proj.shape[0]
    R = min(ROW_CHUNK, T)
    n_chunks = T // R
    n_blk = (2 * QK_W + V_W) // HEAD_DIM

    def body(x_ref, w_ref, do_ref, dproj_ref, dx_ref, dw_ref, dc_sc):
        p = pl.program_id(0)
        is_qk = p < 2 * N_QK
        scale = jnp.where(p < N_QK, HEAD_DIM ** -0.5, 1.0).astype(f32)
        w = _w_rows(w_ref, 4)

        def phase1(ci, dw):
            r0 = pl.multiple_of(ci * R, R)
            cur = x_ref[pl.ds(r0, R), :]
            p8 = _prev8(x_ref, r0, ci)
            shifted = [cur] + [_shift_down(cur, p8, s) for s in range(1, 4)]
            c = shifted[0] * w[3]
            for s in range(1, 4):
                c = c + shifted[s] * w[3 - s]
            y = _silu(c)
            dout = do_ref[pl.ds(r0, R), :]
            n = lax.rsqrt(jnp.sum(y * y, axis=-1, keepdims=True) + EPS)
            dot_ = jnp.sum(dout * y, axis=-1, keepdims=True)
            dy = jnp.where(is_qk, scale * (n * dout - y * (n * n * n * dot_)), dout)
            dc = dy * _dsilu(c)
            dc_sc[pl.ds(r0, R), :] = dc
            return tuple(dw[j] + jnp.sum(dc * shifted[3 - j], axis=0, keepdims=True) for j in range(4))

        dw = lax.fori_loop(0, n_chunks, phase1, tuple(jnp.zeros((1, HEAD_DIM), f32) for _ in range(4)))
        for j in range(4):
            dw_ref[j:j + 1, :] = dw[j]

        def phase2(ci, carry):
            r0 = pl.multiple_of(ci * R, R)
            cur = dc_sc[pl.ds(r0, R), :]
            n8 = _next8(dc_sc, r0, R, ci, n_chunks)
            acc = cur * w[3]
            for s in range(1, 4):
                acc = acc + _shift_up(cur, n8, s) * w[3 - s]
            dx_ref[pl.ds(r0, R), :] = acc.astype(dx_ref.dtype)
            return carry

        lax.fori_loop(0, n_chunks, phase2, 0)

    col = lambda p: (0, p)
    return pl.pallas_call(
        body, grid=(n_blk,),
        in_specs=[pl.BlockSpec((T, HEAD_DIM), col), pl.BlockSpec((4, HEAD_DIM), col), pl.BlockSpec((T, HEAD_DIM), col), ANY],
        out_specs=[pl.BlockSpec((T, HEAD_DIM), col), pl.BlockSpec((4, HEAD_DIM), col)],
        out_shape=[SDS(dproj.shape, bf16), SDS((4, n_blk * HEAD_DIM), f32)], input_output_aliases={3: 0},
        scratch_shapes=[pltpu.VMEM((T, HEAD_DIM), f32)], name=name, compiler_params=_cp("parallel"),
    )(proj, convw, dqkv, dproj)


def _tri_masks(C):
    row = lax.broadcasted_iota(jnp.int32, (C, C), 0)
    col = lax.broadcasted_iota(jnp.int32, (C, C), 1)
    return row, col


def _lane_pick(blk, lane, idx):
    return jnp.sum(jnp.where(lane == idx, blk, 0.0), axis=1, keepdims=True)


def _gate_block(ba, ea, dtb, lane):
    sig = _sig(ba)
    gblk = -ea * _softplus(ba + dtb)
    return sig, gblk


def _head_decay(gam_all, rg_all, tot, lane, h, row, col):
    C = row.shape[0]
    gam_c = _lane_pick(gam_all, lane, N_V + h)
    rg_c = _lane_pick(rg_all, lane, N_V + h)
    lane1 = lax.broadcasted_iota(jnp.int32, (1, LANES), 1)
    tot_h = jnp.sum(jnp.where(lane1 == N_V + h, tot, 0.0), axis=1, keepdims=True)
    gcb = jnp.broadcast_to(gam_c, (C, C))
    dlt = gcb - gcb.T
    dm = jnp.where(row >= col, jnp.exp(jnp.minimum(dlt, 0.0)), 0.0)
    return gam_c, rg_c, tot_h, dm


def _delta_prep(qkv, ba, ea_row, dtb_row, name):
    T = qkv.shape[0]
    C = CHUNK_A
    N = T // C

    SUB = 16

    def body(q_ref, k_ref, v_ref, ba_ref, ea_ref, dtb_ref,
             qd_ref, kd_ref, u_ref, w_ref, attn_ref, tinv_ref, cd_ref, bg_ref, at_sc, t_sc, a_sc, rk_sc):
        row, col = _tri_masks(C)
        lane = lax.broadcasted_iota(jnp.int32, (C, LANES), 1)
        sig, gblk = _gate_block(ba_ref[...], ea_ref[...], dtb_ref[...], lane)
        bg_ref[...] = jnp.where(lane < N_V, sig, gblk)
        lower = (row >= col).astype(f32)
        upper_s = (col > row).astype(f32)
        eye = (row == col).astype(f32)
        same16 = (row >> 4) == (col >> 4)
        same32 = (row >> 5) == (col >> 5)
        gam_all = _dot(lower, gblk, HI)
        rg_all = _dot(upper_s, gblk, HI)
        tot = jnp.sum(gblk, axis=0, keepdims=True)
        for h in range(N_V):
            qk = pl.ds((h // 2) * HEAD_DIM, HEAD_DIM)
            hs = pl.ds(h * HEAD_DIM, HEAD_DIM)
            qh, kh, vh = q_ref[:, qk], k_ref[:, qk], v_ref[:, hs]
            beta_c = _lane_pick(sig, lane, h)
            gam_c, rg_c, tot_h, dm = _head_decay(gam_all, rg_all, tot, lane, h, row, col)
            kk = _dot_nt(_b(kh), _b(kh))
            a = jnp.where(row > col, beta_c * kk * dm, 0.0)
            a_sc[h] = a
            at_sc[h] = jnp.where(same16, a, 0.0).T
            t_sc[h] = eye
            eg = jnp.exp(gam_c)
            attn_ref[0, h] = (_dot_nt(_b(qh), _b(kh)) * dm).astype(attn_ref.dtype)
            qd_ref[:, hs] = (qh * eg).astype(qd_ref.dtype)
            kd_ref[:, hs] = (kh * jnp.exp(rg_c)).astype(kd_ref.dtype)
            cd_ref[0, h] = jnp.broadcast_to(jnp.exp(tot_h), (1, LANES))
            u_ref[:, hs] = vh * beta_c
            rk_sc[:, hs] = kh * (beta_c * eg)

        first_col = (row >> 4) << 4

        def fsub(i, carry):
            for h in range(N_V):
                t = t_sc[h]
                a_col = jnp.sum(jnp.where(col == first_col + i, at_sc[h], 0.0), axis=1, keepdims=True)
                prod = a_col * t
                sums = [jnp.sum(prod[b * SUB:(b + 1) * SUB], axis=0, keepdims=True) for b in range(C // SUB)]
                new = eye - jnp.concatenate([jnp.broadcast_to(s, (SUB, C)) for s in sums], axis=0)
                t_sc[h] = jnp.where(row - first_col == i, new, t)
            return carry

        lax.fori_loop(1, SUB, fsub, 0)
        pair16 = jnp.logical_and(same32, jnp.logical_not(same16))
        for h in range(N_V):
            at_sc[h] = _dot3(t_sc[h], jnp.where(pair16, a_sc[h], 0.0))
        for h in range(N_V):
            p16 = t_sc[h]
            t_sc[h] = p16 - _dot3(at_sc[h], p16)
        for h in range(N_V):
            at_sc[h] = _dot3(t_sc[h], jnp.where(same32, 0.0, a_sc[h]))
        for h in range(N_V):
            p32 = t_sc[h]
            tinv_ref[0, h] = p32 - _dot3(at_sc[h], p32)
        for h in range(N_V):
            hs = pl.ds(h * HEAD_DIM, HEAD_DIM)
            u_ref[:, hs] = _dot3(tinv_ref[0, h], u_ref[:, hs])
        for h in range(N_V):
            hs = pl.ds(h * HEAD_DIM, HEAD_DIM)
            w_ref[:, hs] = _dot3(tinv_ref[0, h], rk_sc[:, hs]).astype(w_ref.dtype)

    big = lambda n: (n, 0)
    return pl.pallas_call(
        body, grid=(N,),
        in_specs=[pl.BlockSpec((C, QK_W), lambda n: (n, 0)), pl.BlockSpec((C, QK_W), lambda n: (n, 1)),
                  pl.BlockSpec((C, V_W), lambda n: (n, 1)), pl.BlockSpec((C, LANES), big),
                  _whole((1, LANES)), _whole((1, LANES))],
        out_specs=[pl.BlockSpec((C, V_W), big)] * 4 + [
            pl.BlockSpec((1, N_V, C, C), lambda n: (n, 0, 0, 0)), pl.BlockSpec((1, N_V, C, C), lambda n: (n, 0, 0, 0)),
            pl.BlockSpec((1, N_V, 1, LANES), lambda n: (n, 0, 0, 0)), pl.BlockSpec((C, LANES), big)],
        out_shape=[SDS((T, V_W), bf16), SDS((T, V_W), bf16), SDS((T, V_W), f32), SDS((T, V_W), bf16),
                   SDS((N, N_V, C, C), bf16), SDS((N, N_V, C, C), f32), SDS((N, N_V, 1, LANES), f32), SDS((T, LANES), f32)],
        scratch_shapes=[pltpu.VMEM((N_V, C, C), f32)] * 3 + [pltpu.VMEM((C, V_W), f32)],
        name=name, compiler_params=_cp("parallel"),
    )(qkv, qkv, qkv, ba, ea_row, dtb_row)


def _delta_scan(qd, kd, u, w, attn, cd, name):
    T = qd.shape[0]
    C = CHUNK_A
    N = T // C

    def body(qd_ref, kd_ref, u_ref, w_ref, attn_ref, cd_ref, o_ref, s_ref, vn_ref, s_sc):
        @pl.when(pl.program_id(0) == 0)
        def _():
            s_sc[...] = jnp.zeros_like(s_sc)

        heads = [(h, pl.ds(h * HEAD_DIM, HEAD_DIM)) for h in range(N_V)]
        for h, hs in heads:
            s_ref[0, h] = s_sc[h]
            vn_ref[:, hs] = _b(u_ref[:, hs] - _dot(w_ref[:, hs], _b(s_sc[h])))
        for h, hs in heads:
            o_ref[:, hs] = _dot(qd_ref[:, hs], _b(s_sc[h])) + _dot(attn_ref[0, h], vn_ref[:, hs])
        for h, hs in heads:
            s_sc[h] = s_sc[h] * cd_ref[0, h] + _dot_tn(kd_ref[:, hs], vn_ref[:, hs])

    blk = pl.BlockSpec((C, V_W), lambda n: (n, 0))
    per_head = lambda a, b: pl.BlockSpec((1, N_V, a, b), lambda n: (n, 0, 0, 0))
    return pl.pallas_call(
        body, grid=(N,), in_specs=[blk, blk, blk, blk, per_head(C, C), per_head(1, LANES)],
        out_specs=[blk, per_head(HEAD_DIM, HEAD_DIM), blk],
        out_shape=[SDS((T, V_W), f32), SDS((N, N_V, HEAD_DIM, HEAD_DIM), f32), SDS((T, V_W), bf16)],
        scratch_shapes=[pltpu.VMEM((N_V, HEAD_DIM, HEAD_DIM), f32)], name=name, compiler_params=_cp("arbitrary"),
    )(qd, kd, u, w, attn, cd)


def _delta_scan_bwd(do, qd, kd, w, attn, cd, s_all, vn, name):
    T = qd.shape[0]
    C = CHUNK_A
    N = T // C

    def body(do_ref, qd_ref, kd_ref, w_ref, attn_ref, cd_ref, s_ref, vn_ref,
             dqd_ref, dkd_ref, du_ref, dw_ref, dattn_ref, dcd_ref, ds_sc):
        @pl.when(pl.program_id(0) == 0)
        def _():
            ds_sc[...] = jnp.zeros_like(ds_sc)

        row, col = _tri_masks(C)
        heads = [(h, pl.ds(h * HEAD_DIM, HEAD_DIM)) for h in range(N_V)]
        for h, hs in heads:
            dob = _b(do_ref[:, hs])
            dqd_ref[:, hs] = _dot_nt(dob, _b(s_ref[0, h]))
            dattn_ref[0, h] = jnp.where(row >= col, _dot_nt(dob, vn_ref[:, hs]), 0.0)
        for h, hs in heads:
            dsn = ds_sc[h]
            dkd_ref[:, hs] = _dot_nt(vn_ref[:, hs], _b(dsn))
            dcd = jnp.sum(jnp.sum(s_ref[0, h] * dsn, axis=1, keepdims=True), axis=0, keepdims=True)
            dcd_ref[0, h] = jnp.broadcast_to(dcd, (1, LANES))
        for h, hs in heads:
            du_ref[:, hs] = _dot_tn(attn_ref[0, h], _b(do_ref[:, hs])) + _dot(kd_ref[:, hs], _b(ds_sc[h]))
        for h, hs in heads:
            dw_ref[:, hs] = -_dot_nt(_b(du_ref[:, hs]), _b(s_ref[0, h]))
        for h, hs in heads:
            ds_sc[h] = (ds_sc[h] * cd_ref[0, h] + _dot_tn(qd_ref[:, hs], _b(do_ref[:, hs]))
                        - _dot_tn(w_ref[:, hs], _b(du_ref[:, hs])))

    blk = pl.BlockSpec((C, V_W), lambda n: (N - 1 - n, 0))
    per_head = lambda a, b: pl.BlockSpec((1, N_V, a, b), lambda n: (N - 1 - n, 0, 0, 0))
    return pl.pallas_call(
        body, grid=(N,),
        in_specs=[blk, blk, blk, blk, per_head(C, C), per_head(1, LANES), per_head(HEAD_DIM, HEAD_DIM), blk],
        out_specs=[blk, blk, blk, blk, per_head(C, C), per_head(1, LANES)],
        out_shape=[SDS((T, V_W), f32)] * 4 + [SDS((N, N_V, C, C), f32), SDS((N, N_V, 1, LANES), f32)],
        scratch_shapes=[pltpu.VMEM((N_V, HEAD_DIM, HEAD_DIM), f32)], name=name, compiler_params=_cp("arbitrary"),
    )(do, qd, kd, w, attn, cd, s_all, vn)


def _delta_prep_bwd(qkv, ba, ea_row, dtb_row, tinv, u, w, dqd, dkd, du, dw, dattn, dcd, name):
    T = qkv.shape[0]
    C = CHUNK_A
    N = T // C

    def body(q_ref, k_ref, v_ref, ba_ref, ea_ref, dtb_ref, tinv_ref, u_ref, w_ref,
             dqd_ref, dkd_ref, du_ref, dw_ref, dattn_ref, dcd_ref,
             dqkv_ref, dba_ref, dalog_ref, ddtb_ref, drv_sc, drk_sc, da_sc):
        row, col = _tri_masks(C)
        lane = lax.broadcasted_iota(jnp.int32, (C, LANES), 1)
        rowc = lax.broadcasted_iota(jnp.int32, (C, 1), 0)
        for h in range(N_V):
            hs = pl.ds(h * HEAD_DIM, HEAD_DIM)
            drv_sc[:, hs] = _dot3(tinv_ref[0, h], du_ref[:, hs], _TN)
        for h in range(N_V):
            hs = pl.ds(h * HEAD_DIM, HEAD_DIM)
            drk_sc[:, hs] = _dot3(tinv_ref[0, h], dw_ref[:, hs], _TN)
        for h in range(N_V):
            hs = pl.ds(h * HEAD_DIM, HEAD_DIM)
            da_sc[h] = -jnp.where(row > col, _dot_nt(_b(drv_sc[:, hs]), _b(u_ref[:, hs]))
                                  + _dot_nt(_b(drk_sc[:, hs]), w_ref[:, hs]), 0.0)
        ba_v, ea, dtb = ba_ref[...], ea_ref[...], dtb_ref[...]
        sig, gblk = _gate_block(ba_v, ea, dtb, lane)
        lower = (row >= col).astype(f32)
        upper_s = (col > row).astype(f32)
        upper = (col >= row).astype(f32)
        gam_all = _dot(lower, gblk, HI)
        rg_all = _dot(upper_s, gblk, HI)
        tot = jnp.sum(gblk, axis=0, keepdims=True)
        dbeta_blk = jnp.zeros((C, LANES), f32)
        dgam_blk = jnp.zeros((C, LANES), f32)
        for j in range(N_QK):
            qk = pl.ds(j * HEAD_DIM, HEAD_DIM)
            qh, kh = q_ref[:, qk], k_ref[:, qk]
            qhb, khb = _b(qh), _b(kh)
            kk = _dot_nt(khb, khb)
            qkm = _dot_nt(qhb, khb)
            dq_j = jnp.zeros((C, HEAD_DIM), f32)
            dk_j = jnp.zeros((C, HEAD_DIM), f32)
            for h in (2 * j, 2 * j + 1):
                hs = pl.ds(h * HEAD_DIM, HEAD_DIM)
                vh = v_ref[:, hs]
                beta_c = _lane_pick(sig, lane, h)
                gam_c, rg_c, tot_h, dm = _head_decay(gam_all, rg_all, tot, lane, h, row, col)
                eg, er, cdh = jnp.exp(gam_c), jnp.exp(rg_c), jnp.exp(tot_h)
                d_rv, d_rk, da = drv_sc[:, hs], drk_sc[:, hs], da_sc[h]
                dqkv_ref[:, pl.ds(2 * QK_W + h * HEAD_DIM, HEAD_DIM)] = beta_c * d_rv
                dbeta = jnp.sum(d_rv * vh + d_rk * (eg * kh), axis=1, keepdims=True)
                dk_h = (beta_c * eg) * d_rk
                d_eg = jnp.sum(d_rk * kh, axis=1, keepdims=True) * beta_c
                bkd = da * dm
                dbeta = dbeta + jnp.sum(bkd * kk, axis=1, keepdims=True)
                dkk = bkd * beta_c
                ddm = da * beta_c * kk
                dattn_h = dattn_ref[0, h]
                dqk = dattn_h * dm
                ddm = ddm + dattn_h * qkm
                dqd_h, dkd_h = dqd_ref[:, hs], dkd_ref[:, hs]
                dq_j = dq_j + _dot(_b(dqk), khb) + eg * dqd_h
                dk_h = dk_h + _dot_tn(_b(dqk), qhb) + _dot(_b(dkk + dkk.T), khb) + er * dkd_h
                dk_j = dk_j + dk_h
                d_eg = d_eg + jnp.sum(dqd_h * qh, axis=1, keepdims=True)
                d_er = jnp.sum(dkd_h * kh, axis=1, keepdims=True)
                e = ddm * dm
                dgam = jnp.sum(e, axis=1, keepdims=True) - jnp.sum(e.T, axis=1, keepdims=True)
                dgam = dgam + d_eg * eg - d_er * er
                extra = jnp.sum(d_er * er, axis=0, keepdims=True) + jnp.max(dcd_ref[0, h], axis=1, keepdims=True) * cdh
                dgam = dgam + jnp.where(rowc == C - 1, extra, 0.0)
                dbeta_blk = jnp.where(lane == h, dbeta, dbeta_blk)
                dgam_blk = jnp.where(lane == N_V + h, dgam, dgam_blk)
            dqkv_ref[:, qk] = dq_j
            dqkv_ref[:, pl.ds(QK_W + j * HEAD_DIM, HEAD_DIM)] = dk_j
        dg_all = _dot(upper, dgam_blk, HI)
        dsp = dg_all * (-ea) * _sig(ba_v + dtb)
        dba_ref[...] = jnp.where(lane < N_V, dbeta_blk * sig * (1.0 - sig), dsp)
        part_alog = jnp.sum(dg_all * gblk, axis=0, keepdims=True)
        part_dtb = jnp.sum(dsp, axis=0, keepdims=True)

        @pl.when(pl.program_id(0) == 0)
        def _():
            dalog_ref[...] = part_alog
            ddtb_ref[...] = part_dtb

        @pl.when(pl.program_id(0) > 0)
        def _():
            dalog_ref[...] += part_alog
            ddtb_ref[...] += part_dtb

    big = lambda n: (n, 0)
    wide = pl.BlockSpec((C, V_W), big)
    sq = pl.BlockSpec((1, N_V, C, C), lambda n: (n, 0, 0, 0))
    return pl.pallas_call(
        body, grid=(N,),
        in_specs=[pl.BlockSpec((C, QK_W), lambda n: (n, 0)), pl.BlockSpec((C, QK_W), lambda n: (n, 1)),
                  pl.BlockSpec((C, V_W), lambda n: (n, 1)), pl.BlockSpec((C, LANES), big),
                  _whole((1, LANES)), _whole((1, LANES)), sq, wide, wide, wide, wide, wide, wide, sq,
                  pl.BlockSpec((1, N_V, 1, LANES), lambda n: (n, 0, 0, 0))],
        out_specs=[pl.BlockSpec((C, 2 * QK_W + V_W), big), pl.BlockSpec((C, LANES), big),
                   _whole((1, LANES)), _whole((1, LANES))],
        out_shape=[SDS((T, 2 * QK_W + V_W), f32), SDS((T, LANES), f32), SDS((1, LANES), f32), SDS((1, LANES), f32)],
        scratch_shapes=[pltpu.VMEM((C, V_W), f32), pltpu.VMEM((C, V_W), f32), pltpu.VMEM((N_V, C, C), f32)],
        name=name, compiler_params=_cp("arbitrary"),
    )(qkv, qkv, qkv, ba, ea_row, dtb_row, tinv, u, w, dqd, dkd, du, dw, dattn, dcd)


def _onorm_fwd(o, proj, hg, name):
    T = o.shape[0]
    tr = _divisor(T, 256, 16)

    def body(o_ref, z_ref, g_ref, out_ref):
        g = g_ref[...]
        for h in range(N_V):
            hs = pl.ds(h * HEAD_DIM, HEAD_DIM)
            oh = o_ref[:, hs]
            r = lax.rsqrt(jnp.mean(oh * oh, axis=-1, keepdims=True) + EPS)
            out_ref[:, hs] = (oh * r * g * _silu(z_ref[:, hs])).astype(out_ref.dtype)

    return pl.pallas_call(
        body, grid=(T // tr,), in_specs=[_rows(tr, V_W), _rows(tr, V_W, C_Z // V_W), _whole((1, HEAD_DIM))],
        out_specs=_rows(tr, V_W), out_shape=SDS((T, V_W), bf16), name=name, compiler_params=_cp("parallel"),
    )(o, proj, hg)


def _onorm_bwd(don, o, proj, hg, dproj, name):
    T = o.shape[0]
    tr = _divisor(T, 256, 16)

    def body(don_ref, o_ref, z_ref, g_ref, dproj_ref, do_ref, dz_ref, dg_ref):
        g = g_ref[...]
        dg = jnp.zeros((1, HEAD_DIM), f32)
        for h in range(N_V):
            hs = pl.ds(h * HEAD_DIM, HEAD_DIM)
            oh, zh, dh = o_ref[:, hs], z_ref[:, hs], don_ref[:, hs]
            r = lax.rsqrt(jnp.mean(oh * oh, axis=-1, keepdims=True) + EPS)
            d_n = dh * _silu(zh)
            dz_ref[:, hs] = (dh * (oh * r * g) * _dsilu(zh)).astype(dz_ref.dtype)
            dy = d_n * g
            m = jnp.mean(dy * oh, axis=-1, keepdims=True)
            do_ref[:, hs] = r * dy - oh * (r * r * r * m)
            dg = dg + jnp.sum(d_n * oh * r, axis=0, keepdims=True)

        @pl.when(pl.program_id(0) == 0)
        def _():
            dg_ref[...] = dg

        @pl.when(pl.program_id(0) > 0)
        def _():
            dg_ref[...] += dg

    return pl.pallas_call(
        body, grid=(T // tr,),
        in_specs=[_rows(tr, V_W), _rows(tr, V_W), _rows(tr, V_W, C_Z // V_W), _whole((1, HEAD_DIM)), ANY],
        out_specs=[_rows(tr, V_W), _rows(tr, V_W, C_Z // V_W), _whole((1, HEAD_DIM))],
        out_shape=[SDS((T, V_W), f32), SDS(dproj.shape, bf16), SDS((1, HEAD_DIM), f32)], input_output_aliases={4: 1},
        name=name, compiler_params=_cp("arbitrary"),
    )(don, o, proj, hg, dproj)


def _sgu_parts(ub, vb, gain):
    gv = _gelu(vb)
    r = lax.rsqrt(jnp.mean(gv * gv, axis=-1, keepdims=True) + EPS)
    return _gelu(ub), gv, r, gv * r * gain


def _sgu_fwd(proj, gain, w_s, b_bc, name):
    T = proj.shape[0]
    C = CHUNK_B

    def body(ub_ref, vb_ref, g_ref, w_ref, b_ref, o_ref):
        row, col = _tri_masks(C)
        u, _, _, vn = _sgu_parts(ub_ref[...], vb_ref[...], g_ref[...])
        for g in range(N_GROUPS):
            gs = pl.ds(g * GROUP_DIM, GROUP_DIM)
            wg = jnp.where(row >= col, w_ref[g], 0.0)
            mixed = _dot(_b(wg), _b(vn[:, g * GROUP_DIM:(g + 1) * GROUP_DIM])) + b_ref[g]
            o_ref[:, gs] = (u[:, g * GROUP_DIM:(g + 1) * GROUP_DIM] * mixed).astype(o_ref.dtype)

    return pl.pallas_call(
        body, grid=(T // C,),
        in_specs=[_rows(C, WIDTH_B, C_UB // WIDTH_B), _rows(C, WIDTH_B, C_VB // WIDTH_B), _whole((1, WIDTH_B)),
                  _whole((N_GROUPS, C, C)), _whole((N_GROUPS, C, GROUP_DIM))],
        out_specs=_rows(C, WIDTH_B), out_shape=SDS((T, WIDTH_B), bf16), name=name, compiler_params=_cp("parallel"),
    )(proj, proj, gain, w_s, b_bc)


def _sgu_bwd(dsgu, proj, gain, w_s, b_bc, dproj, name):
    T = proj.shape[0]
    C = CHUNK_B

    def body(d_ref, ub_ref, vb_ref, g_ref, w_ref, b_ref, dproj_ref, duv_ref, dw_ref, db_ref, dg_ref):
        dub_ref = duv_ref.at[:, pl.ds(0, WIDTH_B)]
        dvb_ref = duv_ref.at[:, pl.ds(WIDTH_B, WIDTH_B)]
        first = pl.program_id(0) == 0
        row, col = _tri_masks(C)
        ub, vb, gain_v = ub_ref[...], vb_ref[...], g_ref[...]
        u, gv, r, vn = _sgu_parts(ub, vb, gain_v)
        d = d_ref[...]
        dvn_parts = []
        for g in range(N_GROUPS):
            sl = slice(g * GROUP_DIM, (g + 1) * GROUP_DIM)
            wg = jnp.where(row >= col, w_ref[g], 0.0)
            vng = _b(vn[:, sl])
            mixed = _dot(_b(wg), vng) + b_ref[g]
            dub_ref[:, pl.ds(g * GROUP_DIM, GROUP_DIM)] = (d[:, sl] * mixed * _dgelu(ub[:, sl])).astype(dub_ref.dtype)
            dmix = d[:, sl] * u[:, sl]
            dmb = _b(dmix)
            dwg = jnp.where(row >= col, _dot_nt(dmb, vng), 0.0)
            dbg = jnp.sum(dmix, axis=1, keepdims=True)

            @pl.when(first)
            def _():
                dw_ref[g] = dwg
                db_ref[g] = dbg

            @pl.when(jnp.logical_not(first))
            def _():
                dw_ref[g] += dwg
                db_ref[g] += dbg

            dvn_parts.append(_dot_tn(_b(wg), dmb))
        dvn = jnp.concatenate(dvn_parts, axis=1)
        dy = dvn * gain_v
        m = jnp.mean(dy * gv, axis=-1, keepdims=True)
        dgv = r * dy - gv * (r * r * r * m)
        dvb_ref[...] = (dgv * _dgelu(vb)).astype(dvb_ref.dtype)
        dgain = jnp.sum(dvn * gv * r, axis=0, keepdims=True)

        @pl.when(first)
        def _():
            dg_ref[...] = dgain

        @pl.when(jnp.logical_not(first))
        def _():
            dg_ref[...] += dgain

    return pl.pallas_call(
        body, grid=(T // C,),
        in_specs=[_rows(C, WIDTH_B), _rows(C, WIDTH_B, C_UB // WIDTH_B), _rows(C, WIDTH_B, C_VB // WIDTH_B),
                  _whole((1, WIDTH_B)), _whole((N_GROUPS, C, C)), _whole((N_GROUPS, C, GROUP_DIM)), ANY],
        out_specs=[_rows(C, 2 * WIDTH_B, C_UB // (2 * WIDTH_B)), _whole((N_GROUPS, C, C)), _whole((N_GROUPS, C, 1)),
                   _whole((1, WIDTH_B))],
        out_shape=[SDS(dproj.shape, bf16), SDS((N_GROUPS, C, C), f32), SDS((N_GROUPS, C, 1), f32), SDS((1, WIDTH_B), f32)],
        input_output_aliases={6: 0}, name=name, compiler_params=_cp("arbitrary"),
    )(dsgu, proj, proj, gain, w_s, b_bc, dproj)


def _merge_fwd(proj, ya, yb, name):
    T = proj.shape[0]
    tr = _divisor(T, 256, 16)

    def body(ga_ref, gb_ref, ya_ref, yb_ref, o_ref):
        o_ref[...] = (_sig(ga_ref[...]) * ya_ref[...] + _sig(gb_ref[...]) * yb_ref[...]).astype(o_ref.dtype)

    return pl.pallas_call(
        body, grid=(T // tr,),
        in_specs=[_rows(tr, D_MODEL, C_GA // D_MODEL), _rows(tr, D_MODEL, C_GB // D_MODEL), _rows(tr, D_MODEL), _rows(tr, D_MODEL)],
        out_specs=_rows(tr, D_MODEL), out_shape=SDS((T, D_MODEL), bf16), name=name, compiler_params=_cp("parallel"),
    )(proj, proj, ya, yb)


def _merge_bwd(dm, proj, ya, yb, name):
    T = proj.shape[0]
    tr = _divisor(T, 256, 16)

    def body(dm_ref, ga_ref, gb_ref, ya_ref, yb_ref, dya_ref, dyb_ref, dg_ref):
        d = dm_ref[...]
        sa, sb = _sig(ga_ref[...]), _sig(gb_ref[...])
        dya_ref[...] = (d * sa).astype(bf16)
        dyb_ref[...] = (d * sb).astype(bf16)
        dg_ref[:, :D_MODEL] = (d * ya_ref[...] * sa * (1.0 - sa)).astype(bf16)
        dg_ref[:, D_MODEL:] = (d * yb_ref[...] * sb * (1.0 - sb)).astype(bf16)

    return pl.pallas_call(
        body, grid=(T // tr,),
        in_specs=[_rows(tr, D_MODEL), _rows(tr, D_MODEL, C_GA // D_MODEL), _rows(tr, D_MODEL, C_GB // D_MODEL),
                  _rows(tr, D_MODEL), _rows(tr, D_MODEL)],
        out_specs=[_rows(tr, D_MODEL), _rows(tr, D_MODEL), _rows(tr, 2 * D_MODEL, C_GA // (2 * D_MODEL))],
        out_shape=[SDS((T, D_MODEL), bf16), SDS((T, D_MODEL), bf16), SDS((T, PM), bf16)], name=name,
        compiler_params=_cp("parallel"),
    )(dm, proj, proj, ya, yb)


def _ffnconv_fwd(upg, upv, wg, wv, bg, bv, name):
    T, F = upg.shape
    R = min(ROW_CHUNK, T)
    n_chunks = T // R

    def body(g_ref, v_ref, wg_ref, wv_ref, bg_ref, bv_ref, o_ref):
        wgv, wvv = _w_rows(wg_ref, 3), _w_rows(wv_ref, 3)

        def chunk(ci, carry):
            r0 = pl.multiple_of(ci * R, R)
            cg = _causal_conv(g_ref[pl.ds(r0, R), :], _prev8(g_ref, r0, ci), wgv, 3) + bg_ref[...]
            cv = _causal_conv(v_ref[pl.ds(r0, R), :], _prev8(v_ref, r0, ci), wvv, 3) + bv_ref[...]
            o_ref[pl.ds(r0, R), :] = (_silu(cg) * cv).astype(o_ref.dtype)
            return carry

        lax.fori_loop(0, n_chunks, chunk, 0)

    col = lambda p: (0, p)
    return pl.pallas_call(
        body, grid=(F // LANES,),
        in_specs=[pl.BlockSpec((T, LANES), col)] * 2 + [pl.BlockSpec((3, LANES), col)] * 2 + [pl.BlockSpec((1, LANES), col)] * 2,
        out_specs=pl.BlockSpec((T, LANES), col), out_shape=SDS((T, F), bf16), name=name, compiler_params=_cp("parallel"),
    )(upg, upv, wg, wv, bg, bv)


def _ffnconv_bwd(dact, upg, upv, wg, wv, bg, bv, name):
    T, F = upg.shape
    R = min(ROW_CHUNK, T)
    n_chunks = T // R

    def body(d_ref, g_ref, v_ref, wg_ref, wv_ref, bg_ref, bv_ref,
             dg_ref, dv_ref, dwg_ref, dwv_ref, dbg_ref, dbv_ref, sg, sv):
        wgv, wvv = _w_rows(wg_ref, 3), _w_rows(wv_ref, 3)

        def phase1(ci, carry):
            dwg, dwv = carry
            r0 = pl.multiple_of(ci * R, R)
            gcur, vcur = g_ref[pl.ds(r0, R), :], v_ref[pl.ds(r0, R), :]
            gp, vp = _prev8(g_ref, r0, ci), _prev8(v_ref, r0, ci)
            gsh = [gcur] + [_shift_down(gcur, gp, s) for s in (1, 2)]
            vsh = [vcur] + [_shift_down(vcur, vp, s) for s in (1, 2)]
            cg = gsh[0] * wgv[2] + gsh[1] * wgv[1] + gsh[2] * wgv[0] + bg_ref[...]
            cv = vsh[0] * wvv[2] + vsh[1] * wvv[1] + vsh[2] * wvv[0] + bv_ref[...]
            d = d_ref[pl.ds(r0, R), :]
            dcv = d * _silu(cg)
            dcg = d * cv * _dsilu(cg)
            sg[pl.ds(r0, R), :] = dcg
            sv[pl.ds(r0, R), :] = dcv
            rg = [jnp.sum(dcg * gsh[2 - j], axis=0, keepdims=True) for j in range(3)] + [jnp.sum(dcg, axis=0, keepdims=True)]
            rv = [jnp.sum(dcv * vsh[2 - j], axis=0, keepdims=True) for j in range(3)] + [jnp.sum(dcv, axis=0, keepdims=True)]
            return tuple(a + b for a, b in zip(dwg, rg)), tuple(a + b for a, b in zip(dwv, rv))

        z4 = tuple(jnp.zeros((1, LANES), f32) for _ in range(4))
        dwg, dwv = lax.fori_loop(0, n_chunks, phase1, (z4, z4))
        for j in range(3):
            dwg_ref[j:j + 1, :] = dwg[j]
            dwv_ref[j:j + 1, :] = dwv[j]
        dbg_ref[...] = dwg[3]
        dbv_ref[...] = dwv[3]

        def phase2(ci, carry):
            r0 = pl.multiple_of(ci * R, R)
            for sc, wv_, out in ((sg, wgv, dg_ref), (sv, wvv, dv_ref)):
                cur = sc[pl.ds(r0, R), :]
                n8 = _next8(sc, r0, R, ci, n_chunks)
                acc = cur * wv_[2] + _shift_up(cur, n8, 1) * wv_[1] + _shift_up(cur, n8, 2) * wv_[0]
                out[pl.ds(r0, R), :] = acc.astype(out.dtype)
            return carry

        lax.fori_loop(0, n_chunks, phase2, 0)

    col = lambda p: (0, p)
    big, w3, b1 = pl.BlockSpec((T, LANES), col), pl.BlockSpec((3, LANES), col), pl.BlockSpec((1, LANES), col)
    return pl.pallas_call(
        body, grid=(F // LANES,), in_specs=[big, big, big, w3, w3, b1, b1], out_specs=[big, big, w3, w3, b1, b1],
        out_shape=[SDS((T, F), bf16), SDS((T, F), bf16), SDS((3, F), f32), SDS((3, F), f32), SDS((1, F), f32), SDS((1, F), f32)],
        scratch_shapes=[pltpu.VMEM((T, LANES), f32), pltpu.VMEM((T, LANES), f32)], name=name, compiler_params=_cp("parallel"),
    )(dact, upg, upv, wg, wv, bg, bv)


def _loss_head(x, gain, target, name):
    T, Dm = x.shape
    tr = _divisor(T, 256, 16)

    def body(x_ref, g_ref, t_ref, l_ref, dx_ref, dg_ref):
        xv, g = x_ref[...], g_ref[...]
        r = lax.rsqrt(jnp.mean(xv * xv, axis=-1, keepdims=True) + EPS)
        err = xv * r * g - t_ref[...]
        part_l = 0.5 * jnp.sum(jnp.mean(err * err, axis=-1, keepdims=True), axis=0, keepdims=True)
        dy = err * (1.0 / Dm)
        dyg = dy * g
        m = jnp.mean(dyg * xv, axis=-1, keepdims=True)
        dx_ref[...] = r * dyg - xv * (r * r * r * m)
        part_g = jnp.sum(dy * xv * r, axis=0, keepdims=True)
        part_l = jnp.broadcast_to(part_l, (1, LANES))

        @pl.when(pl.program_id(0) == 0)
        def _():
            l_ref[...] = part_l
            dg_ref[...] = part_g

        @pl.when(pl.program_id(0) > 0)
        def _():
            l_ref[...] += part_l
            dg_ref[...] += part_g

    return pl.pallas_call(
        body, grid=(T // tr,), in_specs=[_rows(tr, Dm), _whole((1, Dm)), _rows(tr, Dm)],
        out_specs=[_whole((1, LANES)), _rows(tr, Dm), _whole((1, Dm))],
        out_shape=[SDS((1, LANES), f32), SDS((T, Dm), f32), SDS((1, Dm), f32)], name=name, compiler_params=_cp("arbitrary"),
    )(x, gain, target)


def _lane_row(vec, offset):
    return jnp.pad(vec.astype(f32), (offset, LANES - offset - vec.shape[0]))[None]


def _layer_fwd(x, p_i, W, S, li):
    nm = lambda s: f"{s}_l{li}"
    sv = {"x0": x}
    h1 = _rms_fwd(x, S["norm_mix"], nm("rms_mix"))
    proj = _mm(h1, W["in_main"], "nn", nm("proj_main"))
    ba = _mm(h1, W["in_ba"], "nn", nm("proj_ba"))
    qkv = _qkvconv_fwd(proj, S["conv_qkv"], nm("qkvconv"))
    qd, kd, u, w, attn, tinv, cd, _ = _delta_prep(qkv, ba, S["ea_row"], S["dtb_row"], nm("delta_prep"))
    o, s_all, vn = _delta_scan(qd, kd, u, w, attn, cd, nm("delta_scan"))
    on = _onorm_fwd(o, proj, S["head_norm"], nm("onorm"))
    ya = _mm(on, W["branch_a"], "nn", nm("branch_a"))
    sgu = _sgu_fwd(proj, S["sgu_norm"], S["w_spatial"], S["b_bc"], nm("sgu"))
    yb = _mm(sgu, W["branch_b"], "nn", nm("branch_b"))
    merged = _merge_fwd(proj, ya, yb, nm("merge"))
    x1 = _mm(merged, W["out"], "nn", nm("out_proj"), c=x)
    h2 = _rms_fwd(x1, S["norm_ffn"], nm("rms_ffn"))
    upg = _mm(h2, W["up_g"], "nn", nm("ffn_up_g"))
    upv = _mm(h2, W["up_v"], "nn", nm("ffn_up_v"))
    act = _ffnconv_fwd(upg, upv, S["conv_g"], S["conv_v"], S["bias_g"], S["bias_v"], nm("ffnconv"))
    x2 = _mm(act, W["down"], "nn", nm("ffn_down"), c=x1, tk=2816)
    h3 = _rms_fwd(x2, S["norm_ple"], nm("rms_ple"))
    gl = _mm(h3, W["ple_gate"], "nn", nm("ple_gate"))
    pp = _mm(p_i, W["ple_proj"], "nn", nm("ple_proj"))
    (x3,) = _ew(lambda a, g, q: (a + _sig(g) * q,), [x2, gl, pp], [f32], nm("ple_mix"))
    sv.update(h1=h1, proj=proj, ba=ba, qkv=qkv, qd=qd, kd=kd, u=u, w=w, attn=attn, tinv=tinv, cd=cd, o=o, s_all=s_all,
              vn=vn, on=on, ya=ya, sgu=sgu, yb=yb, merged=merged, x1=x1, h2=h2, upg=upg, upv=upv, act=act, x2=x2,
              h3=h3, gl=gl, pp=pp, p=p_i)
    return x3, sv


def _layer_bwd(dx3, sv, W, S, li):
    nm = lambda s: f"{s}_l{li}"
    G = {}
    dgl, dpp = _ew(lambda d, g, q: ((lambda s: (d * q * s * (1.0 - s), d * s))(_sig(g))),
                   [dx3, sv["gl"], sv["pp"]], [bf16, bf16], nm("ple_mix_bwd"))
    G["w_ple_gate"] = _mm(sv["h3"], dgl, "tn", nm("d_ple_gate"))
    G["w_ple_proj"] = _mm(sv["p"], dpp, "tn", nm("d_ple_proj"))
    dh3 = _mm(dgl, W["ple_gate"], "nt", nm("dh_ple"))
    dx2, G["norm_ple"] = _rms_bwd(sv["x2"], S["norm_ple"], dh3, dx3, nm("rms_ple_bwd"))
    dact = _mm(dx2, W["down"], "nt", nm("d_act"))
    G["w_ffn_down"] = _mm(sv["act"], dx2, "tn", nm("d_ffn_down"))
    dupg, dupv, dcg, dcv, dbg, dbv = _ffnconv_bwd(dact, sv["upg"], sv["upv"], S["conv_g"], S["conv_v"], S["bias_g"],
                                                  S["bias_v"], nm("ffnconv_bwd"))
    G["conv_ffn"] = jnp.concatenate([dcg, dcv], axis=1)
    G["b_conv_ffn"] = jnp.concatenate([dbg, dbv], axis=1)
    G["up_g"] = _mm(sv["h2"], dupg, "tn", nm("d_ffn_up_g"))
    G["up_v"] = _mm(sv["h2"], dupv, "tn", nm("d_ffn_up_v"))
    dh2 = _mm(dupg, W["up_g"], "nt", nm("dh_ffn_g"), tk=2816)
    dh2 = _mm(dupv, W["up_v"], "nt", nm("dh_ffn_v"), c=dh2, tk=2816)
    dx1, G["norm_ffn"] = _rms_bwd(sv["x1"], S["norm_ffn"], dh2, dx2, nm("rms_ffn_bwd"))
    dmerged = _mm(dx1, W["out"], "nt", nm("d_merged"))
    G["w_out"] = _mm(sv["merged"], dx1, "tn", nm("d_w_out"))
    dya, dyb, dproj = _merge_bwd(dmerged, sv["proj"], sv["ya"], sv["yb"], nm("merge_bwd"))
    G["w_branch_a"] = _mm(sv["on"], dya, "tn", nm("d_branch_a"))
    G["w_branch_b"] = _mm(sv["sgu"], dyb, "tn", nm("d_branch_b"))
    don = _mm(dya, W["branch_a"], "nt", nm("d_on"))
    dsgu = _mm(dyb, W["branch_b"], "nt", nm("d_sgu"))
    dproj, G["w_spatial"], db_s, G["sgu_norm"] = _sgu_bwd(dsgu, sv["proj"], S["sgu_norm"], S["w_spatial"], S["b_bc"], dproj,
                                                          nm("sgu_bwd"))
    G["b_spatial"] = db_s.reshape(N_GROUPS, CHUNK_B)
    do, dproj, G["head_norm"] = _onorm_bwd(don, sv["o"], sv["proj"], S["head_norm"], dproj, nm("onorm_bwd"))
    dqd, dkd, du, dw, dattn, dcd = _delta_scan_bwd(do, sv["qd"], sv["kd"], sv["w"], sv["attn"], sv["cd"], sv["s_all"], sv["vn"],
                                                   nm("delta_scan_bwd"))
    dqkv_n, dba, dalog_row, ddtb_row = _delta_prep_bwd(sv["qkv"], sv["ba"], S["ea_row"], S["dtb_row"], sv["tinv"], sv["u"], sv["w"],
                                                       dqd, dkd, du, dw, dattn, dcd, nm("delta_prep_bwd"))
    G["a_log"] = dalog_row[0, N_V:2 * N_V]
    G["dt_bias"] = ddtb_row[0, N_V:2 * N_V]
    dproj, G["conv_qkv"] = _qkvconv_bwd(sv["proj"], S["conv_qkv"], dqkv_n, dproj, nm("qkvconv_bwd"))
    G["in_main"] = _mm(sv["h1"], dproj, "tn", nm("d_in_main"))
    G["in_ba"] = _mm(sv["h1"], dba, "tn", nm("d_in_ba"))
    dh1 = _mm(dba, W["in_ba"], "nt", nm("dh_mix_ba"))
    dh1 = _mm(dproj, W["in_main"], "nt", nm("dh_mix"), c=dh1)
    dx0, G["norm_mix"] = _rms_bwd(sv["x0"], S["norm_mix"], dh1, dx1, nm("rms_mix_bwd"))
    return dx0, G


_BA0 = C_UB
_BA1 = C_UB + 2 * N_V


def _cols(segments, lo, hi):
    out = []
    for start, a in segments:
        s_lo, s_hi = max(lo, start), min(hi, start + a.shape[1])
        if s_lo < s_hi:
            out.append(a[:, s_lo - start:s_hi - start])
    return out


def _big_weights(sh):
    n_in = N_IN // N_CHIPS
    w_in = [(j * n_in, a) for j, a in enumerate(sh["w_in"])]
    rows = lambda k: jnp.concatenate(sh[k], axis=0)
    cols = lambda parts: jnp.concatenate(parts, axis=1)
    ba = jnp.pad(cols(_cols(w_in, _BA0, _BA1)), ((0, 0), (0, LANES - 2 * N_V)))
    return dict(
        in_main=cols(_cols(w_in, 0, _BA0) + _cols(w_in, _BA1, N_IN)), in_ba=ba,
        branch_a=rows("w_branch_a"), branch_b=cols(sh["w_branch_b"]), out=rows("w_out"),
        up_g=cols(sh["w_ffn_up"][:2]), up_v=cols(sh["w_ffn_up"][2:]), down=rows("w_ffn_down"),
        ple_gate=rows("w_ple_gate"), ple_proj=cols(sh["w_ple_proj"]))


def _grads_by_chip(G):
    n_in = N_IN // N_CHIPS
    w_in = [(0, G["in_main"][:, :_BA0]), (_BA0, G["in_ba"][:, :2 * N_V]), (_BA1, G["in_main"][:, _BA0:])]
    split = lambda g: jnp.stack(jnp.split(g, N_CHIPS, axis=1))
    by_rows = lambda g: g.reshape(N_CHIPS, -1, g.shape[1])
    return dict(
        w_in=jnp.stack([jnp.concatenate(_cols(w_in, j * n_in, (j + 1) * n_in), axis=1) for j in range(N_CHIPS)]),
        w_branch_a=by_rows(G["w_branch_a"]), w_branch_b=split(G["w_branch_b"]), w_out=by_rows(G["w_out"]),
        w_ffn_up=jnp.stack(jnp.split(G["up_g"], 2, axis=1) + jnp.split(G["up_v"], 2, axis=1)),
        w_ffn_down=by_rows(G["w_ffn_down"]), w_ple_gate=by_rows(G["w_ple_gate"]), w_ple_proj=split(G["w_ple_proj"]))


def _small_params(sm, i):
    return dict(
        norm_mix=sm["norm_mix"][i][None], conv_qkv=sm["conv_qkv"][i], ea_row=_lane_row(jnp.exp(sm["a_log"][i]), N_V),
        dtb_row=_lane_row(sm["dt_bias"][i], N_V), head_norm=sm["head_norm"][i][None], sgu_norm=sm["sgu_norm"][i][None],
        w_spatial=sm["w_spatial"][i],
        b_bc=jnp.broadcast_to(sm["b_spatial"][i][:, :, None], (N_GROUPS, CHUNK_B, GROUP_DIM)),
        norm_ffn=sm["norm_ffn"][i][None], conv_g=sm["conv_ffn"][i][:, :D_FF], conv_v=sm["conv_ffn"][i][:, D_FF:],
        bias_g=sm["b_conv_ffn"][i][None, :D_FF], bias_v=sm["b_conv_ffn"][i][None, D_FF:], norm_ple=sm["norm_ple"][i][None])


MESH = pl.DeviceIdType.MESH
N_CHIPS = 4
ANY = pl.BlockSpec(memory_space=pl.ANY)
ROW_ALIGN = 32


def _place():
    x, y, c = lax.axis_index("x"), lax.axis_index("y"), lax.axis_index("c")
    chips = [(1 - x, y), (x, 1 - y), (1 - x, 1 - y)]
    return x, y, c, 2 * x + y, chips, (x, y, 1 - c)


def _halves(rows, c):
    h = rows // 2
    return pl.ds(pl.multiple_of(c * h, 16), h), pl.ds(pl.multiple_of((1 - c) * h, 16), h)


def _remote(src, dst, send_sems, recv_sems, k, dev):
    return pltpu.make_async_remote_copy(src_ref=src, dst_ref=dst, send_sem=send_sems.at[k], recv_sem=recv_sems.at[k],
                                        device_id=dev, device_id_type=MESH)


def _gather_shards(bufs, name):
    n = len(bufs)

    def body(*refs):
        ins, outs = refs[:n], refs[n:2 * n]
        send_sems, recv_sems = refs[2 * n:]
        x, y, c, me, chips, sib = _place()
        for i in range(n):
            mine, _ = _halves(ins[i].shape[0], c)
            for k, (px, py) in enumerate(chips):
                _remote(ins[i].at[mine], outs[i].at[me, mine], send_sems, recv_sems, 6 * i + k, (px, py, c)).start()
        for i in range(n):
            mine, _ = _halves(ins[i].shape[0], c)
            for k, (px, py) in enumerate(chips):
                pc = 2 * px + py
                _remote(ins[i].at[mine], outs[i].at[pc, mine], send_sems, recv_sems, 6 * i + k, (px, py, c)).wait_recv()
                _remote(outs[i].at[pc, mine], outs[i].at[pc, mine], send_sems, recv_sems, 6 * i + 3 + k, sib).start()
        for i in range(n):
            mine, other = _halves(ins[i].shape[0], c)
            for k, (px, py) in enumerate(chips):
                pc = 2 * px + py
                _remote(outs[i].at[pc, mine], outs[i].at[pc, other], send_sems, recv_sems, 6 * i + 3 + k, sib).wait_recv()
        for i in range(n):
            mine, _ = _halves(ins[i].shape[0], c)
            for k, (px, py) in enumerate(chips):
                pc = 2 * px + py
                _remote(ins[i].at[mine], outs[i].at[me, mine], send_sems, recv_sems, 6 * i + k, (px, py, c)).wait_send()
                _remote(outs[i].at[pc, mine], outs[i].at[pc, mine], send_sems, recv_sems, 6 * i + 3 + k, sib).wait_send()

    return pl.pallas_call(
        body, in_specs=[ANY] * n, out_specs=[ANY] * n,
        out_shape=[SDS((N_CHIPS,) + b.shape, b.dtype) for b in bufs],
        scratch_shapes=[pltpu.SemaphoreType.DMA((6 * n,)), pltpu.SemaphoreType.DMA((6 * n,))],
        name=name,
    )(*bufs)


def _exchange(srcs_of, out_shapes, n_sems, name):
    n = len(out_shapes)

    def body(*refs):
        n_in = len(refs) - n - 2
        ins, outs = refs[:n_in], refs[n_in:n_in + n]
        send_sems, recv_sems = refs[-2:]
        copies = [_remote(s, d, send_sems, recv_sems, k, dev) for s, d, k, dev in srcs_of(ins, outs, _place())]
        for cp in copies:
            cp.start()
        for cp in copies:
            cp.wait()

    def call(*arrs):
        return pl.pallas_call(
            body, in_specs=[ANY] * len(arrs), out_specs=[ANY] * n, out_shape=out_shapes,
            scratch_shapes=[pltpu.SemaphoreType.DMA((n_sems,)), pltpu.SemaphoreType.DMA((n_sems,))], name=name,
        )(*arrs)

    return call


HBM = pl.BlockSpec(memory_space=pltpu.HBM)
SEM = pl.BlockSpec(memory_space=pltpu.SEMAPHORE)
_SIDE_EFFECT = pltpu.SideEffectType.DATAFLOW_SIDE_EFFECTING


def _exchange_start(copies_of, srcs, land_shapes, n_sems, name):
    ns, nl = len(srcs), len(land_shapes)

    def body(*refs):
        ins = refs[:ns + nl]
        send_sems, recv_sems = refs[ns + nl], refs[ns + nl + 1]
        token = refs[-1]
        for s, d, k, dev in copies_of(ins[:ns], ins[ns:], _place()):
            _remote(s, d, send_sems, recv_sems, k, dev).start()
        token[...] = jnp.zeros_like(token)

    lands = [lax.empty(s.shape, s.dtype) for s in land_shapes]
    operands = [pltpu.with_memory_space_constraint(a, pltpu.HBM) for a in list(srcs) + lands]
    outs = pl.pallas_call(
        body, name=name,
        out_shape=(pltpu.SemaphoreType.DMA((n_sems,)), pltpu.SemaphoreType.DMA((n_sems,)),
                   *[pltpu.HBM(a.shape, a.dtype) for a in operands], SDS((8, LANES), f32)),
        in_specs=[HBM] * (ns + nl), out_specs=(SEM, SEM, *[HBM] * (ns + nl), pl.BlockSpec(memory_space=pltpu.VMEM)),
        input_output_aliases={i: 2 + i for i in range(ns + nl)},
        compiler_params=pltpu.CompilerParams(has_side_effects=_SIDE_EFFECT),
    )(*operands)
    return dict(send=outs[0], recv=outs[1], bufs=list(outs[2:2 + ns + nl]), token=outs[-1], ns=ns, copies_of=copies_of)


def _exchange_wait(handle, after, name):
    ns, bufs, copies_of = handle["ns"], handle["bufs"], handle["copies_of"]
    nb = len(bufs)

    def body(*refs):
        ins = refs[:nb]
        send_sems, recv_sems = refs[nb], refs[nb + 1]
        for s, d, k, dev in copies_of(ins[:ns], ins[ns:], _place()):
            cp = _remote(s, d, send_sems, recv_sems, k, dev)
            cp.wait_send()
            cp.wait_recv()

    outs = pl.pallas_call(
        body, name=name, out_shape=tuple(pltpu.HBM(a.shape, a.dtype) for a in bufs),
        in_specs=[HBM] * nb + [SEM, SEM, ANY], out_specs=tuple([HBM] * nb),
        input_output_aliases={i: i for i in range(nb)},
        compiler_params=pltpu.CompilerParams(has_side_effects=_SIDE_EFFECT),
    )(*bufs, handle["send"], handle["recv"], after)
    return list(outs[:ns]), list(outs[ns:])


def _pick(idx, parts):
    out = parts[-1]
    for j in reversed(range(len(parts) - 1)):
        out = jnp.where(idx == j, parts[j], out)
    return out


def _rs_pair(gs, name):
    def copies(ins, outs, place):
        x, y, c, me, chips, sib = place
        return [(g.at[:, _halves(g.shape[1], c)[1]], o, i, sib) for i, (g, o) in enumerate(zip(ins, outs))]

    shapes = [SDS((N_CHIPS, g.shape[1] // 2, g.shape[2]), g.dtype) for g in gs]
    return _exchange(copies, shapes, len(gs), name)(*gs)


def _gather_copies(ins, outs, place):
    x, y, c, me, chips, sib = place
    out = []
    for i, (s, o) in enumerate(zip(ins, outs)):
        mine, _ = _halves(s.shape[0], c)
        out += [(s.at[mine], o.at[me, mine], 3 * i + k, (px, py, c)) for k, (px, py) in enumerate(chips)]
    return out


def _gather_start(shards, name):
    return _exchange_start(_gather_copies, shards, [SDS((N_CHIPS,) + s.shape, s.dtype) for s in shards], 3 * len(shards), name)


def _gather_forward(lands, name):
    n = len(lands)

    def body(*refs):
        ins, outs = refs[:n], refs[n:2 * n]
        send_sems, recv_sems = refs[2 * n:]
        x, y, c, me, chips, sib = _place()
        copies = []
        for i in range(n):
            mine, _ = _halves(ins[i].shape[1], c)
            for k, (px, py) in enumerate(chips):
                pc = 2 * px + py
                copies.append(_remote(ins[i].at[pc, mine], outs[i].at[pc, mine], send_sems, recv_sems, 3 * i + k, sib))
        for cp in copies:
            cp.start()
        for cp in copies:
            cp.wait()

    return pl.pallas_call(
        body, in_specs=[ANY] * n, out_specs=[ANY] * n, out_shape=[SDS(a.shape, a.dtype) for a in lands],
        input_output_aliases={i: i for i in range(n)},
        scratch_shapes=[pltpu.SemaphoreType.DMA((3 * n,)), pltpu.SemaphoreType.DMA((3 * n,))], name=name,
    )(*lands)


def _rs_chips_copies(ins, outs, place):
    x, y, c, me, chips, sib = place
    return [(p.at[2 * px + py], o.at[k], 3 * i + k, (px, py, c))
            for i, (p, o) in enumerate(zip(ins, outs)) for k, (px, py) in enumerate(chips)]


def _rs_tile(h, n_cols):
    return _divisor(h, max(16, (1 << 20) // (4 * n_cols)), 16)


def _rs_add2(g, got, name):
    _, K, Nc = g.shape
    h = K // 2
    tr = _rs_tile(h, Nc)
    nb = h // tr

    def body(g_ref, got_ref, o_ref):
        o_ref[...] = (g_ref[...].astype(f32) + got_ref[...].astype(f32)).astype(o_ref.dtype)

    blk = lambda j, i: (j, i, 0)
    return pl.pallas_call(
        body, name=name, out_shape=SDS((N_CHIPS, h, Nc), bf16), grid=(N_CHIPS, nb),
        in_specs=[pl.BlockSpec((1, tr, Nc), lambda j, i: (j, lax.axis_index("c") * nb + i, 0)), pl.BlockSpec((1, tr, Nc), blk)],
        out_specs=pl.BlockSpec((1, tr, Nc), blk), compiler_params=_cp("parallel", "parallel"),
    )(g, got)


def _rs_add4(p, parts, name):
    _, h, Nc = p.shape
    tr = _rs_tile(h, Nc)
    nb = h // tr

    def body(own_ref, a_ref, b_ref, c_ref, o_ref):
        up = lambda r: r[0].astype(f32)
        o_ref[...] = (up(own_ref) + up(b_ref)) + (up(a_ref) + up(c_ref))

    part = lambda k: pl.BlockSpec((1, tr, Nc), lambda i, k=k: (k, i, 0))
    me = lambda: 2 * lax.axis_index("x") + lax.axis_index("y")
    return pl.pallas_call(
        body, name=name, out_shape=SDS((2 * h, Nc), f32), grid=(nb,),
        in_specs=[pl.BlockSpec((1, tr, Nc), lambda i: (me(), i, 0)), part(0), part(1), part(2)],
        out_specs=pl.BlockSpec((tr, Nc), lambda i: (lax.axis_index("c") * nb + i, 0)), compiler_params=_cp("parallel"),
    )(p, parts, parts, parts)


def _rs_join(fs, name):
    n = len(fs)

    def body(*refs):
        ins, outs = refs[:n], refs[n:2 * n]
        send_sems, recv_sems = refs[2 * n:]
        x, y, c, me, chips, sib = _place()
        copies = []
        for i in range(n):
            mine, _ = _halves(ins[i].shape[0], c)
            copies.append(_remote(ins[i].at[mine], outs[i].at[mine], send_sems, recv_sems, i, sib))
        for cp in copies:
            cp.start()
        for cp in copies:
            cp.wait()

    return pl.pallas_call(
        body, in_specs=[ANY] * n, out_specs=[ANY] * n, out_shape=[SDS(a.shape, a.dtype) for a in fs],
        input_output_aliases={i: i for i in range(n)},
        scratch_shapes=[pltpu.SemaphoreType.DMA((n,)), pltpu.SemaphoreType.DMA((n,))], name=name,
    )(*fs)


def _rs_begin(gs, names, tag):
    gots = _rs_pair(gs, f"rs_pair_{tag}")
    ps = [_rs_add2(g, got, f"rs_add2_{k}_{tag}") for g, got, k in zip(gs, gots, names)]
    lands = [SDS((3,) + p.shape[1:], p.dtype) for p in ps]
    return _exchange_start(_rs_chips_copies, ps, lands, 3 * len(ps), f"rs_chips_start_{tag}")


def _rs_end(handle, after, names, tag):
    ps, parts = _exchange_wait(handle, after, f"rs_chips_wait_{tag}")
    fs = [_rs_add4(p, part, f"rs_add4_{k}_{tag}") for p, part, k in zip(ps, parts, names)]
    return _rs_join(fs, f"rs_join_{tag}")


def _allreduce_small(buf, name):
    R, L = buf.shape

    def body(x_ref, o_ref, r0, s1, r1, send_sems, recv_sems):
        x, y, c, me, chips, sib = _place()
        cp = _remote(x_ref, r0, send_sems, recv_sems, 0, sib)
        cp.start()
        cp.wait()
        s1[...] = x_ref[...] + r0[...]
        cps = []
        for k, (px, py) in enumerate(chips):
            cp = _remote(s1, r1.at[k], send_sems, recv_sems, 1 + k, (px, py, c))
            cp.start()
            cps.append(cp)
        for cp in cps:
            cp.wait()
        o_ref[...] = (s1[...] + r1[1]) + (r1[0] + r1[2])

    vm = pl.BlockSpec(memory_space=pltpu.VMEM)
    return pl.pallas_call(
        body, in_specs=[vm], out_specs=vm, out_shape=SDS((R, L), f32),
        scratch_shapes=[pltpu.VMEM((R, L), f32), pltpu.VMEM((R, L), f32), pltpu.VMEM((3, R, L), f32),
                        pltpu.SemaphoreType.DMA((4,)), pltpu.SemaphoreType.DMA((4,))],
        name=name, compiler_params=pltpu.CompilerParams(vmem_limit_bytes=VMEM_LIMIT_BYTES),
    )(buf)


BIG = ("w_in", "w_branch_a", "w_branch_b", "w_out", "w_ffn_up", "w_ffn_down", "w_ple_gate", "w_ple_proj")
SMALL_REPL = ("norm_mix", "a_log", "dt_bias", "head_norm", "sgu_norm", "w_spatial", "b_spatial", "norm_ffn", "b_conv_ffn",
              "norm_ple", "norm_final")
SMALL_COLS = ("conv_qkv", "conv_ffn")
WEIGHTS = ("norm_mix", "w_in", "conv_qkv", "a_log", "dt_bias", "head_norm", "sgu_norm", "w_spatial", "b_spatial", "w_branch_a",
           "w_branch_b", "w_out", "norm_ffn", "w_ffn_up", "conv_ffn", "b_conv_ffn", "w_ffn_down", "norm_ple", "w_ple_gate",
           "w_ple_proj", "norm_final")


def _flat(arrs, dtype):
    cat = jnp.concatenate([a.astype(dtype).reshape(-1) for a in arrs])
    unit = LANES * ROW_ALIGN
    cat = jnp.pad(cat, (0, -cat.shape[0] % unit))
    return cat.reshape(-1, LANES)


def _unflat(buf, shapes):
    flat = buf.reshape(-1)
    out, off = [], 0
    for s in shapes:
        n = math.prod(s)
        out.append(flat[off:off + n].reshape(s))
        off += n
    return out


def _adamw(w, g, m, v):
    m2 = ADAM_B1 * m + (1.0 - ADAM_B1) * g
    v2 = ADAM_B2 * v + (1.0 - ADAM_B2) * (g * g)
    m_hat = m2 * (1.0 / (1.0 - ADAM_B1 ** ADAM_STEP))
    v_hat = v2 * (1.0 / (1.0 - ADAM_B2 ** ADAM_STEP))
    delta = -ADAM_LR * (m_hat / (jnp.sqrt(v_hat) + ADAM_EPS) + ADAM_WD * w)
    return delta, m2, v2


def _adamw_layer(w, g, m, v, prev, behind, layer, name):
    K, Nc = g.shape
    tr = _divisor(K, max(16, (1 << 20) // (4 * Nc)), 16)
    nb = K // tr
    rows = pl.BlockSpec((tr, Nc), lambda i: (layer * nb + i, 0))
    n_prev = 0 if prev is None else 4

    def body(w_ref, g_ref, m_ref, v_ref, behind_ref, *refs):
        go_ref, d_ref, mo_ref, vo_ref = refs[n_prev:]
        gv = g_ref[...]
        d, m2, v2 = _adamw(w_ref[...], gv, m_ref[...], v_ref[...])
        go_ref[...] = gv
        d_ref[...] = d
        mo_ref[...] = m2
        vo_ref[...] = v2

    return pl.pallas_call(
        body, grid=(nb,), in_specs=[rows, pl.BlockSpec((tr, Nc), lambda i: (i, 0)), rows, rows, ANY] + [ANY] * n_prev,
        out_specs=[rows] * 4, out_shape=[SDS(w.shape, f32)] * 4,
        input_output_aliases={5 + j: j for j in range(n_prev)}, name=name, compiler_params=_cp("parallel"),
    )(w, g, m, v, behind, *(prev or ()))


def kernel(x, p, norm_mix, w_in, conv_qkv, a_log, dt_bias, head_norm, sgu_norm, w_spatial, b_spatial, w_branch_a, w_branch_b, w_out, norm_ffn, w_ffn_up, conv_ffn, b_conv_ffn, w_ffn_down, norm_ple, w_ple_gate, w_ple_proj, norm_final, loss_target, m_norm_mix, m_w_in, m_conv_qkv, m_a_log, m_dt_bias, m_head_norm, m_sgu_norm, m_w_spatial, m_b_spatial, m_w_branch_a, m_w_branch_b, m_w_out, m_norm_ffn, m_w_ffn_up, m_conv_ffn, m_b_conv_ffn, m_w_ffn_down, m_norm_ple, m_w_ple_gate, m_w_ple_proj, m_norm_final, v_norm_mix, v_w_in, v_conv_qkv, v_a_log, v_dt_bias, v_head_norm, v_sgu_norm, v_w_spatial, v_b_spatial, v_w_branch_a, v_w_branch_b, v_w_out, v_norm_ffn, v_w_ffn_up, v_conv_ffn, v_b_conv_ffn, v_w_ffn_down, v_norm_ple, v_w_ple_gate, v_w_ple_proj, v_norm_final):
    w = dict(norm_mix=norm_mix, w_in=w_in, conv_qkv=conv_qkv, a_log=a_log, dt_bias=dt_bias, head_norm=head_norm, sgu_norm=sgu_norm,
             w_spatial=w_spatial, b_spatial=b_spatial, w_branch_a=w_branch_a, w_branch_b=w_branch_b, w_out=w_out, norm_ffn=norm_ffn,
             w_ffn_up=w_ffn_up, conv_ffn=conv_ffn, b_conv_ffn=b_conv_ffn, w_ffn_down=w_ffn_down, norm_ple=norm_ple,
             w_ple_gate=w_ple_gate, w_ple_proj=w_ple_proj, norm_final=norm_final)
    m = dict(norm_mix=m_norm_mix, w_in=m_w_in, conv_qkv=m_conv_qkv, a_log=m_a_log, dt_bias=m_dt_bias, head_norm=m_head_norm,
             sgu_norm=m_sgu_norm, w_spatial=m_w_spatial, b_spatial=m_b_spatial, w_branch_a=m_w_branch_a, w_branch_b=m_w_branch_b,
             w_out=m_w_out, norm_ffn=m_norm_ffn, w_ffn_up=m_w_ffn_up, conv_ffn=m_conv_ffn, b_conv_ffn=m_b_conv_ffn,
             w_ffn_down=m_w_ffn_down, norm_ple=m_norm_ple, w_ple_gate=m_w_ple_gate, w_ple_proj=m_w_ple_proj, norm_final=m_norm_final)
    v = dict(norm_mix=v_norm_mix, w_in=v_w_in, conv_qkv=v_conv_qkv, a_log=v_a_log, dt_bias=v_dt_bias, head_norm=v_head_norm,
             sgu_norm=v_sgu_norm, w_spatial=v_w_spatial, b_spatial=v_b_spatial, w_branch_a=v_w_branch_a, w_branch_b=v_w_branch_b,
             w_out=v_w_out, norm_ffn=v_norm_ffn, w_ffn_up=v_w_ffn_up, conv_ffn=v_conv_ffn, b_conv_ffn=v_b_conv_ffn,
             w_ffn_down=v_w_ffn_down, norm_ple=v_norm_ple, w_ple_gate=v_w_ple_gate, w_ple_proj=v_w_ple_proj, norm_final=v_norm_final)
    chip = 2 * lax.axis_index("x") + lax.axis_index("y")
    core = lax.axis_index("c")

    def by_chip(own, gathered):
        return [jnp.where(chip == j, own, gathered[j]) for j in range(N_CHIPS)]

    conv_buf = _flat([w[k] for k in SMALL_COLS], f32)
    (conv_all,) = _gather_shards([conv_buf], "gather_conv")
    sm = {k: w[k] for k in SMALL_REPL}
    conv_parts = [_unflat(b, [w[k].shape for k in SMALL_COLS]) for b in by_chip(conv_buf, conv_all)]
    for n, k in enumerate(SMALL_COLS):
        sm[k] = jnp.concatenate([conv_parts[j][n] for j in range(N_CHIPS)], axis=-1)
    Ss = [_small_params(sm, i) for i in range(DEPTH)]
    def start_gather(i, behind):
        own = [w[k][i].astype(bf16) for k in BIG]
        own[-1] = own[-1] + behind.astype(bf16)
        return _gather_start(own, f"gather_start_l{i}")

    h = x[0]
    saved, Ws = [], []
    gather = start_gather(0, jnp.zeros((), f32))
    for i in range(DEPTH):
        own, lands = _exchange_wait(gather, gather["token"] if i == 0 else h, f"gather_wait_l{i}")
        S_i = Ss[i]
        if i + 1 < DEPTH:
            gather = start_gather(i + 1, own[-1][0, 0] * 0)
            S_i = dict(S_i, norm_mix=S_i["norm_mix"] + gather["token"][0, 0])
        lands = _gather_forward(lands, f"gather_forward_l{i}")
        Ws.append(_big_weights({k: by_chip(o, g) for k, o, g in zip(BIG, own, lands)}))
        h, sv = _layer_fwd(h, p[i, 0], Ws[i], S_i, i)
        saved.append(sv)
    loss_row, dh, g_final = _loss_head(h, sm["norm_final"][None], loss_target[0], "loss_head")

    gsm = {k: [None] * DEPTH for k in SMALL_REPL + SMALL_COLS if k != "norm_final"}
    gbig = {k: [None] * DEPTH for k in BIG}
    pending = None

    def finish(job, after):
        j, handle = job
        for k, g in zip(BIG, _rs_end(handle, after, BIG, f"l{j}")):
            gbig[k][j] = g

    for i in reversed(range(DEPTH)):
        S_i = Ss[i] if pending is None else dict(Ss[i], norm_ple=Ss[i]["norm_ple"] + pending[1]["token"][0, 0])
        dh, G = _layer_bwd(dh, saved[i], Ws[i], S_i, i)
        parts = _grads_by_chip(G)
        begun = (i, _rs_begin([parts[k] for k in BIG], BIG, f"l{i}"))
        if pending is not None:
            finish(pending, dh)
        pending = begun
        for k in gsm:
            gsm[k][i] = G[k].reshape(w[k].shape[1:-1] + (-1,)) if k in SMALL_COLS else G[k].reshape(w[k].shape[1:])

    small_names = [k for k in SMALL_REPL + SMALL_COLS if k != "norm_final"]
    small_local = [jnp.stack(gsm[k]) for k in small_names] + [g_final.reshape(-1), loss_row[0, :1]]
    small_shapes = [a.shape for a in small_local]
    small_buf = _allreduce_small(_flat(small_local, f32), "allreduce_small")

    two = lambda a: a.reshape(-1, a.shape[-1])
    big_out = {k: None for k in BIG}

    def update_layer(j):
        for k in BIG:
            big_out[k] = _adamw_layer(two(w[k]), gbig[k][j], two(m[k]), two(v[k]), big_out[k], pending[1]["token"], j,
                                      f"adamw_{k}_l{j}")

    for j in reversed(range(1, DEPTH)):
        update_layer(j)
    finish(pending, jnp.broadcast_to(sum(big_out[k][1][0, 0] for k in BIG), (8, LANES)))
    update_layer(0)
    small_sum = _unflat(small_buf, small_shapes)
    gs = dict(zip(small_names + ["norm_final"], small_sum[:-1]))
    loss = small_sum[-1][0]
    for k in SMALL_COLS:
        n = w[k].shape[-1]
        gs[k] = _pick(chip, [gs[k][..., j * n:(j + 1) * n] for j in range(N_CHIPS)])

    grads, deltas, new_m, new_v = {}, {}, {}, {}
    for k in BIG:
        grads[k], deltas[k], new_m[k], new_v[k] = [a.reshape(w[k].shape) for a in big_out[k]]
    small_all = [k for k in WEIGHTS if k not in BIG]
    shapes = [w[k].shape for k in small_all]
    d, m2, v2 = _ew(_adamw, [_flat([w[k] for k in small_all], f32), _flat([gs[k] for k in small_all], f32),
                             _flat([m[k] for k in small_all], f32), _flat([v[k] for k in small_all], f32)],
                    [f32, f32, f32], "adamw_small")
    for k, a, b, c_ in zip(small_all, _unflat(d, shapes), _unflat(m2, shapes), _unflat(v2, shapes)):
        grads[k], deltas[k], new_m[k], new_v[k] = gs[k], a, b, c_
    return (loss, dh[None], *[grads[k] for k in WEIGHTS], *[deltas[k] for k in WEIGHTS],
            *[new_m[k] for k in WEIGHTS], *[new_v[k] for k in WEIGHTS])
```
